```python
import math
import jax, jax.numpy as jnp
from jax import lax
import numpy as np

D_MODEL = 2048
BATCH = 2
SEQ = 4096
DEPTH = 1

HEAD_DIM = 128
N_HEADS = D_MODEL // HEAD_DIM
H_A = N_HEADS // 2
H_B = N_HEADS - H_A
D_A = H_A * HEAD_DIM
D_B = H_B * HEAD_DIM
D_PROJ = 3 * D_A + 3 * D_B + H_B
DILATED_PATTERNS = ((128, 1), (512, 4), (2048, 16))
NUM_BUCKETS = 32
MAX_DISTANCE = 2048
FOX_BLOCK = 128
D_FF = ((8 * D_MODEL // 3 + 127) // 128) * 128
RMS_EPS = 1e-6
NEG_INF = -1e30
ATTN_SCALE = HEAD_DIM ** -0.5

kernel_name = "hybrid_dilated_fox_macaron"


def rmsnorm(x, g):
    xf = x.astype(jnp.float32)
    y = xf * lax.rsqrt(jnp.mean(xf * xf, axis=-1, keepdims=True) + RMS_EPS)
    return (y * g.astype(jnp.float32)).astype(x.dtype)


def swiglu_ffn(x, w_in, w_out):
    gate, up = jnp.split(x @ w_in, 2, axis=-1)
    return (jax.nn.silu(gate) * up) @ w_out


def t5_causal_bucket(dist):
    max_exact = NUM_BUCKETS // 2
    d = dist.astype(jnp.float32)
    large = max_exact + (jnp.log(jnp.maximum(d, 1.0) / max_exact)
                         / math.log(MAX_DISTANCE / max_exact)
                         * (NUM_BUCKETS - max_exact)).astype(jnp.int32)
    large = jnp.minimum(large, NUM_BUCKETS - 1)
    return jnp.where(dist < max_exact, dist, large)


def dilated_window_partial(q, k, v, rel_table, window, dilation):
    B, S, H, D = q.shape
    m = window // dilation
    L = m
    span = dilation * L
    Sp = -(-S // span) * span
    n = Sp // dilation
    nb = n // L

    def to_sub(t):
        t = jnp.pad(t, ((0, 0), (0, Sp - S), (0, 0), (0, 0)))
        t = t.reshape(B, n, dilation, H, D).transpose(0, 2, 3, 1, 4)
        return t.reshape(B, dilation, H, nb, L, D)

    def with_prev(t):
        prev = jnp.pad(t, ((0, 0), (0, 0), (0, 0), (1, 0), (0, 0), (0, 0)))[:, :, :, :-1]
        return jnp.concatenate([prev, t], axis=4)

    qs = to_sub(q)
    kw = with_prev(to_sub(k))
    vw = with_prev(to_sub(v))

    iq = jnp.arange(L)[:, None] + L
    ik = jnp.arange(2 * L)[None, :]
    delta = iq - ik
    band = (delta >= 0) & (delta <= m)
    bucket = t5_causal_bucket(jnp.clip(delta, 0, None) * dilation)
    bias = rel_table[bucket].astype(jnp.float32).transpose(2, 0, 1)
    blk = jnp.arange(nb)
    key_ok = (blk[:, None] * L + ik - L) >= 0
    mask = band[None] & key_ok[:, None, :]

    logits = jnp.einsum('bzhnqd,bzhnkd->bzhnqk', qs, kw) * ATTN_SCALE + bias[:, None]
    logits = jnp.where(mask, logits, NEG_INF)
    mx = jnp.max(logits, axis=-1)
    p = jnp.exp(logits - mx[..., None])
    den = jnp.sum(p, axis=-1)
    num = jnp.einsum('bzhnqk,bzhnkd->bzhnqd', p, vw)

    def from_sub(t):
        rest = t.shape[5:]
        t = t.reshape((B, dilation, H, n) + rest)
        t = jnp.moveaxis(t, 3, 1)
        return t.reshape((B, Sp, H) + rest)[:, :S]

    return from_sub(num), from_sub(den), from_sub(mx)


def dilated_attention(q, k, v, rel_table):
    parts = [dilated_window_partial(q, k, v, rel_table, w, d) for (w, d) in DILATED_PATTERNS]
    mx = jnp.max(jnp.stack([pt[2] for pt in parts], axis=0), axis=0)
    num = None
    den = None
    for pnum, pden, pmx in parts:
        scale = jnp.exp(pmx - mx)
        num = pnum * scale[..., None] if num is None else num + pnum * scale[..., None]
        den = pden * scale if den is None else den + pden * scale
    return num / den[..., None]


def forgetting_attention(q, k, v, log_f):
    B, S, H, D = q.shape
    c = jnp.cumsum(log_f, axis=1)
    c_k = c.transpose(0, 2, 1)
    nblk = S // FOX_BLOCK
    q_blocks = q.reshape(B, nblk, FOX_BLOCK, H, D).transpose(1, 0, 3, 2, 4)
    c_blocks = c.reshape(B, nblk, FOX_BLOCK, H).transpose(1, 0, 3, 2)
    kpos = jnp.arange(S)

    def block(args):
        q_blk, c_blk, i = args
        qpos = i * FOX_BLOCK + jnp.arange(FOX_BLOCK)
        logits = (jnp.einsum('bhqd,bkhd->bhqk', q_blk, k) * ATTN_SCALE
                  + c_blk[..., None] - c_k[:, :, None, :])
        logits = jnp.where(kpos[None, :] <= qpos[:, None], logits, NEG_INF)
        p = jax.nn.softmax(logits, axis=-1)
        return jnp.einsum('bhqk,bkhd->bhqd', p, v)

    out = lax.map(block, (q_blocks, c_blocks, jnp.arange(nblk)))
    return out.transpose(1, 0, 3, 2, 4).reshape(B, S, H, D)


def setup_inputs(seed: int = 0) -> dict:
    key = jax.random.key(seed)
    ks = jax.random.split(key, 17)
    f32 = jnp.float32

    def gain(k, shape):
        return (1.0 + 0.02 * jax.random.normal(k, shape)).astype(f32)

    x = jax.random.normal(ks[0], (BATCH, SEQ, D_MODEL), f32)
    ffn1_norm = gain(ks[1], (DEPTH, D_MODEL))
    ffn1_w_in = jax.random.normal(ks[2], (DEPTH, D_MODEL, 2 * D_FF), f32) * D_MODEL ** -0.5
    ffn1_w_out = jax.random.normal(ks[3], (DEPTH, D_FF, D_MODEL), f32) * D_FF ** -0.5
    mix_norm = gain(ks[4], (DEPTH, D_MODEL))
    w_in = jax.random.normal(ks[5], (DEPTH, D_MODEL, D_PROJ), f32) * D_MODEL ** -0.5
    q_norm_a = gain(ks[6], (DEPTH, HEAD_DIM))
    k_norm_a = gain(ks[7], (DEPTH, HEAD_DIM))
    q_norm_b = gain(ks[8], (DEPTH, HEAD_DIM))
    k_norm_b = gain(ks[9], (DEPTH, HEAD_DIM))
    forget_bias = (jnp.linspace(1.0, 4.0, H_B, dtype=f32)[None, :]
                   + 0.1 * jax.random.normal(ks[10], (DEPTH, H_B), f32))
    rel_bias_table = 0.5 * jax.random.normal(ks[11], (NUM_BUCKETS, H_A), f32)
    w_out = jax.random.normal(ks[12], (DEPTH, D_A + D_B, D_MODEL), f32) * (D_A + D_B) ** -0.5
    ffn2_norm = gain(ks[13], (DEPTH, D_MODEL))
    ffn2_w_in = jax.random.normal(ks[14], (DEPTH, D_MODEL, 2 * D_FF), f32) * D_MODEL ** -0.5
    ffn2_w_out = jax.random.normal(ks[15], (DEPTH, D_FF, D_MODEL), f32) * D_FF ** -0.5
    return {"x": x, "ffn1_norm": ffn1_norm, "ffn1_w_in": ffn1_w_in, "ffn1_w_out": ffn1_w_out,
            "mix_norm": mix_norm, "w_in": w_in, "q_norm_a": q_norm_a, "k_norm_a": k_norm_a,
            "q_norm_b": q_norm_b, "k_norm_b": k_norm_b, "forget_bias": forget_bias,
            "rel_bias_table": rel_bias_table, "w_out": w_out, "ffn2_norm": ffn2_norm,
            "ffn2_w_in": ffn2_w_in, "ffn2_w_out": ffn2_w_out}


def reference(x, ffn1_norm, ffn1_w_in, ffn1_w_out, mix_norm, w_in, q_norm_a, k_norm_a,
              q_norm_b, k_norm_b, forget_bias, rel_bias_table, w_out, ffn2_norm,
              ffn2_w_in, ffn2_w_out):
    B, S, _ = x.shape
    splits = np.cumsum([D_A, D_A, D_A, D_B, D_B, D_B]).tolist()
    for l in range(DEPTH):
        x = x + 0.5 * swiglu_ffn(rmsnorm(x, ffn1_norm[l]), ffn1_w_in[l], ffn1_w_out[l])

        h = rmsnorm(x, mix_norm[l])
        proj = h @ w_in[l]
        qa, ka, va, qb, kb, vb, f_pre = jnp.split(proj, splits, axis=-1)

        def heads(t, n_h):
            return t.reshape(B, S, n_h, HEAD_DIM).astype(jnp.float32)

        qa = rmsnorm(heads(qa, H_A), q_norm_a[l])
        ka = rmsnorm(heads(ka, H_A), k_norm_a[l])
        va = heads(va, H_A)
        qb = rmsnorm(heads(qb, H_B), q_norm_b[l])
        kb = rmsnorm(heads(kb, H_B), k_norm_b[l])
        vb = heads(vb, H_B)
        log_f = jax.nn.log_sigmoid(f_pre.astype(jnp.float32) + forget_bias[l].astype(jnp.float32))

        out_a = dilated_attention(qa, ka, va, rel_bias_table)
        out_b = forgetting_attention(qb, kb, vb, log_f)
        mixed = jnp.concatenate([out_a.reshape(B, S, D_A), out_b.reshape(B, S, D_B)],
                                axis=-1).astype(x.dtype)
        x = x + mixed @ w_out[l]

        x = x + 0.5 * swiglu_ffn(rmsnorm(x, ffn2_norm[l]), ffn2_w_in[l], ffn2_w_out[l])
    return x
```

```python
import functools
import math

import numpy as np
import jax
import jax.numpy as jnp
from jax import lax
from jax.experimental import pallas as pl
from jax.experimental.pallas import tpu as pltpu

D_MODEL = 2048
HEAD_DIM = 128
N_HEADS = D_MODEL // HEAD_DIM
H_A = N_HEADS // 2
H_B = N_HEADS - H_A
D_A = H_A * HEAD_DIM
D_B = H_B * HEAD_DIM
D_QKV = 3 * D_A + 3 * D_B
DILATED_PATTERNS = ((128, 1), (512, 4), (2048, 16))
BAND = 128
NUM_BUCKETS = 32
MAX_DISTANCE = 2048
D_FF = ((8 * D_MODEL // 3 + 127) // 128) * 128
RMS_EPS = 1e-6
NEG_INF = -1e30
ATTN_SCALE = HEAD_DIM ** -0.5

LANES = 128
FF_TILE = 512
D_FF_PAD = -(-D_FF // FF_TILE) * FF_TILE
ROW_TILE = 1024
COL_TILE = 512
FOX_TQ = 512
FOX_TK = 512
VMEM_LIMIT = 56 * 1024 * 1024

F32 = jnp.float32
BF16 = jnp.bfloat16


def _params(*sem):
    return pltpu.CompilerParams(dimension_semantics=sem, vmem_limit_bytes=VMEM_LIMIT)


def _rms_rows(x, gain):
    ms = jnp.mean(x * x, axis=-1, keepdims=True)
    return x * lax.rsqrt(ms + RMS_EPS) * gain


def _ffn_in_kernel(x_ref, g_ref, w_ref, act_ref, h_ref):
    @pl.when(pl.program_id(1) == 0)
    def _():
        h_ref[...] = _rms_rows(x_ref[...], g_ref[...]).astype(BF16)

    r = jnp.dot(h_ref[...], w_ref[...], preferred_element_type=F32)
    gate = r[:, :FF_TILE]
    up = r[:, FF_TILE:]
    act_ref[...] = (gate * (1.0 / (1.0 + jnp.exp(-gate))) * up).astype(BF16)


def _ffn_out_kernel(act_ref, w_ref, x_ref, o_ref):
    y = jnp.dot(act_ref[...], w_ref[...], preferred_element_type=F32)
    o_ref[...] = x_ref[...] + 0.5 * y


def _ffn(x, gain, w_in, w_out):
    m = x.shape[0]
    n_ff = D_FF_PAD // FF_TILE
    pad = D_FF_PAD - D_FF
    wg = jnp.pad(w_in[:, :D_FF], ((0, 0), (0, pad))).reshape(D_MODEL, n_ff, 1, FF_TILE)
    wu = jnp.pad(w_in[:, D_FF:], ((0, 0), (0, pad))).reshape(D_MODEL, n_ff, 1, FF_TILE)
    w_gu = jnp.concatenate([wg, wu], axis=2).reshape(D_MODEL, 2 * D_FF_PAD).astype(BF16)
    w_o = jnp.pad(w_out, ((0, pad), (0, 0))).astype(BF16)

    act = pl.pallas_call(
        _ffn_in_kernel,
        grid=(m // ROW_TILE, n_ff),
        in_specs=[
            pl.BlockSpec((ROW_TILE, D_MODEL), lambda i, j: (i, 0)),
            pl.BlockSpec((1, D_MODEL), lambda i, j: (0, 0)),
            pl.BlockSpec((D_MODEL, 2 * FF_TILE), lambda i, j: (0, j)),
        ],
        out_specs=pl.BlockSpec((ROW_TILE, FF_TILE), lambda i, j: (i, j)),
        out_shape=jax.ShapeDtypeStruct((m, D_FF_PAD), BF16),
        scratch_shapes=[pltpu.VMEM((ROW_TILE, D_MODEL), BF16)],
        compiler_params=_params("parallel", "arbitrary"),
        name="ffn_in",
    )(x, gain.reshape(1, D_MODEL), w_gu)

    return pl.pallas_call(
        _ffn_out_kernel,
        grid=(m // ROW_TILE, D_MODEL // COL_TILE),
        in_specs=[
            pl.BlockSpec((ROW_TILE, D_FF_PAD), lambda i, j: (i, 0)),
            pl.BlockSpec((D_FF_PAD, COL_TILE), lambda i, j: (0, j)),
            pl.BlockSpec((ROW_TILE, COL_TILE), lambda i, j: (i, j)),
        ],
        out_specs=pl.BlockSpec((ROW_TILE, COL_TILE), lambda i, j: (i, j)),
        out_shape=jax.ShapeDtypeStruct((m, D_MODEL), F32),
        compiler_params=_params("parallel", "parallel"),
        name="ffn_out",
    )(act, w_o, x)


HEADS_PER_TILE = COL_TILE // HEAD_DIM
N_PROJ_TILES = D_QKV // COL_TILE
V_TILES = tuple(t for t in range(N_PROJ_TILES) if (t * COL_TILE // D_A) % 3 == 2)


def _proj_kernel(x_ref, g_ref, w_ref, wf_ref, hg_ref, o_ref, f_ref, h_ref):
    j = pl.program_id(1)

    @pl.when(j == 0)
    def _():
        h_ref[...] = _rms_rows(x_ref[...], g_ref[...]).astype(BF16)
        f_ref[...] = jnp.dot(h_ref[...], wf_ref[...], preferred_element_type=F32)

    r = jnp.dot(h_ref[...], w_ref[...], preferred_element_type=F32)
    is_v = functools.reduce(jnp.logical_or, [j == t for t in V_TILES])

    @pl.when(is_v)
    def _():
        o_ref[...] = r.astype(BF16)

    @pl.when(jnp.logical_not(is_v))
    def _():
        for hh in range(HEADS_PER_TILE):
            sl = slice(hh * HEAD_DIM, (hh + 1) * HEAD_DIM)
            o_ref[:, sl] = _rms_rows(r[:, sl], hg_ref[:, sl]).astype(BF16)


def _proj(x, gain, w_qkv, w_f, head_gain):
    m = x.shape[0]
    return pl.pallas_call(
        _proj_kernel,
        grid=(m // ROW_TILE, N_PROJ_TILES),
        in_specs=[
            pl.BlockSpec((ROW_TILE, D_MODEL), lambda i, j: (i, 0)),
            pl.BlockSpec((1, D_MODEL), lambda i, j: (0, 0)),
            pl.BlockSpec((D_MODEL, COL_TILE), lambda i, j: (0, j)),
            pl.BlockSpec((D_MODEL, LANES), lambda i, j: (0, 0)),
            pl.BlockSpec((1, COL_TILE), lambda i, j: (0, j)),
        ],
        out_specs=[
            pl.BlockSpec((ROW_TILE, COL_TILE), lambda i, j: (i, j)),
            pl.BlockSpec((ROW_TILE, LANES), lambda i, j: (i, 0)),
        ],
        out_shape=[
            jax.ShapeDtypeStruct((m, D_QKV), BF16),
            jax.ShapeDtypeStruct((m, LANES), F32),
        ],
        scratch_shapes=[pltpu.VMEM((ROW_TILE, D_MODEL), BF16)],
        compiler_params=_params("parallel", "arbitrary"),
        name="proj",
    )(x, gain.reshape(1, D_MODEL), w_qkv, w_f, head_gain)


def _gate_kernel(f_ref, b_ref, c_ref):
    z = f_ref[0] + b_ref[...]
    c = jnp.minimum(z, 0.0) - jnp.log1p(jnp.exp(-jnp.abs(z)))
    row = lax.broadcasted_iota(jnp.int32, c.shape, 0)
    shift = 1
    while shift < c.shape[0]:
        c = c + jnp.where(row >= shift, pltpu.roll(c, shift, axis=0), 0.0)
        shift *= 2
    c_ref[0] = c


def _gate(f_pre, bias):
    b, s, _ = f_pre.shape
    return pl.pallas_call(
        _gate_kernel,
        grid=(b,),
        in_specs=[
            pl.BlockSpec((1, s, LANES), lambda i: (i, 0, 0)),
            pl.BlockSpec((1, LANES), lambda i: (0, 0)),
        ],
        out_specs=pl.BlockSpec((1, s, LANES), lambda i: (i, 0, 0)),
        out_shape=jax.ShapeDtypeStruct((b, s, LANES), F32),
        compiler_params=_params("parallel"),
        name="gate",
    )(f_pre, bias)


def _bucket_steps(dilation):
    dist = np.arange(BAND + 1) * dilation
    max_exact = NUM_BUCKETS // 2
    large = max_exact + np.floor(
        np.log(np.maximum(dist, 1) / max_exact) / math.log(MAX_DISTANCE / max_exact)
        * (NUM_BUCKETS - max_exact)).astype(np.int64)
    bucket = np.where(dist < max_exact, dist, np.minimum(large, NUM_BUCKETS - 1))
    steps = [(0, int(bucket[0]))]
    for delta in range(1, BAND + 1):
        if bucket[delta] != bucket[delta - 1]:
            steps.append((delta, int(bucket[delta])))
    return steps


def _bias_kernel(table_ref, o_ref):
    iq = lax.broadcasted_iota(jnp.int32, (BAND, 2 * BAND), 0)
    ik = lax.broadcasted_iota(jnp.int32, (BAND, 2 * BAND), 1)
    delta = iq + BAND - ik
    in_band = jnp.logical_and(delta >= 0, delta <= BAND)
    for p, (_, dilation) in enumerate(DILATED_PATTERNS):
        steps = _bucket_steps(dilation)
        for h in range(H_A):
            val = jnp.full((BAND, 2 * BAND), table_ref[steps[0][1], h], F32)
            for start, bucket in steps[1:]:
                val = jnp.where(delta >= start, table_ref[bucket, h], val)
            o_ref[p, h] = jnp.where(in_band, val, NEG_INF)


def _band_bias(rel_table):
    n_pat = len(DILATED_PATTERNS)
    return pl.pallas_call(
        _bias_kernel,
        in_specs=[pl.BlockSpec(memory_space=pltpu.SMEM)],
        out_specs=pl.BlockSpec(memory_space=pltpu.VMEM),
        out_shape=jax.ShapeDtypeStruct((n_pat, H_A, BAND, 2 * BAND), F32),
        name="band_bias",
    )(rel_table)


def _dilated_kernel(q_ref, kp_ref, kc_ref, vp_ref, vc_ref, bias_ref, o_ref, st_ref):
    first = pl.program_id(2) == 0
    key_lane = lax.broadcasted_iota(jnp.int32, (1, 2 * BAND), 1)
    no_prev = jnp.where(jnp.logical_and(first, key_lane < BAND), NEG_INF, 0.0)
    stat_lane = lax.broadcasted_iota(jnp.int32, (BAND, LANES), 1)
    stats = jnp.zeros((BAND, LANES), F32)
    for h in range(H_A):
        sl = slice(h * HEAD_DIM, (h + 1) * HEAD_DIM)
        k = jnp.concatenate([kp_ref[0, :, sl], kc_ref[0, :, sl]], axis=0)
        v = jnp.concatenate([vp_ref[0, :, sl], vc_ref[0, :, sl]], axis=0)
        s = lax.dot_general(q_ref[0, :, sl], k, (((1,), (1,)), ((), ())),
                            preferred_element_type=F32)
        s = s + bias_ref[0, h] + no_prev
        mx = jnp.max(s, axis=-1, keepdims=True)
        p = jnp.exp(s - mx)
        den = jnp.sum(p, axis=-1, keepdims=True)
        o_ref[0, :, sl] = jnp.dot(p.astype(BF16), v, preferred_element_type=F32)
        stats = jnp.where(stat_lane == h, mx, stats)
        stats = jnp.where(stat_lane == H_A + h, den, stats)
    st_ref[0] = stats


def _dilated(qkv, bias, pattern):
    b, s, _ = qkv.shape
    _, dilation = DILATED_PATTERNS[pattern]
    n_sub = s // dilation
    nb = n_sub // BAND
    qkv_v = qkv.reshape(b, n_sub, dilation * D_QKV)
    w_blocks = D_QKV // D_A

    def spec(which, prev):
        def index(bi, r, i):
            blk = jnp.maximum(i - 1, 0) if prev else i
            return (bi, blk, r * w_blocks + which)
        return pl.BlockSpec((1, BAND, D_A), index)

    num, stats = pl.pallas_call(
        _dilated_kernel,
        grid=(b, dilation, nb),
        in_specs=[
            spec(0, False), spec(1, True), spec(1, False), spec(2, True), spec(2, False),
            pl.BlockSpec((1, H_A, BAND, 2 * BAND), lambda bi, r, i: (pattern, 0, 0, 0)),
        ],
        out_specs=[
            pl.BlockSpec((1, BAND, D_A), lambda bi, r, i: (bi, i, r)),
            pl.BlockSpec((1, BAND, LANES), lambda bi, r, i: (bi, i, r)),
        ],
        out_shape=[
            jax.ShapeDtypeStruct((b, n_sub, dilation * D_A), F32),
            jax.ShapeDtypeStruct((b, n_sub, dilation * LANES), F32),
        ],
        compiler_params=_params("parallel", "parallel", "arbitrary"),
        name=f"dilated_{dilation}",
    )(qkv_v, qkv_v, qkv_v, qkv_v, qkv_v, bias)
    return num.reshape(b * s, D_A), stats.reshape(b * s, LANES)


def _combine_kernel(*refs):
    n_pat = len(DILATED_PATTERNS)
    num_refs, st_refs, o_ref = refs[:n_pat], refs[n_pat:2 * n_pat], refs[2 * n_pat]
    stats = [r[...] for r in st_refs]
    for h in range(H_A):
        sl = slice(h * HEAD_DIM, (h + 1) * HEAD_DIM)
        mxs = [st[:, h:h + 1] for st in stats]
        dens = [st[:, H_A + h:H_A + h + 1] for st in stats]
        mx = functools.reduce(jnp.maximum, mxs)
        num = None
        den = None
        for pnum, pden, pmx in zip(num_refs, dens, mxs):
            scale = jnp.exp(pmx - mx)
            num = pnum[:, sl] * scale if num is None else num + pnum[:, sl] * scale
            den = pden * scale if den is None else den + pden * scale
        o_ref[:, sl] = (num / den).astype(BF16)


def _combine(nums, stats):
    m = nums[0].shape[0]
    rows = 512
    return pl.pallas_call(
        _combine_kernel,
        grid=(m // rows,),
        in_specs=[pl.BlockSpec((rows, D_A), lambda i: (i, 0)) for _ in nums]
        + [pl.BlockSpec((rows, LANES), lambda i: (i, 0)) for _ in stats],
        out_specs=pl.BlockSpec((rows, D_A), lambda i: (i, 0)),
        out_shape=jax.ShapeDtypeStruct((m, D_A), BF16),
        compiler_params=_params("parallel"),
        name="combine_a",
    )(*nums, *stats)


def _fox_kernel(q_ref, k_ref, v_ref, cq_ref, ck_ref, o_ref, m_ref, l_ref, acc_ref):
    iq = pl.program_id(2)
    q = q_ref[0]
    cq = cq_ref[0]
    m_ref[...] = jnp.full(m_ref.shape, NEG_INF, F32)
    l_ref[...] = jnp.zeros(l_ref.shape, F32)
    acc_ref[...] = jnp.zeros(acc_ref.shape, F32)

    def step(kb, diagonal):
        start = pl.multiple_of(kb * FOX_TK, FOX_TK)
        k = k_ref[0, pl.ds(start, FOX_TK), :]
        v = v_ref[0, pl.ds(start, FOX_TK), :]
        s = lax.dot_general(q, k, (((1,), (1,)), ((), ())), preferred_element_type=F32)
        s = s + cq - ck_ref[0, :, pl.ds(start, FOX_TK)]
        if diagonal:
            row = lax.broadcasted_iota(jnp.int32, s.shape, 0)
            col = lax.broadcasted_iota(jnp.int32, s.shape, 1)
            s = jnp.where(col <= row, s, NEG_INF)
        m_prev = m_ref[...]
        m_new = jnp.maximum(m_prev, jnp.max(s, axis=-1, keepdims=True))
        alpha = jnp.exp(m_prev - m_new)
        p = jnp.exp(s - m_new)
        l_ref[...] = alpha * l_ref[...] + jnp.sum(p, axis=-1, keepdims=True)
        acc_ref[...] = alpha * acc_ref[...] + jnp.dot(p.astype(BF16), v,
                                                      preferred_element_type=F32)
        m_ref[...] = m_new

    def body(kb, carry):
        step(kb, False)
        return carry

    lax.fori_loop(0, iq, body, 0)
    step(iq, True)
    o_ref[0] = (acc_ref[...] / l_ref[...]).astype(BF16)


def _fox(qkv, c_col, c_row):
    b, s, _ = qkv.shape
    qb, kb, vb = [(3 * D_A + t * D_B) // HEAD_DIM for t in range(3)]
    return pl.pallas_call(
        _fox_kernel,
        grid=(b, H_B, s // FOX_TQ),
        in_specs=[
            pl.BlockSpec((1, FOX_TQ, HEAD_DIM), lambda bi, h, i: (bi, i, qb + h)),
            pl.BlockSpec((1, s, HEAD_DIM), lambda bi, h, i: (bi, 0, kb + h)),
            pl.BlockSpec((1, s, HEAD_DIM), lambda bi, h, i: (bi, 0, vb + h)),
            pl.BlockSpec((1, FOX_TQ, 1), lambda bi, h, i: (bi * H_B + h, i, 0)),
            pl.BlockSpec((1, 1, s), lambda bi, h, i: (bi * H_B + h, 0, 0)),
        ],
        out_specs=pl.BlockSpec((1, FOX_TQ, HEAD_DIM), lambda bi, h, i: (bi, i, h)),
        out_shape=jax.ShapeDtypeStruct((b, s, D_B), BF16),
        scratch_shapes=[
            pltpu.VMEM((FOX_TQ, 1), F32),
            pltpu.VMEM((FOX_TQ, 1), F32),
            pltpu.VMEM((FOX_TQ, HEAD_DIM), F32),
        ],
        compiler_params=_params("parallel", "parallel", "arbitrary"),
        name="fox",
    )(qkv, qkv, qkv, c_col, c_row)


def _out_proj_kernel(a_ref, b_ref, w_ref, x_ref, o_ref):
    mixed = jnp.concatenate([a_ref[...], b_ref[...]], axis=-1)
    o_ref[...] = x_ref[...] + jnp.dot(mixed, w_ref[...], preferred_element_type=F32)


def _out_proj(out_a, out_b, w_out, x):
    m = x.shape[0]
    return pl.pallas_call(
        _out_proj_kernel,
        grid=(m // ROW_TILE, D_MODEL // COL_TILE),
        in_specs=[
            pl.BlockSpec((ROW_TILE, D_A), lambda i, j: (i, 0)),
            pl.BlockSpec((ROW_TILE, D_B), lambda i, j: (i, 0)),
            pl.BlockSpec((D_A + D_B, COL_TILE), lambda i, j: (0, j)),
            pl.BlockSpec((ROW_TILE, COL_TILE), lambda i, j: (i, j)),
        ],
        out_specs=pl.BlockSpec((ROW_TILE, COL_TILE), lambda i, j: (i, j)),
        out_shape=jax.ShapeDtypeStruct((m, D_MODEL), F32),
        compiler_params=_params("parallel", "parallel"),
        name="out_proj",
    )(out_a, out_b, w_out, x)


def _mixer(x, b, s, mix_norm, w_in, q_norm_a, k_norm_a, q_norm_b, k_norm_b, forget_bias,
           rel_bias_table, w_out):
    ones = jnp.ones((D_A,), F32)
    head_gain = jnp.concatenate([
        jnp.tile(q_norm_a * ATTN_SCALE, H_A), jnp.tile(k_norm_a, H_A), ones,
        jnp.tile(q_norm_b * ATTN_SCALE, H_B), jnp.tile(k_norm_b, H_B), ones,
    ]).reshape(1, D_QKV)
    w_qkv = w_in[:, :D_QKV].astype(BF16)
    w_f = jnp.pad(w_in[:, D_QKV:], ((0, 0), (0, LANES - H_B))).astype(BF16)
    qkv, f_pre = _proj(x, mix_norm, w_qkv, w_f, head_gain)

    f_bias = jnp.pad(forget_bias, (0, LANES - H_B)).reshape(1, LANES)
    c = _gate(f_pre.reshape(b, s, LANES), f_bias)[:, :, :H_B]
    c_heads = c.transpose(0, 2, 1).reshape(b * H_B, s)
    qkv3 = qkv.reshape(b, s, D_QKV)
    out_b = _fox(qkv3, c_heads[:, :, None], c_heads[:, None, :]).reshape(b * s, D_B)

    bias = _band_bias(rel_bias_table)
    parts = [_dilated(qkv3, bias, p) for p in range(len(DILATED_PATTERNS))]
    out_a = _combine([pt[0] for pt in parts], [pt[1] for pt in parts])

    return _out_proj(out_a, out_b, w_out.astype(BF16), x)


def kernel(x, ffn1_norm, ffn1_w_in, ffn1_w_out, mix_norm, w_in, q_norm_a, k_norm_a, q_norm_b,
           k_norm_b, forget_bias, rel_bias_table, w_out, ffn2_norm, ffn2_w_in, ffn2_w_out):
    b, s, d = x.shape
    depth = ffn1_norm.shape[0]
    x = x.reshape(b * s, d)
    for l in range(depth):
        x = _ffn(x, ffn1_norm[l], ffn1_w_in[l], ffn1_w_out[l])
        x = _mixer(x, b, s, mix_norm[l], w_in[l], q_norm_a[l], k_norm_a[l], q_norm_b[l],
                   k_norm_b[l], forget_bias[l], rel_bias_table, w_out[l])
        x = _ffn(x, ffn2_norm[l], ffn2_w_in[l], ffn2_w_out[l])
    return x.reshape(b, s, d)
```

```python
import functools
import math

import numpy as np
import jax
import jax.numpy as jnp
from jax import lax
from jax.experimental import pallas as pl
from jax.experimental.pallas import tpu as pltpu

D_MODEL = 2048
HEAD_DIM = 128
N_HEADS = D_MODEL // HEAD_DIM
H_A = N_HEADS // 2
H_B = N_HEADS - H_A
D_A = H_A * HEAD_DIM
D_B = H_B * HEAD_DIM
D_QKV = 3 * D_A + 3 * D_B
DILATED_PATTERNS = ((128, 1), (512, 4), (2048, 16))
BAND = 128
NUM_BUCKETS = 32
MAX_DISTANCE = 2048
D_FF = ((8 * D_MODEL // 3 + 127) // 128) * 128
RMS_EPS = 1e-6
NEG_INF = -1e30
ATTN_SCALE = HEAD_DIM ** -0.5

LANES = 128
FF_TILE = 512
FFN_OUT_TILE = 256
ROW_TILE = 1024
COL_TILE = 512
FOX_TQ = 512
FOX_TK = 512
VMEM_LIMIT = 56 * 1024 * 1024

F32 = jnp.float32
BF16 = jnp.bfloat16


def _params(*sem):
    return pltpu.CompilerParams(dimension_semantics=sem, vmem_limit_bytes=VMEM_LIMIT)


def _rms_rows(x, gain):
    ms = jnp.mean(x * x, axis=-1, keepdims=True)
    return x * lax.rsqrt(ms + RMS_EPS) * gain


def _ffn_in_kernel(x_ref, g_ref, wg_ref, wu_ref, act_ref, h_ref):
    @pl.when(pl.program_id(1) == 0)
    def _():
        h_ref[...] = _rms_rows(x_ref[...], g_ref[...]).astype(BF16)

    h = h_ref[...]
    gate = jnp.dot(h, wg_ref[...].astype(BF16), preferred_element_type=F32)
    up = jnp.dot(h, wu_ref[...].astype(BF16), preferred_element_type=F32)
    act_ref[...] = (gate * (1.0 / (1.0 + jnp.exp(-gate))) * up).astype(BF16)


def _ffn_out_kernel(act_ref, w_ref, x_ref, o_ref):
    y = jnp.dot(act_ref[...], w_ref[...].astype(BF16), preferred_element_type=F32)
    o_ref[...] = x_ref[...] + 0.5 * y


def _ffn(x, gain, w_in, w_out):
    m = x.shape[0]
    n_ff = pl.cdiv(D_FF, FF_TILE)

    def ff_start(j):
        return pl.multiple_of(jnp.minimum(j * FF_TILE, D_FF - FF_TILE), LANES)

    act = pl.pallas_call(
        _ffn_in_kernel,
        grid=(m // ROW_TILE, n_ff),
        in_specs=[
            pl.BlockSpec((ROW_TILE, D_MODEL), lambda i, j: (i, 0)),
            pl.BlockSpec((1, D_MODEL), lambda i, j: (0, 0)),
            pl.BlockSpec((pl.Element(D_MODEL), pl.Element(FF_TILE)),
                         lambda i, j: (0, ff_start(j))),
            pl.BlockSpec((pl.Element(D_MODEL), pl.Element(FF_TILE)),
                         lambda i, j: (0, pl.multiple_of(D_FF + ff_start(j), LANES))),
        ],
        out_specs=pl.BlockSpec((pl.Element(ROW_TILE), pl.Element(FF_TILE)),
                               lambda i, j: (pl.multiple_of(i * ROW_TILE, ROW_TILE),
                                             ff_start(j))),
        out_shape=jax.ShapeDtypeStruct((m, D_FF), BF16),
        scratch_shapes=[pltpu.VMEM((ROW_TILE, D_MODEL), BF16)],
        compiler_params=_params("parallel", "arbitrary"),
        name="ffn_in",
    )(x, gain.reshape(1, D_MODEL), w_in, w_in)

    return pl.pallas_call(
        _ffn_out_kernel,
        grid=(m // ROW_TILE, D_MODEL // FFN_OUT_TILE),
        in_specs=[
            pl.BlockSpec((ROW_TILE, D_FF), lambda i, j: (i, 0)),
            pl.BlockSpec((D_FF, FFN_OUT_TILE), lambda i, j: (0, j)),
            pl.BlockSpec((ROW_TILE, FFN_OUT_TILE), lambda i, j: (i, j)),
        ],
        out_specs=pl.BlockSpec((ROW_TILE, FFN_OUT_TILE), lambda i, j: (i, j)),
        out_shape=jax.ShapeDtypeStruct((m, D_MODEL), F32),
        compiler_params=_params("parallel", "parallel"),
        name="ffn_out",
    )(act, w_out, x)


HEADS_PER_TILE = COL_TILE // HEAD_DIM
N_PROJ_TILES = D_QKV // COL_TILE
V_TILES = tuple(t for t in range(N_PROJ_TILES) if (t * COL_TILE // D_A) % 3 == 2)


def _proj_kernel(x_ref, g_ref, w_ref, wf_ref, hg_ref, o_ref, f_ref, h_ref):
    j = pl.program_id(1)

    @pl.when(j == 0)
    def _():
        h_ref[...] = _rms_rows(x_ref[...], g_ref[...]).astype(BF16)
        f_ref[...] = jnp.dot(h_ref[...], wf_ref[...], preferred_element_type=F32)

    r = jnp.dot(h_ref[...], w_ref[...].astype(BF16), preferred_element_type=F32)
    is_v = functools.reduce(jnp.logical_or, [j == t for t in V_TILES])

    @pl.when(is_v)
    def _():
        o_ref[...] = r.astype(BF16)

    @pl.when(jnp.logical_not(is_v))
    def _():
        for hh in range(HEADS_PER_TILE):
            sl = slice(hh * HEAD_DIM, (hh + 1) * HEAD_DIM)
            o_ref[:, sl] = _rms_rows(r[:, sl], hg_ref[:, sl]).astype(BF16)


def _proj(x, gain, w_qkv, w_f, head_gain):
    m = x.shape[0]
    return pl.pallas_call(
        _proj_kernel,
        grid=(m // ROW_TILE, N_PROJ_TILES),
        in_specs=[
            pl.BlockSpec((ROW_TILE, D_MODEL), lambda i, j: (i, 0)),
            pl.BlockSpec((1, D_MODEL), lambda i, j: (0, 0)),
            pl.BlockSpec((D_MODEL, COL_TILE), lambda i, j: (0, j)),
            pl.BlockSpec((D_MODEL, LANES), lambda i, j: (0, 0)),
            pl.BlockSpec((1, COL_TILE), lambda i, j: (0, j)),
        ],
        out_specs=[
            pl.BlockSpec((ROW_TILE, COL_TILE), lambda i, j: (i, j)),
            pl.BlockSpec((ROW_TILE, LANES), lambda i, j: (i, 0)),
        ],
        out_shape=[
            jax.ShapeDtypeStruct((m, D_QKV), BF16),
            jax.ShapeDtypeStruct((m, LANES), F32),
        ],
        scratch_shapes=[pltpu.VMEM((ROW_TILE, D_MODEL), BF16)],
        compiler_params=_params("parallel", "arbitrary"),
        name="proj",
    )(x, gain.reshape(1, D_MODEL), w_qkv, w_f, head_gain)


def _gate_kernel(f_ref, b_ref, c_ref):
    z = f_ref[0] + b_ref[...]
    c = jnp.minimum(z, 0.0) - jnp.log1p(jnp.exp(-jnp.abs(z)))
    row = lax.broadcasted_iota(jnp.int32, c.shape, 0)
    shift = 1
    while shift < c.shape[0]:
        c = c + jnp.where(row >= shift, pltpu.roll(c, shift, axis=0), 0.0)
        shift *= 2
    c_ref[0] = c


def _gate(f_pre, bias):
    b, s, _ = f_pre.shape
    return pl.pallas_call(
        _gate_kernel,
        grid=(b,),
        in_specs=[
            pl.BlockSpec((1, s, LANES), lambda i: (i, 0, 0)),
            pl.BlockSpec((1, LANES), lambda i: (0, 0)),
        ],
        out_specs=pl.BlockSpec((1, s, LANES), lambda i: (i, 0, 0)),
        out_shape=jax.ShapeDtypeStruct((b, s, LANES), F32),
        compiler_params=_params("parallel"),
        name="gate",
    )(f_pre, bias)


def _bucket_steps(dilation):
    dist = np.arange(BAND + 1) * dilation
    max_exact = NUM_BUCKETS // 2
    large = max_exact + np.floor(
        np.log(np.maximum(dist, 1) / max_exact) / math.log(MAX_DISTANCE / max_exact)
        * (NUM_BUCKETS - max_exact)).astype(np.int64)
    bucket = np.where(dist < max_exact, dist, np.minimum(large, NUM_BUCKETS - 1))
    steps = [(0, int(bucket[0]))]
    for delta in range(1, BAND + 1):
        if bucket[delta] != bucket[delta - 1]:
            steps.append((delta, int(bucket[delta])))
    return steps


def _bias_kernel(table_ref, o_ref):
    iq = lax.broadcasted_iota(jnp.int32, (BAND, 2 * BAND), 0)
    ik = lax.broadcasted_iota(jnp.int32, (BAND, 2 * BAND), 1)
    delta = iq + BAND - ik
    in_band = jnp.logical_and(delta >= 0, delta <= BAND)
    for p, (_, dilation) in enumerate(DILATED_PATTERNS):
        steps = _bucket_steps(dilation)
        for h in range(H_A):
            val = jnp.full((BAND, 2 * BAND), table_ref[steps[0][1], h], F32)
            for start, bucket in steps[1:]:
                val = jnp.where(delta >= start, table_ref[bucket, h], val)
            o_ref[p, h] = jnp.where(in_band, val, NEG_INF)


def _band_bias(rel_table):
    n_pat = len(DILATED_PATTERNS)
    return pl.pallas_call(
        _bias_kernel,
        in_specs=[pl.BlockSpec(memory_space=pltpu.SMEM)],
        out_specs=pl.BlockSpec(memory_space=pltpu.VMEM),
        out_shape=jax.ShapeDtypeStruct((n_pat, H_A, BAND, 2 * BAND), F32),
        name="band_bias",
    )(rel_table)


def _dilated_kernel(q_ref, kp_ref, kc_ref, vp_ref, vc_ref, bias_ref, o_ref, st_ref):
    first = pl.program_id(2) == 0
    key_lane = lax.broadcasted_iota(jnp.int32, (1, 2 * BAND), 1)
    no_prev = jnp.where(jnp.logical_and(first, key_lane < BAND), NEG_INF, 0.0)
    stat_lane = lax.broadcasted_iota(jnp.int32, (BAND, LANES), 1)
    stats = jnp.zeros((BAND, LANES), F32)
    for h in range(H_A):
        sl = slice(h * HEAD_DIM, (h + 1) * HEAD_DIM)
        k = jnp.concatenate([kp_ref[0, :, sl], kc_ref[0, :, sl]], axis=0)
        v = jnp.concatenate([vp_ref[0, :, sl], vc_ref[0, :, sl]], axis=0)
        s = lax.dot_general(q_ref[0, :, sl], k, (((1,), (1,)), ((), ())),
                            preferred_element_type=F32)
        s = s + bias_ref[0, h] + no_prev
        mx = jnp.max(s, axis=-1, keepdims=True)
        p = jnp.exp(s - mx)
        den = jnp.sum(p, axis=-1, keepdims=True)
        o_ref[0, :, sl] = jnp.dot(p.astype(BF16), v, preferred_element_type=F32)
        stats = jnp.where(stat_lane == h, mx, stats)
        stats = jnp.where(stat_lane == H_A + h, den, stats)
    st_ref[0] = stats


def _dilated(qkv, bias, pattern):
    b, s, _ = qkv.shape
    _, dilation = DILATED_PATTERNS[pattern]
    n_sub = s // dilation
    nb = n_sub // BAND
    qkv_v = qkv.reshape(b, n_sub, dilation * D_QKV)
    w_blocks = D_QKV // D_A

    def spec(which, prev):
        def index(bi, r, i):
            blk = jnp.maximum(i - 1, 0) if prev else i
            return (bi, blk, r * w_blocks + which)
        return pl.BlockSpec((1, BAND, D_A), index)

    num, stats = pl.pallas_call(
        _dilated_kernel,
        grid=(b, dilation, nb),
        in_specs=[
            spec(0, False), spec(1, True), spec(1, False), spec(2, True), spec(2, False),
            pl.BlockSpec((1, H_A, BAND, 2 * BAND), lambda bi, r, i: (pattern, 0, 0, 0)),
        ],
        out_specs=[
            pl.BlockSpec((1, BAND, D_A), lambda bi, r, i: (bi, i, r)),
            pl.BlockSpec((1, BAND, LANES), lambda bi, r, i: (bi, i, r)),
        ],
        out_shape=[
            jax.ShapeDtypeStruct((b, n_sub, dilation * D_A), F32),
            jax.ShapeDtypeStruct((b, n_sub, dilation * LANES), F32),
        ],
        compiler_params=_params("parallel", "parallel", "arbitrary"),
        name=f"dilated_{dilation}",
    )(qkv_v, qkv_v, qkv_v, qkv_v, qkv_v, bias)
    return num.reshape(b * s, D_A), stats.reshape(b * s, LANES)


def _combine_kernel(*refs):
    n_pat = len(DILATED_PATTERNS)
    num_refs, st_refs, o_ref = refs[:n_pat], refs[n_pat:2 * n_pat], refs[2 * n_pat]
    stats = [r[...] for r in st_refs]
    for h in range(H_A):
        sl = slice(h * HEAD_DIM, (h + 1) * HEAD_DIM)
        mxs = [st[:, h:h + 1] for st in stats]
        dens = [st[:, H_A + h:H_A + h + 1] for st in stats]
        mx = functools.reduce(jnp.maximum, mxs)
        num = None
        den = None
        for pnum, pden, pmx in zip(num_refs, dens, mxs):
            scale = jnp.exp(pmx - mx)
            num = pnum[:, sl] * scale if num is None else num + pnum[:, sl] * scale
            den = pden * scale if den is None else den + pden * scale
        o_ref[:, sl] = (num / den).astype(BF16)


def _combine(nums, stats):
    m = nums[0].shape[0]
    rows = 512
    return pl.pallas_call(
        _combine_kernel,
        grid=(m // rows,),
        in_specs=[pl.BlockSpec((rows, D_A), lambda i: (i, 0)) for _ in nums]
        + [pl.BlockSpec((rows, LANES), lambda i: (i, 0)) for _ in stats],
        out_specs=pl.BlockSpec((rows, D_A), lambda i: (i, 0)),
        out_shape=jax.ShapeDtypeStruct((m, D_A), BF16),
        compiler_params=_params("parallel"),
        name="combine_a",
    )(*nums, *stats)


def _fox_kernel(q_ref, k_ref, v_ref, cq_ref, ck_ref, o_ref, m_ref, l_ref, acc_ref):
    iq = pl.program_id(2)
    q = q_ref[0]
    cq = cq_ref[0]
    m_ref[...] = jnp.full(m_ref.shape, NEG_INF, F32)
    l_ref[...] = jnp.zeros(l_ref.shape, F32)
    acc_ref[...] = jnp.zeros(acc_ref.shape, F32)

    def step(kb, diagonal):
        start = pl.multiple_of(kb * FOX_TK, FOX_TK)
        k = k_ref[0, pl.ds(start, FOX_TK), :]
        v = v_ref[0, pl.ds(start, FOX_TK), :]
        s = lax.dot_general(q, k, (((1,), (1,)), ((), ())), preferred_element_type=F32)
        s = s + cq - ck_ref[0, :, pl.ds(start, FOX_TK)]
        if diagonal:
            row = lax.broadcasted_iota(jnp.int32, s.shape, 0)
            col = lax.broadcasted_iota(jnp.int32, s.shape, 1)
            s = jnp.where(col <= row, s, NEG_INF)
        m_prev = m_ref[...]
        m_new = jnp.maximum(m_prev, jnp.max(s, axis=-1, keepdims=True))
        alpha = jnp.exp(m_prev - m_new)
        p = jnp.exp(s - m_new)
        l_ref[...] = alpha * l_ref[...] + jnp.sum(p, axis=-1, keepdims=True)
        acc_ref[...] = alpha * acc_ref[...] + jnp.dot(p.astype(BF16), v,
                                                      preferred_element_type=F32)
        m_ref[...] = m_new

    def body(kb, carry):
        step(kb, False)
        return carry

    lax.fori_loop(0, iq, body, 0)
    step(iq, True)
    o_ref[0] = (acc_ref[...] / l_ref[...]).astype(BF16)


def _fox(qkv, c_col, c_row):
    b, s, _ = qkv.shape
    qb, kb, vb = [(3 * D_A + t * D_B) // HEAD_DIM for t in range(3)]
    return pl.pallas_call(
        _fox_kernel,
        grid=(b, H_B, s // FOX_TQ),
        in_specs=[
            pl.BlockSpec((1, FOX_TQ, HEAD_DIM), lambda bi, h, i: (bi, i, qb + h)),
            pl.BlockSpec((1, s, HEAD_DIM), lambda bi, h, i: (bi, 0, kb + h)),
            pl.BlockSpec((1, s, HEAD_DIM), lambda bi, h, i: (bi, 0, vb + h)),
            pl.BlockSpec((1, FOX_TQ, 1), lambda bi, h, i: (bi * H_B + h, i, 0)),
            pl.BlockSpec((1, 1, s), lambda bi, h, i: (bi * H_B + h, 0, 0)),
        ],
        out_specs=pl.BlockSpec((1, FOX_TQ, HEAD_DIM), lambda bi, h, i: (bi, i, h)),
        out_shape=jax.ShapeDtypeStruct((b, s, D_B), BF16),
        scratch_shapes=[
            pltpu.VMEM((FOX_TQ, 1), F32),
            pltpu.VMEM((FOX_TQ, 1), F32),
            pltpu.VMEM((FOX_TQ, HEAD_DIM), F32),
        ],
        compiler_params=_params("parallel", "parallel", "arbitrary"),
        name="fox",
    )(qkv, qkv, qkv, c_col, c_row)


def _out_proj_kernel(a_ref, b_ref, w_ref, x_ref, o_ref):
    mixed = jnp.concatenate([a_ref[...], b_ref[...]], axis=-1)
    o_ref[...] = x_ref[...] + jnp.dot(mixed, w_ref[...].astype(BF16),
                                      preferred_element_type=F32)


def _out_proj(out_a, out_b, w_out, x):
    m = x.shape[0]
    return pl.pallas_call(
        _out_proj_kernel,
        grid=(m // ROW_TILE, D_MODEL // COL_TILE),
        in_specs=[
            pl.BlockSpec((ROW_TILE, D_A), lambda i, j: (i, 0)),
            pl.BlockSpec((ROW_TILE, D_B), lambda i, j: (i, 0)),
            pl.BlockSpec((D_A + D_B, COL_TILE), lambda i, j: (0, j)),
            pl.BlockSpec((ROW_TILE, COL_TILE), lambda i, j: (i, j)),
        ],
        out_specs=pl.BlockSpec((ROW_TILE, COL_TILE), lambda i, j: (i, j)),
        out_shape=jax.ShapeDtypeStruct((m, D_MODEL), F32),
        compiler_params=_params("parallel", "parallel"),
        name="out_proj",
    )(out_a, out_b, w_out, x)


def _mixer(x, b, s, mix_norm, w_in, q_norm_a, k_norm_a, q_norm_b, k_norm_b, forget_bias,
           rel_bias_table, w_out):
    ones = jnp.ones((D_A,), F32)
    head_gain = jnp.concatenate([
        jnp.tile(q_norm_a * ATTN_SCALE, H_A), jnp.tile(k_norm_a, H_A), ones,
        jnp.tile(q_norm_b * ATTN_SCALE, H_B), jnp.tile(k_norm_b, H_B), ones,
    ]).reshape(1, D_QKV)
    w_f = jnp.pad(w_in[:, D_QKV:], ((0, 0), (0, LANES - H_B))).astype(BF16)
    qkv, f_pre = _proj(x, mix_norm, w_in, w_f, head_gain)

    f_bias = jnp.pad(forget_bias, (0, LANES - H_B)).reshape(1, LANES)
    c = _gate(f_pre.reshape(b, s, LANES), f_bias)[:, :, :H_B]
    c_heads = c.transpose(0, 2, 1).reshape(b * H_B, s)
    qkv3 = qkv.reshape(b, s, D_QKV)
    out_b = _fox(qkv3, c_heads[:, :, None], c_heads[:, None, :]).reshape(b * s, D_B)

    bias = _band_bias(rel_bias_table)
    parts = [_dilated(qkv3, bias, p) for p in range(len(DILATED_PATTERNS))]
    out_a = _combine([pt[0] for pt in parts], [pt[1] for pt in parts])

    return _out_proj(out_a, out_b, w_out, x)


def kernel(x, ffn1_norm, ffn1_w_in, ffn1_w_out, mix_norm, w_in, q_norm_a, k_norm_a, q_norm_b,
           k_norm_b, forget_bias, rel_bias_table, w_out, ffn2_norm, ffn2_w_in, ffn2_w_out):
    b, s, d = x.shape
    depth = ffn1_norm.shape[0]
    x = x.reshape(b * s, d)
    for l in range(depth):
        x = _ffn(x, ffn1_norm[l], ffn1_w_in[l], ffn1_w_out[l])
        x = _mixer(x, b, s, mix_norm[l], w_in[l], q_norm_a[l], k_norm_a[l], q_norm_b[l],
                   k_norm_b[l], forget_bias[l], rel_bias_table, w_out[l])
        x = _ffn(x, ffn2_norm[l], ffn2_w_in[l], ffn2_w_out[l])
    return x.reshape(b, s, d)
```

```python
import functools
import math

import numpy as np
import jax
import jax.numpy as jnp
from jax import lax
from jax.experimental import pallas as pl
from jax.experimental.pallas import tpu as pltpu

D_MODEL = 2048
HEAD_DIM = 128
N_HEADS = D_MODEL // HEAD_DIM
H_A = N_HEADS // 2
H_B = N_HEADS - H_A
D_A = H_A * HEAD_DIM
D_B = H_B * HEAD_DIM
D_QKV = 3 * D_A + 3 * D_B
DILATED_PATTERNS = ((128, 1), (512, 4), (2048, 16))
BAND = 128
NUM_BUCKETS = 32
MAX_DISTANCE = 2048
D_FF = ((8 * D_MODEL // 3 + 127) // 128) * 128
RMS_EPS = 1e-6
NEG_INF = -1e30
ATTN_SCALE = HEAD_DIM ** -0.5

LANES = 128
FF_TILE = 512
FF_TAIL = D_FF - (D_FF // FF_TILE) * FF_TILE
FFN_OUT_TILE = 256
ROW_TILE = 1024
COL_TILE = 512
FOX_TQ = 512
FOX_TK = 512
FOX_HEADS = 2
VMEM_LIMIT = 56 * 1024 * 1024

F32 = jnp.float32
BF16 = jnp.bfloat16


def _params(*sem):
    return pltpu.CompilerParams(dimension_semantics=sem, vmem_limit_bytes=VMEM_LIMIT)


def _rms_rows(x, gain):
    ms = jnp.mean(x * x, axis=-1, keepdims=True)
    return x * lax.rsqrt(ms + RMS_EPS) * gain


def _ffn_in_kernel(x_ref, g_ref, wg_ref, wu_ref, act_ref, h_ref):
    @pl.when(pl.program_id(1) == 0)
    def _():
        h_ref[...] = _rms_rows(x_ref[...], g_ref[...]).astype(BF16)

    h = h_ref[...]
    gate = jnp.dot(h, wg_ref[...].astype(BF16), preferred_element_type=F32)
    up = jnp.dot(h, wu_ref[...].astype(BF16), preferred_element_type=F32)
    act = (gate * (1.0 / (1.0 + jnp.exp(-gate))) * up).astype(BF16)

    j = pl.program_id(1)
    last = pl.num_programs(1) - 1

    @pl.when(j < last)
    def _():
        act_ref[...] = act

    @pl.when(j == last)
    def _():
        act_ref[:, :FF_TAIL] = act[:, FF_TILE - FF_TAIL:]


def _ffn_out_kernel(act_ref, w_ref, x_ref, o_ref):
    y = jnp.dot(act_ref[...], w_ref[...].astype(BF16), preferred_element_type=F32)
    o_ref[...] = x_ref[...] + 0.5 * y


def _ffn(x, gain, w_in, w_out):
    m = x.shape[0]
    n_ff = pl.cdiv(D_FF, FF_TILE)

    def ff_start(j):
        return pl.multiple_of(jnp.minimum(j * FF_TILE, D_FF - FF_TILE), LANES)

    act = pl.pallas_call(
        _ffn_in_kernel,
        grid=(m // ROW_TILE, n_ff),
        in_specs=[
            pl.BlockSpec((ROW_TILE, D_MODEL), lambda i, j: (i, 0)),
            pl.BlockSpec((1, D_MODEL), lambda i, j: (0, 0)),
            pl.BlockSpec((pl.Element(D_MODEL), pl.Element(FF_TILE)),
                         lambda i, j: (0, ff_start(j))),
            pl.BlockSpec((pl.Element(D_MODEL), pl.Element(FF_TILE)),
                         lambda i, j: (0, pl.multiple_of(D_FF + ff_start(j), LANES))),
        ],
        out_specs=pl.BlockSpec((ROW_TILE, FF_TILE), lambda i, j: (i, j)),
        out_shape=jax.ShapeDtypeStruct((m, D_FF), BF16),
        scratch_shapes=[pltpu.VMEM((ROW_TILE, D_MODEL), BF16)],
        compiler_params=_params("parallel", "arbitrary"),
        name="ffn_in",
    )(x, gain.reshape(1, D_MODEL), w_in, w_in)

    return pl.pallas_call(
        _ffn_out_kernel,
        grid=(m // ROW_TILE, D_MODEL // FFN_OUT_TILE),
        in_specs=[
            pl.BlockSpec((ROW_TILE, D_FF), lambda i, j: (i, 0)),
            pl.BlockSpec((D_FF, FFN_OUT_TILE), lambda i, j: (0, j)),
            pl.BlockSpec((ROW_TILE, FFN_OUT_TILE), lambda i, j: (i, j)),
        ],
        out_specs=pl.BlockSpec((ROW_TILE, FFN_OUT_TILE), lambda i, j: (i, j)),
        out_shape=jax.ShapeDtypeStruct((m, D_MODEL), F32),
        compiler_params=_params("parallel", "parallel"),
        name="ffn_out",
    )(act, w_out, x)


HEADS_PER_TILE = COL_TILE // HEAD_DIM
N_PROJ_TILES = D_QKV // COL_TILE
TILES_PER_GROUP = D_A // COL_TILE
A_TILES = 3 * TILES_PER_GROUP
AUG = 2 * HEAD_DIM
LOG2E = math.log2(math.e)


def _log_gate_scan(z, carry):
    c = (jnp.minimum(z, 0.0) - jnp.log1p(jnp.exp(-jnp.abs(z)))) * LOG2E
    row = lax.broadcasted_iota(jnp.int32, c.shape, 0)
    shift = 1
    while shift < c.shape[0]:
        c = c + jnp.where(row >= shift, pltpu.roll(c, shift, axis=0), 0.0)
        shift *= 2
    return c + carry


def _split3(c):
    hi = c.astype(BF16).astype(F32)
    mid = (c - hi).astype(BF16).astype(F32)
    lo = (c - hi - mid).astype(BF16).astype(F32)
    return hi, mid, lo


def _proj_kernel(x_ref, g_ref, w_ref, wf_ref, fb_ref, hg_ref,
                 a_ref, qb_ref, kb_ref, vt_ref, h_ref, c_ref, carry_ref, *, tiles_per_seq):
    i = pl.program_id(0)
    j = pl.program_id(1)

    @pl.when(j == 0)
    def _():
        h_ref[...] = _rms_rows(x_ref[...], g_ref[...]).astype(BF16)

        @pl.when(i % tiles_per_seq == 0)
        def _():
            carry_ref[...] = jnp.zeros(carry_ref.shape, F32)

        z = jnp.dot(h_ref[...], wf_ref[...], preferred_element_type=F32) + fb_ref[...]
        c = _log_gate_scan(z, carry_ref[...])
        c_ref[...] = c
        carry_ref[...] = c[ROW_TILE - 1:ROW_TILE, :]

    r = jnp.dot(h_ref[...], w_ref[...].astype(BF16), preferred_element_type=F32)

    def normed(hh):
        sl = slice(hh * HEAD_DIM, (hh + 1) * HEAD_DIM)
        return _rms_rows(r[:, sl], hg_ref[:, sl]).astype(BF16)

    for t in range(N_PROJ_TILES):
        group, part = divmod(t, TILES_PER_GROUP)

        @pl.when(j == t)
        def _(group=group, part=part):
            if group in (0, 1):
                for hh in range(HEADS_PER_TILE):
                    a_ref[:, hh * HEAD_DIM:(hh + 1) * HEAD_DIM] = normed(hh)
            elif group == 2:
                a_ref[...] = r.astype(BF16)
            elif group in (3, 4):
                out = qb_ref if group == 3 else kb_ref
                lane = lax.broadcasted_iota(jnp.int32, (ROW_TILE, HEAD_DIM), 1)
                for hh in range(HEADS_PER_TILE):
                    head = part * HEADS_PER_TILE + hh
                    hi, mid, lo = _split3(c_ref[:, head:head + 1])
                    if group == 3:
                        aug = jnp.where(lane == 0, hi, jnp.where(lane == 1, mid, jnp.where(
                            lane == 2, lo, jnp.where(lane < 6, 1.0, 0.0))))
                    else:
                        aug = jnp.where(lane == 3, -hi, jnp.where(lane == 4, -mid, jnp.where(
                            lane == 5, -lo, jnp.where(lane < 3, 1.0, 0.0))))
                    out[0, hh, :, :HEAD_DIM] = normed(hh)
                    out[0, hh, :, HEAD_DIM:] = aug.astype(BF16)
            else:
                for hh in range(HEADS_PER_TILE):
                    vt_ref[0, hh] = r[:, hh * HEAD_DIM:(hh + 1) * HEAD_DIM].T.astype(BF16)


def _proj(x, b, s, gain, w_in, w_f, f_bias, head_gain):
    m = x.shape[0]
    tps = s // ROW_TILE

    def head_block(first):
        return lambda i, j: (i // tps, jnp.clip(j - first, 0, TILES_PER_GROUP - 1), i % tps, 0)

    vt_first = A_TILES + 2 * TILES_PER_GROUP
    return pl.pallas_call(
        functools.partial(_proj_kernel, tiles_per_seq=tps),
        grid=(m // ROW_TILE, N_PROJ_TILES),
        in_specs=[
            pl.BlockSpec((ROW_TILE, D_MODEL), lambda i, j: (i, 0)),
            pl.BlockSpec((1, D_MODEL), lambda i, j: (0, 0)),
            pl.BlockSpec((D_MODEL, COL_TILE), lambda i, j: (0, j)),
            pl.BlockSpec((D_MODEL, LANES), lambda i, j: (0, 0)),
            pl.BlockSpec((1, LANES), lambda i, j: (0, 0)),
            pl.BlockSpec((1, COL_TILE), lambda i, j: (0, j)),
        ],
        out_specs=[
            pl.BlockSpec((ROW_TILE, COL_TILE), lambda i, j: (i, jnp.minimum(j, A_TILES - 1))),
            pl.BlockSpec((1, HEADS_PER_TILE, ROW_TILE, AUG), head_block(A_TILES)),
            pl.BlockSpec((1, HEADS_PER_TILE, ROW_TILE, AUG),
                         head_block(A_TILES + TILES_PER_GROUP)),
            pl.BlockSpec((1, HEADS_PER_TILE, HEAD_DIM, ROW_TILE),
                         lambda i, j: (i // tps, jnp.clip(j - vt_first, 0, TILES_PER_GROUP - 1),
                                       0, i % tps)),
        ],
        out_shape=[
            jax.ShapeDtypeStruct((m, 3 * D_A), BF16),
            jax.ShapeDtypeStruct((b, H_B, s, AUG), BF16),
            jax.ShapeDtypeStruct((b, H_B, s, AUG), BF16),
            jax.ShapeDtypeStruct((b, H_B, HEAD_DIM, s), BF16),
        ],
        scratch_shapes=[
            pltpu.VMEM((ROW_TILE, D_MODEL), BF16),
            pltpu.VMEM((ROW_TILE, LANES), F32),
            pltpu.VMEM((1, LANES), F32),
        ],
        compiler_params=_params("arbitrary", "arbitrary"),
        name="proj",
    )(x, gain.reshape(1, D_MODEL), w_in, w_f, f_bias, head_gain)


def _bucket_steps(dilation):
    dist = np.arange(BAND + 1) * dilation
    max_exact = NUM_BUCKETS // 2
    large = max_exact + np.floor(
        np.log(np.maximum(dist, 1) / max_exact) / math.log(MAX_DISTANCE / max_exact)
        * (NUM_BUCKETS - max_exact)).astype(np.int64)
    bucket = np.where(dist < max_exact, dist, np.minimum(large, NUM_BUCKETS - 1))
    steps = [(0, int(bucket[0]))]
    for delta in range(1, BAND + 1):
        if bucket[delta] != bucket[delta - 1]:
            steps.append((delta, int(bucket[delta])))
    return steps


def _bias_kernel(table_ref, o_ref):
    iq = lax.broadcasted_iota(jnp.int32, (BAND, 2 * BAND), 0)
    ik = lax.broadcasted_iota(jnp.int32, (BAND, 2 * BAND), 1)
    delta = iq + BAND - ik
    in_band = jnp.logical_and(delta >= 0, delta <= BAND)
    for p, (_, dilation) in enumerate(DILATED_PATTERNS):
        steps = _bucket_steps(dilation)
        for h in range(H_A):
            val = jnp.full((BAND, 2 * BAND), table_ref[steps[0][1], h], F32)
            for start, bucket in steps[1:]:
                val = jnp.where(delta >= start, table_ref[bucket, h], val)
            o_ref[p, h] = jnp.where(in_band, val, NEG_INF)


def _band_bias(rel_table):
    n_pat = len(DILATED_PATTERNS)
    return pl.pallas_call(
        _bias_kernel,
        in_specs=[pl.BlockSpec(memory_space=pltpu.SMEM)],
        out_specs=pl.BlockSpec(memory_space=pltpu.VMEM),
        out_shape=jax.ShapeDtypeStruct((n_pat, H_A, BAND, 2 * BAND), F32),
        name="band_bias",
    )(rel_table)


def _dilated_kernel(q_ref, kp_ref, kc_ref, vp_ref, vc_ref, bias_ref, o_ref, st_ref):
    first = pl.program_id(2) == 0
    key_lane = lax.broadcasted_iota(jnp.int32, (1, 2 * BAND), 1)
    no_prev = jnp.where(jnp.logical_and(first, key_lane < BAND), NEG_INF, 0.0)
    stat_lane = lax.broadcasted_iota(jnp.int32, (BAND, LANES), 1)
    stats = jnp.zeros((BAND, LANES), F32)
    for h in range(H_A):
        sl = slice(h * HEAD_DIM, (h + 1) * HEAD_DIM)
        k = jnp.concatenate([kp_ref[0, :, sl], kc_ref[0, :, sl]], axis=0)
        v = jnp.concatenate([vp_ref[0, :, sl], vc_ref[0, :, sl]], axis=0)
        s = lax.dot_general(q_ref[0, :, sl], k, (((1,), (1,)), ((), ())),
                            preferred_element_type=F32)
        s = s + bias_ref[0, h] + no_prev
        mx = jnp.max(s, axis=-1, keepdims=True)
        p = jnp.exp(s - mx)
        den = jnp.sum(p, axis=-1, keepdims=True)
        o_ref[0, :, sl] = jnp.dot(p.astype(BF16), v, preferred_element_type=F32)
        stats = jnp.where(stat_lane == h, mx, stats)
        stats = jnp.where(stat_lane == H_A + h, den, stats)
    st_ref[0] = stats


def _dilated(qkv, bias, pattern):
    b, s, width = qkv.shape
    _, dilation = DILATED_PATTERNS[pattern]
    n_sub = s // dilation
    nb = n_sub // BAND
    qkv_v = qkv.reshape(b, n_sub, dilation * width)
    w_blocks = width // D_A

    def spec(which, prev):
        def index(bi, r, i):
            blk = jnp.maximum(i - 1, 0) if prev else i
            return (bi, blk, r * w_blocks + which)
        return pl.BlockSpec((1, BAND, D_A), index)

    num, stats = pl.pallas_call(
        _dilated_kernel,
        grid=(b, dilation, nb),
        in_specs=[
            spec(0, False), spec(1, True), spec(1, False), spec(2, True), spec(2, False),
            pl.BlockSpec((1, H_A, BAND, 2 * BAND), lambda bi, r, i: (pattern, 0, 0, 0)),
        ],
        out_specs=[
            pl.BlockSpec((1, BAND, D_A), lambda bi, r, i: (bi, i, r)),
            pl.BlockSpec((1, BAND, LANES), lambda bi, r, i: (bi, i, r)),
        ],
        out_shape=[
            jax.ShapeDtypeStruct((b, n_sub, dilation * D_A), F32),
            jax.ShapeDtypeStruct((b, n_sub, dilation * LANES), F32),
        ],
        compiler_params=_params("parallel", "parallel", "arbitrary"),
        name=f"dilated_{dilation}",
    )(qkv_v, qkv_v, qkv_v, qkv_v, qkv_v, bias)
    return num.reshape(b * s, D_A), stats.reshape(b * s, LANES)


def _combine_kernel(*refs):
    n_pat = len(DILATED_PATTERNS)
    num_refs, st_refs, o_ref = refs[:n_pat], refs[n_pat:2 * n_pat], refs[2 * n_pat]
    stats = [r[...] for r in st_refs]
    for h in range(H_A):
        sl = slice(h * HEAD_DIM, (h + 1) * HEAD_DIM)
        mxs = [st[:, h:h + 1] for st in stats]
        dens = [st[:, H_A + h:H_A + h + 1] for st in stats]
        mx = functools.reduce(jnp.maximum, mxs)
        num = None
        den = None
        for pnum, pden, pmx in zip(num_refs, dens, mxs):
            scale = jnp.exp(pmx - mx)
            num = pnum[:, sl] * scale if num is None else num + pnum[:, sl] * scale
            den = pden * scale if den is None else den + pden * scale
        o_ref[:, sl] = (num / den).astype(BF16)


def _combine(nums, stats):
    m = nums[0].shape[0]
    rows = 512
    return pl.pallas_call(
        _combine_kernel,
        grid=(m // rows,),
        in_specs=[pl.BlockSpec((rows, D_A), lambda i: (i, 0)) for _ in nums]
        + [pl.BlockSpec((rows, LANES), lambda i: (i, 0)) for _ in stats],
        out_specs=pl.BlockSpec((rows, D_A), lambda i: (i, 0)),
        out_shape=jax.ShapeDtypeStruct((m, D_A), BF16),
        compiler_params=_params("parallel"),
        name="combine_a",
    )(*nums, *stats)


def _fox_kernel(q_ref, k_ref, vt_ref, o_ref, m_ref, l_ref, acc_ref):
    iq = pl.program_id(2)
    m_ref[...] = jnp.full(m_ref.shape, NEG_INF, F32)
    l_ref[...] = jnp.zeros(l_ref.shape, F32)
    acc_ref[...] = jnp.zeros(acc_ref.shape, F32)

    def step(kb, diagonal):
        start = pl.multiple_of(kb * FOX_TK, FOX_TK)
        for hh in range(FOX_HEADS):
            k = k_ref[0, hh, pl.ds(start, FOX_TK), :]
            st = lax.dot_general(k, q_ref[0, hh], (((1,), (1,)), ((), ())),
                                 preferred_element_type=F32)
            if diagonal:
                key = lax.broadcasted_iota(jnp.int32, st.shape, 0)
                qry = lax.broadcasted_iota(jnp.int32, st.shape, 1)
                st = jnp.where(key <= qry, st, NEG_INF)
            m_prev = m_ref[hh]
            m_new = jnp.maximum(m_prev, jnp.max(st, axis=0, keepdims=True))
            alpha = jnp.exp2(m_prev - m_new)
            p = jnp.exp2(st - m_new)
            l_ref[hh] = alpha * l_ref[hh] + jnp.sum(p, axis=0, keepdims=True)
            vt = vt_ref[0, hh, :, pl.ds(start, FOX_TK)]
            acc_ref[hh] = alpha * acc_ref[hh] + jnp.dot(vt, p.astype(BF16),
                                                        preferred_element_type=F32)
            m_ref[hh] = m_new

    def body(kb, carry):
        step(kb, False)
        return carry

    lax.fori_loop(0, iq, body, 0)
    step(iq, True)
    for hh in range(FOX_HEADS):
        o_ref[0, :, hh * HEAD_DIM:(hh + 1) * HEAD_DIM] = (
            acc_ref[hh] / l_ref[hh]).T.astype(BF16)


def _fox(q_aug, k_aug, v_t):
    b, h_b, s, _ = q_aug.shape
    assert FOX_TQ == FOX_TK
    return pl.pallas_call(
        _fox_kernel,
        grid=(b, h_b // FOX_HEADS, s // FOX_TQ),
        in_specs=[
            pl.BlockSpec((1, FOX_HEADS, FOX_TQ, AUG), lambda bi, h, i: (bi, h, i, 0)),
            pl.BlockSpec((1, FOX_HEADS, s, AUG), lambda bi, h, i: (bi, h, 0, 0)),
            pl.BlockSpec((1, FOX_HEADS, HEAD_DIM, s), lambda bi, h, i: (bi, h, 0, 0)),
        ],
        out_specs=pl.BlockSpec((1, FOX_TQ, FOX_HEADS * HEAD_DIM), lambda bi, h, i: (bi, i, h)),
        out_shape=jax.ShapeDtypeStruct((b, s, h_b * HEAD_DIM), BF16),
        scratch_shapes=[
            pltpu.VMEM((FOX_HEADS, 1, FOX_TQ), F32),
            pltpu.VMEM((FOX_HEADS, 1, FOX_TQ), F32),
            pltpu.VMEM((FOX_HEADS, HEAD_DIM, FOX_TQ), F32),
        ],
        compiler_params=_params("parallel", "parallel", "arbitrary"),
        name="fox",
    )(q_aug, k_aug, v_t)


def _out_proj_kernel(a_ref, b_ref, w_ref, x_ref, o_ref):
    mixed = jnp.concatenate([a_ref[...], b_ref[...]], axis=-1)
    o_ref[...] = x_ref[...] + jnp.dot(mixed, w_ref[...].astype(BF16),
                                      preferred_element_type=F32)


def _out_proj(out_a, out_b, w_out, x):
    m = x.shape[0]
    return pl.pallas_call(
        _out_proj_kernel,
        grid=(m // ROW_TILE, D_MODEL // COL_TILE),
        in_specs=[
            pl.BlockSpec((ROW_TILE, D_A), lambda i, j: (i, 0)),
            pl.BlockSpec((ROW_TILE, D_B), lambda i, j: (i, 0)),
            pl.BlockSpec((D_A + D_B, COL_TILE), lambda i, j: (0, j)),
            pl.BlockSpec((ROW_TILE, COL_TILE), lambda i, j: (i, j)),
        ],
        out_specs=pl.BlockSpec((ROW_TILE, COL_TILE), lambda i, j: (i, j)),
        out_shape=jax.ShapeDtypeStruct((m, D_MODEL), F32),
        compiler_params=_params("parallel", "parallel"),
        name="out_proj",
    )(out_a, out_b, w_out, x)


def _mixer(x, b, s, mix_norm, w_in, q_norm_a, k_norm_a, q_norm_b, k_norm_b, forget_bias,
           rel_bias_table, w_out):
    ones = jnp.ones((D_A,), F32)
    head_gain = jnp.concatenate([
        jnp.tile(q_norm_a * ATTN_SCALE, H_A), jnp.tile(k_norm_a, H_A), ones,
        jnp.tile(q_norm_b * (ATTN_SCALE * LOG2E), H_B), jnp.tile(k_norm_b, H_B), ones,
    ]).reshape(1, D_QKV)
    w_f = jnp.pad(w_in[:, D_QKV:], ((0, 0), (0, LANES - H_B))).astype(BF16)
    f_bias = jnp.pad(forget_bias, (0, LANES - H_B)).reshape(1, LANES)
    qkv_a, q_aug, k_aug, v_t = _proj(x, b, s, mix_norm, w_in, w_f, f_bias, head_gain)

    out_b = _fox(q_aug, k_aug, v_t).reshape(b * s, D_B)

    bias = _band_bias(rel_bias_table)
    qkv3 = qkv_a.reshape(b, s, 3 * D_A)
    parts = [_dilated(qkv3, bias, p) for p in range(len(DILATED_PATTERNS))]
    out_a = _combine([pt[0] for pt in parts], [pt[1] for pt in parts])

    return _out_proj(out_a, out_b, w_out, x)


def kernel(x, ffn1_norm, ffn1_w_in, ffn1_w_out, mix_norm, w_in, q_norm_a, k_norm_a, q_norm_b,
           k_norm_b, forget_bias, rel_bias_table, w_out, ffn2_norm, ffn2_w_in, ffn2_w_out):
    b, s, d = x.shape
    depth = ffn1_norm.shape[0]
    x = x.reshape(b * s, d)
    for l in range(depth):
        x = _ffn(x, ffn1_norm[l], ffn1_w_in[l], ffn1_w_out[l])
        x = _mixer(x, b, s, mix_norm[l], w_in[l], q_norm_a[l], k_norm_a[l], q_norm_b[l],
                   k_norm_b[l], forget_bias[l], rel_bias_table, w_out[l])
        x = _ffn(x, ffn2_norm[l], ffn2_w_in[l], ffn2_w_out[l])
    return x.reshape(b, s, d)
```

```python
import functools
import math

import numpy as np
import jax
import jax.numpy as jnp
from jax import lax
from jax.experimental import pallas as pl
from jax.experimental.pallas import tpu as pltpu

D_MODEL = 2048
HEAD_DIM = 128
N_HEADS = D_MODEL // HEAD_DIM
H_A = N_HEADS // 2
H_B = N_HEADS - H_A
D_A = H_A * HEAD_DIM
D_B = H_B * HEAD_DIM
D_QKV = 3 * D_A + 3 * D_B
DILATED_PATTERNS = ((128, 1), (512, 4), (2048, 16))
BAND = 128
NUM_BUCKETS = 32
MAX_DISTANCE = 2048
D_FF = ((8 * D_MODEL // 3 + 127) // 128) * 128
RMS_EPS = 1e-6
NEG_INF = -1e30
ATTN_SCALE = HEAD_DIM ** -0.5

LANES = 128
FF_TILE = 512
FF_TAIL = D_FF - (D_FF // FF_TILE) * FF_TILE
FFN_OUT_TILE = 256
ROW_TILE = 1024
COL_TILE = 512
FOX_TQ = 512
FOX_TK = 512
FOX_HEADS = 2
VMEM_LIMIT = 56 * 1024 * 1024

F32 = jnp.float32
BF16 = jnp.bfloat16


def _params(*sem):
    return pltpu.CompilerParams(dimension_semantics=sem, vmem_limit_bytes=VMEM_LIMIT)


def _rms_rows(x, gain):
    ms = jnp.mean(x * x, axis=-1, keepdims=True)
    return x * lax.rsqrt(ms + RMS_EPS) * gain


def _ffn_in_kernel(x_ref, g_ref, wg_ref, wu_ref, act_ref, h_ref):
    @pl.when(pl.program_id(1) == 0)
    def _():
        h_ref[...] = _rms_rows(x_ref[...], g_ref[...]).astype(BF16)

    h = h_ref[...]
    gate = jnp.dot(h, wg_ref[...].astype(BF16), preferred_element_type=F32)
    up = jnp.dot(h, wu_ref[...].astype(BF16), preferred_element_type=F32)
    act = (gate * (1.0 / (1.0 + jnp.exp(-gate))) * up).astype(BF16)

    j = pl.program_id(1)
    last = pl.num_programs(1) - 1

    @pl.when(j < last)
    def _():
        act_ref[...] = act

    @pl.when(j == last)
    def _():
        act_ref[:, :FF_TAIL] = act[:, FF_TILE - FF_TAIL:]


def _ffn_out_kernel(act_ref, w_ref, x_ref, o_ref):
    y = jnp.dot(act_ref[...], w_ref[...].astype(BF16), preferred_element_type=F32)
    o_ref[...] = x_ref[...] + 0.5 * y


def _ffn(x, gain, w_in, w_out):
    m = x.shape[0]
    n_ff = pl.cdiv(D_FF, FF_TILE)

    def ff_start(j):
        return pl.multiple_of(jnp.minimum(j * FF_TILE, D_FF - FF_TILE), LANES)

    act = pl.pallas_call(
        _ffn_in_kernel,
        grid=(m // ROW_TILE, n_ff),
        in_specs=[
            pl.BlockSpec((ROW_TILE, D_MODEL), lambda i, j: (i, 0)),
            pl.BlockSpec((1, D_MODEL), lambda i, j: (0, 0)),
            pl.BlockSpec((pl.Element(D_MODEL), pl.Element(FF_TILE)),
                         lambda i, j: (0, ff_start(j))),
            pl.BlockSpec((pl.Element(D_MODEL), pl.Element(FF_TILE)),
                         lambda i, j: (0, pl.multiple_of(D_FF + ff_start(j), LANES))),
        ],
        out_specs=pl.BlockSpec((ROW_TILE, FF_TILE), lambda i, j: (i, j)),
        out_shape=jax.ShapeDtypeStruct((m, D_FF), BF16),
        scratch_shapes=[pltpu.VMEM((ROW_TILE, D_MODEL), BF16)],
        compiler_params=_params("parallel", "arbitrary"),
        name="ffn_in",
    )(x, gain.reshape(1, D_MODEL), w_in, w_in)

    return pl.pallas_call(
        _ffn_out_kernel,
        grid=(m // ROW_TILE, D_MODEL // FFN_OUT_TILE),
        in_specs=[
            pl.BlockSpec((ROW_TILE, D_FF), lambda i, j: (i, 0)),
            pl.BlockSpec((D_FF, FFN_OUT_TILE), lambda i, j: (0, j)),
            pl.BlockSpec((ROW_TILE, FFN_OUT_TILE), lambda i, j: (i, j)),
        ],
        out_specs=pl.BlockSpec((ROW_TILE, FFN_OUT_TILE), lambda i, j: (i, j)),
        out_shape=jax.ShapeDtypeStruct((m, D_MODEL), F32),
        compiler_params=_params("parallel", "parallel"),
        name="ffn_out",
    )(act, w_out, x)


HEADS_PER_TILE = COL_TILE // HEAD_DIM
N_PROJ_TILES = D_QKV // COL_TILE
TILES_PER_GROUP = D_A // COL_TILE
A_TILES = 3 * TILES_PER_GROUP
AUG = 2 * HEAD_DIM
LOG2E = math.log2(math.e)


def _log_gate_scan(z, carry):
    c = (jnp.minimum(z, 0.0) - jnp.log1p(jnp.exp(-jnp.abs(z)))) * LOG2E
    row = lax.broadcasted_iota(jnp.int32, c.shape, 0)
    shift = 1
    while shift < c.shape[0]:
        c = c + jnp.where(row >= shift, pltpu.roll(c, shift, axis=0), 0.0)
        shift *= 2
    return c + carry


def _split3(c):
    hi = c.astype(BF16)
    rest = c - hi.astype(F32)
    mid = rest.astype(BF16)
    lo = (rest - mid.astype(F32)).astype(BF16)
    return hi, mid, lo


def _head_mean_matrix():
    head = np.arange(COL_TILE) // HEAD_DIM
    return jnp.asarray((head[:, None] == head[None, :]) / HEAD_DIM, BF16)


def _gate_placement_matrices():
    mats = np.zeros((2, TILES_PER_GROUP, COL_TILE, COL_TILE), np.float32)
    ones_row = 3 * LANES
    for part in range(TILES_PER_GROUP):
        for hh in range(HEADS_PER_TILE):
            head = part * HEADS_PER_TILE + hh
            col = hh * HEAD_DIM
            for term in range(3):
                mats[0, part, term * LANES + head, col + term] = 1.0
                mats[0, part, ones_row, col + 3 + term] = 1.0
                mats[1, part, term * LANES + head, col + 3 + term] = -1.0
                mats[1, part, ones_row, col + term] = 1.0
    return jnp.asarray(mats.reshape(2 * TILES_PER_GROUP, COL_TILE, COL_TILE), BF16)


def _proj_kernel(x_ref, g_ref, w_ref, wf_ref, fb_ref, hg_ref, mean_ref, place_ref,
                 a1_ref, a4_ref, a16_ref, qb_ref, kb_ref, vt_ref,
                 h_ref, cs_ref, carry_ref, y_ref, *, tiles_per_seq):
    i = pl.program_id(0)
    j = pl.program_id(1)

    @pl.when(j == 0)
    def _():
        h_ref[...] = _rms_rows(x_ref[...], g_ref[...]).astype(BF16)

        @pl.when(i % tiles_per_seq == 0)
        def _():
            carry_ref[...] = jnp.zeros(carry_ref.shape, F32)

        z = jnp.dot(h_ref[...], wf_ref[...], preferred_element_type=F32) + fb_ref[...]
        c = _log_gate_scan(z, carry_ref[...])
        carry_ref[...] = c[ROW_TILE - 1:ROW_TILE, :]
        for term, part in enumerate(_split3(c)):
            cs_ref[:, term * LANES:(term + 1) * LANES] = part
        cs_ref[:, 3 * LANES:] = jnp.ones((ROW_TILE, LANES), BF16)

    r = jnp.dot(h_ref[...], w_ref[...].astype(BF16), preferred_element_type=F32)

    def normed():
        ms = jnp.dot((r * r).astype(BF16), mean_ref[...], preferred_element_type=F32)
        return r * lax.rsqrt(ms + RMS_EPS) * hg_ref[...]

    def store_dilated(y):
        a1_ref[0, 0] = y.astype(BF16)
        for c in range(HEADS_PER_TILE):
            sl = slice(c * LANES, (c + 1) * LANES)
            y_ref[c] = y[:, sl]
            for ref in (a4_ref, a16_ref):
                dilation = ref.shape[1]
                for rc in range(dilation):
                    rows = y_ref[c, pl.ds(rc, ROW_TILE // dilation, stride=dilation), :]
                    ref[0, rc, :, sl] = rows.astype(BF16)

    for t in range(N_PROJ_TILES):
        group, part = divmod(t, TILES_PER_GROUP)

        @pl.when(j == t)
        def _(group=group):
            if group in (0, 1):
                store_dilated(normed())
            elif group == 2:
                store_dilated(r)
            elif group in (3, 4):
                out = qb_ref if group == 3 else kb_ref
                y = normed().astype(BF16)
                aug = jnp.dot(cs_ref[...], place_ref[0],
                              preferred_element_type=F32).astype(BF16)
                for hh in range(HEADS_PER_TILE):
                    sl = slice(hh * HEAD_DIM, (hh + 1) * HEAD_DIM)
                    out[0, hh, :, :HEAD_DIM] = y[:, sl]
                    out[0, hh, :, HEAD_DIM:] = aug[:, sl]
            else:
                for hh in range(HEADS_PER_TILE):
                    vt_ref[0, hh] = r[:, hh * HEAD_DIM:(hh + 1) * HEAD_DIM].T.astype(BF16)


def _proj(x, b, s, gain, w_in, w_f, f_bias, head_gain):
    m = x.shape[0]
    tps = s // ROW_TILE

    def a_spec(dilation):
        return pl.BlockSpec((1, dilation, ROW_TILE // dilation, COL_TILE),
                            lambda i, j: (i // tps, 0, i % tps, jnp.minimum(j, A_TILES - 1)))

    def head_block(first):
        return lambda i, j: (i // tps, jnp.clip(j - first, 0, TILES_PER_GROUP - 1), i % tps, 0)

    qb_first = A_TILES
    vt_first = A_TILES + 2 * TILES_PER_GROUP
    dilations = [d for _, d in DILATED_PATTERNS]
    return pl.pallas_call(
        functools.partial(_proj_kernel, tiles_per_seq=tps),
        grid=(m // ROW_TILE, N_PROJ_TILES),
        in_specs=[
            pl.BlockSpec((ROW_TILE, D_MODEL), lambda i, j: (i, 0)),
            pl.BlockSpec((1, D_MODEL), lambda i, j: (0, 0)),
            pl.BlockSpec((D_MODEL, COL_TILE), lambda i, j: (0, j)),
            pl.BlockSpec((D_MODEL, LANES), lambda i, j: (0, 0)),
            pl.BlockSpec((1, LANES), lambda i, j: (0, 0)),
            pl.BlockSpec((1, COL_TILE), lambda i, j: (0, j)),
            pl.BlockSpec((COL_TILE, COL_TILE), lambda i, j: (0, 0)),
            pl.BlockSpec((1, COL_TILE, COL_TILE),
                         lambda i, j: (jnp.clip(j - qb_first, 0, 2 * TILES_PER_GROUP - 1), 0, 0)),
        ],
        out_specs=[a_spec(d) for d in dilations] + [
            pl.BlockSpec((1, HEADS_PER_TILE, ROW_TILE, AUG), head_block(qb_first)),
            pl.BlockSpec((1, HEADS_PER_TILE, ROW_TILE, AUG),
                         head_block(qb_first + TILES_PER_GROUP)),
            pl.BlockSpec((1, HEADS_PER_TILE, HEAD_DIM, ROW_TILE),
                         lambda i, j: (i // tps, jnp.clip(j - vt_first, 0, TILES_PER_GROUP - 1),
                                       0, i % tps)),
        ],
        out_shape=[jax.ShapeDtypeStruct((b, d, s // d, 3 * D_A), BF16) for d in dilations] + [
            jax.ShapeDtypeStruct((b, H_B, s, AUG), BF16),
            jax.ShapeDtypeStruct((b, H_B, s, AUG), BF16),
            jax.ShapeDtypeStruct((b, H_B, HEAD_DIM, s), BF16),
        ],
        scratch_shapes=[
            pltpu.VMEM((ROW_TILE, D_MODEL), BF16),
            pltpu.VMEM((ROW_TILE, 4 * LANES), BF16),
            pltpu.VMEM((1, LANES), F32),
            pltpu.VMEM((HEADS_PER_TILE, ROW_TILE, LANES), F32),
        ],
        compiler_params=_params("arbitrary", "arbitrary"),
        name="proj",
    )(x, gain.reshape(1, D_MODEL), w_in, w_f, f_bias, head_gain,
      _head_mean_matrix(), _gate_placement_matrices())


def _bucket_steps(dilation):
    dist = np.arange(BAND + 1) * dilation
    max_exact = NUM_BUCKETS // 2
    large = max_exact + np.floor(
        np.log(np.maximum(dist, 1) / max_exact) / math.log(MAX_DISTANCE / max_exact)
        * (NUM_BUCKETS - max_exact)).astype(np.int64)
    bucket = np.where(dist < max_exact, dist, np.minimum(large, NUM_BUCKETS - 1))
    steps = [(0, int(bucket[0]))]
    for delta in range(1, BAND + 1):
        if bucket[delta] != bucket[delta - 1]:
            steps.append((delta, int(bucket[delta])))
    return steps


def _bias_kernel(table_ref, o_ref):
    iq = lax.broadcasted_iota(jnp.int32, (BAND, 2 * BAND), 0)
    ik = lax.broadcasted_iota(jnp.int32, (BAND, 2 * BAND), 1)
    delta = iq + BAND - ik
    in_band = jnp.logical_and(delta >= 0, delta <= BAND)
    for p, (_, dilation) in enumerate(DILATED_PATTERNS):
        steps = _bucket_steps(dilation)
        for h in range(H_A):
            val = jnp.full((BAND, 2 * BAND), table_ref[steps[0][1], h], F32)
            for start, bucket in steps[1:]:
                val = jnp.where(delta >= start, table_ref[bucket, h], val)
            o_ref[p, h] = jnp.where(in_band, val, NEG_INF)


def _band_bias(rel_table):
    n_pat = len(DILATED_PATTERNS)
    return pl.pallas_call(
        _bias_kernel,
        in_specs=[pl.BlockSpec(memory_space=pltpu.SMEM)],
        out_specs=pl.BlockSpec(memory_space=pltpu.VMEM),
        out_shape=jax.ShapeDtypeStruct((n_pat, H_A, BAND, 2 * BAND), F32),
        name="band_bias",
    )(rel_table)


def _dilated_kernel(q_ref, kp_ref, kc_ref, vp_ref, vc_ref, bias_ref, o_ref, st_ref):
    first = pl.program_id(2) == 0
    key_lane = lax.broadcasted_iota(jnp.int32, (1, 2 * BAND), 1)
    no_prev = jnp.where(jnp.logical_and(first, key_lane < BAND), NEG_INF, 0.0)
    stat_lane = lax.broadcasted_iota(jnp.int32, (BAND, LANES), 1)
    stats = jnp.zeros((BAND, LANES), F32)
    for h in range(H_A):
        sl = slice(h * HEAD_DIM, (h + 1) * HEAD_DIM)
        k = jnp.concatenate([kp_ref[0, 0, :, sl], kc_ref[0, 0, :, sl]], axis=0)
        v = jnp.concatenate([vp_ref[0, 0, :, sl], vc_ref[0, 0, :, sl]], axis=0)
        s = lax.dot_general(q_ref[0, 0, :, sl], k, (((1,), (1,)), ((), ())),
                            preferred_element_type=F32)
        s = s + bias_ref[0, h] + no_prev
        mx = jnp.max(s, axis=-1, keepdims=True)
        p = jnp.exp(s - mx)
        den = jnp.sum(p, axis=-1, keepdims=True)
        o_ref[0, 0, :, sl] = jnp.dot(p.astype(BF16), v, preferred_element_type=F32)
        stats = jnp.where(stat_lane == h, mx, stats)
        stats = jnp.where(stat_lane == H_A + h, den, stats)
    st_ref[0, 0] = stats


def _dilated(qkv, bias, pattern):
    b, dilation, n_sub, _ = qkv.shape
    nb = n_sub // BAND

    def spec(which, prev):
        def index(bi, r, i):
            return (bi, r, jnp.maximum(i - 1, 0) if prev else i, which)
        return pl.BlockSpec((1, 1, BAND, D_A), index)

    return pl.pallas_call(
        _dilated_kernel,
        grid=(b, dilation, nb),
        in_specs=[
            spec(0, False), spec(1, True), spec(1, False), spec(2, True), spec(2, False),
            pl.BlockSpec((1, H_A, BAND, 2 * BAND), lambda bi, r, i: (pattern, 0, 0, 0)),
        ],
        out_specs=[
            pl.BlockSpec((1, 1, BAND, D_A), lambda bi, r, i: (bi, r, i, 0)),
            pl.BlockSpec((1, 1, BAND, LANES), lambda bi, r, i: (bi, r, i, 0)),
        ],
        out_shape=[
            jax.ShapeDtypeStruct((b, dilation, n_sub, D_A), F32),
            jax.ShapeDtypeStruct((b, dilation, n_sub, LANES), F32),
        ],
        compiler_params=_params("parallel", "parallel", "arbitrary"),
        name=f"dilated_{dilation}",
    )(qkv, qkv, qkv, qkv, qkv, bias)


COMBINE_ROWS = 512


def _combine_kernel(*refs):
    n_pat = len(DILATED_PATTERNS)
    num_refs, st_refs = refs[:n_pat], refs[n_pat:2 * n_pat]
    o_ref, num_buf, st_buf = refs[2 * n_pat:]
    for p in range(n_pat):
        dilation = num_refs[p].shape[1]
        for rc in range(dilation):
            rows = pl.ds(rc, COMBINE_ROWS // dilation, stride=dilation)
            st_buf[p, rows, :] = st_refs[p][0, rc]
            for h in range(H_A):
                num_buf[p, h, rows, :] = num_refs[p][0, rc, :, h * HEAD_DIM:(h + 1) * HEAD_DIM]
    stats = [st_buf[p] for p in range(n_pat)]
    for h in range(H_A):
        sl = slice(h * HEAD_DIM, (h + 1) * HEAD_DIM)
        mxs = [st[:, h:h + 1] for st in stats]
        dens = [st[:, H_A + h:H_A + h + 1] for st in stats]
        mx = functools.reduce(jnp.maximum, mxs)
        num = None
        den = None
        for p, (pden, pmx) in enumerate(zip(dens, mxs)):
            scale = jnp.exp(pmx - mx)
            pnum = num_buf[p, h]
            num = pnum * scale if num is None else num + pnum * scale
            den = pden * scale if den is None else den + pden * scale
        o_ref[:, sl] = (num / den).astype(BF16)


def _combine(nums, stats):
    b, _, s, _ = nums[0].shape
    tps = s // COMBINE_ROWS

    def spec(arr):
        dilation, width = arr.shape[1], arr.shape[3]
        return pl.BlockSpec((1, dilation, COMBINE_ROWS // dilation, width),
                            lambda i: (i // tps, 0, i % tps, 0))

    n_pat = len(nums)
    return pl.pallas_call(
        _combine_kernel,
        grid=(b * tps,),
        in_specs=[spec(a) for a in nums] + [spec(a) for a in stats],
        out_specs=pl.BlockSpec((COMBINE_ROWS, D_A), lambda i: (i, 0)),
        out_shape=jax.ShapeDtypeStruct((b * s, D_A), BF16),
        scratch_shapes=[
            pltpu.VMEM((n_pat, H_A, COMBINE_ROWS, HEAD_DIM), F32),
            pltpu.VMEM((n_pat, COMBINE_ROWS, LANES), F32),
        ],
        compiler_params=_params("parallel"),
        name="combine_a",
    )(*nums, *stats)


def _fox_kernel(q_ref, k_ref, vt_ref, o_ref, m_ref, l_ref, acc_ref):
    iq = pl.program_id(2)
    m_ref[...] = jnp.full(m_ref.shape, NEG_INF, F32)
    l_ref[...] = jnp.zeros(l_ref.shape, F32)
    acc_ref[...] = jnp.zeros(acc_ref.shape, F32)

    def step(kb, diagonal):
        start = pl.multiple_of(kb * FOX_TK, FOX_TK)
        for hh in range(FOX_HEADS):
            k = k_ref[0, hh, pl.ds(start, FOX_TK), :]
            st = lax.dot_general(k, q_ref[0, hh], (((1,), (1,)), ((), ())),
                                 preferred_element_type=F32)
            if diagonal:
                key = lax.broadcasted_iota(jnp.int32, st.shape, 0)
                qry = lax.broadcasted_iota(jnp.int32, st.shape, 1)
                st = jnp.where(key <= qry, st, NEG_INF)
            m_prev = m_ref[hh]
            m_new = jnp.maximum(m_prev, jnp.max(st, axis=0, keepdims=True))
            alpha = jnp.exp2(m_prev - m_new)
            p = jnp.exp2(st - m_new)
            l_ref[hh] = alpha * l_ref[hh] + jnp.sum(p, axis=0, keepdims=True)
            vt = vt_ref[0, hh, :, pl.ds(start, FOX_TK)]
            acc_ref[hh] = alpha * acc_ref[hh] + jnp.dot(vt, p.astype(BF16),
                                                        preferred_element_type=F32)
            m_ref[hh] = m_new

    def body(kb, carry):
        step(kb, False)
        return carry

    lax.fori_loop(0, iq, body, 0)
    step(iq, True)
    for hh in range(FOX_HEADS):
        o_ref[0, :, hh * HEAD_DIM:(hh + 1) * HEAD_DIM] = (
            acc_ref[hh] / l_ref[hh]).T.astype(BF16)


def _fox(q_aug, k_aug, v_t):
    b, h_b, s, _ = q_aug.shape
    assert FOX_TQ == FOX_TK
    return pl.pallas_call(
        _fox_kernel,
        grid=(b, h_b // FOX_HEADS, s // FOX_TQ),
        in_specs=[
            pl.BlockSpec((1, FOX_HEADS, FOX_TQ, AUG), lambda bi, h, i: (bi, h, i, 0)),
            pl.BlockSpec((1, FOX_HEADS, s, AUG), lambda bi, h, i: (bi, h, 0, 0)),
            pl.BlockSpec((1, FOX_HEADS, HEAD_DIM, s), lambda bi, h, i: (bi, h, 0, 0)),
        ],
        out_specs=pl.BlockSpec((1, FOX_TQ, FOX_HEADS * HEAD_DIM), lambda bi, h, i: (bi, i, h)),
        out_shape=jax.ShapeDtypeStruct((b, s, h_b * HEAD_DIM), BF16),
        scratch_shapes=[
            pltpu.VMEM((FOX_HEADS, 1, FOX_TQ), F32),
            pltpu.VMEM((FOX_HEADS, 1, FOX_TQ), F32),
            pltpu.VMEM((FOX_HEADS, HEAD_DIM, FOX_TQ), F32),
        ],
        compiler_params=_params("parallel", "parallel", "arbitrary"),
        name="fox",
    )(q_aug, k_aug, v_t)


def _out_proj_kernel(a_ref, b_ref, w_ref, x_ref, o_ref):
    mixed = jnp.concatenate([a_ref[...], b_ref[...]], axis=-1)
    o_ref[...] = x_ref[...] + jnp.dot(mixed, w_ref[...].astype(BF16),
                                      preferred_element_type=F32)


def _out_proj(out_a, out_b, w_out, x):
    m = x.shape[0]
    return pl.pallas_call(
        _out_proj_kernel,
        grid=(m // ROW_TILE, D_MODEL // COL_TILE),
        in_specs=[
            pl.BlockSpec((ROW_TILE, D_A), lambda i, j: (i, 0)),
            pl.BlockSpec((ROW_TILE, D_B), lambda i, j: (i, 0)),
            pl.BlockSpec((D_A + D_B, COL_TILE), lambda i, j: (0, j)),
            pl.BlockSpec((ROW_TILE, COL_TILE), lambda i, j: (i, j)),
        ],
        out_specs=pl.BlockSpec((ROW_TILE, COL_TILE), lambda i, j: (i, j)),
        out_shape=jax.ShapeDtypeStruct((m, D_MODEL), F32),
        compiler_params=_params("parallel", "parallel"),
        name="out_proj",
    )(out_a, out_b, w_out, x)


def _mixer(x, b, s, mix_norm, w_in, q_norm_a, k_norm_a, q_norm_b, k_norm_b, forget_bias,
           rel_bias_table, w_out):
    ones = jnp.ones((D_A,), F32)
    head_gain = jnp.concatenate([
        jnp.tile(q_norm_a * ATTN_SCALE, H_A), jnp.tile(k_norm_a, H_A), ones,
        jnp.tile(q_norm_b * (ATTN_SCALE * LOG2E), H_B), jnp.tile(k_norm_b, H_B), ones,
    ]).reshape(1, D_QKV)
    w_f = jnp.pad(w_in[:, D_QKV:], ((0, 0), (0, LANES - H_B))).astype(BF16)
    f_bias = jnp.pad(forget_bias, (0, LANES - H_B)).reshape(1, LANES)
    *qkv_a, q_aug, k_aug, v_t = _proj(x, b, s, mix_norm, w_in, w_f, f_bias, head_gain)

    out_b = _fox(q_aug, k_aug, v_t).reshape(b * s, D_B)

    bias = _band_bias(rel_bias_table)
    parts = [_dilated(qkv, bias, p) for p, qkv in enumerate(qkv_a)]
    out_a = _combine([pt[0] for pt in parts], [pt[1] for pt in parts])

    return _out_proj(out_a, out_b, w_out, x)


def kernel(x, ffn1_norm, ffn1_w_in, ffn1_w_out, mix_norm, w_in, q_norm_a, k_norm_a, q_norm_b,
           k_norm_b, forget_bias, rel_bias_table, w_out, ffn2_norm, ffn2_w_in, ffn2_w_out):
    b, s, d = x.shape
    depth = ffn1_norm.shape[0]
    x = x.reshape(b * s, d)
    for l in range(depth):
        x = _ffn(x, ffn1_norm[l], ffn1_w_in[l], ffn1_w_out[l])
        x = _mixer(x, b, s, mix_norm[l], w_in[l], q_norm_a[l], k_norm_a[l], q_norm_b[l],
                   k_norm_b[l], forget_bias[l], rel_bias_table, w_out[l])
        x = _ffn(x, ffn2_norm[l], ffn2_w_in[l], ffn2_w_out[l])
    return x.reshape(b, s, d)
```

```python
import functools
import math

import numpy as np
import jax
import jax.numpy as jnp
from jax import lax
from jax.experimental import pallas as pl
from jax.experimental.pallas import tpu as pltpu

D_MODEL = 2048
HEAD_DIM = 128
N_HEADS = D_MODEL // HEAD_DIM
H_A = N_HEADS // 2
H_B = N_HEADS - H_A
D_A = H_A * HEAD_DIM
D_B = H_B * HEAD_DIM
D_QKV = 3 * D_A + 3 * D_B
DILATED_PATTERNS = ((128, 1), (512, 4), (2048, 16))
BAND = 128
NUM_BUCKETS = 32
MAX_DISTANCE = 2048
D_FF = ((8 * D_MODEL // 3 + 127) // 128) * 128
RMS_EPS = 1e-6
NEG_INF = -1e30
ATTN_SCALE = HEAD_DIM ** -0.5

LANES = 128
FF_TILE = 512
FF_TAIL = D_FF - (D_FF // FF_TILE) * FF_TILE
FFN_OUT_TILE = 256
ROW_TILE = 1024
COL_TILE = 512
FOX_TQ = 512
FOX_TK = 512
FOX_HEADS = 4
VMEM_LIMIT = 56 * 1024 * 1024

F32 = jnp.float32
BF16 = jnp.bfloat16


def _params(*sem):
    return pltpu.CompilerParams(dimension_semantics=sem, vmem_limit_bytes=VMEM_LIMIT)


def _rms_rows(x, gain):
    ms = jnp.mean(x * x, axis=-1, keepdims=True)
    return x * lax.rsqrt(ms + RMS_EPS) * gain


def _ffn_in_kernel(x_ref, g_ref, wg_ref, wu_ref, act_ref, h_ref):
    @pl.when(pl.program_id(1) == 0)
    def _():
        h_ref[...] = _rms_rows(x_ref[...], g_ref[...]).astype(BF16)

    h = h_ref[...]
    gate = jnp.dot(h, wg_ref[...].astype(BF16), preferred_element_type=F32)
    up = jnp.dot(h, wu_ref[...].astype(BF16), preferred_element_type=F32)
    act = (gate * (1.0 / (1.0 + jnp.exp(-gate))) * up).astype(BF16)

    j = pl.program_id(1)
    last = pl.num_programs(1) - 1

    @pl.when(j < last)
    def _():
        act_ref[...] = act

    @pl.when(j == last)
    def _():
        act_ref[:, :FF_TAIL] = act[:, FF_TILE - FF_TAIL:]


def _ffn_out_kernel(act_ref, w_ref, x_ref, o_ref):
    y = jnp.dot(act_ref[...], w_ref[...].astype(BF16), preferred_element_type=F32)
    o_ref[...] = x_ref[...] + 0.5 * y


def _ffn(x, gain, w_in, w_out):
    m = x.shape[0]
    n_ff = pl.cdiv(D_FF, FF_TILE)

    def ff_start(j):
        return pl.multiple_of(jnp.minimum(j * FF_TILE, D_FF - FF_TILE), LANES)

    act = pl.pallas_call(
        _ffn_in_kernel,
        grid=(m // ROW_TILE, n_ff),
        in_specs=[
            pl.BlockSpec((ROW_TILE, D_MODEL), lambda i, j: (i, 0)),
            pl.BlockSpec((1, D_MODEL), lambda i, j: (0, 0)),
            pl.BlockSpec((pl.Element(D_MODEL), pl.Element(FF_TILE)),
                         lambda i, j: (0, ff_start(j))),
            pl.BlockSpec((pl.Element(D_MODEL), pl.Element(FF_TILE)),
                         lambda i, j: (0, pl.multiple_of(D_FF + ff_start(j), LANES))),
        ],
        out_specs=pl.BlockSpec((ROW_TILE, FF_TILE), lambda i, j: (i, j)),
        out_shape=jax.ShapeDtypeStruct((m, D_FF), BF16),
        scratch_shapes=[pltpu.VMEM((ROW_TILE, D_MODEL), BF16)],
        compiler_params=_params("parallel", "arbitrary"),
        name="ffn_in",
    )(x, gain.reshape(1, D_MODEL), w_in, w_in)

    return pl.pallas_call(
        _ffn_out_kernel,
        grid=(m // ROW_TILE, D_MODEL // FFN_OUT_TILE),
        in_specs=[
            pl.BlockSpec((ROW_TILE, D_FF), lambda i, j: (i, 0)),
            pl.BlockSpec((D_FF, FFN_OUT_TILE), lambda i, j: (0, j)),
            pl.BlockSpec((ROW_TILE, FFN_OUT_TILE), lambda i, j: (i, j)),
        ],
        out_specs=pl.BlockSpec((ROW_TILE, FFN_OUT_TILE), lambda i, j: (i, j)),
        out_shape=jax.ShapeDtypeStruct((m, D_MODEL), F32),
        compiler_params=_params("parallel", "parallel"),
        name="ffn_out",
    )(act, w_out, x)


HEADS_PER_TILE = COL_TILE // HEAD_DIM
N_PROJ_TILES = D_QKV // COL_TILE
TILES_PER_GROUP = D_A // COL_TILE
A_TILES = 3 * TILES_PER_GROUP
AUG = 2 * HEAD_DIM
LOG2E = math.log2(math.e)


def _log_gate_scan(z, carry):
    c = (jnp.minimum(z, 0.0) - jnp.log1p(jnp.exp(-jnp.abs(z)))) * LOG2E
    row = lax.broadcasted_iota(jnp.int32, c.shape, 0)
    shift = 1
    while shift < c.shape[0]:
        c = c + jnp.where(row >= shift, pltpu.roll(c, shift, axis=0), 0.0)
        shift *= 2
    return c + carry


def _split3(c):
    hi = c.astype(BF16)
    rest = c - hi.astype(F32)
    mid = rest.astype(BF16)
    lo = (rest - mid.astype(F32)).astype(BF16)
    return hi, mid, lo


def _head_mean_matrix():
    head = np.arange(COL_TILE) // HEAD_DIM
    return jnp.asarray((head[:, None] == head[None, :]) / HEAD_DIM, BF16)


def _gate_placement_matrices():
    mats = np.zeros((2, TILES_PER_GROUP, COL_TILE, COL_TILE), np.float32)
    ones_row = 3 * LANES
    for part in range(TILES_PER_GROUP):
        for hh in range(HEADS_PER_TILE):
            head = part * HEADS_PER_TILE + hh
            col = hh * HEAD_DIM
            for term in range(3):
                mats[0, part, term * LANES + head, col + term] = 1.0
                mats[0, part, ones_row, col + 3 + term] = 1.0
                mats[1, part, term * LANES + head, col + 3 + term] = -1.0
                mats[1, part, ones_row, col + term] = 1.0
    return jnp.asarray(mats.reshape(2 * TILES_PER_GROUP, COL_TILE, COL_TILE), BF16)


def _proj_kernel(x_ref, g_ref, w_ref, wf_ref, fb_ref, hg_ref, mean_ref, place_ref,
                 a1_ref, a4_ref, a16_ref, qb_ref, kb_ref, vt_ref,
                 h_ref, cs_ref, carry_ref, y_ref, *, tiles_per_seq):
    i = pl.program_id(0)
    j = pl.program_id(1)

    @pl.when(j == 0)
    def _():
        h_ref[...] = _rms_rows(x_ref[...], g_ref[...]).astype(BF16)

        @pl.when(i % tiles_per_seq == 0)
        def _():
            carry_ref[...] = jnp.zeros(carry_ref.shape, F32)

        z = jnp.dot(h_ref[...], wf_ref[...], preferred_element_type=F32) + fb_ref[...]
        c = _log_gate_scan(z, carry_ref[...])
        carry_ref[...] = c[ROW_TILE - 1:ROW_TILE, :]
        for term, part in enumerate(_split3(c)):
            cs_ref[:, term * LANES:(term + 1) * LANES] = part
        cs_ref[:, 3 * LANES:] = jnp.ones((ROW_TILE, LANES), BF16)

    r = jnp.dot(h_ref[...], w_ref[...].astype(BF16), preferred_element_type=F32)

    def normed():
        ms = jnp.dot((r * r).astype(BF16), mean_ref[...], preferred_element_type=F32)
        return r * lax.rsqrt(ms + RMS_EPS) * hg_ref[...]

    def store_dilated(y):
        a1_ref[0, 0] = y.astype(BF16)
        for c in range(HEADS_PER_TILE):
            sl = slice(c * LANES, (c + 1) * LANES)
            y_ref[c] = y[:, sl]
            for ref in (a4_ref, a16_ref):
                dilation = ref.shape[1]
                for rc in range(dilation):
                    rows = y_ref[c, pl.ds(rc, ROW_TILE // dilation, stride=dilation), :]
                    ref[0, rc, :, sl] = rows.astype(BF16)

    for t in range(N_PROJ_TILES):
        group, part = divmod(t, TILES_PER_GROUP)

        @pl.when(j == t)
        def _(group=group):
            if group in (0, 1):
                store_dilated(normed())
            elif group == 2:
                store_dilated(r)
            elif group in (3, 4):
                out = qb_ref if group == 3 else kb_ref
                y = normed().astype(BF16)
                aug = jnp.dot(cs_ref[...], place_ref[0],
                              preferred_element_type=F32).astype(BF16)
                for hh in range(HEADS_PER_TILE):
                    sl = slice(hh * HEAD_DIM, (hh + 1) * HEAD_DIM)
                    out[0, hh, :, :HEAD_DIM] = y[:, sl]
                    out[0, hh, :, HEAD_DIM:] = aug[:, sl]
            else:
                for hh in range(HEADS_PER_TILE):
                    vt_ref[0, hh] = r[:, hh * HEAD_DIM:(hh + 1) * HEAD_DIM].T.astype(BF16)


def _proj(x, b, s, gain, w_in, layer, w_f, f_bias, head_gain):
    m = x.shape[0]
    tps = s // ROW_TILE

    def a_spec(dilation):
        return pl.BlockSpec((1, dilation, ROW_TILE // dilation, COL_TILE),
                            lambda i, j: (i // tps, 0, i % tps, jnp.minimum(j, A_TILES - 1)))

    def head_block(first):
        return lambda i, j: (i // tps, jnp.clip(j - first, 0, TILES_PER_GROUP - 1), i % tps, 0)

    qb_first = A_TILES
    vt_first = A_TILES + 2 * TILES_PER_GROUP
    dilations = [d for _, d in DILATED_PATTERNS]
    return pl.pallas_call(
        functools.partial(_proj_kernel, tiles_per_seq=tps),
        grid=(m // ROW_TILE, N_PROJ_TILES),
        in_specs=[
            pl.BlockSpec((ROW_TILE, D_MODEL), lambda i, j: (i, 0)),
            pl.BlockSpec((1, D_MODEL), lambda i, j: (0, 0)),
            pl.BlockSpec((None, D_MODEL, COL_TILE), lambda i, j: (layer, 0, j)),
            pl.BlockSpec((D_MODEL, LANES), lambda i, j: (0, 0)),
            pl.BlockSpec((1, LANES), lambda i, j: (0, 0)),
            pl.BlockSpec((1, COL_TILE), lambda i, j: (0, j)),
            pl.BlockSpec((COL_TILE, COL_TILE), lambda i, j: (0, 0)),
            pl.BlockSpec((1, COL_TILE, COL_TILE),
                         lambda i, j: (jnp.clip(j - qb_first, 0, 2 * TILES_PER_GROUP - 1), 0, 0)),
        ],
        out_specs=[a_spec(d) for d in dilations] + [
            pl.BlockSpec((1, HEADS_PER_TILE, ROW_TILE, AUG), head_block(qb_first)),
            pl.BlockSpec((1, HEADS_PER_TILE, ROW_TILE, AUG),
                         head_block(qb_first + TILES_PER_GROUP)),
            pl.BlockSpec((1, HEADS_PER_TILE, HEAD_DIM, ROW_TILE),
                         lambda i, j: (i // tps, jnp.clip(j - vt_first, 0, TILES_PER_GROUP - 1),
                                       0, i % tps)),
        ],
        out_shape=[jax.ShapeDtypeStruct((b, d, s // d, 3 * D_A), BF16) for d in dilations] + [
            jax.ShapeDtypeStruct((b, H_B, s, AUG), BF16),
            jax.ShapeDtypeStruct((b, H_B, s, AUG), BF16),
            jax.ShapeDtypeStruct((b, H_B, HEAD_DIM, s), BF16),
        ],
        scratch_shapes=[
            pltpu.VMEM((ROW_TILE, D_MODEL), BF16),
            pltpu.VMEM((ROW_TILE, 4 * LANES), BF16),
            pltpu.VMEM((1, LANES), F32),
            pltpu.VMEM((HEADS_PER_TILE, ROW_TILE, LANES), F32),
        ],
        compiler_params=_params("arbitrary", "arbitrary"),
        name="proj",
    )(x, gain.reshape(1, D_MODEL), w_in, w_f, f_bias, head_gain,
      _head_mean_matrix(), _gate_placement_matrices())


def _bucket_steps(dilation):
    dist = np.arange(BAND + 1) * dilation
    max_exact = NUM_BUCKETS // 2
    large = max_exact + np.floor(
        np.log(np.maximum(dist, 1) / max_exact) / math.log(MAX_DISTANCE / max_exact)
        * (NUM_BUCKETS - max_exact)).astype(np.int64)
    bucket = np.where(dist < max_exact, dist, np.minimum(large, NUM_BUCKETS - 1))
    steps = [(0, int(bucket[0]))]
    for delta in range(1, BAND + 1):
        if bucket[delta] != bucket[delta - 1]:
            steps.append((delta, int(bucket[delta])))
    return steps


def _bias_kernel(table_ref, o_ref):
    iq = lax.broadcasted_iota(jnp.int32, (BAND, 2 * BAND), 0)
    ik = lax.broadcasted_iota(jnp.int32, (BAND, 2 * BAND), 1)
    delta = iq + BAND - ik
    in_band = jnp.logical_and(delta >= 0, delta <= BAND)
    for p, (_, dilation) in enumerate(DILATED_PATTERNS):
        steps = _bucket_steps(dilation)
        for h in range(H_A):
            val = jnp.full((BAND, 2 * BAND), table_ref[steps[0][1], h], F32)
            for start, bucket in steps[1:]:
                val = jnp.where(delta >= start, table_ref[bucket, h], val)
            o_ref[p, h] = jnp.where(in_band, val, NEG_INF)


def _band_bias(rel_table):
    n_pat = len(DILATED_PATTERNS)
    return pl.pallas_call(
        _bias_kernel,
        in_specs=[pl.BlockSpec(memory_space=pltpu.SMEM)],
        out_specs=pl.BlockSpec(memory_space=pltpu.VMEM),
        out_shape=jax.ShapeDtypeStruct((n_pat, H_A, BAND, 2 * BAND), F32),
        name="band_bias",
    )(rel_table)


def _dilated_kernel(q_ref, kp_ref, kc_ref, vp_ref, vc_ref, bias_ref, o_ref, st_ref):
    first = pl.program_id(2) == 0
    key_lane = lax.broadcasted_iota(jnp.int32, (1, 2 * BAND), 1)
    no_prev = jnp.where(jnp.logical_and(first, key_lane < BAND), NEG_INF, 0.0)
    stat_lane = lax.broadcasted_iota(jnp.int32, (BAND, LANES), 1)
    stats = jnp.zeros((BAND, LANES), F32)
    for h in range(H_A):
        sl = slice(h * HEAD_DIM, (h + 1) * HEAD_DIM)
        k = jnp.concatenate([kp_ref[0, 0, :, sl], kc_ref[0, 0, :, sl]], axis=0)
        v = jnp.concatenate([vp_ref[0, 0, :, sl], vc_ref[0, 0, :, sl]], axis=0)
        s = lax.dot_general(q_ref[0, 0, :, sl], k, (((1,), (1,)), ((), ())),
                            preferred_element_type=F32)
        s = s + bias_ref[0, h] + no_prev
        mx = jnp.max(s, axis=-1, keepdims=True)
        p = jnp.exp(s - mx)
        den = jnp.sum(p, axis=-1, keepdims=True)
        o_ref[0, 0, :, sl] = jnp.dot(p.astype(BF16), v, preferred_element_type=F32)
        stats = jnp.where(stat_lane == h, mx, stats)
        stats = jnp.where(stat_lane == H_A + h, den, stats)
    st_ref[0, 0] = stats


def _dilated(qkv, bias, pattern):
    b, dilation, n_sub, _ = qkv.shape
    nb = n_sub // BAND

    def spec(which, prev):
        def index(bi, r, i):
            return (bi, r, jnp.maximum(i - 1, 0) if prev else i, which)
        return pl.BlockSpec((1, 1, BAND, D_A), index)

    return pl.pallas_call(
        _dilated_kernel,
        grid=(b, dilation, nb),
        in_specs=[
            spec(0, False), spec(1, True), spec(1, False), spec(2, True), spec(2, False),
            pl.BlockSpec((1, H_A, BAND, 2 * BAND), lambda bi, r, i: (pattern, 0, 0, 0)),
        ],
        out_specs=[
            pl.BlockSpec((1, 1, BAND, D_A), lambda bi, r, i: (bi, r, i, 0)),
            pl.BlockSpec((1, 1, BAND, LANES), lambda bi, r, i: (bi, r, i, 0)),
        ],
        out_shape=[
            jax.ShapeDtypeStruct((b, dilation, n_sub, D_A), F32),
            jax.ShapeDtypeStruct((b, dilation, n_sub, LANES), F32),
        ],
        compiler_params=_params("parallel", "parallel", "arbitrary"),
        name=f"dilated_{dilation}",
    )(qkv, qkv, qkv, qkv, qkv, bias)


COMBINE_ROWS = 512


def _combine_kernel(*refs):
    n_pat = len(DILATED_PATTERNS)
    num_refs, st_refs = refs[:n_pat], refs[n_pat:2 * n_pat]
    o_ref, num_buf, st_buf = refs[2 * n_pat:]
    for p in range(n_pat):
        dilation = num_refs[p].shape[1]
        for rc in range(dilation):
            rows = pl.ds(rc, COMBINE_ROWS // dilation, stride=dilation)
            st_buf[p, rows, :] = st_refs[p][0, rc]
            for h in range(H_A):
                num_buf[p, h, rows, :] = num_refs[p][0, rc, :, h * HEAD_DIM:(h + 1) * HEAD_DIM]
    stats = [st_buf[p] for p in range(n_pat)]
    for h in range(H_A):
        sl = slice(h * HEAD_DIM, (h + 1) * HEAD_DIM)
        mxs = [st[:, h:h + 1] for st in stats]
        dens = [st[:, H_A + h:H_A + h + 1] for st in stats]
        mx = functools.reduce(jnp.maximum, mxs)
        num = None
        den = None
        for p, (pden, pmx) in enumerate(zip(dens, mxs)):
            scale = jnp.exp(pmx - mx)
            pnum = num_buf[p, h]
            num = pnum * scale if num is None else num + pnum * scale
            den = pden * scale if den is None else den + pden * scale
        o_ref[:, sl] = (num / den).astype(BF16)


def _combine(nums, stats):
    b, _, s, _ = nums[0].shape
    tps = s // COMBINE_ROWS

    def spec(arr):
        dilation, width = arr.shape[1], arr.shape[3]
        return pl.BlockSpec((1, dilation, COMBINE_ROWS // dilation, width),
                            lambda i: (i // tps, 0, i % tps, 0))

    n_pat = len(nums)
    return pl.pallas_call(
        _combine_kernel,
        grid=(b * tps,),
        in_specs=[spec(a) for a in nums] + [spec(a) for a in stats],
        out_specs=pl.BlockSpec((COMBINE_ROWS, D_A), lambda i: (i, 0)),
        out_shape=jax.ShapeDtypeStruct((b * s, D_A), BF16),
        scratch_shapes=[
            pltpu.VMEM((n_pat, H_A, COMBINE_ROWS, HEAD_DIM), F32),
            pltpu.VMEM((n_pat, COMBINE_ROWS, LANES), F32),
        ],
        compiler_params=_params("parallel"),
        name="combine_a",
    )(*nums, *stats)


def _fox_kernel(q_ref, k_ref, vt_ref, o_ref, m_ref, l_ref, acc_ref):
    iq = pl.program_id(2)
    m_ref[...] = jnp.full(m_ref.shape, NEG_INF, F32)
    l_ref[...] = jnp.zeros(l_ref.shape, F32)
    acc_ref[...] = jnp.zeros(acc_ref.shape, F32)

    def step(kb, diagonal):
        start = pl.multiple_of(kb * FOX_TK, FOX_TK)
        scores = []
        for hh in range(FOX_HEADS):
            k = k_ref[0, hh, pl.ds(start, FOX_TK), :]
            scores.append(lax.dot_general(k, q_ref[0, hh], (((1,), (1,)), ((), ())),
                                          preferred_element_type=F32))
        for hh, st in enumerate(scores):
            if diagonal:
                key = lax.broadcasted_iota(jnp.int32, st.shape, 0)
                qry = lax.broadcasted_iota(jnp.int32, st.shape, 1)
                st = jnp.where(key <= qry, st, NEG_INF)
            m_prev = m_ref[hh]
            m_new = jnp.maximum(m_prev, jnp.max(st, axis=0, keepdims=True))
            alpha = jnp.exp2(m_prev - m_new)
            p = jnp.exp2(st - m_new)
            l_ref[hh] = alpha * l_ref[hh] + jnp.sum(p, axis=0, keepdims=True)
            vt = vt_ref[0, hh, :, pl.ds(start, FOX_TK)]
            acc_ref[hh] = alpha * acc_ref[hh] + jnp.dot(vt, p.astype(BF16),
                                                        preferred_element_type=F32)
            m_ref[hh] = m_new

    def body(kb, carry):
        step(kb, False)
        return carry

    lax.fori_loop(0, iq, body, 0)
    step(iq, True)
    for hh in range(FOX_HEADS):
        o_ref[0, :, hh * HEAD_DIM:(hh + 1) * HEAD_DIM] = (
            acc_ref[hh] / l_ref[hh]).T.astype(BF16)


def _fox(q_aug, k_aug, v_t):
    b, h_b, s, _ = q_aug.shape
    assert FOX_TQ == FOX_TK
    return pl.pallas_call(
        _fox_kernel,
        grid=(b, h_b // FOX_HEADS, s // FOX_TQ),
        in_specs=[
            pl.BlockSpec((1, FOX_HEADS, FOX_TQ, AUG), lambda bi, h, i: (bi, h, i, 0)),
            pl.BlockSpec((1, FOX_HEADS, s, AUG), lambda bi, h, i: (bi, h, 0, 0)),
            pl.BlockSpec((1, FOX_HEADS, HEAD_DIM, s), lambda bi, h, i: (bi, h, 0, 0)),
        ],
        out_specs=pl.BlockSpec((1, FOX_TQ, FOX_HEADS * HEAD_DIM), lambda bi, h, i: (bi, i, h)),
        out_shape=jax.ShapeDtypeStruct((b, s, h_b * HEAD_DIM), BF16),
        scratch_shapes=[
            pltpu.VMEM((FOX_HEADS, 1, FOX_TQ), F32),
            pltpu.VMEM((FOX_HEADS, 1, FOX_TQ), F32),
            pltpu.VMEM((FOX_HEADS, HEAD_DIM, FOX_TQ), F32),
        ],
        compiler_params=_params("parallel", "parallel", "arbitrary"),
        name="fox",
    )(q_aug, k_aug, v_t)


def _out_proj_kernel(a_ref, b_ref, w_ref, x_ref, o_ref):
    mixed = jnp.concatenate([a_ref[...], b_ref[...]], axis=-1)
    o_ref[...] = x_ref[...] + jnp.dot(mixed, w_ref[...].astype(BF16),
                                      preferred_element_type=F32)


def _out_proj(out_a, out_b, w_out, x):
    m = x.shape[0]
    return pl.pallas_call(
        _out_proj_kernel,
        grid=(m // ROW_TILE, D_MODEL // COL_TILE),
        in_specs=[
            pl.BlockSpec((ROW_TILE, D_A), lambda i, j: (i, 0)),
            pl.BlockSpec((ROW_TILE, D_B), lambda i, j: (i, 0)),
            pl.BlockSpec((D_A + D_B, COL_TILE), lambda i, j: (0, j)),
            pl.BlockSpec((ROW_TILE, COL_TILE), lambda i, j: (i, j)),
        ],
        out_specs=pl.BlockSpec((ROW_TILE, COL_TILE), lambda i, j: (i, j)),
        out_shape=jax.ShapeDtypeStruct((m, D_MODEL), F32),
        compiler_params=_params("parallel", "parallel"),
        name="out_proj",
    )(out_a, out_b, w_out, x)


def _mixer(x, b, s, mix_norm, w_in, layer, q_norm_a, k_norm_a, q_norm_b, k_norm_b, forget_bias,
           rel_bias_table, w_out):
    ones = jnp.ones((D_A,), F32)
    head_gain = jnp.concatenate([
        jnp.tile(q_norm_a * ATTN_SCALE, H_A), jnp.tile(k_norm_a, H_A), ones,
        jnp.tile(q_norm_b * (ATTN_SCALE * LOG2E), H_B), jnp.tile(k_norm_b, H_B), ones,
    ]).reshape(1, D_QKV)
    w_f = jnp.pad(w_in[layer, :, D_QKV:], ((0, 0), (0, LANES - H_B))).astype(BF16)
    f_bias = jnp.pad(forget_bias, (0, LANES - H_B)).reshape(1, LANES)
    *qkv_a, q_aug, k_aug, v_t = _proj(x, b, s, mix_norm, w_in, layer, w_f, f_bias, head_gain)

    out_b = _fox(q_aug, k_aug, v_t).reshape(b * s, D_B)

    bias = _band_bias(rel_bias_table)
    parts = [_dilated(qkv, bias, p) for p, qkv in enumerate(qkv_a)]
    out_a = _combine([pt[0] for pt in parts], [pt[1] for pt in parts])

    return _out_proj(out_a, out_b, w_out, x)


def kernel(x, ffn1_norm, ffn1_w_in, ffn1_w_out, mix_norm, w_in, q_norm_a, k_norm_a, q_norm_b,
           k_norm_b, forget_bias, rel_bias_table, w_out, ffn2_norm, ffn2_w_in, ffn2_w_out):
    b, s, d = x.shape
    depth = ffn1_norm.shape[0]
    x = x.reshape(b * s, d)
    for l in range(depth):
        x = _ffn(x, ffn1_norm[l], ffn1_w_in[l], ffn1_w_out[l])
        x = _mixer(x, b, s, mix_norm[l], w_in, l, q_norm_a[l], k_norm_a[l], q_norm_b[l],
                   k_norm_b[l], forget_bias[l], rel_bias_table, w_out[l])
        x = _ffn(x, ffn2_norm[l], ffn2_w_in[l], ffn2_w_out[l])
    return x.reshape(b, s, d)
```

```python
import functools
import math

import numpy as np
import jax
import jax.numpy as jnp
from jax import lax
from jax.experimental import pallas as pl
from jax.experimental.pallas import tpu as pltpu

D_MODEL = 2048
HEAD_DIM = 128
N_HEADS = D_MODEL // HEAD_DIM
H_A = N_HEADS // 2
H_B = N_HEADS - H_A
D_A = H_A * HEAD_DIM
D_B = H_B * HEAD_DIM
D_QKV = 3 * D_A + 3 * D_B
DILATED_PATTERNS = ((128, 1), (512, 4), (2048, 16))
BAND = 128
NUM_BUCKETS = 32
MAX_DISTANCE = 2048
D_FF = ((8 * D_MODEL // 3 + 127) // 128) * 128
RMS_EPS = 1e-6
NEG_INF = -1e30
ATTN_SCALE = HEAD_DIM ** -0.5

LANES = 128
FF_TILE = 512
FF_TAIL = D_FF - (D_FF // FF_TILE) * FF_TILE
FFN_OUT_TILE = 256
ROW_TILE = 1024
COL_TILE = 512
FOX_TQ = 512
FOX_TK = 512
FOX_HEADS = 4
VMEM_LIMIT = 56 * 1024 * 1024

F32 = jnp.float32
BF16 = jnp.bfloat16


def _params(*sem):
    return pltpu.CompilerParams(dimension_semantics=sem, vmem_limit_bytes=VMEM_LIMIT)


def _rms_rows(x, gain):
    ms = jnp.mean(x * x, axis=-1, keepdims=True)
    return x * lax.rsqrt(ms + RMS_EPS) * gain


def _ffn_in_kernel(x_ref, g_ref, wg_ref, wu_ref, act_ref, h_ref):
    @pl.when(pl.program_id(1) == 0)
    def _():
        h_ref[...] = _rms_rows(x_ref[...], g_ref[...]).astype(BF16)

    h = h_ref[...]
    gate = jnp.dot(h, wg_ref[...].astype(BF16), preferred_element_type=F32)
    up = jnp.dot(h, wu_ref[...].astype(BF16), preferred_element_type=F32)
    act = (gate * (1.0 / (1.0 + jnp.exp(-gate))) * up).astype(BF16)

    j = pl.program_id(1)
    last = pl.num_programs(1) - 1

    @pl.when(j < last)
    def _():
        act_ref[...] = act

    @pl.when(j == last)
    def _():
        act_ref[:, :FF_TAIL] = act[:, FF_TILE - FF_TAIL:]


def _ffn_out_kernel(act_ref, w_ref, x_ref, o_ref):
    y = jnp.dot(act_ref[...], w_ref[...].astype(BF16), preferred_element_type=F32)
    o_ref[...] = x_ref[...] + 0.5 * y


def _ffn(x, gain, w_in, w_out):
    m = x.shape[0]
    n_ff = pl.cdiv(D_FF, FF_TILE)

    def ff_start(j):
        return pl.multiple_of(jnp.minimum(j * FF_TILE, D_FF - FF_TILE), LANES)

    act = pl.pallas_call(
        _ffn_in_kernel,
        grid=(m // ROW_TILE, n_ff),
        in_specs=[
            pl.BlockSpec((ROW_TILE, D_MODEL), lambda i, j: (i, 0)),
            pl.BlockSpec((1, D_MODEL), lambda i, j: (0, 0)),
            pl.BlockSpec((pl.Element(D_MODEL), pl.Element(FF_TILE)),
                         lambda i, j: (0, ff_start(j))),
            pl.BlockSpec((pl.Element(D_MODEL), pl.Element(FF_TILE)),
                         lambda i, j: (0, pl.multiple_of(D_FF + ff_start(j), LANES))),
        ],
        out_specs=pl.BlockSpec((ROW_TILE, FF_TILE), lambda i, j: (i, j)),
        out_shape=jax.ShapeDtypeStruct((m, D_FF), BF16),
        scratch_shapes=[pltpu.VMEM((ROW_TILE, D_MODEL), BF16)],
        compiler_params=_params("parallel", "arbitrary"),
        name="ffn_in",
    )(x, gain.reshape(1, D_MODEL), w_in, w_in)

    return pl.pallas_call(
        _ffn_out_kernel,
        grid=(m // ROW_TILE, D_MODEL // FFN_OUT_TILE),
        in_specs=[
            pl.BlockSpec((ROW_TILE, D_FF), lambda i, j: (i, 0)),
            pl.BlockSpec((D_FF, FFN_OUT_TILE), lambda i, j: (0, j)),
            pl.BlockSpec((ROW_TILE, FFN_OUT_TILE), lambda i, j: (i, j)),
        ],
        out_specs=pl.BlockSpec((ROW_TILE, FFN_OUT_TILE), lambda i, j: (i, j)),
        out_shape=jax.ShapeDtypeStruct((m, D_MODEL), F32),
        compiler_params=_params("parallel", "parallel"),
        name="ffn_out",
    )(act, w_out, x)


HEADS_PER_TILE = COL_TILE // HEAD_DIM
N_PROJ_TILES = D_QKV // COL_TILE
TILES_PER_GROUP = D_A // COL_TILE
A_TILES = 3 * TILES_PER_GROUP
AUG = 2 * HEAD_DIM
LOG2E = math.log2(math.e)


def _log_gate_scan(z, carry):
    c = (jnp.minimum(z, 0.0) - jnp.log1p(jnp.exp(-jnp.abs(z)))) * LOG2E
    row = lax.broadcasted_iota(jnp.int32, c.shape, 0)
    shift = 1
    while shift < c.shape[0]:
        c = c + jnp.where(row >= shift, pltpu.roll(c, shift, axis=0), 0.0)
        shift *= 2
    return c + carry


def _split3(c):
    hi = c.astype(BF16)
    rest = c - hi.astype(F32)
    mid = rest.astype(BF16)
    lo = (rest - mid.astype(F32)).astype(BF16)
    return hi, mid, lo


def _head_mean_matrix():
    head = np.arange(COL_TILE) // HEAD_DIM
    return jnp.asarray((head[:, None] == head[None, :]) / HEAD_DIM, BF16)


def _gate_placement_matrices():
    mats = np.zeros((2, TILES_PER_GROUP, COL_TILE, COL_TILE), np.float32)
    ones_row = 3 * LANES
    for part in range(TILES_PER_GROUP):
        for hh in range(HEADS_PER_TILE):
            head = part * HEADS_PER_TILE + hh
            col = hh * HEAD_DIM
            for term in range(3):
                mats[0, part, term * LANES + head, col + term] = 1.0
                mats[0, part, ones_row, col + 3 + term] = 1.0
                mats[1, part, term * LANES + head, col + 3 + term] = -1.0
                mats[1, part, ones_row, col + term] = 1.0
    return jnp.asarray(mats.reshape(2 * TILES_PER_GROUP, COL_TILE, COL_TILE), BF16)


def _proj_kernel(x_ref, g_ref, w_ref, wf_ref, fb_ref, hg_ref, mean_ref, place_ref,
                 a1_ref, a4_ref, a16_ref, qb_ref, kb_ref, vt_ref,
                 h_ref, cs_ref, carry_ref, y_ref, *, tiles_per_seq):
    i = pl.program_id(0)
    j = pl.program_id(1)

    @pl.when(j == 0)
    def _():
        h_ref[...] = _rms_rows(x_ref[...], g_ref[...]).astype(BF16)

        @pl.when(i % tiles_per_seq == 0)
        def _():
            carry_ref[...] = jnp.zeros(carry_ref.shape, F32)

        z = jnp.dot(h_ref[...], wf_ref[...], preferred_element_type=F32) + fb_ref[...]
        c = _log_gate_scan(z, carry_ref[...])
        carry_ref[...] = c[ROW_TILE - 1:ROW_TILE, :]
        for term, part in enumerate(_split3(c)):
            cs_ref[:, term * LANES:(term + 1) * LANES] = part
        cs_ref[:, 3 * LANES:] = jnp.ones((ROW_TILE, LANES), BF16)

    r = lax.dot_general(h_ref[...], w_ref[...].astype(BF16), (((1,), (1,)), ((), ())),
                        preferred_element_type=F32)

    def normed():
        ms = jnp.dot((r * r).astype(BF16), mean_ref[...], preferred_element_type=F32)
        return r * lax.rsqrt(ms + RMS_EPS) * hg_ref[...]

    def store_dilated(y):
        a1_ref[0, 0] = y.astype(BF16)
        for c in range(HEADS_PER_TILE):
            sl = slice(c * LANES, (c + 1) * LANES)
            y_ref[c] = y[:, sl]
            for ref in (a4_ref, a16_ref):
                dilation = ref.shape[1]
                for rc in range(dilation):
                    rows = y_ref[c, pl.ds(rc, ROW_TILE // dilation, stride=dilation), :]
                    ref[0, rc, :, sl] = rows.astype(BF16)

    for t in range(N_PROJ_TILES):
        group, part = divmod(t, TILES_PER_GROUP)

        @pl.when(j == t)
        def _(group=group):
            if group in (0, 1):
                store_dilated(normed())
            elif group == 2:
                store_dilated(r)
            elif group in (3, 4):
                out = qb_ref if group == 3 else kb_ref
                y = normed().astype(BF16)
                aug = jnp.dot(cs_ref[...], place_ref[0],
                              preferred_element_type=F32).astype(BF16)
                for hh in range(HEADS_PER_TILE):
                    sl = slice(hh * HEAD_DIM, (hh + 1) * HEAD_DIM)
                    out[0, hh, :, :HEAD_DIM] = y[:, sl]
                    out[0, hh, :, HEAD_DIM:] = aug[:, sl]
            else:
                for hh in range(HEADS_PER_TILE):
                    vt_ref[0, hh] = r[:, hh * HEAD_DIM:(hh + 1) * HEAD_DIM].T.astype(BF16)


def _proj(x, b, s, gain, w_in, layer, w_f, f_bias, head_gain):
    m = x.shape[0]
    tps = s // ROW_TILE

    def a_spec(dilation):
        return pl.BlockSpec((1, dilation, ROW_TILE // dilation, COL_TILE),
                            lambda i, j: (i // tps, 0, i % tps, jnp.minimum(j, A_TILES - 1)))

    def head_block(first):
        return lambda i, j: (i // tps, jnp.clip(j - first, 0, TILES_PER_GROUP - 1), i % tps, 0)

    qb_first = A_TILES
    vt_first = A_TILES + 2 * TILES_PER_GROUP
    dilations = [d for _, d in DILATED_PATTERNS]
    return pl.pallas_call(
        functools.partial(_proj_kernel, tiles_per_seq=tps),
        grid=(m // ROW_TILE, N_PROJ_TILES),
        in_specs=[
            pl.BlockSpec((ROW_TILE, D_MODEL), lambda i, j: (i, 0)),
            pl.BlockSpec((1, D_MODEL), lambda i, j: (0, 0)),
            pl.BlockSpec((None, COL_TILE, D_MODEL), lambda i, j: (layer, j, 0)),
            pl.BlockSpec((D_MODEL, LANES), lambda i, j: (0, 0)),
            pl.BlockSpec((1, LANES), lambda i, j: (0, 0)),
            pl.BlockSpec((1, COL_TILE), lambda i, j: (0, j)),
            pl.BlockSpec((COL_TILE, COL_TILE), lambda i, j: (0, 0)),
            pl.BlockSpec((1, COL_TILE, COL_TILE),
                         lambda i, j: (jnp.clip(j - qb_first, 0, 2 * TILES_PER_GROUP - 1), 0, 0)),
        ],
        out_specs=[a_spec(d) for d in dilations] + [
            pl.BlockSpec((1, HEADS_PER_TILE, ROW_TILE, AUG), head_block(qb_first)),
            pl.BlockSpec((1, HEADS_PER_TILE, ROW_TILE, AUG),
                         head_block(qb_first + TILES_PER_GROUP)),
            pl.BlockSpec((1, HEADS_PER_TILE, HEAD_DIM, ROW_TILE),
                         lambda i, j: (i // tps, jnp.clip(j - vt_first, 0, TILES_PER_GROUP - 1),
                                       0, i % tps)),
        ],
        out_shape=[jax.ShapeDtypeStruct((b, d, s // d, 3 * D_A), BF16) for d in dilations] + [
            jax.ShapeDtypeStruct((b, H_B, s, AUG), BF16),
            jax.ShapeDtypeStruct((b, H_B, s, AUG), BF16),
            jax.ShapeDtypeStruct((b, H_B, HEAD_DIM, s), BF16),
        ],
        scratch_shapes=[
            pltpu.VMEM((ROW_TILE, D_MODEL), BF16),
            pltpu.VMEM((ROW_TILE, 4 * LANES), BF16),
            pltpu.VMEM((1, LANES), F32),
            pltpu.VMEM((HEADS_PER_TILE, ROW_TILE, LANES), F32),
        ],
        compiler_params=_params("arbitrary", "arbitrary"),
        name="proj",
    )(x, gain.reshape(1, D_MODEL), jnp.swapaxes(w_in, 1, 2), w_f, f_bias, head_gain,
      _head_mean_matrix(), _gate_placement_matrices())


def _bucket_steps(dilation):
    dist = np.arange(BAND + 1) * dilation
    max_exact = NUM_BUCKETS // 2
    large = max_exact + np.floor(
        np.log(np.maximum(dist, 1) / max_exact) / math.log(MAX_DISTANCE / max_exact)
        * (NUM_BUCKETS - max_exact)).astype(np.int64)
    bucket = np.where(dist < max_exact, dist, np.minimum(large, NUM_BUCKETS - 1))
    steps = [(0, int(bucket[0]))]
    for delta in range(1, BAND + 1):
        if bucket[delta] != bucket[delta - 1]:
            steps.append((delta, int(bucket[delta])))
    return steps


def _bias_kernel(table_ref, o_ref):
    iq = lax.broadcasted_iota(jnp.int32, (BAND, 2 * BAND), 0)
    ik = lax.broadcasted_iota(jnp.int32, (BAND, 2 * BAND), 1)
    delta = iq + BAND - ik
    in_band = jnp.logical_and(delta >= 0, delta <= BAND)
    for p, (_, dilation) in enumerate(DILATED_PATTERNS):
        steps = _bucket_steps(dilation)
        for h in range(H_A):
            val = jnp.full((BAND, 2 * BAND), table_ref[steps[0][1], h], F32)
            for start, bucket in steps[1:]:
                val = jnp.where(delta >= start, table_ref[bucket, h], val)
            o_ref[p, h] = jnp.where(in_band, val, NEG_INF)


def _band_bias(rel_table):
    n_pat = len(DILATED_PATTERNS)
    return pl.pallas_call(
        _bias_kernel,
        in_specs=[pl.BlockSpec(memory_space=pltpu.SMEM)],
        out_specs=pl.BlockSpec(memory_space=pltpu.VMEM),
        out_shape=jax.ShapeDtypeStruct((n_pat, H_A, BAND, 2 * BAND), F32),
        name="band_bias",
    )(rel_table)


DIL_BLOCKS = 2


def _dilated_kernel(q_ref, kp_ref, kc_ref, vp_ref, vc_ref, bias_ref, o_ref, st_ref):
    first = pl.program_id(2) == 0
    key_lane = lax.broadcasted_iota(jnp.int32, (1, 2 * BAND), 1)
    no_prev = jnp.where(jnp.logical_and(first, key_lane < BAND), NEG_INF, 0.0)
    stat_lane = lax.broadcasted_iota(jnp.int32, (BAND, LANES), 1)

    def keys(prev_ref, cur_ref, blk, sl):
        if blk == 0:
            return jnp.concatenate([prev_ref[0, 0, :, sl], cur_ref[0, 0, :BAND, sl]], axis=0)
        return cur_ref[0, 0, (blk - 1) * BAND:(blk + 1) * BAND, sl]

    scores = {}
    for blk in range(DIL_BLOCKS):
        for h in range(H_A):
            sl = slice(h * HEAD_DIM, (h + 1) * HEAD_DIM)
            q = q_ref[0, 0, blk * BAND:(blk + 1) * BAND, sl]
            scores[blk, h] = lax.dot_general(q, keys(kp_ref, kc_ref, blk, sl),
                                             (((1,), (1,)), ((), ())),
                                             preferred_element_type=F32)
    for blk in range(DIL_BLOCKS):
        rows = slice(blk * BAND, (blk + 1) * BAND)
        stats = jnp.zeros((BAND, LANES), F32)
        for h in range(H_A):
            sl = slice(h * HEAD_DIM, (h + 1) * HEAD_DIM)
            s = scores[blk, h] + bias_ref[0, h]
            if blk == 0:
                s = s + no_prev
            mx = jnp.max(s, axis=-1, keepdims=True)
            p = jnp.exp(s - mx)
            den = jnp.sum(p, axis=-1, keepdims=True)
            o_ref[0, 0, rows, sl] = jnp.dot(p.astype(BF16), keys(vp_ref, vc_ref, blk, sl),
                                            preferred_element_type=F32)
            stats = jnp.where(stat_lane == h, mx, stats)
            stats = jnp.where(stat_lane == H_A + h, den, stats)
        st_ref[0, 0, rows, :] = stats


def _dilated(qkv, bias, pattern):
    b, dilation, n_sub, _ = qkv.shape
    rows = DIL_BLOCKS * BAND
    nb = n_sub // rows

    def cur(which):
        return pl.BlockSpec((1, 1, rows, D_A), lambda bi, r, i: (bi, r, i, which))

    def prev(which):
        return pl.BlockSpec((1, 1, BAND, D_A),
                            lambda bi, r, i: (bi, r, jnp.maximum(i * DIL_BLOCKS - 1, 0), which))

    return pl.pallas_call(
        _dilated_kernel,
        grid=(b, dilation, nb),
        in_specs=[
            cur(0), prev(1), cur(1), prev(2), cur(2),
            pl.BlockSpec((1, H_A, BAND, 2 * BAND), lambda bi, r, i: (pattern, 0, 0, 0)),
        ],
        out_specs=[
            pl.BlockSpec((1, 1, rows, D_A), lambda bi, r, i: (bi, r, i, 0)),
            pl.BlockSpec((1, 1, rows, LANES), lambda bi, r, i: (bi, r, i, 0)),
        ],
        out_shape=[
            jax.ShapeDtypeStruct((b, dilation, n_sub, D_A), F32),
            jax.ShapeDtypeStruct((b, dilation, n_sub, LANES), F32),
        ],
        compiler_params=_params("parallel", "parallel", "arbitrary"),
        name=f"dilated_{dilation}",
    )(qkv, qkv, qkv, qkv, qkv, bias)


COMBINE_ROWS = 512


def _combine_kernel(*refs):
    n_pat = len(DILATED_PATTERNS)
    num_refs, st_refs = refs[:n_pat], refs[n_pat:2 * n_pat]
    o_ref, num_buf, st_buf = refs[2 * n_pat:]
    for p in range(n_pat):
        dilation = num_refs[p].shape[1]
        for rc in range(dilation):
            rows = pl.ds(rc, COMBINE_ROWS // dilation, stride=dilation)
            st_buf[p, rows, :] = st_refs[p][0, rc]
            for h in range(H_A):
                num_buf[p, h, rows, :] = num_refs[p][0, rc, :, h * HEAD_DIM:(h + 1) * HEAD_DIM]
    stats = [st_buf[p] for p in range(n_pat)]
    for h in range(H_A):
        sl = slice(h * HEAD_DIM, (h + 1) * HEAD_DIM)
        mxs = [st[:, h:h + 1] for st in stats]
        dens = [st[:, H_A + h:H_A + h + 1] for st in stats]
        mx = functools.reduce(jnp.maximum, mxs)
        num = None
        den = None
        for p, (pden, pmx) in enumerate(zip(dens, mxs)):
            scale = jnp.exp(pmx - mx)
            pnum = num_buf[p, h]
            num = pnum * scale if num is None else num + pnum * scale
            den = pden * scale if den is None else den + pden * scale
        o_ref[:, sl] = (num / den).astype(BF16)


def _combine(nums, stats):
    b, _, s, _ = nums[0].shape
    tps = s // COMBINE_ROWS

    def spec(arr):
        dilation, width = arr.shape[1], arr.shape[3]
        return pl.BlockSpec((1, dilation, COMBINE_ROWS // dilation, width),
                            lambda i: (i // tps, 0, i % tps, 0))

    n_pat = len(nums)
    return pl.pallas_call(
        _combine_kernel,
        grid=(b * tps,),
        in_specs=[spec(a) for a in nums] + [spec(a) for a in stats],
        out_specs=pl.BlockSpec((COMBINE_ROWS, D_A), lambda i: (i, 0)),
        out_shape=jax.ShapeDtypeStruct((b * s, D_A), BF16),
        scratch_shapes=[
            pltpu.VMEM((n_pat, H_A, COMBINE_ROWS, HEAD_DIM), F32),
            pltpu.VMEM((n_pat, COMBINE_ROWS, LANES), F32),
        ],
        compiler_params=_params("parallel"),
        name="combine_a",
    )(*nums, *stats)


def _fox_kernel(q_ref, k_ref, vt_ref, o_ref, m_ref, l_ref, acc_ref):
    iq = pl.program_id(2)
    m_ref[...] = jnp.full(m_ref.shape, NEG_INF, F32)
    l_ref[...] = jnp.zeros(l_ref.shape, F32)
    acc_ref[...] = jnp.zeros(acc_ref.shape, F32)

    def step(kb, diagonal):
        start = pl.multiple_of(kb * FOX_TK, FOX_TK)
        scores = []
        for hh in range(FOX_HEADS):
            k = k_ref[0, hh, pl.ds(start, FOX_TK), :]
            scores.append(lax.dot_general(k, q_ref[0, hh], (((1,), (1,)), ((), ())),
                                          preferred_element_type=F32))
        for hh, st in enumerate(scores):
            if diagonal:
                key = lax.broadcasted_iota(jnp.int32, st.shape, 0)
                qry = lax.broadcasted_iota(jnp.int32, st.shape, 1)
                st = jnp.where(key <= qry, st, NEG_INF)
            m_prev = m_ref[hh]
            m_new = jnp.maximum(m_prev, jnp.max(st, axis=0, keepdims=True))
            alpha = jnp.exp2(m_prev - m_new)
            p = jnp.exp2(st - m_new)
            l_ref[hh] = alpha * l_ref[hh] + jnp.sum(p, axis=0, keepdims=True)
            vt = vt_ref[0, hh, :, pl.ds(start, FOX_TK)]
            acc_ref[hh] = alpha * acc_ref[hh] + jnp.dot(vt, p.astype(BF16),
                                                        preferred_element_type=F32)
            m_ref[hh] = m_new

    def body(kb, carry):
        step(kb, False)
        return carry

    lax.fori_loop(0, iq, body, 0)
    step(iq, True)
    for hh in range(FOX_HEADS):
        o_ref[0, :, hh * HEAD_DIM:(hh + 1) * HEAD_DIM] = (
            acc_ref[hh] / l_ref[hh]).T.astype(BF16)


def _fox(q_aug, k_aug, v_t):
    b, h_b, s, _ = q_aug.shape
    assert FOX_TQ == FOX_TK
    return pl.pallas_call(
        _fox_kernel,
        grid=(b, h_b // FOX_HEADS, s // FOX_TQ),
        in_specs=[
            pl.BlockSpec((1, FOX_HEADS, FOX_TQ, AUG), lambda bi, h, i: (bi, h, i, 0)),
            pl.BlockSpec((1, FOX_HEADS, s, AUG), lambda bi, h, i: (bi, h, 0, 0)),
            pl.BlockSpec((1, FOX_HEADS, HEAD_DIM, s), lambda bi, h, i: (bi, h, 0, 0)),
        ],
        out_specs=pl.BlockSpec((1, FOX_TQ, FOX_HEADS * HEAD_DIM), lambda bi, h, i: (bi, i, h)),
        out_shape=jax.ShapeDtypeStruct((b, s, h_b * HEAD_DIM), BF16),
        scratch_shapes=[
            pltpu.VMEM((FOX_HEADS, 1, FOX_TQ), F32),
            pltpu.VMEM((FOX_HEADS, 1, FOX_TQ), F32),
            pltpu.VMEM((FOX_HEADS, HEAD_DIM, FOX_TQ), F32),
        ],
        compiler_params=_params("parallel", "parallel", "arbitrary"),
        name="fox",
    )(q_aug, k_aug, v_t)


def _out_proj_kernel(a_ref, b_ref, w_ref, x_ref, o_ref):
    mixed = jnp.concatenate([a_ref[...], b_ref[...]], axis=-1)
    o_ref[...] = x_ref[...] + jnp.dot(mixed, w_ref[...].astype(BF16),
                                      preferred_element_type=F32)


def _out_proj(out_a, out_b, w_out, x):
    m = x.shape[0]
    return pl.pallas_call(
        _out_proj_kernel,
        grid=(m // ROW_TILE, D_MODEL // COL_TILE),
        in_specs=[
            pl.BlockSpec((ROW_TILE, D_A), lambda i, j: (i, 0)),
            pl.BlockSpec((ROW_TILE, D_B), lambda i, j: (i, 0)),
            pl.BlockSpec((D_A + D_B, COL_TILE), lambda i, j: (0, j)),
            pl.BlockSpec((ROW_TILE, COL_TILE), lambda i, j: (i, j)),
        ],
        out_specs=pl.BlockSpec((ROW_TILE, COL_TILE), lambda i, j: (i, j)),
        out_shape=jax.ShapeDtypeStruct((m, D_MODEL), F32),
        compiler_params=_params("parallel", "parallel"),
        name="out_proj",
    )(out_a, out_b, w_out, x)


def _mixer(x, b, s, mix_norm, w_in, layer, q_norm_a, k_norm_a, q_norm_b, k_norm_b, forget_bias,
           rel_bias_table, w_out):
    ones = jnp.ones((D_A,), F32)
    head_gain = jnp.concatenate([
        jnp.tile(q_norm_a * ATTN_SCALE, H_A), jnp.tile(k_norm_a, H_A), ones,
        jnp.tile(q_norm_b * (ATTN_SCALE * LOG2E), H_B), jnp.tile(k_norm_b, H_B), ones,
    ]).reshape(1, D_QKV)
    w_f = jnp.pad(w_in[layer, :, D_QKV:], ((0, 0), (0, LANES - H_B))).astype(BF16)
    f_bias = jnp.pad(forget_bias, (0, LANES - H_B)).reshape(1, LANES)
    *qkv_a, q_aug, k_aug, v_t = _proj(x, b, s, mix_norm, w_in, layer, w_f, f_bias, head_gain)

    out_b = _fox(q_aug, k_aug, v_t).reshape(b * s, D_B)

    bias = _band_bias(rel_bias_table)
    parts = [_dilated(qkv, bias, p) for p, qkv in enumerate(qkv_a)]
    out_a = _combine([pt[0] for pt in parts], [pt[1] for pt in parts])

    return _out_proj(out_a, out_b, w_out, x)


def kernel(x, ffn1_norm, ffn1_w_in, ffn1_w_out, mix_norm, w_in, q_norm_a, k_norm_a, q_norm_b,
           k_norm_b, forget_bias, rel_bias_table, w_out, ffn2_norm, ffn2_w_in, ffn2_w_out):
    b, s, d = x.shape
    depth = ffn1_norm.shape[0]
    x = x.reshape(b * s, d)
    for l in range(depth):
        x = _ffn(x, ffn1_norm[l], ffn1_w_in[l], ffn1_w_out[l])
        x = _mixer(x, b, s, mix_norm[l], w_in, l, q_norm_a[l], k_norm_a[l], q_norm_b[l],
                   k_norm_b[l], forget_bias[l], rel_bias_table, w_out[l])
        x = _ffn(x, ffn2_norm[l], ffn2_w_in[l], ffn2_w_out[l])
    return x.reshape(b, s, d)
```

```python
import functools
import math

import numpy as np
import jax
import jax.numpy as jnp
from jax import lax
from jax.experimental import pallas as pl
from jax.experimental.pallas import tpu as pltpu

D_MODEL = 2048
HEAD_DIM = 128
N_HEADS = D_MODEL // HEAD_DIM
H_A = N_HEADS // 2
H_B = N_HEADS - H_A
D_A = H_A * HEAD_DIM
D_B = H_B * HEAD_DIM
D_QKV = 3 * D_A + 3 * D_B
DILATED_PATTERNS = ((128, 1), (512, 4), (2048, 16))
BAND = 128
NUM_BUCKETS = 32
MAX_DISTANCE = 2048
D_FF = ((8 * D_MODEL // 3 + 127) // 128) * 128
RMS_EPS = 1e-6
NEG_INF = -1e30
ATTN_SCALE = HEAD_DIM ** -0.5

LANES = 128
FF_TILE = 512
FF_TAIL = D_FF - (D_FF // FF_TILE) * FF_TILE
FFN_OUT_TILE = 256
ROW_TILE = 1024
COL_TILE = 512
FOX_TQ = 512
FOX_TK = 512
FOX_HEADS = 4
VMEM_LIMIT = 56 * 1024 * 1024

F32 = jnp.float32
BF16 = jnp.bfloat16


def _params(*sem):
    return pltpu.CompilerParams(dimension_semantics=sem, vmem_limit_bytes=VMEM_LIMIT)


def _rms_rows(x, gain):
    ms = jnp.mean(x * x, axis=-1, keepdims=True)
    return x * lax.rsqrt(ms + RMS_EPS) * gain


def _ffn_in_kernel(x_ref, g_ref, wg_ref, wu_ref, act_ref, h_ref):
    @pl.when(pl.program_id(1) == 0)
    def _():
        h_ref[...] = _rms_rows(x_ref[...], g_ref[...]).astype(BF16)

    h = h_ref[...]
    gate = jnp.dot(h, wg_ref[...].astype(BF16), preferred_element_type=F32)
    up = jnp.dot(h, wu_ref[...].astype(BF16), preferred_element_type=F32)
    act = (gate * (1.0 / (1.0 + jnp.exp(-gate))) * up).astype(BF16)

    j = pl.program_id(1)
    last = pl.num_programs(1) - 1

    @pl.when(j < last)
    def _():
        act_ref[...] = act

    @pl.when(j == last)
    def _():
        act_ref[:, :FF_TAIL] = act[:, FF_TILE - FF_TAIL:]


def _ffn_out_kernel(act_ref, w_ref, x_ref, o_ref):
    y = jnp.dot(act_ref[...], w_ref[...].astype(BF16), preferred_element_type=F32)
    o_ref[...] = x_ref[...] + 0.5 * y


def _ffn(x, gain, w_in, w_out):
    m = x.shape[0]
    n_ff = pl.cdiv(D_FF, FF_TILE)

    def ff_start(j):
        return pl.multiple_of(jnp.minimum(j * FF_TILE, D_FF - FF_TILE), LANES)

    act = pl.pallas_call(
        _ffn_in_kernel,
        grid=(m // ROW_TILE, n_ff),
        in_specs=[
            pl.BlockSpec((ROW_TILE, D_MODEL), lambda i, j: (i, 0)),
            pl.BlockSpec((1, D_MODEL), lambda i, j: (0, 0)),
            pl.BlockSpec((pl.Element(D_MODEL), pl.Element(FF_TILE)),
                         lambda i, j: (0, ff_start(j))),
            pl.BlockSpec((pl.Element(D_MODEL), pl.Element(FF_TILE)),
                         lambda i, j: (0, pl.multiple_of(D_FF + ff_start(j), LANES))),
        ],
        out_specs=pl.BlockSpec((ROW_TILE, FF_TILE), lambda i, j: (i, j)),
        out_shape=jax.ShapeDtypeStruct((m, D_FF), BF16),
        scratch_shapes=[pltpu.VMEM((ROW_TILE, D_MODEL), BF16)],
        compiler_params=_params("parallel", "arbitrary"),
        name="ffn_in",
    )(x, gain.reshape(1, D_MODEL), w_in, w_in)

    return pl.pallas_call(
        _ffn_out_kernel,
        grid=(m // ROW_TILE, D_MODEL // FFN_OUT_TILE),
        in_specs=[
            pl.BlockSpec((ROW_TILE, D_FF), lambda i, j: (i, 0)),
            pl.BlockSpec((D_FF, FFN_OUT_TILE), lambda i, j: (0, j)),
            pl.BlockSpec((ROW_TILE, FFN_OUT_TILE), lambda i, j: (i, j)),
        ],
        out_specs=pl.BlockSpec((ROW_TILE, FFN_OUT_TILE), lambda i, j: (i, j)),
        out_shape=jax.ShapeDtypeStruct((m, D_MODEL), F32),
        compiler_params=_params("parallel", "parallel"),
        name="ffn_out",
    )(act, w_out, x)


HEADS_PER_TILE = COL_TILE // HEAD_DIM
N_PROJ_TILES = D_QKV // COL_TILE
TILES_PER_GROUP = D_A // COL_TILE
A_TILES = 3 * TILES_PER_GROUP
AUG = 2 * HEAD_DIM
LOG2E = math.log2(math.e)


def _log_gate_scan(z, carry):
    c = (jnp.minimum(z, 0.0) - jnp.log1p(jnp.exp(-jnp.abs(z)))) * LOG2E
    row = lax.broadcasted_iota(jnp.int32, c.shape, 0)
    shift = 1
    while shift < c.shape[0]:
        c = c + jnp.where(row >= shift, pltpu.roll(c, shift, axis=0), 0.0)
        shift *= 2
    return c + carry


def _split3(c):
    hi = c.astype(BF16)
    rest = c - hi.astype(F32)
    mid = rest.astype(BF16)
    lo = (rest - mid.astype(F32)).astype(BF16)
    return hi, mid, lo


def _head_mean_matrix():
    head = np.arange(COL_TILE) // HEAD_DIM
    return jnp.asarray((head[:, None] == head[None, :]) / HEAD_DIM, BF16)


def _gate_placement_matrices():
    mats = np.zeros((2, TILES_PER_GROUP, COL_TILE, COL_TILE), np.float32)
    ones_row = 3 * LANES
    for part in range(TILES_PER_GROUP):
        for hh in range(HEADS_PER_TILE):
            head = part * HEADS_PER_TILE + hh
            col = hh * HEAD_DIM
            for term in range(3):
                mats[0, part, term * LANES + head, col + term] = 1.0
                mats[0, part, ones_row, col + 3 + term] = 1.0
                mats[1, part, term * LANES + head, col + 3 + term] = -1.0
                mats[1, part, ones_row, col + term] = 1.0
    return jnp.asarray(mats.reshape(2 * TILES_PER_GROUP, COL_TILE, COL_TILE), BF16)


def _proj_kernel(x_ref, g_ref, w_ref, wf_ref, fb_ref, hg_ref, mean_ref, place_ref,
                 a1_ref, a4_ref, a16_ref, qb_ref, kb_ref, vt_ref,
                 h_ref, cs_ref, carry_ref, y_ref, *, tiles_per_seq):
    i = pl.program_id(0)
    j = pl.program_id(1)

    @pl.when(j == 0)
    def _():
        h_ref[...] = _rms_rows(x_ref[...], g_ref[...]).astype(BF16)

        @pl.when(i % tiles_per_seq == 0)
        def _():
            carry_ref[...] = jnp.zeros(carry_ref.shape, F32)

        z = jnp.dot(h_ref[...], wf_ref[...], preferred_element_type=F32) + fb_ref[...]
        c = _log_gate_scan(z, carry_ref[...])
        carry_ref[...] = c[ROW_TILE - 1:ROW_TILE, :]
        for term, part in enumerate(_split3(c)):
            cs_ref[:, term * LANES:(term + 1) * LANES] = part
        cs_ref[:, 3 * LANES:] = jnp.ones((ROW_TILE, LANES), BF16)

    r = lax.dot_general(h_ref[...], w_ref[...].astype(BF16), (((1,), (1,)), ((), ())),
                        preferred_element_type=F32)

    def normed():
        ms = jnp.dot((r * r).astype(BF16), mean_ref[...], preferred_element_type=F32)
        return r * lax.rsqrt(ms + RMS_EPS) * hg_ref[...]

    def store_dilated(y):
        a1_ref[0, 0] = y.astype(BF16)
        for c in range(HEADS_PER_TILE):
            sl = slice(c * LANES, (c + 1) * LANES)
            y_ref[c] = y[:, sl]
            for ref in (a4_ref, a16_ref):
                dilation = ref.shape[1]
                for rc in range(dilation):
                    rows = y_ref[c, pl.ds(rc, ROW_TILE // dilation, stride=dilation), :]
                    ref[0, rc, :, sl] = rows.astype(BF16)

    for t in range(N_PROJ_TILES):
        group, part = divmod(t, TILES_PER_GROUP)

        @pl.when(j == t)
        def _(group=group):
            if group in (0, 1):
                store_dilated(normed())
            elif group == 2:
                store_dilated(r)
            elif group in (3, 4):
                out = qb_ref if group == 3 else kb_ref
                y = normed().astype(BF16)
                aug = jnp.dot(cs_ref[...], place_ref[0],
                              preferred_element_type=F32).astype(BF16)
                for hh in range(HEADS_PER_TILE):
                    sl = slice(hh * HEAD_DIM, (hh + 1) * HEAD_DIM)
                    out[0, hh, :, :HEAD_DIM] = y[:, sl]
                    out[0, hh, :, HEAD_DIM:] = aug[:, sl]
            else:
                for hh in range(HEADS_PER_TILE):
                    vt_ref[0, hh] = r[:, hh * HEAD_DIM:(hh + 1) * HEAD_DIM].T.astype(BF16)


def _proj(x, b, s, gain, w_in, layer, w_f, f_bias, head_gain):
    m = x.shape[0]
    tps = s // ROW_TILE

    def a_spec(dilation):
        return pl.BlockSpec((1, dilation, ROW_TILE // dilation, COL_TILE),
                            lambda i, j: (i // tps, 0, i % tps, jnp.minimum(j, A_TILES - 1)))

    def head_block(first):
        return lambda i, j: (i // tps, jnp.clip(j - first, 0, TILES_PER_GROUP - 1), i % tps, 0)

    qb_first = A_TILES
    vt_first = A_TILES + 2 * TILES_PER_GROUP
    dilations = [d for _, d in DILATED_PATTERNS]
    return pl.pallas_call(
        functools.partial(_proj_kernel, tiles_per_seq=tps),
        grid=(m // ROW_TILE, N_PROJ_TILES),
        in_specs=[
            pl.BlockSpec((ROW_TILE, D_MODEL), lambda i, j: (i, 0)),
            pl.BlockSpec((1, D_MODEL), lambda i, j: (0, 0)),
            pl.BlockSpec((None, COL_TILE, D_MODEL), lambda i, j: (layer, j, 0)),
            pl.BlockSpec((D_MODEL, LANES), lambda i, j: (0, 0)),
            pl.BlockSpec((1, LANES), lambda i, j: (0, 0)),
            pl.BlockSpec((1, COL_TILE), lambda i, j: (0, j)),
            pl.BlockSpec((COL_TILE, COL_TILE), lambda i, j: (0, 0)),
            pl.BlockSpec((1, COL_TILE, COL_TILE),
                         lambda i, j: (jnp.clip(j - qb_first, 0, 2 * TILES_PER_GROUP - 1), 0, 0)),
        ],
        out_specs=[a_spec(d) for d in dilations] + [
            pl.BlockSpec((1, HEADS_PER_TILE, ROW_TILE, AUG), head_block(qb_first)),
            pl.BlockSpec((1, HEADS_PER_TILE, ROW_TILE, AUG),
                         head_block(qb_first + TILES_PER_GROUP)),
            pl.BlockSpec((1, HEADS_PER_TILE, HEAD_DIM, ROW_TILE),
                         lambda i, j: (i // tps, jnp.clip(j - vt_first, 0, TILES_PER_GROUP - 1),
                                       0, i % tps)),
        ],
        out_shape=[jax.ShapeDtypeStruct((b, d, s // d, 3 * D_A), BF16) for d in dilations] + [
            jax.ShapeDtypeStruct((b, H_B, s, AUG), BF16),
            jax.ShapeDtypeStruct((b, H_B, s, AUG), BF16),
            jax.ShapeDtypeStruct((b, H_B, HEAD_DIM, s), BF16),
        ],
        scratch_shapes=[
            pltpu.VMEM((ROW_TILE, D_MODEL), BF16),
            pltpu.VMEM((ROW_TILE, 4 * LANES), BF16),
            pltpu.VMEM((1, LANES), F32),
            pltpu.VMEM((HEADS_PER_TILE, ROW_TILE, LANES), F32),
        ],
        compiler_params=_params("arbitrary", "arbitrary"),
        name="proj",
    )(x, gain.reshape(1, D_MODEL), jnp.swapaxes(w_in, 1, 2), w_f, f_bias, head_gain,
      _head_mean_matrix(), _gate_placement_matrices())


def _bucket_steps(dilation):
    dist = np.arange(BAND + 1) * dilation
    max_exact = NUM_BUCKETS // 2
    large = max_exact + np.floor(
        np.log(np.maximum(dist, 1) / max_exact) / math.log(MAX_DISTANCE / max_exact)
        * (NUM_BUCKETS - max_exact)).astype(np.int64)
    bucket = np.where(dist < max_exact, dist, np.minimum(large, NUM_BUCKETS - 1))
    steps = [(0, int(bucket[0]))]
    for delta in range(1, BAND + 1):
        if bucket[delta] != bucket[delta - 1]:
            steps.append((delta, int(bucket[delta])))
    return steps


def _bias_kernel(table_ref, o_ref):
    iq = lax.broadcasted_iota(jnp.int32, (BAND, 2 * BAND), 0)
    ik = lax.broadcasted_iota(jnp.int32, (BAND, 2 * BAND), 1)
    delta = iq + BAND - ik
    in_band = jnp.logical_and(delta >= 0, delta <= BAND)
    for p, (_, dilation) in enumerate(DILATED_PATTERNS):
        steps = _bucket_steps(dilation)
        for h in range(H_A):
            val = jnp.full((BAND, 2 * BAND), table_ref[steps[0][1], h], F32)
            for start, bucket in steps[1:]:
                val = jnp.where(delta >= start, table_ref[bucket, h], val)
            o_ref[p, h] = jnp.where(in_band, val, NEG_INF)


def _band_bias(rel_table):
    n_pat = len(DILATED_PATTERNS)
    return pl.pallas_call(
        _bias_kernel,
        in_specs=[pl.BlockSpec(memory_space=pltpu.SMEM)],
        out_specs=pl.BlockSpec(memory_space=pltpu.VMEM),
        out_shape=jax.ShapeDtypeStruct((n_pat, H_A, BAND, 2 * BAND), F32),
        name="band_bias",
    )(rel_table)


DIL_BLOCKS = 2


def _dilated_kernel(q_ref, kp_ref, kc_ref, vp_ref, vc_ref, bias_ref, o_ref, st_ref):
    first = pl.program_id(2) == 0
    key_lane = lax.broadcasted_iota(jnp.int32, (1, 2 * BAND), 1)
    no_prev = jnp.where(jnp.logical_and(first, key_lane < BAND), NEG_INF, 0.0)
    stat_lane = lax.broadcasted_iota(jnp.int32, (BAND, LANES), 1)

    def keys(prev_ref, cur_ref, blk, sl):
        if blk == 0:
            return jnp.concatenate([prev_ref[0, 0, :, sl], cur_ref[0, 0, :BAND, sl]], axis=0)
        return cur_ref[0, 0, (blk - 1) * BAND:(blk + 1) * BAND, sl]

    scores = {}
    for blk in range(DIL_BLOCKS):
        for h in range(H_A):
            sl = slice(h * HEAD_DIM, (h + 1) * HEAD_DIM)
            q = q_ref[0, 0, blk * BAND:(blk + 1) * BAND, sl]
            scores[blk, h] = lax.dot_general(q, keys(kp_ref, kc_ref, blk, sl),
                                             (((1,), (1,)), ((), ())),
                                             preferred_element_type=F32)
    for blk in range(DIL_BLOCKS):
        rows = slice(blk * BAND, (blk + 1) * BAND)
        stats = jnp.zeros((BAND, LANES), F32)
        for h in range(H_A):
            sl = slice(h * HEAD_DIM, (h + 1) * HEAD_DIM)
            s = scores[blk, h] + bias_ref[0, h]
            if blk == 0:
                s = s + no_prev
            mx = jnp.max(s, axis=-1, keepdims=True)
            p = jnp.exp(s - mx)
            den = jnp.sum(p, axis=-1, keepdims=True)
            o_ref[0, 0, rows, sl] = jnp.dot(p.astype(BF16), keys(vp_ref, vc_ref, blk, sl),
                                            preferred_element_type=F32)
            stats = jnp.where(stat_lane == h, mx, stats)
            stats = jnp.where(stat_lane == H_A + h, den, stats)
        st_ref[0, 0, rows, :] = stats


def _dilated(qkv, bias, pattern):
    b, dilation, n_sub, _ = qkv.shape
    rows = DIL_BLOCKS * BAND
    nb = n_sub // rows

    def cur(which):
        return pl.BlockSpec((1, 1, rows, D_A), lambda bi, r, i: (bi, r, i, which))

    def prev(which):
        return pl.BlockSpec((1, 1, BAND, D_A),
                            lambda bi, r, i: (bi, r, jnp.maximum(i * DIL_BLOCKS - 1, 0), which))

    return pl.pallas_call(
        _dilated_kernel,
        grid=(b, dilation, nb),
        in_specs=[
            cur(0), prev(1), cur(1), prev(2), cur(2),
            pl.BlockSpec((1, H_A, BAND, 2 * BAND), lambda bi, r, i: (pattern, 0, 0, 0)),
        ],
        out_specs=[
            pl.BlockSpec((1, 1, rows, D_A), lambda bi, r, i: (bi, r, i, 0)),
            pl.BlockSpec((1, 1, rows, LANES), lambda bi, r, i: (bi, r, i, 0)),
        ],
        out_shape=[
            jax.ShapeDtypeStruct((b, dilation, n_sub, D_A), F32),
            jax.ShapeDtypeStruct((b, dilation, n_sub, LANES), F32),
        ],
        compiler_params=_params("parallel", "parallel", "arbitrary"),
        name=f"dilated_{dilation}",
    )(qkv, qkv, qkv, qkv, qkv, bias)


COMBINE_ROWS = 512


def _combine_kernel(*refs):
    n_pat = len(DILATED_PATTERNS)
    num_refs, st_refs = refs[:n_pat], refs[n_pat:2 * n_pat]
    o_ref, num_buf, st_buf = refs[2 * n_pat:]
    for p in range(n_pat):
        dilation = num_refs[p].shape[1]
        for rc in range(dilation):
            rows = pl.ds(rc, COMBINE_ROWS // dilation, stride=dilation)
            st_buf[p, rows, :] = st_refs[p][0, rc]
            for h in range(H_A):
                num_buf[p, h, rows, :] = num_refs[p][0, rc, :, h * HEAD_DIM:(h + 1) * HEAD_DIM]
    stats = [st_buf[p] for p in range(n_pat)]
    for h in range(H_A):
        sl = slice(h * HEAD_DIM, (h + 1) * HEAD_DIM)
        mxs = [st[:, h:h + 1] for st in stats]
        dens = [st[:, H_A + h:H_A + h + 1] for st in stats]
        mx = functools.reduce(jnp.maximum, mxs)
        num = None
        den = None
        for p, (pden, pmx) in enumerate(zip(dens, mxs)):
            scale = jnp.exp(pmx - mx)
            pnum = num_buf[p, h]
            num = pnum * scale if num is None else num + pnum * scale
            den = pden * scale if den is None else den + pden * scale
        o_ref[:, sl] = (num / den).astype(BF16)


def _combine(nums, stats):
    b, _, s, _ = nums[0].shape
    tps = s // COMBINE_ROWS

    def spec(arr):
        dilation, width = arr.shape[1], arr.shape[3]
        return pl.BlockSpec((1, dilation, COMBINE_ROWS // dilation, width),
                            lambda i: (i // tps, 0, i % tps, 0))

    n_pat = len(nums)
    return pl.pallas_call(
        _combine_kernel,
        grid=(b * tps,),
        in_specs=[spec(a) for a in nums] + [spec(a) for a in stats],
        out_specs=pl.BlockSpec((COMBINE_ROWS, D_A), lambda i: (i, 0)),
        out_shape=jax.ShapeDtypeStruct((b * s, D_A), BF16),
        scratch_shapes=[
            pltpu.VMEM((n_pat, H_A, COMBINE_ROWS, HEAD_DIM), F32),
            pltpu.VMEM((n_pat, COMBINE_ROWS, LANES), F32),
        ],
        compiler_params=_params("parallel"),
        name="combine_a",
    )(*nums, *stats)


def _fox_kernel(q_ref, k_ref, vt_ref, o_ref, m_ref, l_ref, acc_ref, sa_ref, sb_ref):
    iq = pl.program_id(2)
    m_ref[...] = jnp.full(m_ref.shape, NEG_INF, F32)
    l_ref[...] = jnp.zeros(l_ref.shape, F32)
    acc_ref[...] = jnp.zeros(acc_ref.shape, F32)

    def scores_into(s_ref, kb):
        start = pl.multiple_of(kb * FOX_TK, FOX_TK)
        for hh in range(FOX_HEADS):
            k = k_ref[0, hh, pl.ds(start, FOX_TK), :]
            s_ref[hh] = lax.dot_general(k, q_ref[0, hh], (((1,), (1,)), ((), ())),
                                        preferred_element_type=F32)

    def accumulate(s_ref, kb, diagonal):
        start = pl.multiple_of(kb * FOX_TK, FOX_TK)
        for hh in range(FOX_HEADS):
            st = s_ref[hh]
            if diagonal:
                key = lax.broadcasted_iota(jnp.int32, st.shape, 0)
                qry = lax.broadcasted_iota(jnp.int32, st.shape, 1)
                st = jnp.where(key <= qry, st, NEG_INF)
            m_prev = m_ref[hh]
            m_new = jnp.maximum(m_prev, jnp.max(st, axis=0, keepdims=True))
            alpha = jnp.exp2(m_prev - m_new)
            p = jnp.exp2(st - m_new)
            l_ref[hh] = alpha * l_ref[hh] + jnp.sum(p, axis=0, keepdims=True)
            vt = vt_ref[0, hh, :, pl.ds(start, FOX_TK)]
            acc_ref[hh] = alpha * acc_ref[hh] + jnp.dot(vt, p.astype(BF16),
                                                        preferred_element_type=F32)
            m_ref[hh] = m_new

    scores_into(sa_ref, 0)

    def pair(p, carry):
        scores_into(sb_ref, 2 * p + 1)
        accumulate(sa_ref, 2 * p, False)
        scores_into(sa_ref, 2 * p + 2)
        accumulate(sb_ref, 2 * p + 1, False)
        return carry

    lax.fori_loop(0, iq // 2, pair, 0)

    @pl.when(iq % 2 == 0)
    def _():
        accumulate(sa_ref, iq, True)

    @pl.when(iq % 2 == 1)
    def _():
        scores_into(sb_ref, iq)
        accumulate(sa_ref, iq - 1, False)
        accumulate(sb_ref, iq, True)
    for hh in range(FOX_HEADS):
        o_ref[0, :, hh * HEAD_DIM:(hh + 1) * HEAD_DIM] = (
            acc_ref[hh] / l_ref[hh]).T.astype(BF16)


def _fox(q_aug, k_aug, v_t):
    b, h_b, s, _ = q_aug.shape
    assert FOX_TQ == FOX_TK
    return pl.pallas_call(
        _fox_kernel,
        grid=(b, h_b // FOX_HEADS, s // FOX_TQ),
        in_specs=[
            pl.BlockSpec((1, FOX_HEADS, FOX_TQ, AUG), lambda bi, h, i: (bi, h, i, 0)),
            pl.BlockSpec((1, FOX_HEADS, s, AUG), lambda bi, h, i: (bi, h, 0, 0)),
            pl.BlockSpec((1, FOX_HEADS, HEAD_DIM, s), lambda bi, h, i: (bi, h, 0, 0)),
        ],
        out_specs=pl.BlockSpec((1, FOX_TQ, FOX_HEADS * HEAD_DIM), lambda bi, h, i: (bi, i, h)),
        out_shape=jax.ShapeDtypeStruct((b, s, h_b * HEAD_DIM), BF16),
        scratch_shapes=[
            pltpu.VMEM((FOX_HEADS, 1, FOX_TQ), F32),
            pltpu.VMEM((FOX_HEADS, 1, FOX_TQ), F32),
            pltpu.VMEM((FOX_HEADS, HEAD_DIM, FOX_TQ), F32),
            pltpu.VMEM((FOX_HEADS, FOX_TK, FOX_TQ), F32),
            pltpu.VMEM((FOX_HEADS, FOX_TK, FOX_TQ), F32),
        ],
        compiler_params=_params("parallel", "parallel", "arbitrary"),
        name="fox",
    )(q_aug, k_aug, v_t)


def _out_proj_kernel(a_ref, b_ref, w_ref, x_ref, o_ref):
    mixed = jnp.concatenate([a_ref[...], b_ref[...]], axis=-1)
    o_ref[...] = x_ref[...] + jnp.dot(mixed, w_ref[...].astype(BF16),
                                      preferred_element_type=F32)


OUT_ROW_TILE = 2048


def _out_proj(out_a, out_b, w_out, x):
    m = x.shape[0]
    return pl.pallas_call(
        _out_proj_kernel,
        grid=(m // OUT_ROW_TILE, D_MODEL // COL_TILE),
        in_specs=[
            pl.BlockSpec((OUT_ROW_TILE, D_A), lambda i, j: (i, 0)),
            pl.BlockSpec((OUT_ROW_TILE, D_B), lambda i, j: (i, 0)),
            pl.BlockSpec((D_A + D_B, COL_TILE), lambda i, j: (0, j)),
            pl.BlockSpec((OUT_ROW_TILE, COL_TILE), lambda i, j: (i, j)),
        ],
        out_specs=pl.BlockSpec((OUT_ROW_TILE, COL_TILE), lambda i, j: (i, j)),
        out_shape=jax.ShapeDtypeStruct((m, D_MODEL), F32),
        compiler_params=_params("parallel", "parallel"),
        name="out_proj",
    )(out_a, out_b, w_out, x)


def _mixer(x, b, s, mix_norm, w_in, layer, q_norm_a, k_norm_a, q_norm_b, k_norm_b, forget_bias,
           rel_bias_table, w_out):
    ones = jnp.ones((D_A,), F32)
    head_gain = jnp.concatenate([
        jnp.tile(q_norm_a * ATTN_SCALE, H_A), jnp.tile(k_norm_a, H_A), ones,
        jnp.tile(q_norm_b * (ATTN_SCALE * LOG2E), H_B), jnp.tile(k_norm_b, H_B), ones,
    ]).reshape(1, D_QKV)
    w_f = jnp.pad(w_in[layer, :, D_QKV:], ((0, 0), (0, LANES - H_B))).astype(BF16)
    f_bias = jnp.pad(forget_bias, (0, LANES - H_B)).reshape(1, LANES)
    *qkv_a, q_aug, k_aug, v_t = _proj(x, b, s, mix_norm, w_in, layer, w_f, f_bias, head_gain)

    out_b = _fox(q_aug, k_aug, v_t).reshape(b * s, D_B)

    bias = _band_bias(rel_bias_table)
    parts = [_dilated(qkv, bias, p) for p, qkv in enumerate(qkv_a)]
    out_a = _combine([pt[0] for pt in parts], [pt[1] for pt in parts])

    return _out_proj(out_a, out_b, w_out, x)


def kernel(x, ffn1_norm, ffn1_w_in, ffn1_w_out, mix_norm, w_in, q_norm_a, k_norm_a, q_norm_b,
           k_norm_b, forget_bias, rel_bias_table, w_out, ffn2_norm, ffn2_w_in, ffn2_w_out):
    b, s, d = x.shape
    depth = ffn1_norm.shape[0]
    x = x.reshape(b * s, d)
    for l in range(depth):
        x = _ffn(x, ffn1_norm[l], ffn1_w_in[l], ffn1_w_out[l])
        x = _mixer(x, b, s, mix_norm[l], w_in, l, q_norm_a[l], k_norm_a[l], q_norm_b[l],
                   k_norm_b[l], forget_bias[l], rel_bias_table, w_out[l])
        x = _ffn(x, ffn2_norm[l], ffn2_w_in[l], ffn2_w_out[l])
    return x.reshape(b, s, d)
```

```python
import functools
import math

import numpy as np
import jax
import jax.numpy as jnp
from jax import lax
from jax.experimental import pallas as pl
from jax.experimental.pallas import tpu as pltpu

D_MODEL = 2048
HEAD_DIM = 128
N_HEADS = D_MODEL // HEAD_DIM
H_A = N_HEADS // 2
H_B = N_HEADS - H_A
D_A = H_A * HEAD_DIM
D_B = H_B * HEAD_DIM
D_QKV = 3 * D_A + 3 * D_B
DILATED_PATTERNS = ((128, 1), (512, 4), (2048, 16))
BAND = 128
NUM_BUCKETS = 32
MAX_DISTANCE = 2048
D_FF = ((8 * D_MODEL // 3 + 127) // 128) * 128
RMS_EPS = 1e-6
NEG_INF = -1e30
ATTN_SCALE = HEAD_DIM ** -0.5

LANES = 128
FF_TILE = 512
FF_TAIL = D_FF - (D_FF // FF_TILE) * FF_TILE
FFN_OUT_TILE = 256
ROW_TILE = 1024
COL_TILE = 512
FOX_TQ = 512
FOX_TK = 512
FOX_HEADS = 4
VMEM_LIMIT = 56 * 1024 * 1024

F32 = jnp.float32
BF16 = jnp.bfloat16


def _params(*sem):
    return pltpu.CompilerParams(dimension_semantics=sem, vmem_limit_bytes=VMEM_LIMIT)


def _rms_rows(x, gain):
    ms = jnp.mean(x * x, axis=-1, keepdims=True)
    return x * lax.rsqrt(ms + RMS_EPS) * gain


def _ffn_kernel(x_ref, g_ref, wg_ref, wu_ref, wo_ref, o_ref, h_ref):
    j = pl.program_id(1)

    @pl.when(j == 0)
    def _():
        x = x_ref[...]
        h_ref[...] = _rms_rows(x, g_ref[...]).astype(BF16)
        o_ref[...] = x

    h = h_ref[...]
    gate = jnp.dot(h, wg_ref[...].astype(BF16), preferred_element_type=F32)
    up = jnp.dot(h, wu_ref[...].astype(BF16), preferred_element_type=F32)
    act = gate * (1.0 / (1.0 + jnp.exp(-gate))) * (0.5 * up)
    col = lax.broadcasted_iota(jnp.int32, (1, FF_TILE), 1)
    repeated = j * FF_TILE - jnp.minimum(j * FF_TILE, D_FF - FF_TILE)
    act = jnp.where(col >= repeated, act, 0.0).astype(BF16)
    for c in range(D_MODEL // COL_TILE):
        sl = slice(c * COL_TILE, (c + 1) * COL_TILE)
        o_ref[:, sl] += jnp.dot(act, wo_ref[:, sl].astype(BF16), preferred_element_type=F32)


def _ffn(x, gain, w_in, w_out):
    m = x.shape[0]
    n_ff = pl.cdiv(D_FF, FF_TILE)

    def ff_start(j):
        return pl.multiple_of(jnp.minimum(j * FF_TILE, D_FF - FF_TILE), LANES)

    return pl.pallas_call(
        _ffn_kernel,
        grid=(m // ROW_TILE, n_ff),
        in_specs=[
            pl.BlockSpec((ROW_TILE, D_MODEL), lambda i, j: (i, 0), pipeline_mode=pl.Buffered(1)),
            pl.BlockSpec((1, D_MODEL), lambda i, j: (0, 0)),
            pl.BlockSpec((pl.Element(D_MODEL), pl.Element(FF_TILE)),
                         lambda i, j: (0, ff_start(j))),
            pl.BlockSpec((pl.Element(D_MODEL), pl.Element(FF_TILE)),
                         lambda i, j: (0, pl.multiple_of(D_FF + ff_start(j), LANES))),
            pl.BlockSpec((pl.Element(FF_TILE), pl.Element(D_MODEL)),
                         lambda i, j: (ff_start(j), 0)),
        ],
        out_specs=pl.BlockSpec((ROW_TILE, D_MODEL), lambda i, j: (i, 0),
                               pipeline_mode=pl.Buffered(1)),
        out_shape=jax.ShapeDtypeStruct((m, D_MODEL), F32),
        scratch_shapes=[pltpu.VMEM((ROW_TILE, D_MODEL), BF16)],
        compiler_params=_params("parallel", "arbitrary"),
        name="ffn",
    )(x, gain.reshape(1, D_MODEL), w_in, w_in, w_out)


HEADS_PER_TILE = COL_TILE // HEAD_DIM
N_PROJ_TILES = D_QKV // COL_TILE
TILES_PER_GROUP = D_A // COL_TILE
A_TILES = 3 * TILES_PER_GROUP
AUG = 2 * HEAD_DIM
LOG2E = math.log2(math.e)


def _log_gate_scan(z, carry):
    c = (jnp.minimum(z, 0.0) - jnp.log1p(jnp.exp(-jnp.abs(z)))) * LOG2E
    row = lax.broadcasted_iota(jnp.int32, c.shape, 0)
    shift = 1
    while shift < c.shape[0]:
        c = c + jnp.where(row >= shift, pltpu.roll(c, shift, axis=0), 0.0)
        shift *= 2
    return c + carry


def _split3(c):
    hi = c.astype(BF16)
    rest = c - hi.astype(F32)
    mid = rest.astype(BF16)
    lo = (rest - mid.astype(F32)).astype(BF16)
    return hi, mid, lo


def _head_mean_matrix():
    head = np.arange(COL_TILE) // HEAD_DIM
    return jnp.asarray((head[:, None] == head[None, :]) / HEAD_DIM, BF16)


def _gate_placement_matrices():
    mats = np.zeros((2, TILES_PER_GROUP, COL_TILE, COL_TILE), np.float32)
    ones_row = 3 * LANES
    for part in range(TILES_PER_GROUP):
        for hh in range(HEADS_PER_TILE):
            head = part * HEADS_PER_TILE + hh
            col = hh * HEAD_DIM
            for term in range(3):
                mats[0, part, term * LANES + head, col + term] = 1.0
                mats[0, part, ones_row, col + 3 + term] = 1.0
                mats[1, part, term * LANES + head, col + 3 + term] = -1.0
                mats[1, part, ones_row, col + term] = 1.0
    return jnp.asarray(mats.reshape(2 * TILES_PER_GROUP, COL_TILE, COL_TILE), BF16)


def _proj_kernel(x_ref, g_ref, w_ref, wf_ref, fb_ref, hg_ref, mean_ref, place_ref,
                 a1_ref, a4_ref, a16_ref, qb_ref, kb_ref, vt_ref,
                 h_ref, cs_ref, carry_ref, y_ref, *, tiles_per_seq):
    i = pl.program_id(0)
    j = pl.program_id(1)

    @pl.when(j == 0)
    def _():
        h_ref[...] = _rms_rows(x_ref[...], g_ref[...]).astype(BF16)

        @pl.when(i % tiles_per_seq == 0)
        def _():
            carry_ref[...] = jnp.zeros(carry_ref.shape, F32)

        z = jnp.dot(h_ref[...], wf_ref[...], preferred_element_type=F32) + fb_ref[...]
        c = _log_gate_scan(z, carry_ref[...])
        carry_ref[...] = c[ROW_TILE - 1:ROW_TILE, :]
        for term, part in enumerate(_split3(c)):
            cs_ref[:, term * LANES:(term + 1) * LANES] = part
        cs_ref[:, 3 * LANES:] = jnp.ones((ROW_TILE, LANES), BF16)

    r = lax.dot_general(h_ref[...], w_ref[...].astype(BF16), (((1,), (1,)), ((), ())),
                        preferred_element_type=F32)

    def normed():
        ms = jnp.dot((r * r).astype(BF16), mean_ref[...], preferred_element_type=F32)
        return r * lax.rsqrt(ms + RMS_EPS) * hg_ref[...]

    def store_dilated(y):
        a1_ref[0, 0] = y.astype(BF16)
        for c in range(HEADS_PER_TILE):
            sl = slice(c * LANES, (c + 1) * LANES)
            y_ref[c] = y[:, sl]
            for ref in (a4_ref, a16_ref):
                dilation = ref.shape[1]
                for rc in range(dilation):
                    rows = y_ref[c, pl.ds(rc, ROW_TILE // dilation, stride=dilation), :]
                    ref[0, rc, :, sl] = rows.astype(BF16)

    for t in range(N_PROJ_TILES):
        group, part = divmod(t, TILES_PER_GROUP)

        @pl.when(j == t)
        def _(group=group):
            if group in (0, 1):
                store_dilated(normed())
            elif group == 2:
                store_dilated(r)
            elif group in (3, 4):
                out = qb_ref if group == 3 else kb_ref
                y = normed().astype(BF16)
                aug = jnp.dot(cs_ref[...], place_ref[0],
                              preferred_element_type=F32).astype(BF16)
                for hh in range(HEADS_PER_TILE):
                    sl = slice(hh * HEAD_DIM, (hh + 1) * HEAD_DIM)
                    out[0, hh, :, :HEAD_DIM] = y[:, sl]
                    out[0, hh, :, HEAD_DIM:] = aug[:, sl]
            else:
                for hh in range(HEADS_PER_TILE):
                    vt_ref[0, hh] = r[:, hh * HEAD_DIM:(hh + 1) * HEAD_DIM].T.astype(BF16)


def _proj(x, b, s, gain, w_in, layer, w_f, f_bias, head_gain):
    m = x.shape[0]
    tps = s // ROW_TILE

    def a_spec(dilation):
        return pl.BlockSpec((1, dilation, ROW_TILE // dilation, COL_TILE),
                            lambda i, j: (i // tps, 0, i % tps, jnp.minimum(j, A_TILES - 1)))

    def head_block(first):
        return lambda i, j: (i // tps, jnp.clip(j - first, 0, TILES_PER_GROUP - 1), i % tps, 0)

    qb_first = A_TILES
    vt_first = A_TILES + 2 * TILES_PER_GROUP
    dilations = [d for _, d in DILATED_PATTERNS]
    return pl.pallas_call(
        functools.partial(_proj_kernel, tiles_per_seq=tps),
        grid=(m // ROW_TILE, N_PROJ_TILES),
        in_specs=[
            pl.BlockSpec((ROW_TILE, D_MODEL), lambda i, j: (i, 0)),
            pl.BlockSpec((1, D_MODEL), lambda i, j: (0, 0)),
            pl.BlockSpec((None, COL_TILE, D_MODEL), lambda i, j: (layer, j, 0)),
            pl.BlockSpec((D_MODEL, LANES), lambda i, j: (0, 0)),
            pl.BlockSpec((1, LANES), lambda i, j: (0, 0)),
            pl.BlockSpec((1, COL_TILE), lambda i, j: (0, j)),
            pl.BlockSpec((COL_TILE, COL_TILE), lambda i, j: (0, 0)),
            pl.BlockSpec((1, COL_TILE, COL_TILE),
                         lambda i, j: (jnp.clip(j - qb_first, 0, 2 * TILES_PER_GROUP - 1), 0, 0)),
        ],
        out_specs=[a_spec(d) for d in dilations] + [
            pl.BlockSpec((1, HEADS_PER_TILE, ROW_TILE, AUG), head_block(qb_first)),
            pl.BlockSpec((1, HEADS_PER_TILE, ROW_TILE, AUG),
                         head_block(qb_first + TILES_PER_GROUP)),
            pl.BlockSpec((1, HEADS_PER_TILE, HEAD_DIM, ROW_TILE),
                         lambda i, j: (i // tps, jnp.clip(j - vt_first, 0, TILES_PER_GROUP - 1),
                                       0, i % tps)),
        ],
        out_shape=[jax.ShapeDtypeStruct((b, d, s // d, 3 * D_A), BF16) for d in dilations] + [
            jax.ShapeDtypeStruct((b, H_B, s, AUG), BF16),
            jax.ShapeDtypeStruct((b, H_B, s, AUG), BF16),
            jax.ShapeDtypeStruct((b, H_B, HEAD_DIM, s), BF16),
        ],
        scratch_shapes=[
            pltpu.VMEM((ROW_TILE, D_MODEL), BF16),
            pltpu.VMEM((ROW_TILE, 4 * LANES), BF16),
            pltpu.VMEM((1, LANES), F32),
            pltpu.VMEM((HEADS_PER_TILE, ROW_TILE, LANES), F32),
        ],
        compiler_params=_params("arbitrary", "arbitrary"),
        name="proj",
    )(x, gain.reshape(1, D_MODEL), jnp.swapaxes(w_in, 1, 2), w_f, f_bias, head_gain,
      _head_mean_matrix(), _gate_placement_matrices())


def _bucket_steps(dilation):
    dist = np.arange(BAND + 1) * dilation
    max_exact = NUM_BUCKETS // 2
    large = max_exact + np.floor(
        np.log(np.maximum(dist, 1) / max_exact) / math.log(MAX_DISTANCE / max_exact)
        * (NUM_BUCKETS - max_exact)).astype(np.int64)
    bucket = np.where(dist < max_exact, dist, np.minimum(large, NUM_BUCKETS - 1))
    steps = [(0, int(bucket[0]))]
    for delta in range(1, BAND + 1):
        if bucket[delta] != bucket[delta - 1]:
            steps.append((delta, int(bucket[delta])))
    return steps


def _bias_kernel(table_ref, o_ref):
    iq = lax.broadcasted_iota(jnp.int32, (BAND, 2 * BAND), 0)
    ik = lax.broadcasted_iota(jnp.int32, (BAND, 2 * BAND), 1)
    delta = iq + BAND - ik
    in_band = jnp.logical_and(delta >= 0, delta <= BAND)
    for p, (_, dilation) in enumerate(DILATED_PATTERNS):
        steps = _bucket_steps(dilation)
        for h in range(H_A):
            val = jnp.full((BAND, 2 * BAND), table_ref[steps[0][1], h], F32)
            for start, bucket in steps[1:]:
                val = jnp.where(delta >= start, table_ref[bucket, h], val)
            o_ref[p, h] = jnp.where(in_band, val, NEG_INF)


def _band_bias(rel_table):
    n_pat = len(DILATED_PATTERNS)
    return pl.pallas_call(
        _bias_kernel,
        in_specs=[pl.BlockSpec(memory_space=pltpu.SMEM)],
        out_specs=pl.BlockSpec(memory_space=pltpu.VMEM),
        out_shape=jax.ShapeDtypeStruct((n_pat, H_A, BAND, 2 * BAND), F32),
        name="band_bias",
    )(rel_table)


DIL_BLOCKS = 2


def _dilated_kernel(q_ref, kp_ref, kc_ref, vp_ref, vc_ref, bias_ref, o_ref, st_ref):
    first = pl.program_id(2) == 0
    key_lane = lax.broadcasted_iota(jnp.int32, (1, 2 * BAND), 1)
    no_prev = jnp.where(jnp.logical_and(first, key_lane < BAND), NEG_INF, 0.0)
    stat_lane = lax.broadcasted_iota(jnp.int32, (BAND, LANES), 1)

    def keys(prev_ref, cur_ref, blk, sl):
        if blk == 0:
            return jnp.concatenate([prev_ref[0, 0, :, sl], cur_ref[0, 0, :BAND, sl]], axis=0)
        return cur_ref[0, 0, (blk - 1) * BAND:(blk + 1) * BAND, sl]

    scores = {}
    for blk in range(DIL_BLOCKS):
        for h in range(H_A):
            sl = slice(h * HEAD_DIM, (h + 1) * HEAD_DIM)
            q = q_ref[0, 0, blk * BAND:(blk + 1) * BAND, sl]
            scores[blk, h] = lax.dot_general(q, keys(kp_ref, kc_ref, blk, sl),
                                             (((1,), (1,)), ((), ())),
                                             preferred_element_type=F32)
    for blk in range(DIL_BLOCKS):
        rows = slice(blk * BAND, (blk + 1) * BAND)
        stats = jnp.zeros((BAND, LANES), F32)
        for h in range(H_A):
            sl = slice(h * HEAD_DIM, (h + 1) * HEAD_DIM)
            s = scores[blk, h] + bias_ref[0, h]
            if blk == 0:
                s = s + no_prev
            mx = jnp.max(s, axis=-1, keepdims=True)
            p = jnp.exp(s - mx)
            den = jnp.sum(p, axis=-1, keepdims=True)
            o_ref[0, 0, rows, sl] = jnp.dot(p.astype(BF16), keys(vp_ref, vc_ref, blk, sl),
                                            preferred_element_type=F32)
            stats = jnp.where(stat_lane == h, mx, stats)
            stats = jnp.where(stat_lane == H_A + h, den, stats)
        st_ref[0, 0, rows, :] = stats


def _dilated(qkv, bias, pattern):
    b, dilation, n_sub, _ = qkv.shape
    rows = DIL_BLOCKS * BAND
    nb = n_sub // rows

    def cur(which):
        return pl.BlockSpec((1, 1, rows, D_A), lambda bi, r, i: (bi, r, i, which))

    def prev(which):
        return pl.BlockSpec((1, 1, BAND, D_A),
                            lambda bi, r, i: (bi, r, jnp.maximum(i * DIL_BLOCKS - 1, 0), which))

    return pl.pallas_call(
        _dilated_kernel,
        grid=(b, dilation, nb),
        in_specs=[
            cur(0), prev(1), cur(1), prev(2), cur(2),
            pl.BlockSpec((1, H_A, BAND, 2 * BAND), lambda bi, r, i: (pattern, 0, 0, 0)),
        ],
        out_specs=[
            pl.BlockSpec((1, 1, rows, D_A), lambda bi, r, i: (bi, r, i, 0)),
            pl.BlockSpec((1, 1, rows, LANES), lambda bi, r, i: (bi, r, i, 0)),
        ],
        out_shape=[
            jax.ShapeDtypeStruct((b, dilation, n_sub, D_A), F32),
            jax.ShapeDtypeStruct((b, dilation, n_sub, LANES), F32),
        ],
        compiler_params=_params("parallel", "parallel", "arbitrary"),
        name=f"dilated_{dilation}",
    )(qkv, qkv, qkv, qkv, qkv, bias)


COMBINE_ROWS = 512


def _combine_kernel(*refs):
    n_pat = len(DILATED_PATTERNS)
    num_refs, st_refs = refs[:n_pat], refs[n_pat:2 * n_pat]
    o_ref, num_buf, st_buf = refs[2 * n_pat:]
    for p in range(n_pat):
        dilation = num_refs[p].shape[1]
        for rc in range(dilation):
            rows = pl.ds(rc, COMBINE_ROWS // dilation, stride=dilation)
            st_buf[p, rows, :] = st_refs[p][0, rc]
            for h in range(H_A):
                num_buf[p, h, rows, :] = num_refs[p][0, rc, :, h * HEAD_DIM:(h + 1) * HEAD_DIM]
    stats = [st_buf[p] for p in range(n_pat)]
    for h in range(H_A):
        sl = slice(h * HEAD_DIM, (h + 1) * HEAD_DIM)
        mxs = [st[:, h:h + 1] for st in stats]
        dens = [st[:, H_A + h:H_A + h + 1] for st in stats]
        mx = functools.reduce(jnp.maximum, mxs)
        num = None
        den = None
        for p, (pden, pmx) in enumerate(zip(dens, mxs)):
            scale = jnp.exp(pmx - mx)
            pnum = num_buf[p, h]
            num = pnum * scale if num is None else num + pnum * scale
            den = pden * scale if den is None else den + pden * scale
        o_ref[:, sl] = (num / den).astype(BF16)


def _combine(nums, stats):
    b, _, s, _ = nums[0].shape
    tps = s // COMBINE_ROWS

    def spec(arr):
        dilation, width = arr.shape[1], arr.shape[3]
        return pl.BlockSpec((1, dilation, COMBINE_ROWS // dilation, width),
                            lambda i: (i // tps, 0, i % tps, 0))

    n_pat = len(nums)
    return pl.pallas_call(
        _combine_kernel,
        grid=(b * tps,),
        in_specs=[spec(a) for a in nums] + [spec(a) for a in stats],
        out_specs=pl.BlockSpec((COMBINE_ROWS, D_A), lambda i: (i, 0)),
        out_shape=jax.ShapeDtypeStruct((b * s, D_A), BF16),
        scratch_shapes=[
            pltpu.VMEM((n_pat, H_A, COMBINE_ROWS, HEAD_DIM), F32),
            pltpu.VMEM((n_pat, COMBINE_ROWS, LANES), F32),
        ],
        compiler_params=_params("parallel"),
        name="combine_a",
    )(*nums, *stats)


def _fox_kernel(q_ref, k_ref, vt_ref, o_ref, m_ref, l_ref, acc_ref, sa_ref, sb_ref):
    iq = pl.program_id(2)
    m_ref[...] = jnp.full(m_ref.shape, NEG_INF, F32)
    l_ref[...] = jnp.zeros(l_ref.shape, F32)
    acc_ref[...] = jnp.zeros(acc_ref.shape, F32)

    def scores_into(s_ref, kb):
        start = pl.multiple_of(kb * FOX_TK, FOX_TK)
        for hh in range(FOX_HEADS):
            k = k_ref[0, hh, pl.ds(start, FOX_TK), :]
            s_ref[hh] = lax.dot_general(k, q_ref[0, hh], (((1,), (1,)), ((), ())),
                                        preferred_element_type=F32)

    def accumulate(s_ref, kb, diagonal):
        start = pl.multiple_of(kb * FOX_TK, FOX_TK)
        for hh in range(FOX_HEADS):
            st = s_ref[hh]
            if diagonal:
                key = lax.broadcasted_iota(jnp.int32, st.shape, 0)
                qry = lax.broadcasted_iota(jnp.int32, st.shape, 1)
                st = jnp.where(key <= qry, st, NEG_INF)
            m_prev = m_ref[hh]
            m_new = jnp.maximum(m_prev, jnp.max(st, axis=0, keepdims=True))
            alpha = jnp.exp2(m_prev - m_new)
            p = jnp.exp2(st - m_new)
            l_ref[hh] = alpha * l_ref[hh] + jnp.sum(p, axis=0, keepdims=True)
            vt = vt_ref[0, hh, :, pl.ds(start, FOX_TK)]
            acc_ref[hh] = alpha * acc_ref[hh] + jnp.dot(vt, p.astype(BF16),
                                                        preferred_element_type=F32)
            m_ref[hh] = m_new

    scores_into(sa_ref, 0)

    def pair(p, carry):
        scores_into(sb_ref, 2 * p + 1)
        accumulate(sa_ref, 2 * p, False)
        scores_into(sa_ref, 2 * p + 2)
        accumulate(sb_ref, 2 * p + 1, False)
        return carry

    lax.fori_loop(0, iq // 2, pair, 0)

    @pl.when(iq % 2 == 0)
    def _():
        accumulate(sa_ref, iq, True)

    @pl.when(iq % 2 == 1)
    def _():
        scores_into(sb_ref, iq)
        accumulate(sa_ref, iq - 1, False)
        accumulate(sb_ref, iq, True)
    for hh in range(FOX_HEADS):
        o_ref[0, :, hh * HEAD_DIM:(hh + 1) * HEAD_DIM] = (
            acc_ref[hh] / l_ref[hh]).T.astype(BF16)


def _fox(q_aug, k_aug, v_t):
    b, h_b, s, _ = q_aug.shape
    assert FOX_TQ == FOX_TK
    return pl.pallas_call(
        _fox_kernel,
        grid=(b, h_b // FOX_HEADS, s // FOX_TQ),
        in_specs=[
            pl.BlockSpec((1, FOX_HEADS, FOX_TQ, AUG), lambda bi, h, i: (bi, h, i, 0)),
            pl.BlockSpec((1, FOX_HEADS, s, AUG), lambda bi, h, i: (bi, h, 0, 0)),
            pl.BlockSpec((1, FOX_HEADS, HEAD_DIM, s), lambda bi, h, i: (bi, h, 0, 0)),
        ],
        out_specs=pl.BlockSpec((1, FOX_TQ, FOX_HEADS * HEAD_DIM), lambda bi, h, i: (bi, i, h)),
        out_shape=jax.ShapeDtypeStruct((b, s, h_b * HEAD_DIM), BF16),
        scratch_shapes=[
            pltpu.VMEM((FOX_HEADS, 1, FOX_TQ), F32),
            pltpu.VMEM((FOX_HEADS, 1, FOX_TQ), F32),
            pltpu.VMEM((FOX_HEADS, HEAD_DIM, FOX_TQ), F32),
            pltpu.VMEM((FOX_HEADS, FOX_TK, FOX_TQ), F32),
            pltpu.VMEM((FOX_HEADS, FOX_TK, FOX_TQ), F32),
        ],
        compiler_params=_params("parallel", "parallel", "arbitrary"),
        name="fox",
    )(q_aug, k_aug, v_t)


def _out_proj_kernel(a_ref, b_ref, w_ref, x_ref, o_ref):
    mixed = jnp.concatenate([a_ref[...], b_ref[...]], axis=-1)
    o_ref[...] = x_ref[...] + jnp.dot(mixed, w_ref[...].astype(BF16),
                                      preferred_element_type=F32)


OUT_ROW_TILE = 2048


def _out_proj(out_a, out_b, w_out, x):
    m = x.shape[0]
    return pl.pallas_call(
        _out_proj_kernel,
        grid=(m // OUT_ROW_TILE, D_MODEL // COL_TILE),
        in_specs=[
            pl.BlockSpec((OUT_ROW_TILE, D_A), lambda i, j: (i, 0)),
            pl.BlockSpec((OUT_ROW_TILE, D_B), lambda i, j: (i, 0)),
            pl.BlockSpec((D_A + D_B, COL_TILE), lambda i, j: (0, j)),
            pl.BlockSpec((OUT_ROW_TILE, COL_TILE), lambda i, j: (i, j)),
        ],
        out_specs=pl.BlockSpec((OUT_ROW_TILE, COL_TILE), lambda i, j: (i, j)),
        out_shape=jax.ShapeDtypeStruct((m, D_MODEL), F32),
        compiler_params=_params("parallel", "parallel"),
        name="out_proj",
    )(out_a, out_b, w_out, x)


def _mixer(x, b, s, mix_norm, w_in, layer, q_norm_a, k_norm_a, q_norm_b, k_norm_b, forget_bias,
           rel_bias_table, w_out):
    ones = jnp.ones((D_A,), F32)
    head_gain = jnp.concatenate([
        jnp.tile(q_norm_a * ATTN_SCALE, H_A), jnp.tile(k_norm_a, H_A), ones,
        jnp.tile(q_norm_b * (ATTN_SCALE * LOG2E), H_B), jnp.tile(k_norm_b, H_B), ones,
    ]).reshape(1, D_QKV)
    w_f = jnp.pad(w_in[layer, :, D_QKV:], ((0, 0), (0, LANES - H_B))).astype(BF16)
    f_bias = jnp.pad(forget_bias, (0, LANES - H_B)).reshape(1, LANES)
    *qkv_a, q_aug, k_aug, v_t = _proj(x, b, s, mix_norm, w_in, layer, w_f, f_bias, head_gain)

    out_b = _fox(q_aug, k_aug, v_t).reshape(b * s, D_B)

    bias = _band_bias(rel_bias_table)
    parts = [_dilated(qkv, bias, p) for p, qkv in enumerate(qkv_a)]
    out_a = _combine([pt[0] for pt in parts], [pt[1] for pt in parts])

    return _out_proj(out_a, out_b, w_out, x)


def kernel(x, ffn1_norm, ffn1_w_in, ffn1_w_out, mix_norm, w_in, q_norm_a, k_norm_a, q_norm_b,
           k_norm_b, forget_bias, rel_bias_table, w_out, ffn2_norm, ffn2_w_in, ffn2_w_out):
    b, s, d = x.shape
    depth = ffn1_norm.shape[0]
    x = x.reshape(b * s, d)
    for l in range(depth):
        x = _ffn(x, ffn1_norm[l], ffn1_w_in[l], ffn1_w_out[l])
        x = _mixer(x, b, s, mix_norm[l], w_in, l, q_norm_a[l], k_norm_a[l], q_norm_b[l],
                   k_norm_b[l], forget_bias[l], rel_bias_table, w_out[l])
        x = _ffn(x, ffn2_norm[l], ffn2_w_in[l], ffn2_w_out[l])
    return x.reshape(b, s, d)
```

```python
import functools
import math

import numpy as np
import jax
import jax.numpy as jnp
from jax import lax
from jax.experimental import pallas as pl
from jax.experimental.pallas import tpu as pltpu

D_MODEL = 2048
HEAD_DIM = 128
N_HEADS = D_MODEL // HEAD_DIM
H_A = N_HEADS // 2
H_B = N_HEADS - H_A
D_A = H_A * HEAD_DIM
D_B = H_B * HEAD_DIM
D_QKV = 3 * D_A + 3 * D_B
DILATED_PATTERNS = ((128, 1), (512, 4), (2048, 16))
BAND = 128
NUM_BUCKETS = 32
MAX_DISTANCE = 2048
D_FF = ((8 * D_MODEL // 3 + 127) // 128) * 128
RMS_EPS = 1e-6
NEG_INF = -1e30
ATTN_SCALE = HEAD_DIM ** -0.5

LANES = 128
FF_TILE = 512
FF_TAIL = D_FF - (D_FF // FF_TILE) * FF_TILE
FFN_ROW_TILE = 2048
FFN_CHUNK = 512
FFN_VMEM_LIMIT = 60 * 1024 * 1024
ROW_TILE = 1024
COL_TILE = 512
FOX_TQ = 512
FOX_TK = 512
FOX_HEADS = 4
VMEM_LIMIT = 56 * 1024 * 1024

F32 = jnp.float32
BF16 = jnp.bfloat16


def _params(*sem):
    return pltpu.CompilerParams(dimension_semantics=sem, vmem_limit_bytes=VMEM_LIMIT)


def _rms_rows(x, gain):
    ms = jnp.mean(x * x, axis=-1, keepdims=True)
    return x * lax.rsqrt(ms + RMS_EPS) * gain


def _ffn_kernel(x_hbm, g_ref, wg_ref, wu_ref, wo_ref, o_ref, h_ref, sem):
    i = pl.program_id(0)
    j = pl.program_id(1)

    @pl.when(j == 0)
    def _():
        def x_copy(c):
            src = x_hbm.at[pl.ds(i * FFN_ROW_TILE + c * FFN_CHUNK, FFN_CHUNK), :]
            return pltpu.make_async_copy(src, o_ref.at[pl.ds(c * FFN_CHUNK, FFN_CHUNK), :],
                                         sem.at[c])

        n_chunks = FFN_ROW_TILE // FFN_CHUNK
        for c in range(n_chunks):
            x_copy(c).start()
        for c in range(n_chunks):
            x_copy(c).wait()
            rows = slice(c * FFN_CHUNK, (c + 1) * FFN_CHUNK)
            h_ref[rows, :] = _rms_rows(o_ref[rows, :], g_ref[...]).astype(BF16)

    col = lax.broadcasted_iota(jnp.int32, (1, FF_TILE), 1)
    repeated = j * FF_TILE - jnp.minimum(j * FF_TILE, D_FF - FF_TILE)
    for r in range(FFN_ROW_TILE // ROW_TILE):
        rows = slice(r * ROW_TILE, (r + 1) * ROW_TILE)
        h = h_ref[rows, :]
        gate = jnp.dot(h, wg_ref[...].astype(BF16), preferred_element_type=F32)
        up = jnp.dot(h, wu_ref[...].astype(BF16), preferred_element_type=F32)
        act = gate * (1.0 / (1.0 + jnp.exp(-gate))) * (0.5 * up)
        act = jnp.where(col >= repeated, act, 0.0).astype(BF16)
        for c in range(D_MODEL // COL_TILE):
            sl = slice(c * COL_TILE, (c + 1) * COL_TILE)
            o_ref[rows, sl] += jnp.dot(act, wo_ref[:, sl].astype(BF16),
                                       preferred_element_type=F32)


def _ffn(x, gain, w_in, w_out):
    m = x.shape[0]
    n_ff = pl.cdiv(D_FF, FF_TILE)

    def ff_start(j):
        return pl.multiple_of(jnp.minimum(j * FF_TILE, D_FF - FF_TILE), LANES)

    return pl.pallas_call(
        _ffn_kernel,
        grid=(m // FFN_ROW_TILE, n_ff),
        in_specs=[
            pl.BlockSpec(memory_space=pl.ANY),
            pl.BlockSpec((1, D_MODEL), lambda i, j: (0, 0)),
            pl.BlockSpec((pl.Element(D_MODEL), pl.Element(FF_TILE)),
                         lambda i, j: (0, ff_start(j))),
            pl.BlockSpec((pl.Element(D_MODEL), pl.Element(FF_TILE)),
                         lambda i, j: (0, pl.multiple_of(D_FF + ff_start(j), LANES))),
            pl.BlockSpec((pl.Element(FF_TILE), pl.Element(D_MODEL)),
                         lambda i, j: (ff_start(j), 0)),
        ],
        out_specs=pl.BlockSpec((FFN_ROW_TILE, D_MODEL), lambda i, j: (i, 0),
                               pipeline_mode=pl.Buffered(1)),
        out_shape=jax.ShapeDtypeStruct((m, D_MODEL), F32),
        scratch_shapes=[
            pltpu.VMEM((FFN_ROW_TILE, D_MODEL), BF16),
            pltpu.SemaphoreType.DMA((FFN_ROW_TILE // FFN_CHUNK,)),
        ],
        compiler_params=pltpu.CompilerParams(
            dimension_semantics=("arbitrary", "arbitrary"), vmem_limit_bytes=FFN_VMEM_LIMIT),
        name="ffn",
    )(x, gain.reshape(1, D_MODEL), w_in, w_in, w_out)


HEADS_PER_TILE = COL_TILE // HEAD_DIM
N_PROJ_TILES = D_QKV // COL_TILE
TILES_PER_GROUP = D_A // COL_TILE
A_TILES = 3 * TILES_PER_GROUP
AUG = 2 * HEAD_DIM
LOG2E = math.log2(math.e)


def _log_gate_scan(z, carry):
    c = (jnp.minimum(z, 0.0) - jnp.log1p(jnp.exp(-jnp.abs(z)))) * LOG2E
    row = lax.broadcasted_iota(jnp.int32, c.shape, 0)
    shift = 1
    while shift < c.shape[0]:
        c = c + jnp.where(row >= shift, pltpu.roll(c, shift, axis=0), 0.0)
        shift *= 2
    return c + carry


def _split3(c):
    hi = c.astype(BF16)
    rest = c - hi.astype(F32)
    mid = rest.astype(BF16)
    lo = (rest - mid.astype(F32)).astype(BF16)
    return hi, mid, lo


def _head_mean_matrix():
    head = np.arange(COL_TILE) // HEAD_DIM
    return jnp.asarray((head[:, None] == head[None, :]) / HEAD_DIM, BF16)


def _gate_placement_matrices():
    mats = np.zeros((2, TILES_PER_GROUP, COL_TILE, COL_TILE), np.float32)
    ones_row = 3 * LANES
    for part in range(TILES_PER_GROUP):
        for hh in range(HEADS_PER_TILE):
            head = part * HEADS_PER_TILE + hh
            col = hh * HEAD_DIM
            for term in range(3):
                mats[0, part, term * LANES + head, col + term] = 1.0
                mats[0, part, ones_row, col + 3 + term] = 1.0
                mats[1, part, term * LANES + head, col + 3 + term] = -1.0
                mats[1, part, ones_row, col + term] = 1.0
    return jnp.asarray(mats.reshape(2 * TILES_PER_GROUP, COL_TILE, COL_TILE), BF16)


def _proj_kernel(x_ref, g_ref, w_ref, wf_ref, fb_ref, hg_ref, mean_ref, place_ref,
                 a1_ref, a4_ref, a16_ref, qb_ref, kb_ref, vt_ref,
                 h_ref, cs_ref, carry_ref, y_ref, *, tiles_per_seq):
    i = pl.program_id(0)
    j = pl.program_id(1)

    @pl.when(j == 0)
    def _():
        h_ref[...] = _rms_rows(x_ref[...], g_ref[...]).astype(BF16)

        @pl.when(i % tiles_per_seq == 0)
        def _():
            carry_ref[...] = jnp.zeros(carry_ref.shape, F32)

        z = jnp.dot(h_ref[...], wf_ref[...], preferred_element_type=F32) + fb_ref[...]
        c = _log_gate_scan(z, carry_ref[...])
        carry_ref[...] = c[ROW_TILE - 1:ROW_TILE, :]
        for term, part in enumerate(_split3(c)):
            cs_ref[:, term * LANES:(term + 1) * LANES] = part
        cs_ref[:, 3 * LANES:] = jnp.ones((ROW_TILE, LANES), BF16)

    r = lax.dot_general(h_ref[...], w_ref[...].astype(BF16), (((1,), (1,)), ((), ())),
                        preferred_element_type=F32)

    def normed():
        ms = jnp.dot((r * r).astype(BF16), mean_ref[...], preferred_element_type=F32)
        return r * lax.rsqrt(ms + RMS_EPS) * hg_ref[...]

    def store_dilated(y):
        a1_ref[0, 0] = y.astype(BF16)
        for c in range(HEADS_PER_TILE):
            sl = slice(c * LANES, (c + 1) * LANES)
            y_ref[c] = y[:, sl]
            for ref in (a4_ref, a16_ref):
                dilation = ref.shape[1]
                for rc in range(dilation):
                    rows = y_ref[c, pl.ds(rc, ROW_TILE // dilation, stride=dilation), :]
                    ref[0, rc, :, sl] = rows.astype(BF16)

    for t in range(N_PROJ_TILES):
        group, part = divmod(t, TILES_PER_GROUP)

        @pl.when(j == t)
        def _(group=group):
            if group in (0, 1):
                store_dilated(normed())
            elif group == 2:
                store_dilated(r)
            elif group in (3, 4):
                out = qb_ref if group == 3 else kb_ref
                y = normed().astype(BF16)
                aug = jnp.dot(cs_ref[...], place_ref[0],
                              preferred_element_type=F32).astype(BF16)
                for hh in range(HEADS_PER_TILE):
                    sl = slice(hh * HEAD_DIM, (hh + 1) * HEAD_DIM)
                    out[0, hh, :, :HEAD_DIM] = y[:, sl]
                    out[0, hh, :, HEAD_DIM:] = aug[:, sl]
            else:
                for hh in range(HEADS_PER_TILE):
                    vt_ref[0, hh] = r[:, hh * HEAD_DIM:(hh + 1) * HEAD_DIM].T.astype(BF16)


def _proj(x, b, s, gain, w_in, layer, w_f, f_bias, head_gain):
    m = x.shape[0]
    tps = s // ROW_TILE

    def a_spec(dilation):
        return pl.BlockSpec((1, dilation, ROW_TILE // dilation, COL_TILE),
                            lambda i, j: (i // tps, 0, i % tps, jnp.minimum(j, A_TILES - 1)))

    def head_block(first):
        return lambda i, j: (i // tps, jnp.clip(j - first, 0, TILES_PER_GROUP - 1), i % tps, 0)

    qb_first = A_TILES
    vt_first = A_TILES + 2 * TILES_PER_GROUP
    dilations = [d for _, d in DILATED_PATTERNS]
    return pl.pallas_call(
        functools.partial(_proj_kernel, tiles_per_seq=tps),
        grid=(m // ROW_TILE, N_PROJ_TILES),
        in_specs=[
            pl.BlockSpec((ROW_TILE, D_MODEL), lambda i, j: (i, 0)),
            pl.BlockSpec((1, D_MODEL), lambda i, j: (0, 0)),
            pl.BlockSpec((None, COL_TILE, D_MODEL), lambda i, j: (layer, j, 0)),
            pl.BlockSpec((D_MODEL, LANES), lambda i, j: (0, 0)),
            pl.BlockSpec((1, LANES), lambda i, j: (0, 0)),
            pl.BlockSpec((1, COL_TILE), lambda i, j: (0, j)),
            pl.BlockSpec((COL_TILE, COL_TILE), lambda i, j: (0, 0)),
            pl.BlockSpec((1, COL_TILE, COL_TILE),
                         lambda i, j: (jnp.clip(j - qb_first, 0, 2 * TILES_PER_GROUP - 1), 0, 0)),
        ],
        out_specs=[a_spec(d) for d in dilations] + [
            pl.BlockSpec((1, HEADS_PER_TILE, ROW_TILE, AUG), head_block(qb_first)),
            pl.BlockSpec((1, HEADS_PER_TILE, ROW_TILE, AUG),
                         head_block(qb_first + TILES_PER_GROUP)),
            pl.BlockSpec((1, HEADS_PER_TILE, HEAD_DIM, ROW_TILE),
                         lambda i, j: (i // tps, jnp.clip(j - vt_first, 0, TILES_PER_GROUP - 1),
                                       0, i % tps)),
        ],
        out_shape=[jax.ShapeDtypeStruct((b, d, s // d, 3 * D_A), BF16) for d in dilations] + [
            jax.ShapeDtypeStruct((b, H_B, s, AUG), BF16),
            jax.ShapeDtypeStruct((b, H_B, s, AUG), BF16),
            jax.ShapeDtypeStruct((b, H_B, HEAD_DIM, s), BF16),
        ],
        scratch_shapes=[
            pltpu.VMEM((ROW_TILE, D_MODEL), BF16),
            pltpu.VMEM((ROW_TILE, 4 * LANES), BF16),
            pltpu.VMEM((1, LANES), F32),
            pltpu.VMEM((HEADS_PER_TILE, ROW_TILE, LANES), F32),
        ],
        compiler_params=_params("arbitrary", "arbitrary"),
        name="proj",
    )(x, gain.reshape(1, D_MODEL), jnp.swapaxes(w_in, 1, 2), w_f, f_bias, head_gain,
      _head_mean_matrix(), _gate_placement_matrices())


def _bucket_steps(dilation):
    dist = np.arange(BAND + 1) * dilation
    max_exact = NUM_BUCKETS // 2
    large = max_exact + np.floor(
        np.log(np.maximum(dist, 1) / max_exact) / math.log(MAX_DISTANCE / max_exact)
        * (NUM_BUCKETS - max_exact)).astype(np.int64)
    bucket = np.where(dist < max_exact, dist, np.minimum(large, NUM_BUCKETS - 1))
    steps = [(0, int(bucket[0]))]
    for delta in range(1, BAND + 1):
        if bucket[delta] != bucket[delta - 1]:
            steps.append((delta, int(bucket[delta])))
    return steps


def _bias_kernel(table_ref, o_ref):
    iq = lax.broadcasted_iota(jnp.int32, (BAND, 2 * BAND), 0)
    ik = lax.broadcasted_iota(jnp.int32, (BAND, 2 * BAND), 1)
    delta = iq + BAND - ik
    in_band = jnp.logical_and(delta >= 0, delta <= BAND)
    for p, (_, dilation) in enumerate(DILATED_PATTERNS):
        steps = _bucket_steps(dilation)
        for h in range(H_A):
            val = jnp.full((BAND, 2 * BAND), table_ref[steps[0][1], h], F32)
            for start, bucket in steps[1:]:
                val = jnp.where(delta >= start, table_ref[bucket, h], val)
            o_ref[p, h] = jnp.where(in_band, val, NEG_INF)


def _band_bias(rel_table):
    n_pat = len(DILATED_PATTERNS)
    return pl.pallas_call(
        _bias_kernel,
        in_specs=[pl.BlockSpec(memory_space=pltpu.SMEM)],
        out_specs=pl.BlockSpec(memory_space=pltpu.VMEM),
        out_shape=jax.ShapeDtypeStruct((n_pat, H_A, BAND, 2 * BAND), F32),
        name="band_bias",
    )(rel_table)


DIL_BLOCKS = 2


def _dilated_kernel(q_ref, kp_ref, kc_ref, vp_ref, vc_ref, bias_ref, o_ref, st_ref):
    first = pl.program_id(2) == 0
    key_lane = lax.broadcasted_iota(jnp.int32, (1, 2 * BAND), 1)
    no_prev = jnp.where(jnp.logical_and(first, key_lane < BAND), NEG_INF, 0.0)
    stat_lane = lax.broadcasted_iota(jnp.int32, (BAND, LANES), 1)

    def keys(prev_ref, cur_ref, blk, sl):
        if blk == 0:
            return jnp.concatenate([prev_ref[0, 0, :, sl], cur_ref[0, 0, :BAND, sl]], axis=0)
        return cur_ref[0, 0, (blk - 1) * BAND:(blk + 1) * BAND, sl]

    scores = {}
    for blk in range(DIL_BLOCKS):
        for h in range(H_A):
            sl = slice(h * HEAD_DIM, (h + 1) * HEAD_DIM)
            q = q_ref[0, 0, blk * BAND:(blk + 1) * BAND, sl]
            scores[blk, h] = lax.dot_general(q, keys(kp_ref, kc_ref, blk, sl),
                                             (((1,), (1,)), ((), ())),
                                             preferred_element_type=F32)
    for blk in range(DIL_BLOCKS):
        rows = slice(blk * BAND, (blk + 1) * BAND)
        stats = jnp.zeros((BAND, LANES), F32)
        for h in range(H_A):
            sl = slice(h * HEAD_DIM, (h + 1) * HEAD_DIM)
            s = scores[blk, h] + bias_ref[0, h]
            if blk == 0:
                s = s + no_prev
            mx = jnp.max(s, axis=-1, keepdims=True)
            p = jnp.exp(s - mx)
            den = jnp.sum(p, axis=-1, keepdims=True)
            o_ref[0, 0, rows, sl] = jnp.dot(p.astype(BF16), keys(vp_ref, vc_ref, blk, sl),
                                            preferred_element_type=F32)
            stats = jnp.where(stat_lane == h, mx, stats)
            stats = jnp.where(stat_lane == H_A + h, den, stats)
        st_ref[0, 0, rows, :] = stats


def _dilated(qkv, bias, pattern):
    b, dilation, n_sub, _ = qkv.shape
    rows = DIL_BLOCKS * BAND
    nb = n_sub // rows

    def cur(which):
        return pl.BlockSpec((1, 1, rows, D_A), lambda bi, r, i: (bi, r, i, which))

    def prev(which):
        return pl.BlockSpec((1, 1, BAND, D_A),
                            lambda bi, r, i: (bi, r, jnp.maximum(i * DIL_BLOCKS - 1, 0), which))

    return pl.pallas_call(
        _dilated_kernel,
        grid=(b, dilation, nb),
        in_specs=[
            cur(0), prev(1), cur(1), prev(2), cur(2),
            pl.BlockSpec((1, H_A, BAND, 2 * BAND), lambda bi, r, i: (pattern, 0, 0, 0)),
        ],
        out_specs=[
            pl.BlockSpec((1, 1, rows, D_A), lambda bi, r, i: (bi, r, i, 0)),
            pl.BlockSpec((1, 1, rows, LANES), lambda bi, r, i: (bi, r, i, 0)),
        ],
        out_shape=[
            jax.ShapeDtypeStruct((b, dilation, n_sub, D_A), F32),
            jax.ShapeDtypeStruct((b, dilation, n_sub, LANES), F32),
        ],
        compiler_params=_params("parallel", "parallel", "arbitrary"),
        name=f"dilated_{dilation}",
    )(qkv, qkv, qkv, qkv, qkv, bias)


COMBINE_ROWS = 512


def _combine_kernel(*refs):
    n_pat = len(DILATED_PATTERNS)
    num_refs, st_refs = refs[:n_pat], refs[n_pat:2 * n_pat]
    o_ref, num_buf, st_buf = refs[2 * n_pat:]
    for p in range(n_pat):
        dilation = num_refs[p].shape[1]
        for rc in range(dilation):
            rows = pl.ds(rc, COMBINE_ROWS // dilation, stride=dilation)
            st_buf[p, rows, :] = st_refs[p][0, rc]
            for h in range(H_A):
                num_buf[p, h, rows, :] = num_refs[p][0, rc, :, h * HEAD_DIM:(h + 1) * HEAD_DIM]
    stats = [st_buf[p] for p in range(n_pat)]
    for h in range(H_A):
        sl = slice(h * HEAD_DIM, (h + 1) * HEAD_DIM)
        mxs = [st[:, h:h + 1] for st in stats]
        dens = [st[:, H_A + h:H_A + h + 1] for st in stats]
        mx = functools.reduce(jnp.maximum, mxs)
        num = None
        den = None
        for p, (pden, pmx) in enumerate(zip(dens, mxs)):
            scale = jnp.exp(pmx - mx)
            pnum = num_buf[p, h]
            num = pnum * scale if num is None else num + pnum * scale
            den = pden * scale if den is None else den + pden * scale
        o_ref[:, sl] = (num / den).astype(BF16)


def _combine(nums, stats):
    b, _, s, _ = nums[0].shape
    tps = s // COMBINE_ROWS

    def spec(arr):
        dilation, width = arr.shape[1], arr.shape[3]
        return pl.BlockSpec((1, dilation, COMBINE_ROWS // dilation, width),
                            lambda i: (i // tps, 0, i % tps, 0))

    n_pat = len(nums)
    return pl.pallas_call(
        _combine_kernel,
        grid=(b * tps,),
        in_specs=[spec(a) for a in nums] + [spec(a) for a in stats],
        out_specs=pl.BlockSpec((COMBINE_ROWS, D_A), lambda i: (i, 0)),
        out_shape=jax.ShapeDtypeStruct((b * s, D_A), BF16),
        scratch_shapes=[
            pltpu.VMEM((n_pat, H_A, COMBINE_ROWS, HEAD_DIM), F32),
            pltpu.VMEM((n_pat, COMBINE_ROWS, LANES), F32),
        ],
        compiler_params=_params("parallel"),
        name="combine_a",
    )(*nums, *stats)


def _fox_kernel(q_ref, k_ref, vt_ref, o_ref, m_ref, l_ref, acc_ref, sa_ref, sb_ref):
    iq = pl.program_id(2)
    m_ref[...] = jnp.full(m_ref.shape, NEG_INF, F32)
    l_ref[...] = jnp.zeros(l_ref.shape, F32)
    acc_ref[...] = jnp.zeros(acc_ref.shape, F32)

    def scores_into(s_ref, kb):
        start = pl.multiple_of(kb * FOX_TK, FOX_TK)
        for hh in range(FOX_HEADS):
            k = k_ref[0, hh, pl.ds(start, FOX_TK), :]
            s_ref[hh] = lax.dot_general(k, q_ref[0, hh], (((1,), (1,)), ((), ())),
                                        preferred_element_type=F32)

    def accumulate(s_ref, kb, diagonal):
        start = pl.multiple_of(kb * FOX_TK, FOX_TK)
        for hh in range(FOX_HEADS):
            st = s_ref[hh]
            if diagonal:
                key = lax.broadcasted_iota(jnp.int32, st.shape, 0)
                qry = lax.broadcasted_iota(jnp.int32, st.shape, 1)
                st = jnp.where(key <= qry, st, NEG_INF)
            m_prev = m_ref[hh]
            m_new = jnp.maximum(m_prev, jnp.max(st, axis=0, keepdims=True))
            alpha = jnp.exp2(m_prev - m_new)
            p = jnp.exp2(st - m_new)
            l_ref[hh] = alpha * l_ref[hh] + jnp.sum(p, axis=0, keepdims=True)
            vt = vt_ref[0, hh, :, pl.ds(start, FOX_TK)]
            acc_ref[hh] = alpha * acc_ref[hh] + jnp.dot(vt, p.astype(BF16),
                                                        preferred_element_type=F32)
            m_ref[hh] = m_new

    scores_into(sa_ref, 0)

    def pair(p, carry):
        scores_into(sb_ref, 2 * p + 1)
        accumulate(sa_ref, 2 * p, False)
        scores_into(sa_ref, 2 * p + 2)
        accumulate(sb_ref, 2 * p + 1, False)
        return carry

    lax.fori_loop(0, iq // 2, pair, 0)

    @pl.when(iq % 2 == 0)
    def _():
        accumulate(sa_ref, iq, True)

    @pl.when(iq % 2 == 1)
    def _():
        scores_into(sb_ref, iq)
        accumulate(sa_ref, iq - 1, False)
        accumulate(sb_ref, iq, True)
    for hh in range(FOX_HEADS):
        o_ref[0, :, hh * HEAD_DIM:(hh + 1) * HEAD_DIM] = (
            acc_ref[hh] / l_ref[hh]).T.astype(BF16)


def _fox(q_aug, k_aug, v_t):
    b, h_b, s, _ = q_aug.shape
    assert FOX_TQ == FOX_TK
    return pl.pallas_call(
        _fox_kernel,
        grid=(b, h_b // FOX_HEADS, s // FOX_TQ),
        in_specs=[
            pl.BlockSpec((1, FOX_HEADS, FOX_TQ, AUG), lambda bi, h, i: (bi, h, i, 0)),
            pl.BlockSpec((1, FOX_HEADS, s, AUG), lambda bi, h, i: (bi, h, 0, 0)),
            pl.BlockSpec((1, FOX_HEADS, HEAD_DIM, s), lambda bi, h, i: (bi, h, 0, 0)),
        ],
        out_specs=pl.BlockSpec((1, FOX_TQ, FOX_HEADS * HEAD_DIM), lambda bi, h, i: (bi, i, h)),
        out_shape=jax.ShapeDtypeStruct((b, s, h_b * HEAD_DIM), BF16),
        scratch_shapes=[
            pltpu.VMEM((FOX_HEADS, 1, FOX_TQ), F32),
            pltpu.VMEM((FOX_HEADS, 1, FOX_TQ), F32),
            pltpu.VMEM((FOX_HEADS, HEAD_DIM, FOX_TQ), F32),
            pltpu.VMEM((FOX_HEADS, FOX_TK, FOX_TQ), F32),
            pltpu.VMEM((FOX_HEADS, FOX_TK, FOX_TQ), F32),
        ],
        compiler_params=_params("parallel", "parallel", "arbitrary"),
        name="fox",
    )(q_aug, k_aug, v_t)


def _out_proj_kernel(a_ref, b_ref, w_ref, x_ref, o_ref):
    mixed = jnp.concatenate([a_ref[...], b_ref[...]], axis=-1)
    o_ref[...] = x_ref[...] + jnp.dot(mixed, w_ref[...].astype(BF16),
                                      preferred_element_type=F32)


OUT_ROW_TILE = 2048


def _out_proj(out_a, out_b, w_out, x):
    m = x.shape[0]
    return pl.pallas_call(
        _out_proj_kernel,
        grid=(m // OUT_ROW_TILE, D_MODEL // COL_TILE),
        in_specs=[
            pl.BlockSpec((OUT_ROW_TILE, D_A), lambda i, j: (i, 0)),
            pl.BlockSpec((OUT_ROW_TILE, D_B), lambda i, j: (i, 0)),
            pl.BlockSpec((D_A + D_B, COL_TILE), lambda i, j: (0, j)),
            pl.BlockSpec((OUT_ROW_TILE, COL_TILE), lambda i, j: (i, j)),
        ],
        out_specs=pl.BlockSpec((OUT_ROW_TILE, COL_TILE), lambda i, j: (i, j)),
        out_shape=jax.ShapeDtypeStruct((m, D_MODEL), F32),
        compiler_params=_params("parallel", "parallel"),
        name="out_proj",
    )(out_a, out_b, w_out, x)


def _mixer(x, b, s, mix_norm, w_in, layer, q_norm_a, k_norm_a, q_norm_b, k_norm_b, forget_bias,
           rel_bias_table, w_out):
    ones = jnp.ones((D_A,), F32)
    head_gain = jnp.concatenate([
        jnp.tile(q_norm_a * ATTN_SCALE, H_A), jnp.tile(k_norm_a, H_A), ones,
        jnp.tile(q_norm_b * (ATTN_SCALE * LOG2E), H_B), jnp.tile(k_norm_b, H_B), ones,
    ]).reshape(1, D_QKV)
    w_f = jnp.pad(w_in[layer, :, D_QKV:], ((0, 0), (0, LANES - H_B))).astype(BF16)
    f_bias = jnp.pad(forget_bias, (0, LANES - H_B)).reshape(1, LANES)
    *qkv_a, q_aug, k_aug, v_t = _proj(x, b, s, mix_norm, w_in, layer, w_f, f_bias, head_gain)

    out_b = _fox(q_aug, k_aug, v_t).reshape(b * s, D_B)

    bias = _band_bias(rel_bias_table)
    parts = [_dilated(qkv, bias, p) for p, qkv in enumerate(qkv_a)]
    out_a = _combine([pt[0] for pt in parts], [pt[1] for pt in parts])

    return _out_proj(out_a, out_b, w_out, x)


def kernel(x, ffn1_norm, ffn1_w_in, ffn1_w_out, mix_norm, w_in, q_norm_a, k_norm_a, q_norm_b,
           k_norm_b, forget_bias, rel_bias_table, w_out, ffn2_norm, ffn2_w_in, ffn2_w_out):
    b, s, d = x.shape
    depth = ffn1_norm.shape[0]
    x = x.reshape(b * s, d)
    for l in range(depth):
        x = _ffn(x, ffn1_norm[l], ffn1_w_in[l], ffn1_w_out[l])
        x = _mixer(x, b, s, mix_norm[l], w_in, l, q_norm_a[l], k_norm_a[l], q_norm_b[l],
                   k_norm_b[l], forget_bias[l], rel_bias_table, w_out[l])
        x = _ffn(x, ffn2_norm[l], ffn2_w_in[l], ffn2_w_out[l])
    return x.reshape(b, s, d)
```

```python
import functools
import math

import numpy as np
import jax
import jax.numpy as jnp
from jax import lax
from jax.experimental import pallas as pl
from jax.experimental.pallas import tpu as pltpu

D_MODEL = 2048
HEAD_DIM = 128
N_HEADS = D_MODEL // HEAD_DIM
H_A = N_HEADS // 2
H_B = N_HEADS - H_A
D_A = H_A * HEAD_DIM
D_B = H_B * HEAD_DIM
D_QKV = 3 * D_A + 3 * D_B
DILATED_PATTERNS = ((128, 1), (512, 4), (2048, 16))
BAND = 128
NUM_BUCKETS = 32
MAX_DISTANCE = 2048
D_FF = ((8 * D_MODEL // 3 + 127) // 128) * 128
RMS_EPS = 1e-6
NEG_INF = -1e30
ATTN_SCALE = HEAD_DIM ** -0.5

LANES = 128
FF_TILE = 512
FF_TAIL = D_FF - (D_FF // FF_TILE) * FF_TILE
FFN_ROW_TILE = 2048
FFN_CHUNK = 512
FFN_VMEM_LIMIT = 60 * 1024 * 1024
ROW_TILE = 1024
COL_TILE = 512
FOX_TQ = 512
FOX_TK = 512
FOX_HEADS = 4
VMEM_LIMIT = 56 * 1024 * 1024

F32 = jnp.float32
BF16 = jnp.bfloat16


def _params(*sem):
    return pltpu.CompilerParams(dimension_semantics=sem, vmem_limit_bytes=VMEM_LIMIT)


def _rms_rows(x, gain):
    ms = jnp.mean(x * x, axis=-1, keepdims=True)
    return x * lax.rsqrt(ms + RMS_EPS) * gain


def _ffn_kernel(x_hbm, g_ref, wg_ref, wu_ref, wo_ref, o_ref, h_ref, sem):
    i = pl.program_id(0)
    j = pl.program_id(1)

    @pl.when(j == 0)
    def _():
        def x_copy(c):
            src = x_hbm.at[pl.ds(i * FFN_ROW_TILE + c * FFN_CHUNK, FFN_CHUNK), :]
            return pltpu.make_async_copy(src, o_ref.at[pl.ds(c * FFN_CHUNK, FFN_CHUNK), :],
                                         sem.at[c])

        n_chunks = FFN_ROW_TILE // FFN_CHUNK
        for c in range(n_chunks):
            x_copy(c).start()
        for c in range(n_chunks):
            x_copy(c).wait()
            rows = slice(c * FFN_CHUNK, (c + 1) * FFN_CHUNK)
            h_ref[rows, :] = _rms_rows(o_ref[rows, :], g_ref[...]).astype(BF16)

    col = lax.broadcasted_iota(jnp.int32, (1, FF_TILE), 1)
    repeated = j * FF_TILE - jnp.minimum(j * FF_TILE, D_FF - FF_TILE)
    for r in range(FFN_ROW_TILE // ROW_TILE):
        rows = slice(r * ROW_TILE, (r + 1) * ROW_TILE)
        h = h_ref[rows, :]
        gate = jnp.dot(h, wg_ref[...].astype(BF16), preferred_element_type=F32)
        up = jnp.dot(h, wu_ref[...].astype(BF16), preferred_element_type=F32)
        act = gate * (1.0 / (1.0 + jnp.exp(-gate))) * (0.5 * up)
        act = jnp.where(col >= repeated, act, 0.0).astype(BF16)
        for c in range(D_MODEL // COL_TILE):
            sl = slice(c * COL_TILE, (c + 1) * COL_TILE)
            o_ref[rows, sl] += jnp.dot(act, wo_ref[:, sl].astype(BF16),
                                       preferred_element_type=F32)


def _ffn(x, gain, w_in, w_out):
    m = x.shape[0]
    n_ff = pl.cdiv(D_FF, FF_TILE)

    def ff_start(j):
        return pl.multiple_of(jnp.minimum(j * FF_TILE, D_FF - FF_TILE), LANES)

    return pl.pallas_call(
        _ffn_kernel,
        grid=(m // FFN_ROW_TILE, n_ff),
        in_specs=[
            pl.BlockSpec(memory_space=pl.ANY),
            pl.BlockSpec((1, D_MODEL), lambda i, j: (0, 0)),
            pl.BlockSpec((pl.Element(D_MODEL), pl.Element(FF_TILE)),
                         lambda i, j: (0, ff_start(j))),
            pl.BlockSpec((pl.Element(D_MODEL), pl.Element(FF_TILE)),
                         lambda i, j: (0, pl.multiple_of(D_FF + ff_start(j), LANES))),
            pl.BlockSpec((pl.Element(FF_TILE), pl.Element(D_MODEL)),
                         lambda i, j: (ff_start(j), 0)),
        ],
        out_specs=pl.BlockSpec((FFN_ROW_TILE, D_MODEL), lambda i, j: (i, 0),
                               pipeline_mode=pl.Buffered(1)),
        out_shape=jax.ShapeDtypeStruct((m, D_MODEL), F32),
        scratch_shapes=[
            pltpu.VMEM((FFN_ROW_TILE, D_MODEL), BF16),
            pltpu.SemaphoreType.DMA((FFN_ROW_TILE // FFN_CHUNK,)),
        ],
        compiler_params=pltpu.CompilerParams(
            dimension_semantics=("arbitrary", "arbitrary"), vmem_limit_bytes=FFN_VMEM_LIMIT),
        name="ffn",
    )(x, gain.reshape(1, D_MODEL), w_in, w_in, w_out)


HEADS_PER_TILE = COL_TILE // HEAD_DIM
N_PROJ_TILES = D_QKV // COL_TILE
TILES_PER_GROUP = D_A // COL_TILE
A_TILES = 3 * TILES_PER_GROUP
AUG = 2 * HEAD_DIM
LOG2E = math.log2(math.e)


def _log_gate_scan(z, carry):
    c = (jnp.minimum(z, 0.0) - jnp.log1p(jnp.exp(-jnp.abs(z)))) * LOG2E
    row = lax.broadcasted_iota(jnp.int32, c.shape, 0)
    shift = 1
    while shift < c.shape[0]:
        c = c + jnp.where(row >= shift, pltpu.roll(c, shift, axis=0), 0.0)
        shift *= 2
    return c + carry


def _split3(c):
    hi = c.astype(BF16)
    rest = c - hi.astype(F32)
    mid = rest.astype(BF16)
    lo = (rest - mid.astype(F32)).astype(BF16)
    return hi, mid, lo


def _head_mean_matrix():
    head = np.arange(COL_TILE) // HEAD_DIM
    return jnp.asarray((head[:, None] == head[None, :]) / HEAD_DIM, BF16)


def _gate_placement_matrices():
    mats = np.zeros((2, TILES_PER_GROUP, COL_TILE, COL_TILE), np.float32)
    ones_row = 3 * LANES
    for part in range(TILES_PER_GROUP):
        for hh in range(HEADS_PER_TILE):
            head = part * HEADS_PER_TILE + hh
            col = hh * HEAD_DIM
            for term in range(3):
                mats[0, part, term * LANES + head, col + term] = 1.0
                mats[0, part, ones_row, col + 3 + term] = 1.0
                mats[1, part, term * LANES + head, col + 3 + term] = -1.0
                mats[1, part, ones_row, col + term] = 1.0
    return jnp.asarray(mats.reshape(2 * TILES_PER_GROUP, COL_TILE, COL_TILE), BF16)


def _proj_kernel(x_ref, g_ref, w_ref, wf_ref, fb_ref, hg_ref, mean_ref, place_ref,
                 a1_ref, a4_ref, a16_ref, qb_ref, kb_ref, vt_ref,
                 h_ref, cs_ref, carry_ref, y_ref, *, tiles_per_seq):
    i = pl.program_id(0)
    j = pl.program_id(1)

    @pl.when(j == 0)
    def _():
        h_ref[...] = _rms_rows(x_ref[...], g_ref[...]).astype(BF16)

        @pl.when(i % tiles_per_seq == 0)
        def _():
            carry_ref[...] = jnp.zeros(carry_ref.shape, F32)

        z = jnp.dot(h_ref[...], wf_ref[...], preferred_element_type=F32) + fb_ref[...]
        c = _log_gate_scan(z, carry_ref[...])
        carry_ref[...] = c[ROW_TILE - 1:ROW_TILE, :]
        for term, part in enumerate(_split3(c)):
            cs_ref[:, term * LANES:(term + 1) * LANES] = part
        cs_ref[:, 3 * LANES:] = jnp.ones((ROW_TILE, LANES), BF16)

    r = lax.dot_general(h_ref[...], w_ref[...].astype(BF16), (((1,), (1,)), ((), ())),
                        preferred_element_type=F32)

    def normed():
        ms = jnp.dot((r * r).astype(BF16), mean_ref[...], preferred_element_type=F32)
        return r * lax.rsqrt(ms + RMS_EPS) * hg_ref[...]

    def store_dilated(y):
        a1_ref[0, 0] = y.astype(BF16)
        for c in range(HEADS_PER_TILE):
            sl = slice(c * LANES, (c + 1) * LANES)
            y_ref[c] = y[:, sl]
            for ref in (a4_ref, a16_ref):
                dilation = ref.shape[1]
                for rc in range(dilation):
                    rows = y_ref[c, pl.ds(rc, ROW_TILE // dilation, stride=dilation), :]
                    ref[0, rc, :, sl] = rows.astype(BF16)

    for t in range(N_PROJ_TILES):
        group, part = divmod(t, TILES_PER_GROUP)

        @pl.when(j == t)
        def _(group=group):
            if group in (0, 1):
                store_dilated(normed())
            elif group == 2:
                store_dilated(r)
            elif group in (3, 4):
                out = qb_ref if group == 3 else kb_ref
                y = normed().astype(BF16)
                aug = jnp.dot(cs_ref[...], place_ref[0],
                              preferred_element_type=F32).astype(BF16)
                for hh in range(HEADS_PER_TILE):
                    sl = slice(hh * HEAD_DIM, (hh + 1) * HEAD_DIM)
                    out[0, hh, :, :HEAD_DIM] = y[:, sl]
                    out[0, hh, :, HEAD_DIM:] = aug[:, sl]
            else:
                for hh in range(HEADS_PER_TILE):
                    vt_ref[0, hh] = r[:, hh * HEAD_DIM:(hh + 1) * HEAD_DIM].T.astype(BF16)


def _proj(x, b, s, gain, w_in, layer, w_f, f_bias, head_gain):
    m = x.shape[0]
    tps = s // ROW_TILE

    def a_spec(dilation):
        return pl.BlockSpec((1, dilation, ROW_TILE // dilation, COL_TILE),
                            lambda i, j: (i // tps, 0, i % tps, jnp.minimum(j, A_TILES - 1)))

    def head_block(first):
        return lambda i, j: (i // tps, jnp.clip(j - first, 0, TILES_PER_GROUP - 1), i % tps, 0)

    qb_first = A_TILES
    vt_first = A_TILES + 2 * TILES_PER_GROUP
    dilations = [d for _, d in DILATED_PATTERNS]
    return pl.pallas_call(
        functools.partial(_proj_kernel, tiles_per_seq=tps),
        grid=(m // ROW_TILE, N_PROJ_TILES),
        in_specs=[
            pl.BlockSpec((ROW_TILE, D_MODEL), lambda i, j: (i, 0)),
            pl.BlockSpec((1, D_MODEL), lambda i, j: (0, 0)),
            pl.BlockSpec((None, COL_TILE, D_MODEL), lambda i, j: (layer, j, 0)),
            pl.BlockSpec((D_MODEL, LANES), lambda i, j: (0, 0)),
            pl.BlockSpec((1, LANES), lambda i, j: (0, 0)),
            pl.BlockSpec((1, COL_TILE), lambda i, j: (0, j)),
            pl.BlockSpec((COL_TILE, COL_TILE), lambda i, j: (0, 0)),
            pl.BlockSpec((1, COL_TILE, COL_TILE),
                         lambda i, j: (jnp.clip(j - qb_first, 0, 2 * TILES_PER_GROUP - 1), 0, 0)),
        ],
        out_specs=[a_spec(d) for d in dilations] + [
            pl.BlockSpec((1, HEADS_PER_TILE, ROW_TILE, AUG), head_block(qb_first)),
            pl.BlockSpec((1, HEADS_PER_TILE, ROW_TILE, AUG),
                         head_block(qb_first + TILES_PER_GROUP)),
            pl.BlockSpec((1, HEADS_PER_TILE, HEAD_DIM, ROW_TILE),
                         lambda i, j: (i // tps, jnp.clip(j - vt_first, 0, TILES_PER_GROUP - 1),
                                       0, i % tps)),
        ],
        out_shape=[jax.ShapeDtypeStruct((b, d, s // d, 3 * D_A), BF16) for d in dilations] + [
            jax.ShapeDtypeStruct((b, H_B, s, AUG), BF16),
            jax.ShapeDtypeStruct((b, H_B, s, AUG), BF16),
            jax.ShapeDtypeStruct((b, H_B, HEAD_DIM, s), BF16),
        ],
        scratch_shapes=[
            pltpu.VMEM((ROW_TILE, D_MODEL), BF16),
            pltpu.VMEM((ROW_TILE, 4 * LANES), BF16),
            pltpu.VMEM((1, LANES), F32),
            pltpu.VMEM((HEADS_PER_TILE, ROW_TILE, LANES), F32),
        ],
        compiler_params=_params("arbitrary", "arbitrary"),
        name="proj",
    )(x, gain.reshape(1, D_MODEL), jnp.swapaxes(w_in, 1, 2), w_f, f_bias, head_gain,
      _head_mean_matrix(), _gate_placement_matrices())


def _bucket_steps(dilation):
    dist = np.arange(BAND + 1) * dilation
    max_exact = NUM_BUCKETS // 2
    large = max_exact + np.floor(
        np.log(np.maximum(dist, 1) / max_exact) / math.log(MAX_DISTANCE / max_exact)
        * (NUM_BUCKETS - max_exact)).astype(np.int64)
    bucket = np.where(dist < max_exact, dist, np.minimum(large, NUM_BUCKETS - 1))
    steps = [(0, int(bucket[0]))]
    for delta in range(1, BAND + 1):
        if bucket[delta] != bucket[delta - 1]:
            steps.append((delta, int(bucket[delta])))
    return steps


def _bias_kernel(table_ref, o_ref):
    iq = lax.broadcasted_iota(jnp.int32, (BAND, 2 * BAND), 0)
    ik = lax.broadcasted_iota(jnp.int32, (BAND, 2 * BAND), 1)
    delta = iq + BAND - ik
    in_band = jnp.logical_and(delta >= 0, delta <= BAND)
    for p, (_, dilation) in enumerate(DILATED_PATTERNS):
        steps = _bucket_steps(dilation)
        for h in range(H_A):
            val = jnp.full((BAND, 2 * BAND), table_ref[steps[0][1], h], F32)
            for start, bucket in steps[1:]:
                val = jnp.where(delta >= start, table_ref[bucket, h], val)
            o_ref[p, h] = jnp.where(in_band, val, NEG_INF)


def _band_bias(rel_table):
    n_pat = len(DILATED_PATTERNS)
    return pl.pallas_call(
        _bias_kernel,
        in_specs=[pl.BlockSpec(memory_space=pltpu.SMEM)],
        out_specs=pl.BlockSpec(memory_space=pltpu.VMEM),
        out_shape=jax.ShapeDtypeStruct((n_pat, H_A, BAND, 2 * BAND), F32),
        name="band_bias",
    )(rel_table)


DIL_BLOCKS = 2


def _dilated_kernel(q_ref, kp_ref, kc_ref, vp_ref, vc_ref, bias_ref, o_ref, st_ref):
    first = pl.program_id(2) == 0
    key_lane = lax.broadcasted_iota(jnp.int32, (1, 2 * BAND), 1)
    no_prev = jnp.where(jnp.logical_and(first, key_lane < BAND), NEG_INF, 0.0)
    stat_lane = lax.broadcasted_iota(jnp.int32, (BAND, LANES), 1)

    def keys(prev_ref, cur_ref, blk, sl):
        if blk == 0:
            return jnp.concatenate([prev_ref[0, 0, :, sl], cur_ref[0, 0, :BAND, sl]], axis=0)
        return cur_ref[0, 0, (blk - 1) * BAND:(blk + 1) * BAND, sl]

    scores = {}
    for blk in range(DIL_BLOCKS):
        for h in range(H_A):
            sl = slice(h * HEAD_DIM, (h + 1) * HEAD_DIM)
            q = q_ref[0, 0, blk * BAND:(blk + 1) * BAND, sl]
            scores[blk, h] = lax.dot_general(q, keys(kp_ref, kc_ref, blk, sl),
                                             (((1,), (1,)), ((), ())),
                                             preferred_element_type=F32)
    for blk in range(DIL_BLOCKS):
        rows = slice(blk * BAND, (blk + 1) * BAND)
        stats = jnp.zeros((BAND, LANES), F32)
        for h in range(H_A):
            sl = slice(h * HEAD_DIM, (h + 1) * HEAD_DIM)
            s = scores[blk, h] + bias_ref[0, h]
            if blk == 0:
                s = s + no_prev
            mx = jnp.max(s, axis=-1, keepdims=True)
            p = jnp.exp(s - mx)
            den = jnp.sum(p, axis=-1, keepdims=True)
            o_ref[0, 0, rows, sl] = jnp.dot(p.astype(BF16), keys(vp_ref, vc_ref, blk, sl),
                                            preferred_element_type=F32).astype(o_ref.dtype)
            stats = jnp.where(stat_lane == h, mx, stats)
            stats = jnp.where(stat_lane == H_A + h, den, stats)
        st_ref[0, 0, rows, :] = stats


def _dilated(qkv, bias, pattern):
    b, dilation, n_sub, _ = qkv.shape
    rows = DIL_BLOCKS * BAND
    nb = n_sub // rows

    def cur(which):
        return pl.BlockSpec((1, 1, rows, D_A), lambda bi, r, i: (bi, r, i, which))

    def prev(which):
        return pl.BlockSpec((1, 1, BAND, D_A),
                            lambda bi, r, i: (bi, r, jnp.maximum(i * DIL_BLOCKS - 1, 0), which))

    return pl.pallas_call(
        _dilated_kernel,
        grid=(b, dilation, nb),
        in_specs=[
            cur(0), prev(1), cur(1), prev(2), cur(2),
            pl.BlockSpec((1, H_A, BAND, 2 * BAND), lambda bi, r, i: (pattern, 0, 0, 0)),
        ],
        out_specs=[
            pl.BlockSpec((1, 1, rows, D_A), lambda bi, r, i: (bi, r, i, 0)),
            pl.BlockSpec((1, 1, rows, LANES), lambda bi, r, i: (bi, r, i, 0)),
        ],
        out_shape=[
            jax.ShapeDtypeStruct((b, dilation, n_sub, D_A), BF16),
            jax.ShapeDtypeStruct((b, dilation, n_sub, LANES), F32),
        ],
        compiler_params=_params("parallel", "parallel", "arbitrary"),
        name=f"dilated_{dilation}",
    )(qkv, qkv, qkv, qkv, qkv, bias)


COMBINE_ROWS = 512


def _combine_kernel(*refs):
    n_pat = len(DILATED_PATTERNS)
    num_refs, st_refs = refs[:n_pat], refs[n_pat:2 * n_pat]
    o_ref, num_buf, st_buf = refs[2 * n_pat:]
    for p in range(n_pat):
        dilation = num_refs[p].shape[1]
        for rc in range(dilation):
            rows = pl.ds(rc, COMBINE_ROWS // dilation, stride=dilation)
            st_buf[p, rows, :] = st_refs[p][0, rc]
            for h in range(H_A):
                sl = slice(h * HEAD_DIM, (h + 1) * HEAD_DIM)
                num_buf[p, h, rows, :] = num_refs[p][0, rc, :, sl].astype(F32)
    stats = [st_buf[p] for p in range(n_pat)]
    for h in range(H_A):
        sl = slice(h * HEAD_DIM, (h + 1) * HEAD_DIM)
        mxs = [st[:, h:h + 1] for st in stats]
        dens = [st[:, H_A + h:H_A + h + 1] for st in stats]
        mx = functools.reduce(jnp.maximum, mxs)
        num = None
        den = None
        for p, (pden, pmx) in enumerate(zip(dens, mxs)):
            scale = jnp.exp(pmx - mx)
            pnum = num_buf[p, h]
            num = pnum * scale if num is None else num + pnum * scale
            den = pden * scale if den is None else den + pden * scale
        o_ref[:, sl] = (num / den).astype(BF16)


def _combine(nums, stats):
    b, _, s, _ = nums[0].shape
    tps = s // COMBINE_ROWS

    def spec(arr):
        dilation, width = arr.shape[1], arr.shape[3]
        return pl.BlockSpec((1, dilation, COMBINE_ROWS // dilation, width),
                            lambda i: (i // tps, 0, i % tps, 0))

    n_pat = len(nums)
    return pl.pallas_call(
        _combine_kernel,
        grid=(b * tps,),
        in_specs=[spec(a) for a in nums] + [spec(a) for a in stats],
        out_specs=pl.BlockSpec((COMBINE_ROWS, D_A), lambda i: (i, 0)),
        out_shape=jax.ShapeDtypeStruct((b * s, D_A), BF16),
        scratch_shapes=[
            pltpu.VMEM((n_pat, H_A, COMBINE_ROWS, HEAD_DIM), F32),
            pltpu.VMEM((n_pat, COMBINE_ROWS, LANES), F32),
        ],
        compiler_params=_params("parallel"),
        name="combine_a",
    )(*nums, *stats)


def _fox_kernel(q_ref, k_ref, vt_ref, o_ref, m_ref, l_ref, acc_ref, sa_ref, sb_ref):
    iq = pl.program_id(2)
    m_ref[...] = jnp.full(m_ref.shape, NEG_INF, F32)
    l_ref[...] = jnp.zeros(l_ref.shape, F32)
    acc_ref[...] = jnp.zeros(acc_ref.shape, F32)

    def scores_into(s_ref, kb):
        start = pl.multiple_of(kb * FOX_TK, FOX_TK)
        for hh in range(FOX_HEADS):
            k = k_ref[0, hh, pl.ds(start, FOX_TK), :]
            s_ref[hh] = lax.dot_general(k, q_ref[0, hh], (((1,), (1,)), ((), ())),
                                        preferred_element_type=F32)

    def accumulate(s_ref, kb, diagonal):
        start = pl.multiple_of(kb * FOX_TK, FOX_TK)
        for hh in range(FOX_HEADS):
            st = s_ref[hh]
            if diagonal:
                key = lax.broadcasted_iota(jnp.int32, st.shape, 0)
                qry = lax.broadcasted_iota(jnp.int32, st.shape, 1)
                st = jnp.where(key <= qry, st, NEG_INF)
            m_prev = m_ref[hh]
            m_new = jnp.maximum(m_prev, jnp.max(st, axis=0, keepdims=True))
            alpha = jnp.exp2(m_prev - m_new)
            p = jnp.exp2(st - m_new)
            l_ref[hh] = alpha * l_ref[hh] + jnp.sum(p, axis=0, keepdims=True)
            vt = vt_ref[0, hh, :, pl.ds(start, FOX_TK)]
            acc_ref[hh] = alpha * acc_ref[hh] + jnp.dot(vt, p.astype(BF16),
                                                        preferred_element_type=F32)
            m_ref[hh] = m_new

    scores_into(sa_ref, 0)

    def pair(p, carry):
        scores_into(sb_ref, 2 * p + 1)
        accumulate(sa_ref, 2 * p, False)
        scores_into(sa_ref, 2 * p + 2)
        accumulate(sb_ref, 2 * p + 1, False)
        return carry

    lax.fori_loop(0, iq // 2, pair, 0)

    @pl.when(iq % 2 == 0)
    def _():
        accumulate(sa_ref, iq, True)

    @pl.when(iq % 2 == 1)
    def _():
        scores_into(sb_ref, iq)
        accumulate(sa_ref, iq - 1, False)
        accumulate(sb_ref, iq, True)
    for hh in range(FOX_HEADS):
        o_ref[0, :, hh * HEAD_DIM:(hh + 1) * HEAD_DIM] = (
            acc_ref[hh] / l_ref[hh]).T.astype(BF16)


def _fox(q_aug, k_aug, v_t):
    b, h_b, s, _ = q_aug.shape
    assert FOX_TQ == FOX_TK
    return pl.pallas_call(
        _fox_kernel,
        grid=(b, h_b // FOX_HEADS, s // FOX_TQ),
        in_specs=[
            pl.BlockSpec((1, FOX_HEADS, FOX_TQ, AUG), lambda bi, h, i: (bi, h, i, 0)),
            pl.BlockSpec((1, FOX_HEADS, s, AUG), lambda bi, h, i: (bi, h, 0, 0)),
            pl.BlockSpec((1, FOX_HEADS, HEAD_DIM, s), lambda bi, h, i: (bi, h, 0, 0)),
        ],
        out_specs=pl.BlockSpec((1, FOX_TQ, FOX_HEADS * HEAD_DIM), lambda bi, h, i: (bi, i, h)),
        out_shape=jax.ShapeDtypeStruct((b, s, h_b * HEAD_DIM), BF16),
        scratch_shapes=[
            pltpu.VMEM((FOX_HEADS, 1, FOX_TQ), F32),
            pltpu.VMEM((FOX_HEADS, 1, FOX_TQ), F32),
            pltpu.VMEM((FOX_HEADS, HEAD_DIM, FOX_TQ), F32),
            pltpu.VMEM((FOX_HEADS, FOX_TK, FOX_TQ), F32),
            pltpu.VMEM((FOX_HEADS, FOX_TK, FOX_TQ), F32),
        ],
        compiler_params=_params("parallel", "parallel", "arbitrary"),
        name="fox",
    )(q_aug, k_aug, v_t)


def _out_proj_kernel(a_ref, b_ref, w_ref, x_ref, o_ref):
    mixed = jnp.concatenate([a_ref[...], b_ref[...]], axis=-1)
    o_ref[...] = x_ref[...] + jnp.dot(mixed, w_ref[...].astype(BF16),
                                      preferred_element_type=F32)


OUT_ROW_TILE = 2048


def _out_proj(out_a, out_b, w_out, x):
    m = x.shape[0]
    return pl.pallas_call(
        _out_proj_kernel,
        grid=(m // OUT_ROW_TILE, D_MODEL // COL_TILE),
        in_specs=[
            pl.BlockSpec((OUT_ROW_TILE, D_A), lambda i, j: (i, 0)),
            pl.BlockSpec((OUT_ROW_TILE, D_B), lambda i, j: (i, 0)),
            pl.BlockSpec((D_A + D_B, COL_TILE), lambda i, j: (0, j)),
            pl.BlockSpec((OUT_ROW_TILE, COL_TILE), lambda i, j: (i, j)),
        ],
        out_specs=pl.BlockSpec((OUT_ROW_TILE, COL_TILE), lambda i, j: (i, j)),
        out_shape=jax.ShapeDtypeStruct((m, D_MODEL), F32),
        compiler_params=_params("parallel", "parallel"),
        name="out_proj",
    )(out_a, out_b, w_out, x)


def _mixer(x, b, s, mix_norm, w_in, layer, q_norm_a, k_norm_a, q_norm_b, k_norm_b, forget_bias,
           rel_bias_table, w_out):
    ones = jnp.ones((D_A,), F32)
    head_gain = jnp.concatenate([
        jnp.tile(q_norm_a * ATTN_SCALE, H_A), jnp.tile(k_norm_a, H_A), ones,
        jnp.tile(q_norm_b * (ATTN_SCALE * LOG2E), H_B), jnp.tile(k_norm_b, H_B), ones,
    ]).reshape(1, D_QKV)
    w_f = jnp.pad(w_in[layer, :, D_QKV:], ((0, 0), (0, LANES - H_B))).astype(BF16)
    f_bias = jnp.pad(forget_bias, (0, LANES - H_B)).reshape(1, LANES)
    *qkv_a, q_aug, k_aug, v_t = _proj(x, b, s, mix_norm, w_in, layer, w_f, f_bias, head_gain)

    out_b = _fox(q_aug, k_aug, v_t).reshape(b * s, D_B)

    bias = _band_bias(rel_bias_table)
    parts = [_dilated(qkv, bias, p) for p, qkv in enumerate(qkv_a)]
    out_a = _combine([pt[0] for pt in parts], [pt[1] for pt in parts])

    return _out_proj(out_a, out_b, w_out, x)


def kernel(x, ffn1_norm, ffn1_w_in, ffn1_w_out, mix_norm, w_in, q_norm_a, k_norm_a, q_norm_b,
           k_norm_b, forget_bias, rel_bias_table, w_out, ffn2_norm, ffn2_w_in, ffn2_w_out):
    b, s, d = x.shape
    depth = ffn1_norm.shape[0]
    x = x.reshape(b * s, d)
    for l in range(depth):
        x = _ffn(x, ffn1_norm[l], ffn1_w_in[l], ffn1_w_out[l])
        x = _mixer(x, b, s, mix_norm[l], w_in, l, q_norm_a[l], k_norm_a[l], q_norm_b[l],
                   k_norm_b[l], forget_bias[l], rel_bias_table, w_out[l])
        x = _ffn(x, ffn2_norm[l], ffn2_w_in[l], ffn2_w_out[l])
    return x.reshape(b, s, d)
```

```python
import functools
import math

import numpy as np
import jax
import jax.numpy as jnp
from jax import lax
from jax.experimental import pallas as pl
from jax.experimental.pallas import tpu as pltpu

D_MODEL = 2048
HEAD_DIM = 128
N_HEADS = D_MODEL // HEAD_DIM
H_A = N_HEADS // 2
H_B = N_HEADS - H_A
D_A = H_A * HEAD_DIM
D_B = H_B * HEAD_DIM
D_QKV = 3 * D_A + 3 * D_B
DILATED_PATTERNS = ((128, 1), (512, 4), (2048, 16))
BAND = 128
NUM_BUCKETS = 32
MAX_DISTANCE = 2048
D_FF = ((8 * D_MODEL // 3 + 127) // 128) * 128
RMS_EPS = 1e-6
NEG_INF = -1e30
ATTN_SCALE = HEAD_DIM ** -0.5

LANES = 128
FF_TILE = 512
FFN_ROW_TILE = 2048
FFN_CHUNK = 512
FFN_VMEM_LIMIT = 60 * 1024 * 1024
ROW_TILE = 1024
COL_TILE = 512
FOX_TQ = 512
FOX_TK = 512
FOX_HEADS = 4
VMEM_LIMIT = 56 * 1024 * 1024

F32 = jnp.float32
BF16 = jnp.bfloat16


def _params(*sem):
    return pltpu.CompilerParams(dimension_semantics=sem, vmem_limit_bytes=VMEM_LIMIT)


def _rms_rows(x, gain):
    ms = jnp.mean(x * x, axis=-1, keepdims=True)
    return x * lax.rsqrt(ms + RMS_EPS) * gain


def _ffn_kernel(x_hbm, g_ref, wg_ref, wu_ref, wo_ref, o_ref, h_ref, sem):
    i = pl.program_id(0)
    j = pl.program_id(1)

    @pl.when(j == 0)
    def _():
        def x_copy(c):
            src = x_hbm.at[pl.ds(i * FFN_ROW_TILE + c * FFN_CHUNK, FFN_CHUNK), :]
            return pltpu.make_async_copy(src, o_ref.at[pl.ds(c * FFN_CHUNK, FFN_CHUNK), :],
                                         sem.at[c])

        n_chunks = FFN_ROW_TILE // FFN_CHUNK
        for c in range(n_chunks):
            x_copy(c).start()
        for c in range(n_chunks):
            x_copy(c).wait()
            rows = slice(c * FFN_CHUNK, (c + 1) * FFN_CHUNK)
            h_ref[rows, :] = _rms_rows(o_ref[rows, :], g_ref[...]).astype(BF16)

    col = lax.broadcasted_iota(jnp.int32, (1, FF_TILE), 1)
    repeated = j * FF_TILE - jnp.minimum(j * FF_TILE, D_FF - FF_TILE)
    for r in range(FFN_ROW_TILE // ROW_TILE):
        rows = slice(r * ROW_TILE, (r + 1) * ROW_TILE)
        h = h_ref[rows, :]
        gate = jnp.dot(h, wg_ref[...].astype(BF16), preferred_element_type=F32)
        up = jnp.dot(h, wu_ref[...].astype(BF16), preferred_element_type=F32)
        act = gate * (1.0 / (1.0 + jnp.exp(-gate))) * (0.5 * up)
        act = jnp.where(col >= repeated, act, 0.0).astype(BF16)
        for c in range(D_MODEL // COL_TILE):
            sl = slice(c * COL_TILE, (c + 1) * COL_TILE)
            o_ref[rows, sl] += jnp.dot(act, wo_ref[:, sl].astype(BF16),
                                       preferred_element_type=F32)


def _ffn(x, gain, w_in, w_out):
    m = x.shape[0]
    n_ff = pl.cdiv(D_FF, FF_TILE)

    def ff_start(j):
        return pl.multiple_of(jnp.minimum(j * FF_TILE, D_FF - FF_TILE), LANES)

    return pl.pallas_call(
        _ffn_kernel,
        grid=(m // FFN_ROW_TILE, n_ff),
        in_specs=[
            pl.BlockSpec(memory_space=pl.ANY),
            pl.BlockSpec((1, D_MODEL), lambda i, j: (0, 0)),
            pl.BlockSpec((pl.Element(D_MODEL), pl.Element(FF_TILE)),
                         lambda i, j: (0, ff_start(j))),
            pl.BlockSpec((pl.Element(D_MODEL), pl.Element(FF_TILE)),
                         lambda i, j: (0, pl.multiple_of(D_FF + ff_start(j), LANES))),
            pl.BlockSpec((pl.Element(FF_TILE), pl.Element(D_MODEL)),
                         lambda i, j: (ff_start(j), 0)),
        ],
        out_specs=pl.BlockSpec((FFN_ROW_TILE, D_MODEL), lambda i, j: (i, 0),
                               pipeline_mode=pl.Buffered(1)),
        out_shape=jax.ShapeDtypeStruct((m, D_MODEL), F32),
        scratch_shapes=[
            pltpu.VMEM((FFN_ROW_TILE, D_MODEL), BF16),
            pltpu.SemaphoreType.DMA((FFN_ROW_TILE // FFN_CHUNK,)),
        ],
        compiler_params=pltpu.CompilerParams(
            dimension_semantics=("arbitrary", "arbitrary"), vmem_limit_bytes=FFN_VMEM_LIMIT),
        name="ffn",
    )(x, gain.reshape(1, D_MODEL), w_in, w_in, w_out)


HEADS_PER_TILE = COL_TILE // HEAD_DIM
N_PROJ_TILES = D_QKV // COL_TILE
TILES_PER_GROUP = D_A // COL_TILE
A_TILES = 3 * TILES_PER_GROUP
AUG = 2 * HEAD_DIM
LOG2E = math.log2(math.e)


def _log_gate_scan(z, carry):
    c = (jnp.minimum(z, 0.0) - jnp.log1p(jnp.exp(-jnp.abs(z)))) * LOG2E
    row = lax.broadcasted_iota(jnp.int32, c.shape, 0)
    shift = 1
    while shift < c.shape[0]:
        c = c + jnp.where(row >= shift, pltpu.roll(c, shift, axis=0), 0.0)
        shift *= 2
    return c + carry


def _split3(c):
    hi = c.astype(BF16)
    rest = c - hi.astype(F32)
    mid = rest.astype(BF16)
    lo = (rest - mid.astype(F32)).astype(BF16)
    return hi, mid, lo


def _head_mean_matrix():
    head = np.arange(COL_TILE) // HEAD_DIM
    return jnp.asarray((head[:, None] == head[None, :]) / HEAD_DIM, BF16)


def _gate_placement_matrices():
    mats = np.zeros((2, TILES_PER_GROUP, COL_TILE, COL_TILE), np.float32)
    ones_row = 3 * LANES
    for part in range(TILES_PER_GROUP):
        for hh in range(HEADS_PER_TILE):
            head = part * HEADS_PER_TILE + hh
            col = hh * HEAD_DIM
            for term in range(3):
                mats[0, part, term * LANES + head, col + term] = 1.0
                mats[0, part, ones_row, col + 3 + term] = 1.0
                mats[1, part, term * LANES + head, col + 3 + term] = -1.0
                mats[1, part, ones_row, col + term] = 1.0
    return jnp.asarray(mats.reshape(2 * TILES_PER_GROUP, COL_TILE, COL_TILE), BF16)


def _proj_kernel(x_ref, g_ref, w_ref, wf_ref, fb_ref, hg_ref, mean_ref, place_ref,
                 a1_ref, a4_ref, a16_ref, qb_ref, kb_ref, vt_ref,
                 h_ref, cs_ref, carry_ref, y_ref, *, tiles_per_seq):
    i = pl.program_id(0)
    j = pl.program_id(1)

    @pl.when(j == 0)
    def _():
        h_ref[...] = _rms_rows(x_ref[...], g_ref[...]).astype(BF16)

        @pl.when(i % tiles_per_seq == 0)
        def _():
            carry_ref[...] = jnp.zeros(carry_ref.shape, F32)

        z = jnp.dot(h_ref[...], wf_ref[...], preferred_element_type=F32) + fb_ref[...]
        c = _log_gate_scan(z, carry_ref[...])
        carry_ref[...] = c[ROW_TILE - 1:ROW_TILE, :]
        for term, part in enumerate(_split3(c)):
            cs_ref[:, term * LANES:(term + 1) * LANES] = part
        cs_ref[:, 3 * LANES:] = jnp.ones((ROW_TILE, LANES), BF16)

    r = lax.dot_general(h_ref[...], w_ref[...].astype(BF16), (((1,), (1,)), ((), ())),
                        preferred_element_type=F32)

    def normed():
        ms = jnp.dot((r * r).astype(BF16), mean_ref[...], preferred_element_type=F32)
        return r * lax.rsqrt(ms + RMS_EPS) * hg_ref[...]

    def store_dilated(y):
        a1_ref[0, 0] = y.astype(BF16)
        for c in range(HEADS_PER_TILE):
            sl = slice(c * LANES, (c + 1) * LANES)
            y_ref[c] = y[:, sl]
            for ref in (a4_ref, a16_ref):
                dilation = ref.shape[1]
                for rc in range(dilation):
                    rows = y_ref[c, pl.ds(rc, ROW_TILE // dilation, stride=dilation), :]
                    ref[0, rc, :, sl] = rows.astype(BF16)

    for t in range(N_PROJ_TILES):
        group, part = divmod(t, TILES_PER_GROUP)

        @pl.when(j == t)
        def _(group=group):
            if group in (0, 1):
                store_dilated(normed())
            elif group == 2:
                store_dilated(r)
            elif group in (3, 4):
                out = qb_ref if group == 3 else kb_ref
                y = normed().astype(BF16)
                aug = jnp.dot(cs_ref[...], place_ref[0],
                              preferred_element_type=F32).astype(BF16)
                for hh in range(HEADS_PER_TILE):
                    sl = slice(hh * HEAD_DIM, (hh + 1) * HEAD_DIM)
                    out[0, hh, :, :HEAD_DIM] = y[:, sl]
                    out[0, hh, :, HEAD_DIM:] = aug[:, sl]
            else:
                for hh in range(HEADS_PER_TILE):
                    vt_ref[0, hh] = r[:, hh * HEAD_DIM:(hh + 1) * HEAD_DIM].T.astype(BF16)


def _proj(x, b, s, gain, w_in, layer, w_f, f_bias, head_gain):
    m = x.shape[0]
    tps = s // ROW_TILE

    def a_spec(dilation):
        return pl.BlockSpec((1, dilation, ROW_TILE // dilation, COL_TILE),
                            lambda i, j: (i // tps, 0, i % tps, jnp.minimum(j, A_TILES - 1)))

    def head_block(first):
        return lambda i, j: (i // tps, jnp.clip(j - first, 0, TILES_PER_GROUP - 1), i % tps, 0)

    qb_first = A_TILES
    vt_first = A_TILES + 2 * TILES_PER_GROUP
    dilations = [d for _, d in DILATED_PATTERNS]
    return pl.pallas_call(
        functools.partial(_proj_kernel, tiles_per_seq=tps),
        grid=(m // ROW_TILE, N_PROJ_TILES),
        in_specs=[
            pl.BlockSpec((ROW_TILE, D_MODEL), lambda i, j: (i, 0)),
            pl.BlockSpec((1, D_MODEL), lambda i, j: (0, 0)),
            pl.BlockSpec((None, COL_TILE, D_MODEL), lambda i, j: (layer, j, 0)),
            pl.BlockSpec((D_MODEL, LANES), lambda i, j: (0, 0)),
            pl.BlockSpec((1, LANES), lambda i, j: (0, 0)),
            pl.BlockSpec((1, COL_TILE), lambda i, j: (0, j)),
            pl.BlockSpec((COL_TILE, COL_TILE), lambda i, j: (0, 0)),
            pl.BlockSpec((1, COL_TILE, COL_TILE),
                         lambda i, j: (jnp.clip(j - qb_first, 0, 2 * TILES_PER_GROUP - 1), 0, 0)),
        ],
        out_specs=[a_spec(d) for d in dilations] + [
            pl.BlockSpec((1, HEADS_PER_TILE, ROW_TILE, AUG), head_block(qb_first)),
            pl.BlockSpec((1, HEADS_PER_TILE, ROW_TILE, AUG),
                         head_block(qb_first + TILES_PER_GROUP)),
            pl.BlockSpec((1, HEADS_PER_TILE, HEAD_DIM, ROW_TILE),
                         lambda i, j: (i // tps, jnp.clip(j - vt_first, 0, TILES_PER_GROUP - 1),
                                       0, i % tps)),
        ],
        out_shape=[jax.ShapeDtypeStruct((b, d, s // d, 3 * D_A), BF16) for d in dilations] + [
            jax.ShapeDtypeStruct((b, H_B, s, AUG), BF16),
            jax.ShapeDtypeStruct((b, H_B, s, AUG), BF16),
            jax.ShapeDtypeStruct((b, H_B, HEAD_DIM, s), BF16),
        ],
        scratch_shapes=[
            pltpu.VMEM((ROW_TILE, D_MODEL), BF16),
            pltpu.VMEM((ROW_TILE, 4 * LANES), BF16),
            pltpu.VMEM((1, LANES), F32),
            pltpu.VMEM((HEADS_PER_TILE, ROW_TILE, LANES), F32),
        ],
        compiler_params=_params("arbitrary", "arbitrary"),
        name="proj",
    )(x, gain.reshape(1, D_MODEL), jnp.swapaxes(w_in, 1, 2), w_f, f_bias, head_gain,
      _head_mean_matrix(), _gate_placement_matrices())


def _bucket_steps(dilation):
    dist = np.arange(BAND + 1) * dilation
    max_exact = NUM_BUCKETS // 2
    large = max_exact + np.floor(
        np.log(np.maximum(dist, 1) / max_exact) / math.log(MAX_DISTANCE / max_exact)
        * (NUM_BUCKETS - max_exact)).astype(np.int64)
    bucket = np.where(dist < max_exact, dist, np.minimum(large, NUM_BUCKETS - 1))
    steps = [(0, int(bucket[0]))]
    for delta in range(1, BAND + 1):
        if bucket[delta] != bucket[delta - 1]:
            steps.append((delta, int(bucket[delta])))
    return steps


def _bias_kernel(table_ref, o_ref):
    iq = lax.broadcasted_iota(jnp.int32, (BAND, 2 * BAND), 0)
    ik = lax.broadcasted_iota(jnp.int32, (BAND, 2 * BAND), 1)
    delta = iq + BAND - ik
    in_band = jnp.logical_and(delta >= 0, delta <= BAND)
    for p, (_, dilation) in enumerate(DILATED_PATTERNS):
        steps = _bucket_steps(dilation)
        for h in range(H_A):
            val = jnp.full((BAND, 2 * BAND), table_ref[steps[0][1], h], F32)
            for start, bucket in steps[1:]:
                val = jnp.where(delta >= start, table_ref[bucket, h], val)
            o_ref[p, h] = jnp.where(in_band, val, NEG_INF)


def _band_bias(rel_table):
    n_pat = len(DILATED_PATTERNS)
    return pl.pallas_call(
        _bias_kernel,
        in_specs=[pl.BlockSpec(memory_space=pltpu.SMEM)],
        out_specs=pl.BlockSpec(memory_space=pltpu.VMEM),
        out_shape=jax.ShapeDtypeStruct((n_pat, H_A, BAND, 2 * BAND), F32),
        name="band_bias",
    )(rel_table)


DIL_UNITS = 4


def _dilated_kernel(q_ref, kp_ref, kc_ref, vp_ref, vc_ref, bias_ref, o_ref, st_ref):
    n_cls = q_ref.shape[1]
    n_blk = q_ref.shape[2] // BAND
    first = pl.program_id(2) == 0
    key_lane = lax.broadcasted_iota(jnp.int32, (1, 2 * BAND), 1)
    no_prev = jnp.where(jnp.logical_and(first, key_lane < BAND), NEG_INF, 0.0)
    stat_lane = lax.broadcasted_iota(jnp.int32, (BAND, LANES), 1)
    units = [(c, blk) for c in range(n_cls) for blk in range(n_blk)]

    def keys(prev_ref, cur_ref, c, blk, sl):
        if blk == 0:
            return jnp.concatenate([prev_ref[0, c, :, sl], cur_ref[0, c, :BAND, sl]], axis=0)
        return cur_ref[0, c, (blk - 1) * BAND:(blk + 1) * BAND, sl]

    scores = {}
    for c, blk in units:
        for h in range(H_A):
            sl = slice(h * HEAD_DIM, (h + 1) * HEAD_DIM)
            q = q_ref[0, c, blk * BAND:(blk + 1) * BAND, sl]
            scores[c, blk, h] = lax.dot_general(q, keys(kp_ref, kc_ref, c, blk, sl),
                                                (((1,), (1,)), ((), ())),
                                                preferred_element_type=F32)
    for c, blk in units:
        rows = slice(blk * BAND, (blk + 1) * BAND)
        stats = jnp.zeros((BAND, LANES), F32)
        for h in range(H_A):
            sl = slice(h * HEAD_DIM, (h + 1) * HEAD_DIM)
            s = scores[c, blk, h] + bias_ref[0, h]
            if blk == 0:
                s = s + no_prev
            mx = jnp.max(s, axis=-1, keepdims=True)
            p = jnp.exp(s - mx)
            den = jnp.sum(p, axis=-1, keepdims=True)
            o_ref[0, c, rows, sl] = jnp.dot(p.astype(BF16), keys(vp_ref, vc_ref, c, blk, sl),
                                            preferred_element_type=F32).astype(o_ref.dtype)
            stats = jnp.where(stat_lane == h, mx, stats)
            stats = jnp.where(stat_lane == H_A + h, den, stats)
        st_ref[0, c, rows, :] = stats


def _dilated(qkv, bias, pattern):
    b, dilation, n_sub, _ = qkv.shape
    n_blk = min(DIL_UNITS, n_sub // BAND)
    n_cls = DIL_UNITS // n_blk
    rows = n_blk * BAND
    nb = n_sub // rows

    def cur(which):
        return pl.BlockSpec((1, n_cls, rows, D_A), lambda bi, r, i: (bi, r, i, which))

    def prev(which):
        return pl.BlockSpec((1, n_cls, BAND, D_A),
                            lambda bi, r, i: (bi, r, jnp.maximum(i * n_blk - 1, 0), which))

    return pl.pallas_call(
        _dilated_kernel,
        grid=(b, dilation // n_cls, nb),
        in_specs=[
            cur(0), prev(1), cur(1), prev(2), cur(2),
            pl.BlockSpec((1, H_A, BAND, 2 * BAND), lambda bi, r, i: (pattern, 0, 0, 0)),
        ],
        out_specs=[
            pl.BlockSpec((1, n_cls, rows, D_A), lambda bi, r, i: (bi, r, i, 0)),
            pl.BlockSpec((1, n_cls, rows, LANES), lambda bi, r, i: (bi, r, i, 0)),
        ],
        out_shape=[
            jax.ShapeDtypeStruct((b, dilation, n_sub, D_A), BF16),
            jax.ShapeDtypeStruct((b, dilation, n_sub, LANES), F32),
        ],
        compiler_params=_params("parallel", "parallel", "arbitrary"),
        name=f"dilated_{dilation}",
    )(qkv, qkv, qkv, qkv, qkv, bias)


COMBINE_ROWS = 512


def _combine_kernel(*refs):
    n_pat = len(DILATED_PATTERNS)
    num_refs, st_refs = refs[:n_pat], refs[n_pat:2 * n_pat]
    o_ref, num_buf, st_buf = refs[2 * n_pat:]
    for p in range(n_pat):
        dilation = num_refs[p].shape[1]
        for rc in range(dilation):
            rows = pl.ds(rc, COMBINE_ROWS // dilation, stride=dilation)
            st_buf[p, rows, :] = st_refs[p][0, rc]
            for h in range(H_A):
                sl = slice(h * HEAD_DIM, (h + 1) * HEAD_DIM)
                num_buf[p, h, rows, :] = num_refs[p][0, rc, :, sl].astype(F32)
    stats = [st_buf[p] for p in range(n_pat)]
    for h in range(H_A):
        sl = slice(h * HEAD_DIM, (h + 1) * HEAD_DIM)
        mxs = [st[:, h:h + 1] for st in stats]
        dens = [st[:, H_A + h:H_A + h + 1] for st in stats]
        mx = functools.reduce(jnp.maximum, mxs)
        num = None
        den = None
        for p, (pden, pmx) in enumerate(zip(dens, mxs)):
            scale = jnp.exp(pmx - mx)
            pnum = num_buf[p, h]
            num = pnum * scale if num is None else num + pnum * scale
            den = pden * scale if den is None else den + pden * scale
        o_ref[:, sl] = (num / den).astype(BF16)


def _combine(nums, stats):
    b, _, s, _ = nums[0].shape
    tps = s // COMBINE_ROWS

    def spec(arr):
        dilation, width = arr.shape[1], arr.shape[3]
        return pl.BlockSpec((1, dilation, COMBINE_ROWS // dilation, width),
                            lambda i: (i // tps, 0, i % tps, 0))

    n_pat = len(nums)
    return pl.pallas_call(
        _combine_kernel,
        grid=(b * tps,),
        in_specs=[spec(a) for a in nums] + [spec(a) for a in stats],
        out_specs=pl.BlockSpec((COMBINE_ROWS, D_A), lambda i: (i, 0)),
        out_shape=jax.ShapeDtypeStruct((b * s, D_A), BF16),
        scratch_shapes=[
            pltpu.VMEM((n_pat, H_A, COMBINE_ROWS, HEAD_DIM), F32),
            pltpu.VMEM((n_pat, COMBINE_ROWS, LANES), F32),
        ],
        compiler_params=_params("parallel"),
        name="combine_a",
    )(*nums, *stats)


def _fox_kernel(q_ref, k_ref, vt_ref, o_ref, m_ref, l_ref, acc_ref, sa_ref, sb_ref):
    iq = pl.program_id(2)
    m_ref[...] = jnp.full(m_ref.shape, NEG_INF, F32)
    l_ref[...] = jnp.zeros(l_ref.shape, F32)
    acc_ref[...] = jnp.zeros(acc_ref.shape, F32)

    def scores_into(s_ref, kb):
        start = pl.multiple_of(kb * FOX_TK, FOX_TK)
        for hh in range(FOX_HEADS):
            k = k_ref[0, hh, pl.ds(start, FOX_TK), :]
            s_ref[hh] = lax.dot_general(k, q_ref[0, hh], (((1,), (1,)), ((), ())),
                                        preferred_element_type=F32)

    def accumulate(s_ref, kb, diagonal):
        start = pl.multiple_of(kb * FOX_TK, FOX_TK)
        for hh in range(FOX_HEADS):
            st = s_ref[hh]
            if diagonal:
                key = lax.broadcasted_iota(jnp.int32, st.shape, 0)
                qry = lax.broadcasted_iota(jnp.int32, st.shape, 1)
                st = jnp.where(key <= qry, st, NEG_INF)
            m_prev = m_ref[hh]
            m_new = jnp.maximum(m_prev, jnp.max(st, axis=0, keepdims=True))
            alpha = jnp.exp2(m_prev - m_new)
            p = jnp.exp2(st - m_new)
            l_ref[hh] = alpha * l_ref[hh] + jnp.sum(p, axis=0, keepdims=True)
            vt = vt_ref[0, hh, :, pl.ds(start, FOX_TK)]
            acc_ref[hh] = alpha * acc_ref[hh] + jnp.dot(vt, p.astype(BF16),
                                                        preferred_element_type=F32)
            m_ref[hh] = m_new

    scores_into(sa_ref, 0)

    def pair(p, carry):
        scores_into(sb_ref, 2 * p + 1)
        accumulate(sa_ref, 2 * p, False)
        scores_into(sa_ref, 2 * p + 2)
        accumulate(sb_ref, 2 * p + 1, False)
        return carry

    lax.fori_loop(0, iq // 2, pair, 0)

    @pl.when(iq % 2 == 0)
    def _():
        accumulate(sa_ref, iq, True)

    @pl.when(iq % 2 == 1)
    def _():
        scores_into(sb_ref, iq)
        accumulate(sa_ref, iq - 1, False)
        accumulate(sb_ref, iq, True)
    for hh in range(FOX_HEADS):
        o_ref[0, :, hh * HEAD_DIM:(hh + 1) * HEAD_DIM] = (
            acc_ref[hh] / l_ref[hh]).T.astype(BF16)


def _fox(q_aug, k_aug, v_t):
    b, h_b, s, _ = q_aug.shape
    assert FOX_TQ == FOX_TK
    return pl.pallas_call(
        _fox_kernel,
        grid=(b, h_b // FOX_HEADS, s // FOX_TQ),
        in_specs=[
            pl.BlockSpec((1, FOX_HEADS, FOX_TQ, AUG), lambda bi, h, i: (bi, h, i, 0)),
            pl.BlockSpec((1, FOX_HEADS, s, AUG), lambda bi, h, i: (bi, h, 0, 0)),
            pl.BlockSpec((1, FOX_HEADS, HEAD_DIM, s), lambda bi, h, i: (bi, h, 0, 0)),
        ],
        out_specs=pl.BlockSpec((1, FOX_TQ, FOX_HEADS * HEAD_DIM), lambda bi, h, i: (bi, i, h)),
        out_shape=jax.ShapeDtypeStruct((b, s, h_b * HEAD_DIM), BF16),
        scratch_shapes=[
            pltpu.VMEM((FOX_HEADS, 1, FOX_TQ), F32),
            pltpu.VMEM((FOX_HEADS, 1, FOX_TQ), F32),
            pltpu.VMEM((FOX_HEADS, HEAD_DIM, FOX_TQ), F32),
            pltpu.VMEM((FOX_HEADS, FOX_TK, FOX_TQ), F32),
            pltpu.VMEM((FOX_HEADS, FOX_TK, FOX_TQ), F32),
        ],
        compiler_params=_params("parallel", "parallel", "arbitrary"),
        name="fox",
    )(q_aug, k_aug, v_t)


def _out_proj_kernel(a_ref, b_ref, w_ref, x_ref, o_ref):
    mixed = jnp.concatenate([a_ref[...], b_ref[...]], axis=-1)
    o_ref[...] = x_ref[...] + jnp.dot(mixed, w_ref[...].astype(BF16),
                                      preferred_element_type=F32)


OUT_ROW_TILE = 2048


def _out_proj(out_a, out_b, w_out, x):
    m = x.shape[0]
    return pl.pallas_call(
        _out_proj_kernel,
        grid=(m // OUT_ROW_TILE, D_MODEL // COL_TILE),
        in_specs=[
            pl.BlockSpec((OUT_ROW_TILE, D_A), lambda i, j: (i, 0)),
            pl.BlockSpec((OUT_ROW_TILE, D_B), lambda i, j: (i, 0)),
            pl.BlockSpec((D_A + D_B, COL_TILE), lambda i, j: (0, j)),
            pl.BlockSpec((OUT_ROW_TILE, COL_TILE), lambda i, j: (i, j)),
        ],
        out_specs=pl.BlockSpec((OUT_ROW_TILE, COL_TILE), lambda i, j: (i, j)),
        out_shape=jax.ShapeDtypeStruct((m, D_MODEL), F32),
        compiler_params=_params("parallel", "parallel"),
        name="out_proj",
    )(out_a, out_b, w_out, x)


def _mixer(x, b, s, mix_norm, w_in, layer, q_norm_a, k_norm_a, q_norm_b, k_norm_b, forget_bias,
           rel_bias_table, w_out):
    ones = jnp.ones((D_A,), F32)
    head_gain = jnp.concatenate([
        jnp.tile(q_norm_a * ATTN_SCALE, H_A), jnp.tile(k_norm_a, H_A), ones,
        jnp.tile(q_norm_b * (ATTN_SCALE * LOG2E), H_B), jnp.tile(k_norm_b, H_B), ones,
    ]).reshape(1, D_QKV)
    w_f = jnp.pad(w_in[layer, :, D_QKV:], ((0, 0), (0, LANES - H_B))).astype(BF16)
    f_bias = jnp.pad(forget_bias, (0, LANES - H_B)).reshape(1, LANES)
    *qkv_a, q_aug, k_aug, v_t = _proj(x, b, s, mix_norm, w_in, layer, w_f, f_bias, head_gain)

    out_b = _fox(q_aug, k_aug, v_t).reshape(b * s, D_B)

    bias = _band_bias(rel_bias_table)
    parts = [_dilated(qkv, bias, p) for p, qkv in enumerate(qkv_a)]
    out_a = _combine([pt[0] for pt in parts], [pt[1] for pt in parts])

    return _out_proj(out_a, out_b, w_out, x)


def kernel(x, ffn1_norm, ffn1_w_in, ffn1_w_out, mix_norm, w_in, q_norm_a, k_norm_a, q_norm_b,
           k_norm_b, forget_bias, rel_bias_table, w_out, ffn2_norm, ffn2_w_in, ffn2_w_out):
    b, s, d = x.shape
    depth = ffn1_norm.shape[0]
    x = x.reshape(b * s, d)
    for l in range(depth):
        x = _ffn(x, ffn1_norm[l], ffn1_w_in[l], ffn1_w_out[l])
        x = _mixer(x, b, s, mix_norm[l], w_in, l, q_norm_a[l], k_norm_a[l], q_norm_b[l],
                   k_norm_b[l], forget_bias[l], rel_bias_table, w_out[l])
        x = _ffn(x, ffn2_norm[l], ffn2_w_in[l], ffn2_w_out[l])
    return x.reshape(b, s, d)
```

```python
import functools
import math

import numpy as np
import jax
import jax.numpy as jnp
from jax import lax
from jax.experimental import pallas as pl
from jax.experimental.pallas import tpu as pltpu

D_MODEL = 2048
HEAD_DIM = 128
N_HEADS = D_MODEL // HEAD_DIM
H_A = N_HEADS // 2
H_B = N_HEADS - H_A
D_A = H_A * HEAD_DIM
D_B = H_B * HEAD_DIM
D_QKV = 3 * D_A + 3 * D_B
DILATED_PATTERNS = ((128, 1), (512, 4), (2048, 16))
BAND = 128
NUM_BUCKETS = 32
MAX_DISTANCE = 2048
D_FF = ((8 * D_MODEL // 3 + 127) // 128) * 128
RMS_EPS = 1e-6
NEG_INF = -1e30
ATTN_SCALE = HEAD_DIM ** -0.5

LANES = 128
FF_TILE = 512
FFN_ROW_TILE = 2048
FFN_CHUNK = 512
FFN_VMEM_LIMIT = 60 * 1024 * 1024
ROW_TILE = 1024
COL_TILE = 512
FOX_TQ = 512
FOX_TK = 512
FOX_HEADS = 4
VMEM_LIMIT = 56 * 1024 * 1024

F32 = jnp.float32
BF16 = jnp.bfloat16


def _params(*sem):
    return pltpu.CompilerParams(dimension_semantics=sem, vmem_limit_bytes=VMEM_LIMIT)


def _rms_rows(x, gain):
    ms = jnp.mean(x * x, axis=-1, keepdims=True)
    return x * lax.rsqrt(ms + RMS_EPS) * gain


def _ffn_kernel(x_hbm, g_ref, wg_ref, wu_ref, wo_ref, o_ref, h_ref, sem):
    i = pl.program_id(0)
    j = pl.program_id(1)

    @pl.when(j == 0)
    def _():
        def x_copy(c):
            src = x_hbm.at[pl.ds(i * FFN_ROW_TILE + c * FFN_CHUNK, FFN_CHUNK), :]
            return pltpu.make_async_copy(src, o_ref.at[pl.ds(c * FFN_CHUNK, FFN_CHUNK), :],
                                         sem.at[c])

        n_chunks = FFN_ROW_TILE // FFN_CHUNK
        for c in range(n_chunks):
            x_copy(c).start()
        for c in range(n_chunks):
            x_copy(c).wait()
            rows = slice(c * FFN_CHUNK, (c + 1) * FFN_CHUNK)
            h_ref[rows, :] = _rms_rows(o_ref[rows, :], g_ref[...]).astype(BF16)

    col = lax.broadcasted_iota(jnp.int32, (1, FF_TILE), 1)
    repeated = j * FF_TILE - jnp.minimum(j * FF_TILE, D_FF - FF_TILE)
    for r in range(FFN_ROW_TILE // ROW_TILE):
        rows = slice(r * ROW_TILE, (r + 1) * ROW_TILE)
        h = h_ref[rows, :]
        gate = jnp.dot(h, wg_ref[...].astype(BF16), preferred_element_type=F32)
        up = jnp.dot(h, wu_ref[...].astype(BF16), preferred_element_type=F32)
        act = gate * (1.0 / (1.0 + jnp.exp(-gate))) * (0.5 * up)
        act = jnp.where(col >= repeated, act, 0.0).astype(BF16)
        for c in range(D_MODEL // COL_TILE):
            sl = slice(c * COL_TILE, (c + 1) * COL_TILE)
            o_ref[rows, sl] += jnp.dot(act, wo_ref[:, sl].astype(BF16),
                                       preferred_element_type=F32)


def _ffn(x, gain, w_in, w_out):
    m = x.shape[0]
    n_ff = pl.cdiv(D_FF, FF_TILE)

    def ff_start(j):
        return pl.multiple_of(jnp.minimum(j * FF_TILE, D_FF - FF_TILE), LANES)

    return pl.pallas_call(
        _ffn_kernel,
        grid=(m // FFN_ROW_TILE, n_ff),
        in_specs=[
            pl.BlockSpec(memory_space=pl.ANY),
            pl.BlockSpec((1, D_MODEL), lambda i, j: (0, 0)),
            pl.BlockSpec((pl.Element(D_MODEL), pl.Element(FF_TILE)),
                         lambda i, j: (0, ff_start(j))),
            pl.BlockSpec((pl.Element(D_MODEL), pl.Element(FF_TILE)),
                         lambda i, j: (0, pl.multiple_of(D_FF + ff_start(j), LANES))),
            pl.BlockSpec((pl.Element(FF_TILE), pl.Element(D_MODEL)),
                         lambda i, j: (ff_start(j), 0)),
        ],
        out_specs=pl.BlockSpec((FFN_ROW_TILE, D_MODEL), lambda i, j: (i, 0),
                               pipeline_mode=pl.Buffered(1)),
        out_shape=jax.ShapeDtypeStruct((m, D_MODEL), F32),
        scratch_shapes=[
            pltpu.VMEM((FFN_ROW_TILE, D_MODEL), BF16),
            pltpu.SemaphoreType.DMA((FFN_ROW_TILE // FFN_CHUNK,)),
        ],
        compiler_params=pltpu.CompilerParams(
            dimension_semantics=("arbitrary", "arbitrary"), vmem_limit_bytes=FFN_VMEM_LIMIT),
        name="ffn",
    )(x, gain.reshape(1, D_MODEL), w_in, w_in, w_out)


HEADS_PER_TILE = COL_TILE // HEAD_DIM
N_PROJ_TILES = D_QKV // COL_TILE
TILES_PER_GROUP = D_A // COL_TILE
A_TILES = 3 * TILES_PER_GROUP
AUG = 2 * HEAD_DIM
LOG2E = math.log2(math.e)


def _log_gate_scan(z, carry):
    c = (jnp.minimum(z, 0.0) - jnp.log1p(jnp.exp(-jnp.abs(z)))) * LOG2E
    row = lax.broadcasted_iota(jnp.int32, c.shape, 0)
    shift = 1
    while shift < c.shape[0]:
        c = c + jnp.where(row >= shift, pltpu.roll(c, shift, axis=0), 0.0)
        shift *= 2
    return c + carry


def _split3(c):
    hi = c.astype(BF16)
    rest = c - hi.astype(F32)
    mid = rest.astype(BF16)
    lo = (rest - mid.astype(F32)).astype(BF16)
    return hi, mid, lo


def _head_mean_matrix():
    head = np.arange(COL_TILE) // HEAD_DIM
    return jnp.asarray((head[:, None] == head[None, :]) / HEAD_DIM, BF16)


def _gate_placement_matrices():
    mats = np.zeros((2, TILES_PER_GROUP, COL_TILE, COL_TILE), np.float32)
    ones_row = 3 * LANES
    for part in range(TILES_PER_GROUP):
        for hh in range(HEADS_PER_TILE):
            head = part * HEADS_PER_TILE + hh
            col = hh * HEAD_DIM
            for term in range(3):
                mats[0, part, term * LANES + head, col + term] = 1.0
                mats[0, part, ones_row, col + 3 + term] = 1.0
                mats[1, part, term * LANES + head, col + 3 + term] = -1.0
                mats[1, part, ones_row, col + term] = 1.0
    return jnp.asarray(mats.reshape(2 * TILES_PER_GROUP, COL_TILE, COL_TILE), BF16)


def _proj_kernel(x_ref, g_ref, w_ref, wf_ref, fb_ref, hg_ref, mean_ref, place_ref,
                 a1_ref, a4_ref, a16_ref, qb_ref, kb_ref, vt_ref,
                 h_ref, cs_ref, carry_ref, y_ref, *, tiles_per_seq):
    i = pl.program_id(0)
    j = pl.program_id(1)

    @pl.when(j == 0)
    def _():
        h_ref[...] = _rms_rows(x_ref[...], g_ref[...]).astype(BF16)

        @pl.when(i % tiles_per_seq == 0)
        def _():
            carry_ref[...] = jnp.zeros(carry_ref.shape, F32)

        z = jnp.dot(h_ref[...], wf_ref[...], preferred_element_type=F32) + fb_ref[...]
        c = _log_gate_scan(z, carry_ref[...])
        carry_ref[...] = c[ROW_TILE - 1:ROW_TILE, :]
        for term, part in enumerate(_split3(c)):
            cs_ref[:, term * LANES:(term + 1) * LANES] = part
        cs_ref[:, 3 * LANES:] = jnp.ones((ROW_TILE, LANES), BF16)

    r = lax.dot_general(h_ref[...], w_ref[...].astype(BF16), (((1,), (1,)), ((), ())),
                        preferred_element_type=F32)

    def normed():
        ms = jnp.dot((r * r).astype(BF16), mean_ref[...], preferred_element_type=F32)
        return r * lax.rsqrt(ms + RMS_EPS) * hg_ref[...]

    def store_dilated(y):
        a1_ref[0, 0] = y.astype(BF16)
        for c in range(HEADS_PER_TILE):
            sl = slice(c * LANES, (c + 1) * LANES)
            y_ref[c] = y[:, sl]
            for ref in (a4_ref, a16_ref):
                dilation = ref.shape[1]
                for rc in range(dilation):
                    rows = y_ref[c, pl.ds(rc, ROW_TILE // dilation, stride=dilation), :]
                    ref[0, rc, :, sl] = rows.astype(BF16)

    for t in range(N_PROJ_TILES):
        group, part = divmod(t, TILES_PER_GROUP)

        @pl.when(j == t)
        def _(group=group):
            if group in (0, 1):
                store_dilated(normed())
            elif group == 2:
                store_dilated(r)
            elif group in (3, 4):
                out = qb_ref if group == 3 else kb_ref
                y = normed().astype(BF16)
                aug = jnp.dot(cs_ref[...], place_ref[0],
                              preferred_element_type=F32).astype(BF16)
                for hh in range(HEADS_PER_TILE):
                    sl = slice(hh * HEAD_DIM, (hh + 1) * HEAD_DIM)
                    out[0, hh, :, :HEAD_DIM] = y[:, sl]
                    out[0, hh, :, HEAD_DIM:] = aug[:, sl]
            else:
                for hh in range(HEADS_PER_TILE):
                    vt_ref[0, hh] = r[:, hh * HEAD_DIM:(hh + 1) * HEAD_DIM].T.astype(BF16)


def _proj(x, b, s, gain, w_in, layer, w_f, f_bias, head_gain):
    m = x.shape[0]
    tps = s // ROW_TILE

    def a_spec(dilation):
        return pl.BlockSpec((1, dilation, ROW_TILE // dilation, COL_TILE),
                            lambda i, j: (i // tps, 0, i % tps, jnp.minimum(j, A_TILES - 1)))

    def head_block(first):
        return lambda i, j: (i // tps, jnp.clip(j - first, 0, TILES_PER_GROUP - 1), i % tps, 0)

    qb_first = A_TILES
    vt_first = A_TILES + 2 * TILES_PER_GROUP
    dilations = [d for _, d in DILATED_PATTERNS]
    return pl.pallas_call(
        functools.partial(_proj_kernel, tiles_per_seq=tps),
        grid=(m // ROW_TILE, N_PROJ_TILES),
        in_specs=[
            pl.BlockSpec((ROW_TILE, D_MODEL), lambda i, j: (i, 0)),
            pl.BlockSpec((1, D_MODEL), lambda i, j: (0, 0)),
            pl.BlockSpec((None, COL_TILE, D_MODEL), lambda i, j: (layer, j, 0)),
            pl.BlockSpec((D_MODEL, LANES), lambda i, j: (0, 0)),
            pl.BlockSpec((1, LANES), lambda i, j: (0, 0)),
            pl.BlockSpec((1, COL_TILE), lambda i, j: (0, j)),
            pl.BlockSpec((COL_TILE, COL_TILE), lambda i, j: (0, 0)),
            pl.BlockSpec((1, COL_TILE, COL_TILE),
                         lambda i, j: (jnp.clip(j - qb_first, 0, 2 * TILES_PER_GROUP - 1), 0, 0)),
        ],
        out_specs=[a_spec(d) for d in dilations] + [
            pl.BlockSpec((1, HEADS_PER_TILE, ROW_TILE, AUG), head_block(qb_first)),
            pl.BlockSpec((1, HEADS_PER_TILE, ROW_TILE, AUG),
                         head_block(qb_first + TILES_PER_GROUP)),
            pl.BlockSpec((1, HEADS_PER_TILE, HEAD_DIM, ROW_TILE),
                         lambda i, j: (i // tps, jnp.clip(j - vt_first, 0, TILES_PER_GROUP - 1),
                                       0, i % tps)),
        ],
        out_shape=[jax.ShapeDtypeStruct((b, d, s // d, 3 * D_A), BF16) for d in dilations] + [
            jax.ShapeDtypeStruct((b, H_B, s, AUG), BF16),
            jax.ShapeDtypeStruct((b, H_B, s, AUG), BF16),
            jax.ShapeDtypeStruct((b, H_B, HEAD_DIM, s), BF16),
        ],
        scratch_shapes=[
            pltpu.VMEM((ROW_TILE, D_MODEL), BF16),
            pltpu.VMEM((ROW_TILE, 4 * LANES), BF16),
            pltpu.VMEM((1, LANES), F32),
            pltpu.VMEM((HEADS_PER_TILE, ROW_TILE, LANES), F32),
        ],
        compiler_params=_params("arbitrary", "arbitrary"),
        name="proj",
    )(x, gain.reshape(1, D_MODEL), jnp.swapaxes(w_in, 1, 2), w_f, f_bias, head_gain,
      _head_mean_matrix(), _gate_placement_matrices())


def _bucket_steps(dilation):
    dist = np.arange(BAND + 1) * dilation
    max_exact = NUM_BUCKETS // 2
    large = max_exact + np.floor(
        np.log(np.maximum(dist, 1) / max_exact) / math.log(MAX_DISTANCE / max_exact)
        * (NUM_BUCKETS - max_exact)).astype(np.int64)
    bucket = np.where(dist < max_exact, dist, np.minimum(large, NUM_BUCKETS - 1))
    steps = [(0, int(bucket[0]))]
    for delta in range(1, BAND + 1):
        if bucket[delta] != bucket[delta - 1]:
            steps.append((delta, int(bucket[delta])))
    return steps


def _bias_kernel(table_ref, o_ref):
    iq = lax.broadcasted_iota(jnp.int32, (BAND, 2 * BAND), 0)
    ik = lax.broadcasted_iota(jnp.int32, (BAND, 2 * BAND), 1)
    delta = iq + BAND - ik
    in_band = jnp.logical_and(delta >= 0, delta <= BAND)
    for p, (_, dilation) in enumerate(DILATED_PATTERNS):
        steps = _bucket_steps(dilation)
        for h in range(H_A):
            val = jnp.full((BAND, 2 * BAND), table_ref[steps[0][1], h], F32)
            for start, bucket in steps[1:]:
                val = jnp.where(delta >= start, table_ref[bucket, h], val)
            o_ref[p, h] = jnp.where(in_band, val, NEG_INF)


def _band_bias(rel_table):
    n_pat = len(DILATED_PATTERNS)
    return pl.pallas_call(
        _bias_kernel,
        in_specs=[pl.BlockSpec(memory_space=pltpu.SMEM)],
        out_specs=pl.BlockSpec(memory_space=pltpu.VMEM),
        out_shape=jax.ShapeDtypeStruct((n_pat, H_A, BAND, 2 * BAND), F32),
        name="band_bias",
    )(rel_table)


DIL_UNITS = 8


def _dilated_kernel(q_ref, kp_ref, kc_ref, vp_ref, vc_ref, bias_ref, o_ref, st_ref):
    n_cls = q_ref.shape[1]
    n_blk = q_ref.shape[2] // BAND
    first = pl.program_id(2) == 0
    key_lane = lax.broadcasted_iota(jnp.int32, (1, 2 * BAND), 1)
    no_prev = jnp.where(jnp.logical_and(first, key_lane < BAND), NEG_INF, 0.0)
    stat_lane = lax.broadcasted_iota(jnp.int32, (BAND, LANES), 1)
    units = [(c, blk) for c in range(n_cls) for blk in range(n_blk)]

    def keys(prev_ref, cur_ref, c, blk, sl):
        if blk == 0:
            return jnp.concatenate([prev_ref[0, c, :, sl], cur_ref[0, c, :BAND, sl]], axis=0)
        return cur_ref[0, c, (blk - 1) * BAND:(blk + 1) * BAND, sl]

    scores = {}
    for c, blk in units:
        for h in range(H_A):
            sl = slice(h * HEAD_DIM, (h + 1) * HEAD_DIM)
            q = q_ref[0, c, blk * BAND:(blk + 1) * BAND, sl]
            scores[c, blk, h] = lax.dot_general(q, keys(kp_ref, kc_ref, c, blk, sl),
                                                (((1,), (1,)), ((), ())),
                                                preferred_element_type=F32)
    for c, blk in units:
        rows = slice(blk * BAND, (blk + 1) * BAND)
        stats = jnp.zeros((BAND, LANES), F32)
        for h in range(H_A):
            sl = slice(h * HEAD_DIM, (h + 1) * HEAD_DIM)
            s = scores[c, blk, h] + bias_ref[0, h]
            if blk == 0:
                s = s + no_prev
            mx = jnp.max(s, axis=-1, keepdims=True)
            p = jnp.exp(s - mx)
            den = jnp.sum(p, axis=-1, keepdims=True)
            o_ref[0, c, rows, sl] = jnp.dot(p.astype(BF16), keys(vp_ref, vc_ref, c, blk, sl),
                                            preferred_element_type=F32).astype(o_ref.dtype)
            stats = jnp.where(stat_lane == h, mx, stats)
            stats = jnp.where(stat_lane == H_A + h, den, stats)
        st_ref[0, c, rows, :] = stats


def _dilated(qkv, bias, pattern):
    b, dilation, n_sub, _ = qkv.shape
    n_blk = min(DIL_UNITS, n_sub // BAND)
    n_cls = DIL_UNITS // n_blk
    rows = n_blk * BAND
    nb = n_sub // rows

    def cur(which):
        return pl.BlockSpec((1, n_cls, rows, D_A), lambda bi, r, i: (bi, r, i, which))

    def prev(which):
        return pl.BlockSpec((1, n_cls, BAND, D_A),
                            lambda bi, r, i: (bi, r, jnp.maximum(i * n_blk - 1, 0), which))

    return pl.pallas_call(
        _dilated_kernel,
        grid=(b, dilation // n_cls, nb),
        in_specs=[
            cur(0), prev(1), cur(1), prev(2), cur(2),
            pl.BlockSpec((1, H_A, BAND, 2 * BAND), lambda bi, r, i: (pattern, 0, 0, 0)),
        ],
        out_specs=[
            pl.BlockSpec((1, n_cls, rows, D_A), lambda bi, r, i: (bi, r, i, 0)),
            pl.BlockSpec((1, n_cls, rows, LANES), lambda bi, r, i: (bi, r, i, 0)),
        ],
        out_shape=[
            jax.ShapeDtypeStruct((b, dilation, n_sub, D_A), BF16),
            jax.ShapeDtypeStruct((b, dilation, n_sub, LANES), F32),
        ],
        compiler_params=_params("parallel", "parallel", "arbitrary"),
        name=f"dilated_{dilation}",
    )(qkv, qkv, qkv, qkv, qkv, bias)


COMBINE_ROWS = 512


def _combine_kernel(*refs):
    n_pat = len(DILATED_PATTERNS)
    num_refs, st_refs = refs[:n_pat], refs[n_pat:2 * n_pat]
    o_ref, num_buf, st_buf = refs[2 * n_pat:]
    dilations = [ref.shape[1] for ref in num_refs]
    for p, dilation in enumerate(dilations):
        if dilation == 1:
            continue
        for rc in range(dilation):
            rows = pl.ds(rc, COMBINE_ROWS // dilation, stride=dilation)
            st_buf[p, rows, :] = st_refs[p][0, rc]
            for h in range(H_A):
                sl = slice(h * HEAD_DIM, (h + 1) * HEAD_DIM)
                num_buf[p, h, rows, :] = num_refs[p][0, rc, :, sl].astype(F32)
    stats = [st_refs[p][0, 0] if d == 1 else st_buf[p] for p, d in enumerate(dilations)]
    for h in range(H_A):
        sl = slice(h * HEAD_DIM, (h + 1) * HEAD_DIM)
        mxs = [st[:, h:h + 1] for st in stats]
        dens = [st[:, H_A + h:H_A + h + 1] for st in stats]
        mx = functools.reduce(jnp.maximum, mxs)
        num = None
        den = None
        for p, (pden, pmx) in enumerate(zip(dens, mxs)):
            scale = jnp.exp(pmx - mx)
            if dilations[p] == 1:
                pnum = num_refs[p][0, 0, :, sl].astype(F32)
            else:
                pnum = num_buf[p, h]
            num = pnum * scale if num is None else num + pnum * scale
            den = pden * scale if den is None else den + pden * scale
        o_ref[:, sl] = (num / den).astype(BF16)


def _combine(nums, stats):
    b, _, s, _ = nums[0].shape
    tps = s // COMBINE_ROWS

    def spec(arr):
        dilation, width = arr.shape[1], arr.shape[3]
        return pl.BlockSpec((1, dilation, COMBINE_ROWS // dilation, width),
                            lambda i: (i // tps, 0, i % tps, 0))

    n_pat = len(nums)
    return pl.pallas_call(
        _combine_kernel,
        grid=(b * tps,),
        in_specs=[spec(a) for a in nums] + [spec(a) for a in stats],
        out_specs=pl.BlockSpec((COMBINE_ROWS, D_A), lambda i: (i, 0)),
        out_shape=jax.ShapeDtypeStruct((b * s, D_A), BF16),
        scratch_shapes=[
            pltpu.VMEM((n_pat, H_A, COMBINE_ROWS, HEAD_DIM), F32),
            pltpu.VMEM((n_pat, COMBINE_ROWS, LANES), F32),
        ],
        compiler_params=_params("parallel"),
        name="combine_a",
    )(*nums, *stats)


def _fox_kernel(q_ref, k_ref, vt_ref, o_ref, m_ref, l_ref, acc_ref, sa_ref, sb_ref):
    iq = pl.program_id(2)
    m_ref[...] = jnp.full(m_ref.shape, NEG_INF, F32)
    l_ref[...] = jnp.zeros(l_ref.shape, F32)
    acc_ref[...] = jnp.zeros(acc_ref.shape, F32)

    def scores_into(s_ref, kb):
        start = pl.multiple_of(kb * FOX_TK, FOX_TK)
        for hh in range(FOX_HEADS):
            k = k_ref[0, hh, pl.ds(start, FOX_TK), :]
            s_ref[hh] = lax.dot_general(k, q_ref[0, hh], (((1,), (1,)), ((), ())),
                                        preferred_element_type=F32)

    def accumulate(s_ref, kb, diagonal):
        start = pl.multiple_of(kb * FOX_TK, FOX_TK)
        for hh in range(FOX_HEADS):
            st = s_ref[hh]
            if diagonal:
                key = lax.broadcasted_iota(jnp.int32, st.shape, 0)
                qry = lax.broadcasted_iota(jnp.int32, st.shape, 1)
                st = jnp.where(key <= qry, st, NEG_INF)
            m_prev = m_ref[hh]
            m_new = jnp.maximum(m_prev, jnp.max(st, axis=0, keepdims=True))
            alpha = jnp.exp2(m_prev - m_new)
            p = jnp.exp2(st - m_new)
            l_ref[hh] = alpha * l_ref[hh] + jnp.sum(p, axis=0, keepdims=True)
            vt = vt_ref[0, hh, :, pl.ds(start, FOX_TK)]
            acc_ref[hh] = alpha * acc_ref[hh] + jnp.dot(vt, p.astype(BF16),
                                                        preferred_element_type=F32)
            m_ref[hh] = m_new

    scores_into(sa_ref, 0)

    def pair(p, carry):
        scores_into(sb_ref, 2 * p + 1)
        accumulate(sa_ref, 2 * p, False)
        scores_into(sa_ref, 2 * p + 2)
        accumulate(sb_ref, 2 * p + 1, False)
        return carry

    lax.fori_loop(0, iq // 2, pair, 0)

    @pl.when(iq % 2 == 0)
    def _():
        accumulate(sa_ref, iq, True)

    @pl.when(iq % 2 == 1)
    def _():
        scores_into(sb_ref, iq)
        accumulate(sa_ref, iq - 1, False)
        accumulate(sb_ref, iq, True)
    for hh in range(FOX_HEADS):
        o_ref[0, :, hh * HEAD_DIM:(hh + 1) * HEAD_DIM] = (
            acc_ref[hh] / l_ref[hh]).T.astype(BF16)


def _fox(q_aug, k_aug, v_t):
    b, h_b, s, _ = q_aug.shape
    assert FOX_TQ == FOX_TK
    return pl.pallas_call(
        _fox_kernel,
        grid=(b, h_b // FOX_HEADS, s // FOX_TQ),
        in_specs=[
            pl.BlockSpec((1, FOX_HEADS, FOX_TQ, AUG), lambda bi, h, i: (bi, h, i, 0)),
            pl.BlockSpec((1, FOX_HEADS, s, AUG), lambda bi, h, i: (bi, h, 0, 0)),
            pl.BlockSpec((1, FOX_HEADS, HEAD_DIM, s), lambda bi, h, i: (bi, h, 0, 0)),
        ],
        out_specs=pl.BlockSpec((1, FOX_TQ, FOX_HEADS * HEAD_DIM), lambda bi, h, i: (bi, i, h)),
        out_shape=jax.ShapeDtypeStruct((b, s, h_b * HEAD_DIM), BF16),
        scratch_shapes=[
            pltpu.VMEM((FOX_HEADS, 1, FOX_TQ), F32),
            pltpu.VMEM((FOX_HEADS, 1, FOX_TQ), F32),
            pltpu.VMEM((FOX_HEADS, HEAD_DIM, FOX_TQ), F32),
            pltpu.VMEM((FOX_HEADS, FOX_TK, FOX_TQ), F32),
            pltpu.VMEM((FOX_HEADS, FOX_TK, FOX_TQ), F32),
        ],
        compiler_params=_params("parallel", "parallel", "arbitrary"),
        name="fox",
    )(q_aug, k_aug, v_t)


def _out_proj_kernel(a_ref, b_ref, w_ref, x_ref, o_ref):
    mixed = jnp.concatenate([a_ref[...], b_ref[...]], axis=-1)
    o_ref[...] = x_ref[...] + jnp.dot(mixed, w_ref[...].astype(BF16),
                                      preferred_element_type=F32)


OUT_ROW_TILE = 2048


def _out_proj(out_a, out_b, w_out, x):
    m = x.shape[0]
    return pl.pallas_call(
        _out_proj_kernel,
        grid=(m // OUT_ROW_TILE, D_MODEL // COL_TILE),
        in_specs=[
            pl.BlockSpec((OUT_ROW_TILE, D_A), lambda i, j: (i, 0)),
            pl.BlockSpec((OUT_ROW_TILE, D_B), lambda i, j: (i, 0)),
            pl.BlockSpec((D_A + D_B, COL_TILE), lambda i, j: (0, j)),
            pl.BlockSpec((OUT_ROW_TILE, COL_TILE), lambda i, j: (i, j)),
        ],
        out_specs=pl.BlockSpec((OUT_ROW_TILE, COL_TILE), lambda i, j: (i, j)),
        out_shape=jax.ShapeDtypeStruct((m, D_MODEL), F32),
        compiler_params=_params("parallel", "parallel"),
        name="out_proj",
    )(out_a, out_b, w_out, x)


def _mixer(x, b, s, mix_norm, w_in, layer, q_norm_a, k_norm_a, q_norm_b, k_norm_b, forget_bias,
           rel_bias_table, w_out):
    ones = jnp.ones((D_A,), F32)
    head_gain = jnp.concatenate([
        jnp.tile(q_norm_a * ATTN_SCALE, H_A), jnp.tile(k_norm_a, H_A), ones,
        jnp.tile(q_norm_b * (ATTN_SCALE * LOG2E), H_B), jnp.tile(k_norm_b, H_B), ones,
    ]).reshape(1, D_QKV)
    w_f = jnp.pad(w_in[layer, :, D_QKV:], ((0, 0), (0, LANES - H_B))).astype(BF16)
    f_bias = jnp.pad(forget_bias, (0, LANES - H_B)).reshape(1, LANES)
    *qkv_a, q_aug, k_aug, v_t = _proj(x, b, s, mix_norm, w_in, layer, w_f, f_bias, head_gain)

    out_b = _fox(q_aug, k_aug, v_t).reshape(b * s, D_B)

    bias = _band_bias(rel_bias_table)
    parts = [_dilated(qkv, bias, p) for p, qkv in enumerate(qkv_a)]
    out_a = _combine([pt[0] for pt in parts], [pt[1] for pt in parts])

    return _out_proj(out_a, out_b, w_out, x)


def kernel(x, ffn1_norm, ffn1_w_in, ffn1_w_out, mix_norm, w_in, q_norm_a, k_norm_a, q_norm_b,
           k_norm_b, forget_bias, rel_bias_table, w_out, ffn2_norm, ffn2_w_in, ffn2_w_out):
    b, s, d = x.shape
    depth = ffn1_norm.shape[0]
    x = x.reshape(b * s, d)
    for l in range(depth):
        x = _ffn(x, ffn1_norm[l], ffn1_w_in[l], ffn1_w_out[l])
        x = _mixer(x, b, s, mix_norm[l], w_in, l, q_norm_a[l], k_norm_a[l], q_norm_b[l],
                   k_norm_b[l], forget_bias[l], rel_bias_table, w_out[l])
        x = _ffn(x, ffn2_norm[l], ffn2_w_in[l], ffn2_w_out[l])
    return x.reshape(b, s, d)
```

```python
import functools
import math

import numpy as np
import jax
import jax.numpy as jnp
from jax import lax
from jax.experimental import pallas as pl
from jax.experimental.pallas import tpu as pltpu

D_MODEL = 2048
HEAD_DIM = 128
N_HEADS = D_MODEL // HEAD_DIM
H_A = N_HEADS // 2
H_B = N_HEADS - H_A
D_A = H_A * HEAD_DIM
D_B = H_B * HEAD_DIM
D_QKV = 3 * D_A + 3 * D_B
DILATED_PATTERNS = ((128, 1), (512, 4), (2048, 16))
BAND = 128
NUM_BUCKETS = 32
MAX_DISTANCE = 2048
D_FF = ((8 * D_MODEL // 3 + 127) // 128) * 128
RMS_EPS = 1e-6
NEG_INF = -1e30
ATTN_SCALE = HEAD_DIM ** -0.5

LANES = 128
FF_TILE = 512
FFN_ROW_TILE = 2048
FFN_CHUNK = 512
FFN_VMEM_LIMIT = 60 * 1024 * 1024
ROW_TILE = 1024
COL_TILE = 512
FOX_TQ = 512
FOX_TK = 512
FOX_HEADS = 4
VMEM_LIMIT = 56 * 1024 * 1024

F32 = jnp.float32
BF16 = jnp.bfloat16


def _params(*sem):
    return pltpu.CompilerParams(dimension_semantics=sem, vmem_limit_bytes=VMEM_LIMIT)


def _rms_rows(x, gain):
    ms = jnp.mean(x * x, axis=-1, keepdims=True)
    return x * lax.rsqrt(ms + RMS_EPS) * gain


def _ffn_kernel(x_hbm, g_ref, wg_ref, wu_ref, wo_ref, o_hbm, acc_ref, h_ref, in_sem, out_sem):
    i = pl.program_id(0)
    j = pl.program_id(1)
    last_i = pl.num_programs(0) - 1
    last_j = pl.num_programs(1) - 1
    pieces = [(r, c) for r in range(FFN_ROW_TILE // ROW_TILE) for c in range(D_MODEL // COL_TILE)]

    def x_copy(c):
        src = x_hbm.at[pl.ds(i * FFN_ROW_TILE + c * FFN_CHUNK, FFN_CHUNK), :]
        return pltpu.make_async_copy(src, acc_ref.at[pl.ds(c * FFN_CHUNK, FFN_CHUNK), :],
                                     in_sem.at[c])

    def out_copy(row_tile, r, c):
        rows, cols = pl.ds(r * ROW_TILE, ROW_TILE), pl.ds(c * COL_TILE, COL_TILE)
        dst = o_hbm.at[pl.ds(row_tile * FFN_ROW_TILE + r * ROW_TILE, ROW_TILE), cols]
        return pltpu.make_async_copy(acc_ref.at[rows, cols], dst,
                                     out_sem.at[r * (D_MODEL // COL_TILE) + c])

    @pl.when(j == 0)
    def _():
        @pl.when(i > 0)
        def _():
            for r, c in pieces:
                out_copy(i - 1, r, c).wait()

        n_chunks = FFN_ROW_TILE // FFN_CHUNK
        for c in range(n_chunks):
            x_copy(c).start()
        for c in range(n_chunks):
            x_copy(c).wait()
            rows = slice(c * FFN_CHUNK, (c + 1) * FFN_CHUNK)
            h_ref[rows, :] = _rms_rows(acc_ref[rows, :], g_ref[...]).astype(BF16)

    col = lax.broadcasted_iota(jnp.int32, (1, FF_TILE), 1)
    repeated = j * FF_TILE - jnp.minimum(j * FF_TILE, D_FF - FF_TILE)

    def tile(write_back):
        for r in range(FFN_ROW_TILE // ROW_TILE):
            rows = slice(r * ROW_TILE, (r + 1) * ROW_TILE)
            h = h_ref[rows, :]
            gate = jnp.dot(h, wg_ref[...].astype(BF16), preferred_element_type=F32)
            up = jnp.dot(h, wu_ref[...].astype(BF16), preferred_element_type=F32)
            act = gate * (1.0 / (1.0 + jnp.exp(-gate))) * (0.5 * up)
            act = jnp.where(col >= repeated, act, 0.0).astype(BF16)
            for c in range(D_MODEL // COL_TILE):
                sl = slice(c * COL_TILE, (c + 1) * COL_TILE)
                acc_ref[rows, sl] += jnp.dot(act, wo_ref[:, sl].astype(BF16),
                                             preferred_element_type=F32)
                if write_back:
                    out_copy(i, r, c).start()

    @pl.when(j < last_j)
    def _():
        tile(False)

    @pl.when(j == last_j)
    def _():
        tile(True)

        @pl.when(i == last_i)
        def _():
            for r, c in pieces:
                out_copy(i, r, c).wait()


def _ffn(x, gain, w_in, w_out):
    m = x.shape[0]
    n_ff = pl.cdiv(D_FF, FF_TILE)

    def ff_start(j):
        return pl.multiple_of(jnp.minimum(j * FF_TILE, D_FF - FF_TILE), LANES)

    return pl.pallas_call(
        _ffn_kernel,
        grid=(m // FFN_ROW_TILE, n_ff),
        in_specs=[
            pl.BlockSpec(memory_space=pl.ANY),
            pl.BlockSpec((1, D_MODEL), lambda i, j: (0, 0)),
            pl.BlockSpec((pl.Element(D_MODEL), pl.Element(FF_TILE)),
                         lambda i, j: (0, ff_start(j))),
            pl.BlockSpec((pl.Element(D_MODEL), pl.Element(FF_TILE)),
                         lambda i, j: (0, pl.multiple_of(D_FF + ff_start(j), LANES))),
            pl.BlockSpec((pl.Element(FF_TILE), pl.Element(D_MODEL)),
                         lambda i, j: (ff_start(j), 0)),
        ],
        out_specs=pl.BlockSpec(memory_space=pl.ANY),
        out_shape=jax.ShapeDtypeStruct((m, D_MODEL), F32),
        scratch_shapes=[
            pltpu.VMEM((FFN_ROW_TILE, D_MODEL), F32),
            pltpu.VMEM((FFN_ROW_TILE, D_MODEL), BF16),
            pltpu.SemaphoreType.DMA((FFN_ROW_TILE // FFN_CHUNK,)),
            pltpu.SemaphoreType.DMA(((FFN_ROW_TILE // ROW_TILE) * (D_MODEL // COL_TILE),)),
        ],
        compiler_params=pltpu.CompilerParams(
            dimension_semantics=("arbitrary", "arbitrary"), vmem_limit_bytes=FFN_VMEM_LIMIT),
        name="ffn",
    )(x, gain.reshape(1, D_MODEL), w_in, w_in, w_out)


HEADS_PER_TILE = COL_TILE // HEAD_DIM
N_PROJ_TILES = D_QKV // COL_TILE
TILES_PER_GROUP = D_A // COL_TILE
A_TILES = 3 * TILES_PER_GROUP
AUG = 2 * HEAD_DIM
LOG2E = math.log2(math.e)


def _log_gate_scan(z, carry):
    c = (jnp.minimum(z, 0.0) - jnp.log1p(jnp.exp(-jnp.abs(z)))) * LOG2E
    row = lax.broadcasted_iota(jnp.int32, c.shape, 0)
    shift = 1
    while shift < c.shape[0]:
        c = c + jnp.where(row >= shift, pltpu.roll(c, shift, axis=0), 0.0)
        shift *= 2
    return c + carry


def _split3(c):
    hi = c.astype(BF16)
    rest = c - hi.astype(F32)
    mid = rest.astype(BF16)
    lo = (rest - mid.astype(F32)).astype(BF16)
    return hi, mid, lo


def _head_mean_matrix():
    head = np.arange(COL_TILE) // HEAD_DIM
    return jnp.asarray((head[:, None] == head[None, :]) / HEAD_DIM, BF16)


def _gate_placement_matrices():
    mats = np.zeros((2, TILES_PER_GROUP, COL_TILE, COL_TILE), np.float32)
    ones_row = 3 * LANES
    for part in range(TILES_PER_GROUP):
        for hh in range(HEADS_PER_TILE):
            head = part * HEADS_PER_TILE + hh
            col = hh * HEAD_DIM
            for term in range(3):
                mats[0, part, term * LANES + head, col + term] = 1.0
                mats[0, part, ones_row, col + 3 + term] = 1.0
                mats[1, part, term * LANES + head, col + 3 + term] = -1.0
                mats[1, part, ones_row, col + term] = 1.0
    return jnp.asarray(mats.reshape(2 * TILES_PER_GROUP, COL_TILE, COL_TILE), BF16)


def _proj_kernel(x_ref, g_ref, w_ref, wf_ref, fb_ref, hg_ref, mean_ref, place_ref,
                 a1_ref, a4_ref, a16_ref, qb_ref, kb_ref, vt_ref,
                 h_ref, cs_ref, carry_ref, y_ref, *, tiles_per_seq):
    i = pl.program_id(0)
    j = pl.program_id(1)

    @pl.when(j == 0)
    def _():
        h_ref[...] = _rms_rows(x_ref[...], g_ref[...]).astype(BF16)

        @pl.when(i % tiles_per_seq == 0)
        def _():
            carry_ref[...] = jnp.zeros(carry_ref.shape, F32)

        z = jnp.dot(h_ref[...], wf_ref[...], preferred_element_type=F32) + fb_ref[...]
        c = _log_gate_scan(z, carry_ref[...])
        carry_ref[...] = c[ROW_TILE - 1:ROW_TILE, :]
        for term, part in enumerate(_split3(c)):
            cs_ref[:, term * LANES:(term + 1) * LANES] = part
        cs_ref[:, 3 * LANES:] = jnp.ones((ROW_TILE, LANES), BF16)

    r = lax.dot_general(h_ref[...], w_ref[...].astype(BF16), (((1,), (1,)), ((), ())),
                        preferred_element_type=F32)

    def normed():
        ms = jnp.dot((r * r).astype(BF16), mean_ref[...], preferred_element_type=F32)
        return r * lax.rsqrt(ms + RMS_EPS) * hg_ref[...]

    def store_dilated(y):
        a1_ref[0, 0] = y.astype(BF16)
        for c in range(HEADS_PER_TILE):
            sl = slice(c * LANES, (c + 1) * LANES)
            y_ref[c] = y[:, sl]
            for ref in (a4_ref, a16_ref):
                dilation = ref.shape[1]
                for rc in range(dilation):
                    rows = y_ref[c, pl.ds(rc, ROW_TILE // dilation, stride=dilation), :]
                    ref[0, rc, :, sl] = rows.astype(BF16)

    for t in range(N_PROJ_TILES):
        group, part = divmod(t, TILES_PER_GROUP)

        @pl.when(j == t)
        def _(group=group):
            if group in (0, 1):
                store_dilated(normed())
            elif group == 2:
                store_dilated(r)
            elif group in (3, 4):
                out = qb_ref if group == 3 else kb_ref
                y = normed().astype(BF16)
                aug = jnp.dot(cs_ref[...], place_ref[0],
                              preferred_element_type=F32).astype(BF16)
                for hh in range(HEADS_PER_TILE):
                    sl = slice(hh * HEAD_DIM, (hh + 1) * HEAD_DIM)
                    out[0, hh, :, :HEAD_DIM] = y[:, sl]
                    out[0, hh, :, HEAD_DIM:] = aug[:, sl]
            else:
                for hh in range(HEADS_PER_TILE):
                    vt_ref[0, hh] = r[:, hh * HEAD_DIM:(hh + 1) * HEAD_DIM].T.astype(BF16)


def _proj(x, b, s, gain, w_in, layer, w_f, f_bias, head_gain):
    m = x.shape[0]
    tps = s // ROW_TILE

    def a_spec(dilation):
        return pl.BlockSpec((1, dilation, ROW_TILE // dilation, COL_TILE),
                            lambda i, j: (i // tps, 0, i % tps, jnp.minimum(j, A_TILES - 1)))

    def head_block(first):
        return lambda i, j: (i // tps, jnp.clip(j - first, 0, TILES_PER_GROUP - 1), i % tps, 0)

    qb_first = A_TILES
    vt_first = A_TILES + 2 * TILES_PER_GROUP
    dilations = [d for _, d in DILATED_PATTERNS]
    return pl.pallas_call(
        functools.partial(_proj_kernel, tiles_per_seq=tps),
        grid=(m // ROW_TILE, N_PROJ_TILES),
        in_specs=[
            pl.BlockSpec((ROW_TILE, D_MODEL), lambda i, j: (i, 0)),
            pl.BlockSpec((1, D_MODEL), lambda i, j: (0, 0)),
            pl.BlockSpec((None, COL_TILE, D_MODEL), lambda i, j: (layer, j, 0)),
            pl.BlockSpec((D_MODEL, LANES), lambda i, j: (0, 0)),
            pl.BlockSpec((1, LANES), lambda i, j: (0, 0)),
            pl.BlockSpec((1, COL_TILE), lambda i, j: (0, j)),
            pl.BlockSpec((COL_TILE, COL_TILE), lambda i, j: (0, 0)),
            pl.BlockSpec((1, COL_TILE, COL_TILE),
                         lambda i, j: (jnp.clip(j - qb_first, 0, 2 * TILES_PER_GROUP - 1), 0, 0)),
        ],
        out_specs=[a_spec(d) for d in dilations] + [
            pl.BlockSpec((1, HEADS_PER_TILE, ROW_TILE, AUG), head_block(qb_first)),
            pl.BlockSpec((1, HEADS_PER_TILE, ROW_TILE, AUG),
                         head_block(qb_first + TILES_PER_GROUP)),
            pl.BlockSpec((1, HEADS_PER_TILE, HEAD_DIM, ROW_TILE),
                         lambda i, j: (i // tps, jnp.clip(j - vt_first, 0, TILES_PER_GROUP - 1),
                                       0, i % tps)),
        ],
        out_shape=[jax.ShapeDtypeStruct((b, d, s // d, 3 * D_A), BF16) for d in dilations] + [
            jax.ShapeDtypeStruct((b, H_B, s, AUG), BF16),
            jax.ShapeDtypeStruct((b, H_B, s, AUG), BF16),
            jax.ShapeDtypeStruct((b, H_B, HEAD_DIM, s), BF16),
        ],
        scratch_shapes=[
            pltpu.VMEM((ROW_TILE, D_MODEL), BF16),
            pltpu.VMEM((ROW_TILE, 4 * LANES), BF16),
            pltpu.VMEM((1, LANES), F32),
            pltpu.VMEM((HEADS_PER_TILE, ROW_TILE, LANES), F32),
        ],
        compiler_params=_params("arbitrary", "arbitrary"),
        name="proj",
    )(x, gain.reshape(1, D_MODEL), jnp.swapaxes(w_in, 1, 2), w_f, f_bias, head_gain,
      _head_mean_matrix(), _gate_placement_matrices())


def _bucket_steps(dilation):
    dist = np.arange(BAND + 1) * dilation
    max_exact = NUM_BUCKETS // 2
    large = max_exact + np.floor(
        np.log(np.maximum(dist, 1) / max_exact) / math.log(MAX_DISTANCE / max_exact)
        * (NUM_BUCKETS - max_exact)).astype(np.int64)
    bucket = np.where(dist < max_exact, dist, np.minimum(large, NUM_BUCKETS - 1))
    steps = [(0, int(bucket[0]))]
    for delta in range(1, BAND + 1):
        if bucket[delta] != bucket[delta - 1]:
            steps.append((delta, int(bucket[delta])))
    return steps


def _bias_kernel(table_ref, o_ref):
    iq = lax.broadcasted_iota(jnp.int32, (BAND, 2 * BAND), 0)
    ik = lax.broadcasted_iota(jnp.int32, (BAND, 2 * BAND), 1)
    delta = iq + BAND - ik
    in_band = jnp.logical_and(delta >= 0, delta <= BAND)
    for p, (_, dilation) in enumerate(DILATED_PATTERNS):
        steps = _bucket_steps(dilation)
        for h in range(H_A):
            val = jnp.full((BAND, 2 * BAND), table_ref[steps[0][1], h], F32)
            for start, bucket in steps[1:]:
                val = jnp.where(delta >= start, table_ref[bucket, h], val)
            o_ref[p, h] = jnp.where(in_band, val, NEG_INF)


def _band_bias(rel_table):
    n_pat = len(DILATED_PATTERNS)
    return pl.pallas_call(
        _bias_kernel,
        in_specs=[pl.BlockSpec(memory_space=pltpu.SMEM)],
        out_specs=pl.BlockSpec(memory_space=pltpu.VMEM),
        out_shape=jax.ShapeDtypeStruct((n_pat, H_A, BAND, 2 * BAND), F32),
        name="band_bias",
    )(rel_table)


DIL_UNITS = 8


def _dilated_kernel(q_ref, kp_ref, kc_ref, vp_ref, vc_ref, bias_ref, o_ref, st_ref):
    n_cls = q_ref.shape[1]
    n_blk = q_ref.shape[2] // BAND
    first = pl.program_id(2) == 0
    key_lane = lax.broadcasted_iota(jnp.int32, (1, 2 * BAND), 1)
    no_prev = jnp.where(jnp.logical_and(first, key_lane < BAND), NEG_INF, 0.0)
    stat_lane = lax.broadcasted_iota(jnp.int32, (BAND, LANES), 1)
    units = [(c, blk) for c in range(n_cls) for blk in range(n_blk)]

    def keys(prev_ref, cur_ref, c, blk, sl):
        if blk == 0:
            return jnp.concatenate([prev_ref[0, c, :, sl], cur_ref[0, c, :BAND, sl]], axis=0)
        return cur_ref[0, c, (blk - 1) * BAND:(blk + 1) * BAND, sl]

    scores = {}
    for c, blk in units:
        for h in range(H_A):
            sl = slice(h * HEAD_DIM, (h + 1) * HEAD_DIM)
            q = q_ref[0, c, blk * BAND:(blk + 1) * BAND, sl]
            scores[c, blk, h] = lax.dot_general(q, keys(kp_ref, kc_ref, c, blk, sl),
                                                (((1,), (1,)), ((), ())),
                                                preferred_element_type=F32)
    for c, blk in units:
        rows = slice(blk * BAND, (blk + 1) * BAND)
        stats = jnp.zeros((BAND, LANES), F32)
        for h in range(H_A):
            sl = slice(h * HEAD_DIM, (h + 1) * HEAD_DIM)
            s = scores[c, blk, h] + bias_ref[0, h]
            if blk == 0:
                s = s + no_prev
            mx = jnp.max(s, axis=-1, keepdims=True)
            p = jnp.exp(s - mx)
            den = jnp.sum(p, axis=-1, keepdims=True)
            o_ref[0, c, rows, sl] = jnp.dot(p.astype(BF16), keys(vp_ref, vc_ref, c, blk, sl),
                                            preferred_element_type=F32).astype(o_ref.dtype)
            stats = jnp.where(stat_lane == h, mx, stats)
            stats = jnp.where(stat_lane == H_A + h, den, stats)
        st_ref[0, c, rows, :] = stats


def _dilated(qkv, bias, pattern):
    b, dilation, n_sub, _ = qkv.shape
    n_blk = min(DIL_UNITS, n_sub // BAND)
    n_cls = DIL_UNITS // n_blk
    rows = n_blk * BAND
    nb = n_sub // rows

    def cur(which):
        return pl.BlockSpec((1, n_cls, rows, D_A), lambda bi, r, i: (bi, r, i, which))

    def prev(which):
        return pl.BlockSpec((1, n_cls, BAND, D_A),
                            lambda bi, r, i: (bi, r, jnp.maximum(i * n_blk - 1, 0), which))

    return pl.pallas_call(
        _dilated_kernel,
        grid=(b, dilation // n_cls, nb),
        in_specs=[
            cur(0), prev(1), cur(1), prev(2), cur(2),
            pl.BlockSpec((1, H_A, BAND, 2 * BAND), lambda bi, r, i: (pattern, 0, 0, 0)),
        ],
        out_specs=[
            pl.BlockSpec((1, n_cls, rows, D_A), lambda bi, r, i: (bi, r, i, 0)),
            pl.BlockSpec((1, n_cls, rows, LANES), lambda bi, r, i: (bi, r, i, 0)),
        ],
        out_shape=[
            jax.ShapeDtypeStruct((b, dilation, n_sub, D_A), BF16),
            jax.ShapeDtypeStruct((b, dilation, n_sub, LANES), F32),
        ],
        compiler_params=_params("parallel", "parallel", "arbitrary"),
        name=f"dilated_{dilation}",
    )(qkv, qkv, qkv, qkv, qkv, bias)


COMBINE_ROWS = 512


def _combine_kernel(*refs):
    n_pat = len(DILATED_PATTERNS)
    num_refs, st_refs = refs[:n_pat], refs[n_pat:2 * n_pat]
    o_ref, num_buf, st_buf = refs[2 * n_pat:]
    dilations = [ref.shape[1] for ref in num_refs]
    for p, dilation in enumerate(dilations):
        if dilation == 1:
            continue
        for rc in range(dilation):
            rows = pl.ds(rc, COMBINE_ROWS // dilation, stride=dilation)
            st_buf[p, rows, :] = st_refs[p][0, rc]
            for h in range(H_A):
                sl = slice(h * HEAD_DIM, (h + 1) * HEAD_DIM)
                num_buf[p, h, rows, :] = num_refs[p][0, rc, :, sl].astype(F32)
    stats = [st_refs[p][0, 0] if d == 1 else st_buf[p] for p, d in enumerate(dilations)]
    for h in range(H_A):
        sl = slice(h * HEAD_DIM, (h + 1) * HEAD_DIM)
        mxs = [st[:, h:h + 1] for st in stats]
        dens = [st[:, H_A + h:H_A + h + 1] for st in stats]
        mx = functools.reduce(jnp.maximum, mxs)
        num = None
        den = None
        for p, (pden, pmx) in enumerate(zip(dens, mxs)):
            scale = jnp.exp(pmx - mx)
            if dilations[p] == 1:
                pnum = num_refs[p][0, 0, :, sl].astype(F32)
            else:
                pnum = num_buf[p, h]
            num = pnum * scale if num is None else num + pnum * scale
            den = pden * scale if den is None else den + pden * scale
        o_ref[:, sl] = (num / den).astype(BF16)


def _combine(nums, stats):
    b, _, s, _ = nums[0].shape
    tps = s // COMBINE_ROWS

    def spec(arr):
        dilation, width = arr.shape[1], arr.shape[3]
        return pl.BlockSpec((1, dilation, COMBINE_ROWS // dilation, width),
                            lambda i: (i // tps, 0, i % tps, 0))

    n_pat = len(nums)
    return pl.pallas_call(
        _combine_kernel,
        grid=(b * tps,),
        in_specs=[spec(a) for a in nums] + [spec(a) for a in stats],
        out_specs=pl.BlockSpec((COMBINE_ROWS, D_A), lambda i: (i, 0)),
        out_shape=jax.ShapeDtypeStruct((b * s, D_A), BF16),
        scratch_shapes=[
            pltpu.VMEM((n_pat, H_A, COMBINE_ROWS, HEAD_DIM), F32),
            pltpu.VMEM((n_pat, COMBINE_ROWS, LANES), F32),
        ],
        compiler_params=_params("parallel"),
        name="combine_a",
    )(*nums, *stats)


def _fox_kernel(q_ref, k_ref, vt_ref, o_ref, m_ref, l_ref, acc_ref, sa_ref, sb_ref):
    iq = pl.program_id(2)
    m_ref[...] = jnp.full(m_ref.shape, NEG_INF, F32)
    l_ref[...] = jnp.zeros(l_ref.shape, F32)
    acc_ref[...] = jnp.zeros(acc_ref.shape, F32)

    def scores_into(s_ref, kb):
        start = pl.multiple_of(kb * FOX_TK, FOX_TK)
        for hh in range(FOX_HEADS):
            k = k_ref[0, hh, pl.ds(start, FOX_TK), :]
            s_ref[hh] = lax.dot_general(k, q_ref[0, hh], (((1,), (1,)), ((), ())),
                                        preferred_element_type=F32)

    def accumulate(s_ref, kb, diagonal):
        start = pl.multiple_of(kb * FOX_TK, FOX_TK)
        for hh in range(FOX_HEADS):
            st = s_ref[hh]
            if diagonal:
                key = lax.broadcasted_iota(jnp.int32, st.shape, 0)
                qry = lax.broadcasted_iota(jnp.int32, st.shape, 1)
                st = jnp.where(key <= qry, st, NEG_INF)
            m_prev = m_ref[hh]
            m_new = jnp.maximum(m_prev, jnp.max(st, axis=0, keepdims=True))
            alpha = jnp.exp2(m_prev - m_new)
            p = jnp.exp2(st - m_new)
            l_ref[hh] = alpha * l_ref[hh] + jnp.sum(p, axis=0, keepdims=True)
            vt = vt_ref[0, hh, :, pl.ds(start, FOX_TK)]
            acc_ref[hh] = alpha * acc_ref[hh] + jnp.dot(vt, p.astype(BF16),
                                                        preferred_element_type=F32)
            m_ref[hh] = m_new

    scores_into(sa_ref, 0)

    def pair(p, carry):
        scores_into(sb_ref, 2 * p + 1)
        accumulate(sa_ref, 2 * p, False)
        scores_into(sa_ref, 2 * p + 2)
        accumulate(sb_ref, 2 * p + 1, False)
        return carry

    lax.fori_loop(0, iq // 2, pair, 0)

    @pl.when(iq % 2 == 0)
    def _():
        accumulate(sa_ref, iq, True)

    @pl.when(iq % 2 == 1)
    def _():
        scores_into(sb_ref, iq)
        accumulate(sa_ref, iq - 1, False)
        accumulate(sb_ref, iq, True)
    for hh in range(FOX_HEADS):
        o_ref[0, :, hh * HEAD_DIM:(hh + 1) * HEAD_DIM] = (
            acc_ref[hh] / l_ref[hh]).T.astype(BF16)


def _fox(q_aug, k_aug, v_t):
    b, h_b, s, _ = q_aug.shape
    assert FOX_TQ == FOX_TK
    return pl.pallas_call(
        _fox_kernel,
        grid=(b, h_b // FOX_HEADS, s // FOX_TQ),
        in_specs=[
            pl.BlockSpec((1, FOX_HEADS, FOX_TQ, AUG), lambda bi, h, i: (bi, h, i, 0)),
            pl.BlockSpec((1, FOX_HEADS, s, AUG), lambda bi, h, i: (bi, h, 0, 0)),
            pl.BlockSpec((1, FOX_HEADS, HEAD_DIM, s), lambda bi, h, i: (bi, h, 0, 0)),
        ],
        out_specs=pl.BlockSpec((1, FOX_TQ, FOX_HEADS * HEAD_DIM), lambda bi, h, i: (bi, i, h)),
        out_shape=jax.ShapeDtypeStruct((b, s, h_b * HEAD_DIM), BF16),
        scratch_shapes=[
            pltpu.VMEM((FOX_HEADS, 1, FOX_TQ), F32),
            pltpu.VMEM((FOX_HEADS, 1, FOX_TQ), F32),
            pltpu.VMEM((FOX_HEADS, HEAD_DIM, FOX_TQ), F32),
            pltpu.VMEM((FOX_HEADS, FOX_TK, FOX_TQ), F32),
            pltpu.VMEM((FOX_HEADS, FOX_TK, FOX_TQ), F32),
        ],
        compiler_params=_params("parallel", "parallel", "arbitrary"),
        name="fox",
    )(q_aug, k_aug, v_t)


def _out_proj_kernel(a_ref, b_ref, w_ref, x_ref, o_ref):
    mixed = jnp.concatenate([a_ref[...], b_ref[...]], axis=-1)
    o_ref[...] = x_ref[...] + jnp.dot(mixed, w_ref[...].astype(BF16),
                                      preferred_element_type=F32)


OUT_ROW_TILE = 2048


def _out_proj(out_a, out_b, w_out, x):
    m = x.shape[0]
    return pl.pallas_call(
        _out_proj_kernel,
        grid=(m // OUT_ROW_TILE, D_MODEL // COL_TILE),
        in_specs=[
            pl.BlockSpec((OUT_ROW_TILE, D_A), lambda i, j: (i, 0)),
            pl.BlockSpec((OUT_ROW_TILE, D_B), lambda i, j: (i, 0)),
            pl.BlockSpec((D_A + D_B, COL_TILE), lambda i, j: (0, j)),
            pl.BlockSpec((OUT_ROW_TILE, COL_TILE), lambda i, j: (i, j)),
        ],
        out_specs=pl.BlockSpec((OUT_ROW_TILE, COL_TILE), lambda i, j: (i, j)),
        out_shape=jax.ShapeDtypeStruct((m, D_MODEL), F32),
        compiler_params=_params("parallel", "parallel"),
        name="out_proj",
    )(out_a, out_b, w_out, x)


def _mixer(x, b, s, mix_norm, w_in, layer, q_norm_a, k_norm_a, q_norm_b, k_norm_b, forget_bias,
           rel_bias_table, w_out):
    ones = jnp.ones((D_A,), F32)
    head_gain = jnp.concatenate([
        jnp.tile(q_norm_a * ATTN_SCALE, H_A), jnp.tile(k_norm_a, H_A), ones,
        jnp.tile(q_norm_b * (ATTN_SCALE * LOG2E), H_B), jnp.tile(k_norm_b, H_B), ones,
    ]).reshape(1, D_QKV)
    w_f = jnp.pad(w_in[layer, :, D_QKV:], ((0, 0), (0, LANES - H_B))).astype(BF16)
    f_bias = jnp.pad(forget_bias, (0, LANES - H_B)).reshape(1, LANES)
    *qkv_a, q_aug, k_aug, v_t = _proj(x, b, s, mix_norm, w_in, layer, w_f, f_bias, head_gain)

    out_b = _fox(q_aug, k_aug, v_t).reshape(b * s, D_B)

    bias = _band_bias(rel_bias_table)
    parts = [_dilated(qkv, bias, p) for p, qkv in enumerate(qkv_a)]
    out_a = _combine([pt[0] for pt in parts], [pt[1] for pt in parts])

    return _out_proj(out_a, out_b, w_out, x)


def kernel(x, ffn1_norm, ffn1_w_in, ffn1_w_out, mix_norm, w_in, q_norm_a, k_norm_a, q_norm_b,
           k_norm_b, forget_bias, rel_bias_table, w_out, ffn2_norm, ffn2_w_in, ffn2_w_out):
    b, s, d = x.shape
    depth = ffn1_norm.shape[0]
    x = x.reshape(b * s, d)
    for l in range(depth):
        x = _ffn(x, ffn1_norm[l], ffn1_w_in[l], ffn1_w_out[l])
        x = _mixer(x, b, s, mix_norm[l], w_in, l, q_norm_a[l], k_norm_a[l], q_norm_b[l],
                   k_norm_b[l], forget_bias[l], rel_bias_table, w_out[l])
        x = _ffn(x, ffn2_norm[l], ffn2_w_in[l], ffn2_w_out[l])
    return x.reshape(b, s, d)
```

```python
import functools
import math

import numpy as np
import jax
import jax.numpy as jnp
from jax import lax
from jax.experimental import pallas as pl
from jax.experimental.pallas import tpu as pltpu

D_MODEL = 2048
HEAD_DIM = 128
N_HEADS = D_MODEL // HEAD_DIM
H_A = N_HEADS // 2
H_B = N_HEADS - H_A
D_A = H_A * HEAD_DIM
D_B = H_B * HEAD_DIM
D_QKV = 3 * D_A + 3 * D_B
DILATED_PATTERNS = ((128, 1), (512, 4), (2048, 16))
BAND = 128
NUM_BUCKETS = 32
MAX_DISTANCE = 2048
D_FF = ((8 * D_MODEL // 3 + 127) // 128) * 128
RMS_EPS = 1e-6
NEG_INF = -1e30
ATTN_SCALE = HEAD_DIM ** -0.5

LANES = 128
FF_TILE = 512
FFN_ROW_TILE = 2048
FFN_CHUNK = 512
FFN_VMEM_LIMIT = 60 * 1024 * 1024
ROW_TILE = 1024
COL_TILE = 512
FOX_TQ = 512
FOX_TK = 512
FOX_HEADS = 4
VMEM_LIMIT = 56 * 1024 * 1024

F32 = jnp.float32
BF16 = jnp.bfloat16


def _params(*sem):
    return pltpu.CompilerParams(dimension_semantics=sem, vmem_limit_bytes=VMEM_LIMIT)


def _rms_rows(x, gain):
    ms = jnp.mean(x * x, axis=-1, keepdims=True)
    return x * lax.rsqrt(ms + RMS_EPS) * gain


def _ffn_kernel(x_hbm, g_ref, wg_ref, wu_ref, wo_ref, o_hbm, acc_ref, h_ref, in_sem, out_sem):
    i = pl.program_id(0)
    j = pl.program_id(1)
    last_i = pl.num_programs(0) - 1
    last_j = pl.num_programs(1) - 1
    pieces = [(r, c) for r in range(FFN_ROW_TILE // ROW_TILE) for c in range(D_MODEL // COL_TILE)]

    def x_copy(c):
        src = x_hbm.at[pl.ds(i * FFN_ROW_TILE + c * FFN_CHUNK, FFN_CHUNK), :]
        return pltpu.make_async_copy(src, acc_ref.at[pl.ds(c * FFN_CHUNK, FFN_CHUNK), :],
                                     in_sem.at[c])

    def out_copy(row_tile, r, c):
        rows, cols = pl.ds(r * ROW_TILE, ROW_TILE), pl.ds(c * COL_TILE, COL_TILE)
        dst = o_hbm.at[pl.ds(row_tile * FFN_ROW_TILE + r * ROW_TILE, ROW_TILE), cols]
        return pltpu.make_async_copy(acc_ref.at[rows, cols], dst,
                                     out_sem.at[r * (D_MODEL // COL_TILE) + c])

    @pl.when(j == 0)
    def _():
        @pl.when(i > 0)
        def _():
            for r, c in pieces:
                out_copy(i - 1, r, c).wait()

        n_chunks = FFN_ROW_TILE // FFN_CHUNK
        for c in range(n_chunks):
            x_copy(c).start()
        for c in range(n_chunks):
            x_copy(c).wait()
            rows = slice(c * FFN_CHUNK, (c + 1) * FFN_CHUNK)
            h_ref[rows, :] = _rms_rows(acc_ref[rows, :], g_ref[...]).astype(BF16)

    col = lax.broadcasted_iota(jnp.int32, (1, FF_TILE), 1)
    repeated = j * FF_TILE - jnp.minimum(j * FF_TILE, D_FF - FF_TILE)

    def tile(write_back):
        for r in range(FFN_ROW_TILE // ROW_TILE):
            rows = slice(r * ROW_TILE, (r + 1) * ROW_TILE)
            h = h_ref[rows, :]
            gate = jnp.dot(h, wg_ref[...].astype(BF16), preferred_element_type=F32)
            up = jnp.dot(h, wu_ref[...].astype(BF16), preferred_element_type=F32)
            act = gate * (1.0 / (1.0 + jnp.exp(-gate))) * (0.5 * up)
            act = jnp.where(col >= repeated, act, 0.0).astype(BF16)
            for c in range(D_MODEL // COL_TILE):
                sl = slice(c * COL_TILE, (c + 1) * COL_TILE)
                acc_ref[rows, sl] += jnp.dot(act, wo_ref[:, sl].astype(BF16),
                                             preferred_element_type=F32)
                if write_back:
                    out_copy(i, r, c).start()

    @pl.when(j < last_j)
    def _():
        tile(False)

    @pl.when(j == last_j)
    def _():
        tile(True)

        @pl.when(i == last_i)
        def _():
            for r, c in pieces:
                out_copy(i, r, c).wait()


def _ffn(x, gain, w_in, w_out):
    m = x.shape[0]
    n_ff = pl.cdiv(D_FF, FF_TILE)

    def ff_start(j):
        return pl.multiple_of(jnp.minimum(j * FF_TILE, D_FF - FF_TILE), LANES)

    return pl.pallas_call(
        _ffn_kernel,
        grid=(m // FFN_ROW_TILE, n_ff),
        in_specs=[
            pl.BlockSpec(memory_space=pl.ANY),
            pl.BlockSpec((1, D_MODEL), lambda i, j: (0, 0)),
            pl.BlockSpec((pl.Element(D_MODEL), pl.Element(FF_TILE)),
                         lambda i, j: (0, ff_start(j))),
            pl.BlockSpec((pl.Element(D_MODEL), pl.Element(FF_TILE)),
                         lambda i, j: (0, pl.multiple_of(D_FF + ff_start(j), LANES))),
            pl.BlockSpec((pl.Element(FF_TILE), pl.Element(D_MODEL)),
                         lambda i, j: (ff_start(j), 0)),
        ],
        out_specs=pl.BlockSpec(memory_space=pl.ANY),
        out_shape=jax.ShapeDtypeStruct((m, D_MODEL), F32),
        scratch_shapes=[
            pltpu.VMEM((FFN_ROW_TILE, D_MODEL), F32),
            pltpu.VMEM((FFN_ROW_TILE, D_MODEL), BF16),
            pltpu.SemaphoreType.DMA((FFN_ROW_TILE // FFN_CHUNK,)),
            pltpu.SemaphoreType.DMA(((FFN_ROW_TILE // ROW_TILE) * (D_MODEL // COL_TILE),)),
        ],
        compiler_params=pltpu.CompilerParams(
            dimension_semantics=("arbitrary", "arbitrary"), vmem_limit_bytes=FFN_VMEM_LIMIT),
        name="ffn",
    )(x, gain.reshape(1, D_MODEL), w_in, w_in, w_out)


HEADS_PER_TILE = COL_TILE // HEAD_DIM
N_PROJ_TILES = D_QKV // COL_TILE
TILES_PER_GROUP = D_A // COL_TILE
A_TILES = 3 * TILES_PER_GROUP
AUG = 2 * HEAD_DIM
LOG2E = math.log2(math.e)


def _log_gate_scan(z, carry):
    c = (jnp.minimum(z, 0.0) - jnp.log1p(jnp.exp(-jnp.abs(z)))) * LOG2E
    row = lax.broadcasted_iota(jnp.int32, c.shape, 0)
    shift = 1
    while shift < c.shape[0]:
        c = c + jnp.where(row >= shift, pltpu.roll(c, shift, axis=0), 0.0)
        shift *= 2
    return c + carry


def _split3(c):
    hi = c.astype(BF16)
    rest = c - hi.astype(F32)
    mid = rest.astype(BF16)
    lo = (rest - mid.astype(F32)).astype(BF16)
    return hi, mid, lo


def _head_mean_matrix():
    head = np.arange(COL_TILE) // HEAD_DIM
    return jnp.asarray((head[:, None] == head[None, :]) / HEAD_DIM, BF16)


def _gate_placement_matrices():
    mats = np.zeros((2, TILES_PER_GROUP, COL_TILE, COL_TILE), np.float32)
    ones_row = 3 * LANES
    for part in range(TILES_PER_GROUP):
        for hh in range(HEADS_PER_TILE):
            head = part * HEADS_PER_TILE + hh
            col = hh * HEAD_DIM
            for term in range(3):
                mats[0, part, term * LANES + head, col + term] = 1.0
                mats[0, part, ones_row, col + 3 + term] = 1.0
                mats[1, part, term * LANES + head, col + 3 + term] = -1.0
                mats[1, part, ones_row, col + term] = 1.0
    return jnp.asarray(mats.reshape(2 * TILES_PER_GROUP, COL_TILE, COL_TILE), BF16)


def _proj_kernel(x_ref, g_ref, w_ref, wf_ref, fb_ref, hg_ref, mean_ref, place_ref,
                 a1_ref, a4_ref, a16_ref, qb_ref, kb_ref, vt_ref,
                 h_ref, cs_ref, carry_ref, y_ref, yf_ref, *, tiles_per_seq):
    i = pl.program_id(0)
    j = pl.program_id(1)

    @pl.when(j == 0)
    def _():
        h_ref[...] = _rms_rows(x_ref[...], g_ref[...]).astype(BF16)

        @pl.when(i % tiles_per_seq == 0)
        def _():
            carry_ref[...] = jnp.zeros(carry_ref.shape, F32)

        z = jnp.dot(h_ref[...], wf_ref[...], preferred_element_type=F32) + fb_ref[...]
        c = _log_gate_scan(z, carry_ref[...])
        carry_ref[...] = c[ROW_TILE - 1:ROW_TILE, :]
        for term, part in enumerate(_split3(c)):
            cs_ref[:, term * LANES:(term + 1) * LANES] = part
        cs_ref[:, 3 * LANES:] = jnp.ones((ROW_TILE, LANES), BF16)

    r = lax.dot_general(h_ref[...], w_ref[...].astype(BF16), (((1,), (1,)), ((), ())),
                        preferred_element_type=F32)

    def normed():
        ms = jnp.dot((r * r).astype(BF16), mean_ref[...], preferred_element_type=F32)
        return r * lax.rsqrt(ms + RMS_EPS) * hg_ref[...]

    def store_dilated(y):
        a1_ref[0, 0] = y.astype(BF16)
        fine, coarse = a4_ref.shape[1], a16_ref.shape[1]
        step = coarse // fine
        for c in range(HEADS_PER_TILE):
            sl = slice(c * LANES, (c + 1) * LANES)
            y_ref[c] = y[:, sl]
            for rf in range(fine):
                rows = y_ref[c, pl.ds(rf, ROW_TILE // fine, stride=fine), :]
                a4_ref[0, rf, :, sl] = rows.astype(BF16)
                yf_ref[c, rf] = rows
                for q in range(step):
                    rows = yf_ref[c, rf, pl.ds(q, ROW_TILE // coarse, stride=step), :]
                    a16_ref[0, rf + fine * q, :, sl] = rows.astype(BF16)

    for t in range(N_PROJ_TILES):
        group, part = divmod(t, TILES_PER_GROUP)

        @pl.when(j == t)
        def _(group=group):
            if group in (0, 1):
                store_dilated(normed())
            elif group == 2:
                store_dilated(r)
            elif group in (3, 4):
                out = qb_ref if group == 3 else kb_ref
                y = normed().astype(BF16)
                aug = jnp.dot(cs_ref[...], place_ref[0],
                              preferred_element_type=F32).astype(BF16)
                for hh in range(HEADS_PER_TILE):
                    sl = slice(hh * HEAD_DIM, (hh + 1) * HEAD_DIM)
                    out[0, hh, :, :HEAD_DIM] = y[:, sl]
                    out[0, hh, :, HEAD_DIM:] = aug[:, sl]
            else:
                for hh in range(HEADS_PER_TILE):
                    vt_ref[0, hh] = r[:, hh * HEAD_DIM:(hh + 1) * HEAD_DIM].T.astype(BF16)


def _proj(x, b, s, gain, w_in, layer, w_f, f_bias, head_gain):
    m = x.shape[0]
    tps = s // ROW_TILE

    def a_spec(dilation):
        return pl.BlockSpec((1, dilation, ROW_TILE // dilation, COL_TILE),
                            lambda i, j: (i // tps, 0, i % tps, jnp.minimum(j, A_TILES - 1)))

    def head_block(first):
        return lambda i, j: (i // tps, jnp.clip(j - first, 0, TILES_PER_GROUP - 1), i % tps, 0)

    qb_first = A_TILES
    vt_first = A_TILES + 2 * TILES_PER_GROUP
    dilations = [d for _, d in DILATED_PATTERNS]
    return pl.pallas_call(
        functools.partial(_proj_kernel, tiles_per_seq=tps),
        grid=(m // ROW_TILE, N_PROJ_TILES),
        in_specs=[
            pl.BlockSpec((ROW_TILE, D_MODEL), lambda i, j: (i, 0)),
            pl.BlockSpec((1, D_MODEL), lambda i, j: (0, 0)),
            pl.BlockSpec((None, COL_TILE, D_MODEL), lambda i, j: (layer, j, 0)),
            pl.BlockSpec((D_MODEL, LANES), lambda i, j: (0, 0)),
            pl.BlockSpec((1, LANES), lambda i, j: (0, 0)),
            pl.BlockSpec((1, COL_TILE), lambda i, j: (0, j)),
            pl.BlockSpec((COL_TILE, COL_TILE), lambda i, j: (0, 0)),
            pl.BlockSpec((1, COL_TILE, COL_TILE),
                         lambda i, j: (jnp.clip(j - qb_first, 0, 2 * TILES_PER_GROUP - 1), 0, 0)),
        ],
        out_specs=[a_spec(d) for d in dilations] + [
            pl.BlockSpec((1, HEADS_PER_TILE, ROW_TILE, AUG), head_block(qb_first)),
            pl.BlockSpec((1, HEADS_PER_TILE, ROW_TILE, AUG),
                         head_block(qb_first + TILES_PER_GROUP)),
            pl.BlockSpec((1, HEADS_PER_TILE, HEAD_DIM, ROW_TILE),
                         lambda i, j: (i // tps, jnp.clip(j - vt_first, 0, TILES_PER_GROUP - 1),
                                       0, i % tps)),
        ],
        out_shape=[jax.ShapeDtypeStruct((b, d, s // d, 3 * D_A), BF16) for d in dilations] + [
            jax.ShapeDtypeStruct((b, H_B, s, AUG), BF16),
            jax.ShapeDtypeStruct((b, H_B, s, AUG), BF16),
            jax.ShapeDtypeStruct((b, H_B, HEAD_DIM, s), BF16),
        ],
        scratch_shapes=[
            pltpu.VMEM((ROW_TILE, D_MODEL), BF16),
            pltpu.VMEM((ROW_TILE, 4 * LANES), BF16),
            pltpu.VMEM((1, LANES), F32),
            pltpu.VMEM((HEADS_PER_TILE, ROW_TILE, LANES), F32),
            pltpu.VMEM((HEADS_PER_TILE, dilations[1], ROW_TILE // dilations[1], LANES), F32),
        ],
        compiler_params=_params("arbitrary", "arbitrary"),
        name="proj",
    )(x, gain.reshape(1, D_MODEL), jnp.swapaxes(w_in, 1, 2), w_f, f_bias, head_gain,
      _head_mean_matrix(), _gate_placement_matrices())


def _bucket_steps(dilation):
    dist = np.arange(BAND + 1) * dilation
    max_exact = NUM_BUCKETS // 2
    large = max_exact + np.floor(
        np.log(np.maximum(dist, 1) / max_exact) / math.log(MAX_DISTANCE / max_exact)
        * (NUM_BUCKETS - max_exact)).astype(np.int64)
    bucket = np.where(dist < max_exact, dist, np.minimum(large, NUM_BUCKETS - 1))
    steps = [(0, int(bucket[0]))]
    for delta in range(1, BAND + 1):
        if bucket[delta] != bucket[delta - 1]:
            steps.append((delta, int(bucket[delta])))
    return steps


def _bias_kernel(table_ref, o_ref):
    iq = lax.broadcasted_iota(jnp.int32, (BAND, 2 * BAND), 0)
    ik = lax.broadcasted_iota(jnp.int32, (BAND, 2 * BAND), 1)
    delta = iq + BAND - ik
    in_band = jnp.logical_and(delta >= 0, delta <= BAND)
    for p, (_, dilation) in enumerate(DILATED_PATTERNS):
        steps = _bucket_steps(dilation)
        for h in range(H_A):
            val = jnp.full((BAND, 2 * BAND), table_ref[steps[0][1], h], F32)
            for start, bucket in steps[1:]:
                val = jnp.where(delta >= start, table_ref[bucket, h], val)
            o_ref[p, h] = jnp.where(in_band, val, NEG_INF)


def _band_bias(rel_table):
    n_pat = len(DILATED_PATTERNS)
    return pl.pallas_call(
        _bias_kernel,
        in_specs=[pl.BlockSpec(memory_space=pltpu.SMEM)],
        out_specs=pl.BlockSpec(memory_space=pltpu.VMEM),
        out_shape=jax.ShapeDtypeStruct((n_pat, H_A, BAND, 2 * BAND), F32),
        name="band_bias",
    )(rel_table)


DIL_UNITS = 8


def _dilated_kernel(q_ref, kp_ref, kc_ref, vp_ref, vc_ref, bias_ref, o_ref, st_ref):
    n_cls = q_ref.shape[1]
    n_blk = q_ref.shape[2] // BAND
    first = pl.program_id(2) == 0
    key_lane = lax.broadcasted_iota(jnp.int32, (1, 2 * BAND), 1)
    no_prev = jnp.where(jnp.logical_and(first, key_lane < BAND), NEG_INF, 0.0)
    stat_lane = lax.broadcasted_iota(jnp.int32, (BAND, LANES), 1)
    units = [(c, blk) for c in range(n_cls) for blk in range(n_blk)]

    def keys(prev_ref, cur_ref, c, blk, sl):
        if blk == 0:
            return jnp.concatenate([prev_ref[0, c, :, sl], cur_ref[0, c, :BAND, sl]], axis=0)
        return cur_ref[0, c, (blk - 1) * BAND:(blk + 1) * BAND, sl]

    scores = {}
    for c, blk in units:
        for h in range(H_A):
            sl = slice(h * HEAD_DIM, (h + 1) * HEAD_DIM)
            q = q_ref[0, c, blk * BAND:(blk + 1) * BAND, sl]
            scores[c, blk, h] = lax.dot_general(q, keys(kp_ref, kc_ref, c, blk, sl),
                                                (((1,), (1,)), ((), ())),
                                                preferred_element_type=F32)
    for c, blk in units:
        rows = slice(blk * BAND, (blk + 1) * BAND)
        stats = jnp.zeros((BAND, LANES), F32)
        for h in range(H_A):
            sl = slice(h * HEAD_DIM, (h + 1) * HEAD_DIM)
            s = scores[c, blk, h] + bias_ref[0, h]
            if blk == 0:
                s = s + no_prev
            mx = jnp.max(s, axis=-1, keepdims=True)
            p = jnp.exp(s - mx)
            den = jnp.sum(p, axis=-1, keepdims=True)
            o_ref[0, c, rows, sl] = jnp.dot(p.astype(BF16), keys(vp_ref, vc_ref, c, blk, sl),
                                            preferred_element_type=F32).astype(o_ref.dtype)
            stats = jnp.where(stat_lane == h, mx, stats)
            stats = jnp.where(stat_lane == H_A + h, den, stats)
        st_ref[0, c, rows, :] = stats


def _dilated(qkv, bias, pattern):
    b, dilation, n_sub, _ = qkv.shape
    n_blk = min(DIL_UNITS, n_sub // BAND)
    n_cls = DIL_UNITS // n_blk
    rows = n_blk * BAND
    nb = n_sub // rows

    def cur(which):
        return pl.BlockSpec((1, n_cls, rows, D_A), lambda bi, r, i: (bi, r, i, which))

    def prev(which):
        return pl.BlockSpec((1, n_cls, BAND, D_A),
                            lambda bi, r, i: (bi, r, jnp.maximum(i * n_blk - 1, 0), which))

    return pl.pallas_call(
        _dilated_kernel,
        grid=(b, dilation // n_cls, nb),
        in_specs=[
            cur(0), prev(1), cur(1), prev(2), cur(2),
            pl.BlockSpec((1, H_A, BAND, 2 * BAND), lambda bi, r, i: (pattern, 0, 0, 0)),
        ],
        out_specs=[
            pl.BlockSpec((1, n_cls, rows, D_A), lambda bi, r, i: (bi, r, i, 0)),
            pl.BlockSpec((1, n_cls, rows, LANES), lambda bi, r, i: (bi, r, i, 0)),
        ],
        out_shape=[
            jax.ShapeDtypeStruct((b, dilation, n_sub, D_A), BF16),
            jax.ShapeDtypeStruct((b, dilation, n_sub, LANES), F32),
        ],
        compiler_params=_params("parallel", "parallel", "arbitrary"),
        name=f"dilated_{dilation}",
    )(qkv, qkv, qkv, qkv, qkv, bias)


COMBINE_ROWS = 512


def _combine_kernel(*refs):
    n_pat = len(DILATED_PATTERNS)
    num_refs, st_refs = refs[:n_pat], refs[n_pat:2 * n_pat]
    o_ref, num_buf, st_buf, num_tmp, st_tmp = refs[2 * n_pat:]
    dilations = [ref.shape[1] for ref in num_refs]
    fine = min(d for d in dilations if d > 1)
    heads = [slice(h * HEAD_DIM, (h + 1) * HEAD_DIM) for h in range(H_A)]
    for p, dilation in enumerate(dilations):
        if dilation == 1:
            continue
        step = dilation // fine
        for rf in range(fine):
            rows = pl.ds(rf, COMBINE_ROWS // fine, stride=fine)
            if step == 1:
                st_buf[p, rows, :] = st_refs[p][0, rf]
                for h, sl in enumerate(heads):
                    num_buf[p, h, rows, :] = num_refs[p][0, rf, :, sl].astype(F32)
                continue
            for q in range(step):
                part = pl.ds(q, COMBINE_ROWS // dilation, stride=step)
                st_tmp[part, :] = st_refs[p][0, rf + fine * q]
                for h, sl in enumerate(heads):
                    num_tmp[h, part, :] = num_refs[p][0, rf + fine * q, :, sl].astype(F32)
            st_buf[p, rows, :] = st_tmp[...]
            for h in range(H_A):
                num_buf[p, h, rows, :] = num_tmp[h]
    stats = [st_refs[p][0, 0] if d == 1 else st_buf[p] for p, d in enumerate(dilations)]
    for h in range(H_A):
        sl = slice(h * HEAD_DIM, (h + 1) * HEAD_DIM)
        mxs = [st[:, h:h + 1] for st in stats]
        dens = [st[:, H_A + h:H_A + h + 1] for st in stats]
        mx = functools.reduce(jnp.maximum, mxs)
        num = None
        den = None
        for p, (pden, pmx) in enumerate(zip(dens, mxs)):
            scale = jnp.exp(pmx - mx)
            if dilations[p] == 1:
                pnum = num_refs[p][0, 0, :, sl].astype(F32)
            else:
                pnum = num_buf[p, h]
            num = pnum * scale if num is None else num + pnum * scale
            den = pden * scale if den is None else den + pden * scale
        o_ref[:, sl] = (num / den).astype(BF16)


def _combine(nums, stats):
    b, _, s, _ = nums[0].shape
    tps = s // COMBINE_ROWS

    def spec(arr):
        dilation, width = arr.shape[1], arr.shape[3]
        return pl.BlockSpec((1, dilation, COMBINE_ROWS // dilation, width),
                            lambda i: (i // tps, 0, i % tps, 0))

    n_pat = len(nums)
    fine = min(a.shape[1] for a in nums if a.shape[1] > 1)
    return pl.pallas_call(
        _combine_kernel,
        grid=(b * tps,),
        in_specs=[spec(a) for a in nums] + [spec(a) for a in stats],
        out_specs=pl.BlockSpec((COMBINE_ROWS, D_A), lambda i: (i, 0)),
        out_shape=jax.ShapeDtypeStruct((b * s, D_A), BF16),
        scratch_shapes=[
            pltpu.VMEM((n_pat, H_A, COMBINE_ROWS, HEAD_DIM), F32),
            pltpu.VMEM((n_pat, COMBINE_ROWS, LANES), F32),
            pltpu.VMEM((H_A, COMBINE_ROWS // fine, HEAD_DIM), F32),
            pltpu.VMEM((COMBINE_ROWS // fine, LANES), F32),
        ],
        compiler_params=_params("parallel"),
        name="combine_a",
    )(*nums, *stats)


def _fox_kernel(q_ref, k_ref, vt_ref, o_ref, m_ref, l_ref, acc_ref, sa_ref, sb_ref):
    iq = pl.program_id(2)
    m_ref[...] = jnp.full(m_ref.shape, NEG_INF, F32)
    l_ref[...] = jnp.zeros(l_ref.shape, F32)
    acc_ref[...] = jnp.zeros(acc_ref.shape, F32)

    def scores_into(s_ref, kb):
        start = pl.multiple_of(kb * FOX_TK, FOX_TK)
        for hh in range(FOX_HEADS):
            k = k_ref[0, hh, pl.ds(start, FOX_TK), :]
            s_ref[hh] = lax.dot_general(k, q_ref[0, hh], (((1,), (1,)), ((), ())),
                                        preferred_element_type=F32)

    def accumulate(s_ref, kb, diagonal):
        start = pl.multiple_of(kb * FOX_TK, FOX_TK)
        for hh in range(FOX_HEADS):
            st = s_ref[hh]
            if diagonal:
                key = lax.broadcasted_iota(jnp.int32, st.shape, 0)
                qry = lax.broadcasted_iota(jnp.int32, st.shape, 1)
                st = jnp.where(key <= qry, st, NEG_INF)
            m_prev = m_ref[hh]
            m_new = jnp.maximum(m_prev, jnp.max(st, axis=0, keepdims=True))
            alpha = jnp.exp2(m_prev - m_new)
            p = jnp.exp2(st - m_new)
            l_ref[hh] = alpha * l_ref[hh] + jnp.sum(p, axis=0, keepdims=True)
            vt = vt_ref[0, hh, :, pl.ds(start, FOX_TK)]
            acc_ref[hh] = alpha * acc_ref[hh] + jnp.dot(vt, p.astype(BF16),
                                                        preferred_element_type=F32)
            m_ref[hh] = m_new

    scores_into(sa_ref, 0)

    def pair(p, carry):
        scores_into(sb_ref, 2 * p + 1)
        accumulate(sa_ref, 2 * p, False)
        scores_into(sa_ref, 2 * p + 2)
        accumulate(sb_ref, 2 * p + 1, False)
        return carry

    lax.fori_loop(0, iq // 2, pair, 0)

    @pl.when(iq % 2 == 0)
    def _():
        accumulate(sa_ref, iq, True)

    @pl.when(iq % 2 == 1)
    def _():
        scores_into(sb_ref, iq)
        accumulate(sa_ref, iq - 1, False)
        accumulate(sb_ref, iq, True)
    for hh in range(FOX_HEADS):
        o_ref[0, :, hh * HEAD_DIM:(hh + 1) * HEAD_DIM] = (
            acc_ref[hh] / l_ref[hh]).T.astype(BF16)


def _fox(q_aug, k_aug, v_t):
    b, h_b, s, _ = q_aug.shape
    assert FOX_TQ == FOX_TK
    return pl.pallas_call(
        _fox_kernel,
        grid=(b, h_b // FOX_HEADS, s // FOX_TQ),
        in_specs=[
            pl.BlockSpec((1, FOX_HEADS, FOX_TQ, AUG), lambda bi, h, i: (bi, h, i, 0)),
            pl.BlockSpec((1, FOX_HEADS, s, AUG), lambda bi, h, i: (bi, h, 0, 0)),
            pl.BlockSpec((1, FOX_HEADS, HEAD_DIM, s), lambda bi, h, i: (bi, h, 0, 0)),
        ],
        out_specs=pl.BlockSpec((1, FOX_TQ, FOX_HEADS * HEAD_DIM), lambda bi, h, i: (bi, i, h)),
        out_shape=jax.ShapeDtypeStruct((b, s, h_b * HEAD_DIM), BF16),
        scratch_shapes=[
            pltpu.VMEM((FOX_HEADS, 1, FOX_TQ), F32),
            pltpu.VMEM((FOX_HEADS, 1, FOX_TQ), F32),
            pltpu.VMEM((FOX_HEADS, HEAD_DIM, FOX_TQ), F32),
            pltpu.VMEM((FOX_HEADS, FOX_TK, FOX_TQ), F32),
            pltpu.VMEM((FOX_HEADS, FOX_TK, FOX_TQ), F32),
        ],
        compiler_params=_params("parallel", "parallel", "arbitrary"),
        name="fox",
    )(q_aug, k_aug, v_t)


def _out_proj_kernel(a_ref, b_ref, w_ref, x_ref, o_ref):
    mixed = jnp.concatenate([a_ref[...], b_ref[...]], axis=-1)
    o_ref[...] = x_ref[...] + jnp.dot(mixed, w_ref[...].astype(BF16),
                                      preferred_element_type=F32)


OUT_ROW_TILE = 2048


def _out_proj(out_a, out_b, w_out, x):
    m = x.shape[0]
    return pl.pallas_call(
        _out_proj_kernel,
        grid=(m // OUT_ROW_TILE, D_MODEL // COL_TILE),
        in_specs=[
            pl.BlockSpec((OUT_ROW_TILE, D_A), lambda i, j: (i, 0)),
            pl.BlockSpec((OUT_ROW_TILE, D_B), lambda i, j: (i, 0)),
            pl.BlockSpec((D_A + D_B, COL_TILE), lambda i, j: (0, j)),
            pl.BlockSpec((OUT_ROW_TILE, COL_TILE), lambda i, j: (i, j)),
        ],
        out_specs=pl.BlockSpec((OUT_ROW_TILE, COL_TILE), lambda i, j: (i, j)),
        out_shape=jax.ShapeDtypeStruct((m, D_MODEL), F32),
        compiler_params=_params("parallel", "parallel"),
        name="out_proj",
    )(out_a, out_b, w_out, x)


def _mixer(x, b, s, mix_norm, w_in, layer, q_norm_a, k_norm_a, q_norm_b, k_norm_b, forget_bias,
           rel_bias_table, w_out):
    ones = jnp.ones((D_A,), F32)
    head_gain = jnp.concatenate([
        jnp.tile(q_norm_a * ATTN_SCALE, H_A), jnp.tile(k_norm_a, H_A), ones,
        jnp.tile(q_norm_b * (ATTN_SCALE * LOG2E), H_B), jnp.tile(k_norm_b, H_B), ones,
    ]).reshape(1, D_QKV)
    w_f = jnp.pad(w_in[layer, :, D_QKV:], ((0, 0), (0, LANES - H_B))).astype(BF16)
    f_bias = jnp.pad(forget_bias, (0, LANES - H_B)).reshape(1, LANES)
    *qkv_a, q_aug, k_aug, v_t = _proj(x, b, s, mix_norm, w_in, layer, w_f, f_bias, head_gain)

    out_b = _fox(q_aug, k_aug, v_t).reshape(b * s, D_B)

    bias = _band_bias(rel_bias_table)
    parts = [_dilated(qkv, bias, p) for p, qkv in enumerate(qkv_a)]
    out_a = _combine([pt[0] for pt in parts], [pt[1] for pt in parts])

    return _out_proj(out_a, out_b, w_out, x)


def kernel(x, ffn1_norm, ffn1_w_in, ffn1_w_out, mix_norm, w_in, q_norm_a, k_norm_a, q_norm_b,
           k_norm_b, forget_bias, rel_bias_table, w_out, ffn2_norm, ffn2_w_in, ffn2_w_out):
    b, s, d = x.shape
    depth = ffn1_norm.shape[0]
    x = x.reshape(b * s, d)
    for l in range(depth):
        x = _ffn(x, ffn1_norm[l], ffn1_w_in[l], ffn1_w_out[l])
        x = _mixer(x, b, s, mix_norm[l], w_in, l, q_norm_a[l], k_norm_a[l], q_norm_b[l],
                   k_norm_b[l], forget_bias[l], rel_bias_table, w_out[l])
        x = _ffn(x, ffn2_norm[l], ffn2_w_in[l], ffn2_w_out[l])
    return x.reshape(b, s, d)
```

```python
import functools
import math

import numpy as np
import jax
import jax.numpy as jnp
from jax import lax
from jax.experimental import pallas as pl
from jax.experimental.pallas import tpu as pltpu

D_MODEL = 2048
HEAD_DIM = 128
N_HEADS = D_MODEL // HEAD_DIM
H_A = N_HEADS // 2
H_B = N_HEADS - H_A
D_A = H_A * HEAD_DIM
D_B = H_B * HEAD_DIM
D_QKV = 3 * D_A + 3 * D_B
DILATED_PATTERNS = ((128, 1), (512, 4), (2048, 16))
BAND = 128
NUM_BUCKETS = 32
MAX_DISTANCE = 2048
D_FF = ((8 * D_MODEL // 3 + 127) // 128) * 128
RMS_EPS = 1e-6
NEG_INF = -1e30
ATTN_SCALE = HEAD_DIM ** -0.5

LANES = 128
FF_TILE = 512
FFN_ROW_TILE = 2048
FFN_CHUNK = 512
FFN_VMEM_LIMIT = 60 * 1024 * 1024
ROW_TILE = 1024
COL_TILE = 512
FOX_TQ = 512
FOX_TK = 512
FOX_HEADS = 4
VMEM_LIMIT = 56 * 1024 * 1024

F32 = jnp.float32
BF16 = jnp.bfloat16


def _params(*sem):
    return pltpu.CompilerParams(dimension_semantics=sem, vmem_limit_bytes=VMEM_LIMIT)


def _rms_rows(x, gain):
    ms = jnp.mean(x * x, axis=-1, keepdims=True)
    return x * lax.rsqrt(ms + RMS_EPS) * gain


def _ffn_kernel(x_hbm, g_ref, wg_ref, wu_ref, wo_ref, o_hbm, acc_ref, h_ref, in_sem, out_sem):
    i = pl.program_id(0)
    j = pl.program_id(1)
    last_i = pl.num_programs(0) - 1
    last_j = pl.num_programs(1) - 1
    pieces = [(r, c) for r in range(FFN_ROW_TILE // ROW_TILE) for c in range(D_MODEL // COL_TILE)]

    def x_copy(c):
        src = x_hbm.at[pl.ds(i * FFN_ROW_TILE + c * FFN_CHUNK, FFN_CHUNK), :]
        return pltpu.make_async_copy(src, acc_ref.at[pl.ds(c * FFN_CHUNK, FFN_CHUNK), :],
                                     in_sem.at[c])

    def out_copy(row_tile, r, c):
        rows, cols = pl.ds(r * ROW_TILE, ROW_TILE), pl.ds(c * COL_TILE, COL_TILE)
        dst = o_hbm.at[pl.ds(row_tile * FFN_ROW_TILE + r * ROW_TILE, ROW_TILE), cols]
        return pltpu.make_async_copy(acc_ref.at[rows, cols], dst,
                                     out_sem.at[r * (D_MODEL // COL_TILE) + c])

    @pl.when(j == 0)
    def _():
        @pl.when(i > 0)
        def _():
            for r, c in pieces:
                out_copy(i - 1, r, c).wait()

        n_chunks = FFN_ROW_TILE // FFN_CHUNK
        for c in range(n_chunks):
            x_copy(c).start()
        for c in range(n_chunks):
            x_copy(c).wait()
            rows = slice(c * FFN_CHUNK, (c + 1) * FFN_CHUNK)
            h_ref[rows, :] = _rms_rows(acc_ref[rows, :], g_ref[...]).astype(BF16)

    col = lax.broadcasted_iota(jnp.int32, (1, FF_TILE), 1)
    repeated = j * FF_TILE - jnp.minimum(j * FF_TILE, D_FF - FF_TILE)

    def tile(write_back):
        for r in range(FFN_ROW_TILE // ROW_TILE):
            rows = slice(r * ROW_TILE, (r + 1) * ROW_TILE)
            h = h_ref[rows, :]
            gate = jnp.dot(h, wg_ref[...].astype(BF16), preferred_element_type=F32)
            up = jnp.dot(h, wu_ref[...].astype(BF16), preferred_element_type=F32)
            act = gate * (1.0 / (1.0 + jnp.exp(-gate))) * (0.5 * up)
            act = jnp.where(col >= repeated, act, 0.0).astype(BF16)
            for c in range(D_MODEL // COL_TILE):
                sl = slice(c * COL_TILE, (c + 1) * COL_TILE)
                acc_ref[rows, sl] += jnp.dot(act, wo_ref[:, sl].astype(BF16),
                                             preferred_element_type=F32)
                if write_back:
                    out_copy(i, r, c).start()

    @pl.when(j < last_j)
    def _():
        tile(False)

    @pl.when(j == last_j)
    def _():
        tile(True)

        @pl.when(i == last_i)
        def _():
            for r, c in pieces:
                out_copy(i, r, c).wait()


def _ffn(x, gain, w_in, w_out):
    m = x.shape[0]
    n_ff = pl.cdiv(D_FF, FF_TILE)

    def ff_start(j):
        return pl.multiple_of(jnp.minimum(j * FF_TILE, D_FF - FF_TILE), LANES)

    return pl.pallas_call(
        _ffn_kernel,
        grid=(m // FFN_ROW_TILE, n_ff),
        in_specs=[
            pl.BlockSpec(memory_space=pl.ANY),
            pl.BlockSpec((1, D_MODEL), lambda i, j: (0, 0)),
            pl.BlockSpec((pl.Element(D_MODEL), pl.Element(FF_TILE)),
                         lambda i, j: (0, ff_start(j))),
            pl.BlockSpec((pl.Element(D_MODEL), pl.Element(FF_TILE)),
                         lambda i, j: (0, pl.multiple_of(D_FF + ff_start(j), LANES))),
            pl.BlockSpec((pl.Element(FF_TILE), pl.Element(D_MODEL)),
                         lambda i, j: (ff_start(j), 0)),
        ],
        out_specs=pl.BlockSpec(memory_space=pl.ANY),
        out_shape=jax.ShapeDtypeStruct((m, D_MODEL), F32),
        scratch_shapes=[
            pltpu.VMEM((FFN_ROW_TILE, D_MODEL), F32),
            pltpu.VMEM((FFN_ROW_TILE, D_MODEL), BF16),
            pltpu.SemaphoreType.DMA((FFN_ROW_TILE // FFN_CHUNK,)),
            pltpu.SemaphoreType.DMA(((FFN_ROW_TILE // ROW_TILE) * (D_MODEL // COL_TILE),)),
        ],
        compiler_params=pltpu.CompilerParams(
            dimension_semantics=("arbitrary", "arbitrary"), vmem_limit_bytes=FFN_VMEM_LIMIT),
        name="ffn",
    )(x, gain.reshape(1, D_MODEL), w_in, w_in, w_out)


HEADS_PER_TILE = COL_TILE // HEAD_DIM
N_PROJ_TILES = D_QKV // COL_TILE
TILES_PER_GROUP = D_A // COL_TILE
A_TILES = 3 * TILES_PER_GROUP
AUG = 2 * HEAD_DIM
LOG2E = math.log2(math.e)


def _log_gate_scan(z, carry):
    c = (jnp.minimum(z, 0.0) - jnp.log1p(jnp.exp(-jnp.abs(z)))) * LOG2E
    row = lax.broadcasted_iota(jnp.int32, c.shape, 0)
    shift = 1
    while shift < c.shape[0]:
        c = c + jnp.where(row >= shift, pltpu.roll(c, shift, axis=0), 0.0)
        shift *= 2
    return c + carry


def _split3(c):
    hi = c.astype(BF16)
    rest = c - hi.astype(F32)
    mid = rest.astype(BF16)
    lo = (rest - mid.astype(F32)).astype(BF16)
    return hi, mid, lo


def _head_mean_matrix():
    head = np.arange(COL_TILE) // HEAD_DIM
    return jnp.asarray((head[:, None] == head[None, :]) / HEAD_DIM, BF16)


def _gate_placement_matrices():
    mats = np.zeros((2, TILES_PER_GROUP, COL_TILE, COL_TILE), np.float32)
    ones_row = 3 * LANES
    for part in range(TILES_PER_GROUP):
        for hh in range(HEADS_PER_TILE):
            head = part * HEADS_PER_TILE + hh
            col = hh * HEAD_DIM
            for term in range(3):
                mats[0, part, term * LANES + head, col + term] = 1.0
                mats[0, part, ones_row, col + 3 + term] = 1.0
                mats[1, part, term * LANES + head, col + 3 + term] = -1.0
                mats[1, part, ones_row, col + term] = 1.0
    return jnp.asarray(mats.reshape(2 * TILES_PER_GROUP, COL_TILE, COL_TILE), BF16)


def _proj_kernel(x_ref, g_ref, w_ref, wf_ref, fb_ref, hg_ref, mean_ref, place_ref,
                 a1_ref, a4_ref, a16_ref, qb_ref, kb_ref, vt_ref,
                 h_ref, cs_ref, carry_ref, y_ref, yf_ref, *, tiles_per_seq):
    i = pl.program_id(0)
    j = pl.program_id(1)

    @pl.when(j == 0)
    def _():
        h_ref[...] = _rms_rows(x_ref[...], g_ref[...]).astype(BF16)

        @pl.when(i % tiles_per_seq == 0)
        def _():
            carry_ref[...] = jnp.zeros(carry_ref.shape, F32)

        z = jnp.dot(h_ref[...], wf_ref[...], preferred_element_type=F32) + fb_ref[...]
        c = _log_gate_scan(z, carry_ref[...])
        carry_ref[...] = c[ROW_TILE - 1:ROW_TILE, :]
        for term, part in enumerate(_split3(c)):
            cs_ref[:, term * LANES:(term + 1) * LANES] = part
        cs_ref[:, 3 * LANES:] = jnp.ones((ROW_TILE, LANES), BF16)

    def product(transposed=False):
        lhs, rhs = (w_ref[...].astype(BF16), h_ref[...])
        if not transposed:
            lhs, rhs = rhs, lhs
        return lax.dot_general(lhs, rhs, (((1,), (1,)), ((), ())), preferred_element_type=F32)

    def normed():
        r = product()
        ms = jnp.dot((r * r).astype(BF16), mean_ref[...], preferred_element_type=F32)
        return r * lax.rsqrt(ms + RMS_EPS) * hg_ref[...]

    def store_dilated(y):
        a1_ref[0, 0] = y.astype(BF16)
        fine, coarse = a4_ref.shape[1], a16_ref.shape[1]
        step = coarse // fine
        for c in range(HEADS_PER_TILE):
            sl = slice(c * LANES, (c + 1) * LANES)
            y_ref[c] = y[:, sl]
            for rf in range(fine):
                rows = y_ref[c, pl.ds(rf, ROW_TILE // fine, stride=fine), :]
                a4_ref[0, rf, :, sl] = rows.astype(BF16)
                yf_ref[c, rf] = rows
                for q in range(step):
                    rows = yf_ref[c, rf, pl.ds(q, ROW_TILE // coarse, stride=step), :]
                    a16_ref[0, rf + fine * q, :, sl] = rows.astype(BF16)

    for t in range(N_PROJ_TILES):
        group, part = divmod(t, TILES_PER_GROUP)

        @pl.when(j == t)
        def _(group=group):
            if group in (0, 1):
                store_dilated(normed())
            elif group == 2:
                store_dilated(product())
            elif group in (3, 4):
                out = qb_ref if group == 3 else kb_ref
                y = normed().astype(BF16)
                aug = jnp.dot(cs_ref[...], place_ref[0],
                              preferred_element_type=F32).astype(BF16)
                for hh in range(HEADS_PER_TILE):
                    sl = slice(hh * HEAD_DIM, (hh + 1) * HEAD_DIM)
                    out[0, hh, :, :HEAD_DIM] = y[:, sl]
                    out[0, hh, :, HEAD_DIM:] = aug[:, sl]
            else:
                v_t = product(transposed=True).astype(BF16)
                for hh in range(HEADS_PER_TILE):
                    vt_ref[0, hh] = v_t[hh * HEAD_DIM:(hh + 1) * HEAD_DIM, :]


def _proj(x, b, s, gain, w_in, layer, w_f, f_bias, head_gain):
    m = x.shape[0]
    tps = s // ROW_TILE

    def a_spec(dilation):
        return pl.BlockSpec((1, dilation, ROW_TILE // dilation, COL_TILE),
                            lambda i, j: (i // tps, 0, i % tps, jnp.minimum(j, A_TILES - 1)))

    def head_block(first):
        return lambda i, j: (i // tps, jnp.clip(j - first, 0, TILES_PER_GROUP - 1), i % tps, 0)

    qb_first = A_TILES
    vt_first = A_TILES + 2 * TILES_PER_GROUP
    dilations = [d for _, d in DILATED_PATTERNS]
    return pl.pallas_call(
        functools.partial(_proj_kernel, tiles_per_seq=tps),
        grid=(m // ROW_TILE, N_PROJ_TILES),
        in_specs=[
            pl.BlockSpec((ROW_TILE, D_MODEL), lambda i, j: (i, 0)),
            pl.BlockSpec((1, D_MODEL), lambda i, j: (0, 0)),
            pl.BlockSpec((None, COL_TILE, D_MODEL), lambda i, j: (layer, j, 0)),
            pl.BlockSpec((D_MODEL, LANES), lambda i, j: (0, 0)),
            pl.BlockSpec((1, LANES), lambda i, j: (0, 0)),
            pl.BlockSpec((1, COL_TILE), lambda i, j: (0, j)),
            pl.BlockSpec((COL_TILE, COL_TILE), lambda i, j: (0, 0)),
            pl.BlockSpec((1, COL_TILE, COL_TILE),
                         lambda i, j: (jnp.clip(j - qb_first, 0, 2 * TILES_PER_GROUP - 1), 0, 0)),
        ],
        out_specs=[a_spec(d) for d in dilations] + [
            pl.BlockSpec((1, HEADS_PER_TILE, ROW_TILE, AUG), head_block(qb_first)),
            pl.BlockSpec((1, HEADS_PER_TILE, ROW_TILE, AUG),
                         head_block(qb_first + TILES_PER_GROUP)),
            pl.BlockSpec((1, HEADS_PER_TILE, HEAD_DIM, ROW_TILE),
                         lambda i, j: (i // tps, jnp.clip(j - vt_first, 0, TILES_PER_GROUP - 1),
                                       0, i % tps)),
        ],
        out_shape=[jax.ShapeDtypeStruct((b, d, s // d, 3 * D_A), BF16) for d in dilations] + [
            jax.ShapeDtypeStruct((b, H_B, s, AUG), BF16),
            jax.ShapeDtypeStruct((b, H_B, s, AUG), BF16),
            jax.ShapeDtypeStruct((b, H_B, HEAD_DIM, s), BF16),
        ],
        scratch_shapes=[
            pltpu.VMEM((ROW_TILE, D_MODEL), BF16),
            pltpu.VMEM((ROW_TILE, 4 * LANES), BF16),
            pltpu.VMEM((1, LANES), F32),
            pltpu.VMEM((HEADS_PER_TILE, ROW_TILE, LANES), F32),
            pltpu.VMEM((HEADS_PER_TILE, dilations[1], ROW_TILE // dilations[1], LANES), F32),
        ],
        compiler_params=_params("arbitrary", "arbitrary"),
        name="proj",
    )(x, gain.reshape(1, D_MODEL), jnp.swapaxes(w_in, 1, 2), w_f, f_bias, head_gain,
      _head_mean_matrix(), _gate_placement_matrices())


def _bucket_steps(dilation):
    dist = np.arange(BAND + 1) * dilation
    max_exact = NUM_BUCKETS // 2
    large = max_exact + np.floor(
        np.log(np.maximum(dist, 1) / max_exact) / math.log(MAX_DISTANCE / max_exact)
        * (NUM_BUCKETS - max_exact)).astype(np.int64)
    bucket = np.where(dist < max_exact, dist, np.minimum(large, NUM_BUCKETS - 1))
    steps = [(0, int(bucket[0]))]
    for delta in range(1, BAND + 1):
        if bucket[delta] != bucket[delta - 1]:
            steps.append((delta, int(bucket[delta])))
    return steps


def _bias_kernel(table_ref, o_ref):
    iq = lax.broadcasted_iota(jnp.int32, (BAND, 2 * BAND), 0)
    ik = lax.broadcasted_iota(jnp.int32, (BAND, 2 * BAND), 1)
    delta = iq + BAND - ik
    in_band = jnp.logical_and(delta >= 0, delta <= BAND)
    for p, (_, dilation) in enumerate(DILATED_PATTERNS):
        steps = _bucket_steps(dilation)
        for h in range(H_A):
            val = jnp.full((BAND, 2 * BAND), table_ref[steps[0][1], h], F32)
            for start, bucket in steps[1:]:
                val = jnp.where(delta >= start, table_ref[bucket, h], val)
            o_ref[p, h] = jnp.where(in_band, val, NEG_INF)


def _band_bias(rel_table):
    n_pat = len(DILATED_PATTERNS)
    return pl.pallas_call(
        _bias_kernel,
        in_specs=[pl.BlockSpec(memory_space=pltpu.SMEM)],
        out_specs=pl.BlockSpec(memory_space=pltpu.VMEM),
        out_shape=jax.ShapeDtypeStruct((n_pat, H_A, BAND, 2 * BAND), F32),
        name="band_bias",
    )(rel_table)


DIL_UNITS = 8


def _dilated_kernel(q_ref, kp_ref, kc_ref, vp_ref, vc_ref, bias_ref, o_ref, st_ref):
    n_cls = q_ref.shape[1]
    n_blk = q_ref.shape[2] // BAND
    first = pl.program_id(2) == 0
    key_lane = lax.broadcasted_iota(jnp.int32, (1, 2 * BAND), 1)
    no_prev = jnp.where(jnp.logical_and(first, key_lane < BAND), NEG_INF, 0.0)
    stat_lane = lax.broadcasted_iota(jnp.int32, (BAND, LANES), 1)
    units = [(c, blk) for c in range(n_cls) for blk in range(n_blk)]

    def keys(prev_ref, cur_ref, c, blk, sl):
        if blk == 0:
            return jnp.concatenate([prev_ref[0, c, :, sl], cur_ref[0, c, :BAND, sl]], axis=0)
        return cur_ref[0, c, (blk - 1) * BAND:(blk + 1) * BAND, sl]

    scores = {}
    for c, blk in units:
        for h in range(H_A):
            sl = slice(h * HEAD_DIM, (h + 1) * HEAD_DIM)
            q = q_ref[0, c, blk * BAND:(blk + 1) * BAND, sl]
            scores[c, blk, h] = lax.dot_general(q, keys(kp_ref, kc_ref, c, blk, sl),
                                                (((1,), (1,)), ((), ())),
                                                preferred_element_type=F32)
    for c, blk in units:
        rows = slice(blk * BAND, (blk + 1) * BAND)
        stats = jnp.zeros((BAND, LANES), F32)
        for h in range(H_A):
            sl = slice(h * HEAD_DIM, (h + 1) * HEAD_DIM)
            s = scores[c, blk, h] + bias_ref[0, h]
            if blk == 0:
                s = s + no_prev
            mx = jnp.max(s, axis=-1, keepdims=True)
            p = jnp.exp(s - mx)
            den = jnp.sum(p, axis=-1, keepdims=True)
            o_ref[0, c, rows, sl] = jnp.dot(p.astype(BF16), keys(vp_ref, vc_ref, c, blk, sl),
                                            preferred_element_type=F32).astype(o_ref.dtype)
            stats = jnp.where(stat_lane == h, mx, stats)
            stats = jnp.where(stat_lane == H_A + h, den, stats)
        st_ref[0, c, rows, :] = stats


def _dilated(qkv, bias, pattern):
    b, dilation, n_sub, _ = qkv.shape
    n_blk = min(DIL_UNITS, n_sub // BAND)
    n_cls = DIL_UNITS // n_blk
    rows = n_blk * BAND
    nb = n_sub // rows

    def cur(which):
        return pl.BlockSpec((1, n_cls, rows, D_A), lambda bi, r, i: (bi, r, i, which))

    def prev(which):
        return pl.BlockSpec((1, n_cls, BAND, D_A),
                            lambda bi, r, i: (bi, r, jnp.maximum(i * n_blk - 1, 0), which))

    return pl.pallas_call(
        _dilated_kernel,
        grid=(b, dilation // n_cls, nb),
        in_specs=[
            cur(0), prev(1), cur(1), prev(2), cur(2),
            pl.BlockSpec((1, H_A, BAND, 2 * BAND), lambda bi, r, i: (pattern, 0, 0, 0)),
        ],
        out_specs=[
            pl.BlockSpec((1, n_cls, rows, D_A), lambda bi, r, i: (bi, r, i, 0)),
            pl.BlockSpec((1, n_cls, rows, LANES), lambda bi, r, i: (bi, r, i, 0)),
        ],
        out_shape=[
            jax.ShapeDtypeStruct((b, dilation, n_sub, D_A), BF16),
            jax.ShapeDtypeStruct((b, dilation, n_sub, LANES), F32),
        ],
        compiler_params=_params("parallel", "parallel", "arbitrary"),
        name=f"dilated_{dilation}",
    )(qkv, qkv, qkv, qkv, qkv, bias)


COMBINE_ROWS = 512


def _combine_kernel(*refs):
    n_pat = len(DILATED_PATTERNS)
    num_refs, st_refs = refs[:n_pat], refs[n_pat:2 * n_pat]
    o_ref, num_buf, st_buf, num_tmp, st_tmp = refs[2 * n_pat:]
    dilations = [ref.shape[1] for ref in num_refs]
    fine = min(d for d in dilations if d > 1)
    heads = [slice(h * HEAD_DIM, (h + 1) * HEAD_DIM) for h in range(H_A)]
    for p, dilation in enumerate(dilations):
        if dilation == 1:
            continue
        step = dilation // fine
        for rf in range(fine):
            rows = pl.ds(rf, COMBINE_ROWS // fine, stride=fine)
            if step == 1:
                st_buf[p, rows, :] = st_refs[p][0, rf]
                for h, sl in enumerate(heads):
                    num_buf[p, h, rows, :] = num_refs[p][0, rf, :, sl].astype(F32)
                continue
            for q in range(step):
                part = pl.ds(q, COMBINE_ROWS // dilation, stride=step)
                st_tmp[part, :] = st_refs[p][0, rf + fine * q]
                for h, sl in enumerate(heads):
                    num_tmp[h, part, :] = num_refs[p][0, rf + fine * q, :, sl].astype(F32)
            st_buf[p, rows, :] = st_tmp[...]
            for h in range(H_A):
                num_buf[p, h, rows, :] = num_tmp[h]
    stats = [st_refs[p][0, 0] if d == 1 else st_buf[p] for p, d in enumerate(dilations)]
    for h in range(H_A):
        sl = slice(h * HEAD_DIM, (h + 1) * HEAD_DIM)
        mxs = [st[:, h:h + 1] for st in stats]
        dens = [st[:, H_A + h:H_A + h + 1] for st in stats]
        mx = functools.reduce(jnp.maximum, mxs)
        num = None
        den = None
        for p, (pden, pmx) in enumerate(zip(dens, mxs)):
            scale = jnp.exp(pmx - mx)
            if dilations[p] == 1:
                pnum = num_refs[p][0, 0, :, sl].astype(F32)
            else:
                pnum = num_buf[p, h]
            num = pnum * scale if num is None else num + pnum * scale
            den = pden * scale if den is None else den + pden * scale
        o_ref[:, sl] = (num / den).astype(BF16)


def _combine(nums, stats):
    b, _, s, _ = nums[0].shape
    tps = s // COMBINE_ROWS

    def spec(arr):
        dilation, width = arr.shape[1], arr.shape[3]
        return pl.BlockSpec((1, dilation, COMBINE_ROWS // dilation, width),
                            lambda i: (i // tps, 0, i % tps, 0))

    n_pat = len(nums)
    fine = min(a.shape[1] for a in nums if a.shape[1] > 1)
    return pl.pallas_call(
        _combine_kernel,
        grid=(b * tps,),
        in_specs=[spec(a) for a in nums] + [spec(a) for a in stats],
        out_specs=pl.BlockSpec((COMBINE_ROWS, D_A), lambda i: (i, 0)),
        out_shape=jax.ShapeDtypeStruct((b * s, D_A), BF16),
        scratch_shapes=[
            pltpu.VMEM((n_pat, H_A, COMBINE_ROWS, HEAD_DIM), F32),
            pltpu.VMEM((n_pat, COMBINE_ROWS, LANES), F32),
            pltpu.VMEM((H_A, COMBINE_ROWS // fine, HEAD_DIM), F32),
            pltpu.VMEM((COMBINE_ROWS // fine, LANES), F32),
        ],
        compiler_params=_params("parallel"),
        name="combine_a",
    )(*nums, *stats)


def _fox_kernel(q_ref, k_ref, vt_ref, o_ref, m_ref, l_ref, acc_ref, sa_ref, sb_ref):
    iq = pl.program_id(2)
    m_ref[...] = jnp.full(m_ref.shape, NEG_INF, F32)
    l_ref[...] = jnp.zeros(l_ref.shape, F32)
    acc_ref[...] = jnp.zeros(acc_ref.shape, F32)

    def scores_into(s_ref, kb):
        start = pl.multiple_of(kb * FOX_TK, FOX_TK)
        for hh in range(FOX_HEADS):
            k = k_ref[0, hh, pl.ds(start, FOX_TK), :]
            s_ref[hh] = lax.dot_general(k, q_ref[0, hh], (((1,), (1,)), ((), ())),
                                        preferred_element_type=F32)

    def accumulate(s_ref, kb, diagonal):
        start = pl.multiple_of(kb * FOX_TK, FOX_TK)
        for hh in range(FOX_HEADS):
            st = s_ref[hh]
            if diagonal:
                key = lax.broadcasted_iota(jnp.int32, st.shape, 0)
                qry = lax.broadcasted_iota(jnp.int32, st.shape, 1)
                st = jnp.where(key <= qry, st, NEG_INF)
            m_prev = m_ref[hh]
            m_new = jnp.maximum(m_prev, jnp.max(st, axis=0, keepdims=True))
            alpha = jnp.exp2(m_prev - m_new)
            p = jnp.exp2(st - m_new)
            l_ref[hh] = alpha * l_ref[hh] + jnp.sum(p, axis=0, keepdims=True)
            vt = vt_ref[0, hh, :, pl.ds(start, FOX_TK)]
            acc_ref[hh] = alpha * acc_ref[hh] + jnp.dot(vt, p.astype(BF16),
                                                        preferred_element_type=F32)
            m_ref[hh] = m_new

    scores_into(sa_ref, 0)

    def pair(p, carry):
        scores_into(sb_ref, 2 * p + 1)
        accumulate(sa_ref, 2 * p, False)
        scores_into(sa_ref, 2 * p + 2)
        accumulate(sb_ref, 2 * p + 1, False)
        return carry

    lax.fori_loop(0, iq // 2, pair, 0)

    @pl.when(iq % 2 == 0)
    def _():
        accumulate(sa_ref, iq, True)

    @pl.when(iq % 2 == 1)
    def _():
        scores_into(sb_ref, iq)
        accumulate(sa_ref, iq - 1, False)
        accumulate(sb_ref, iq, True)
    for hh in range(FOX_HEADS):
        o_ref[0, :, hh * HEAD_DIM:(hh + 1) * HEAD_DIM] = (
            acc_ref[hh] / l_ref[hh]).T.astype(BF16)


def _fox(q_aug, k_aug, v_t):
    b, h_b, s, _ = q_aug.shape
    assert FOX_TQ == FOX_TK
    return pl.pallas_call(
        _fox_kernel,
        grid=(b, h_b // FOX_HEADS, s // FOX_TQ),
        in_specs=[
            pl.BlockSpec((1, FOX_HEADS, FOX_TQ, AUG), lambda bi, h, i: (bi, h, i, 0)),
            pl.BlockSpec((1, FOX_HEADS, s, AUG), lambda bi, h, i: (bi, h, 0, 0)),
            pl.BlockSpec((1, FOX_HEADS, HEAD_DIM, s), lambda bi, h, i: (bi, h, 0, 0)),
        ],
        out_specs=pl.BlockSpec((1, FOX_TQ, FOX_HEADS * HEAD_DIM), lambda bi, h, i: (bi, i, h)),
        out_shape=jax.ShapeDtypeStruct((b, s, h_b * HEAD_DIM), BF16),
        scratch_shapes=[
            pltpu.VMEM((FOX_HEADS, 1, FOX_TQ), F32),
            pltpu.VMEM((FOX_HEADS, 1, FOX_TQ), F32),
            pltpu.VMEM((FOX_HEADS, HEAD_DIM, FOX_TQ), F32),
            pltpu.VMEM((FOX_HEADS, FOX_TK, FOX_TQ), F32),
            pltpu.VMEM((FOX_HEADS, FOX_TK, FOX_TQ), F32),
        ],
        compiler_params=_params("parallel", "parallel", "arbitrary"),
        name="fox",
    )(q_aug, k_aug, v_t)


def _out_proj_kernel(a_ref, b_ref, w_ref, x_ref, o_ref):
    mixed = jnp.concatenate([a_ref[...], b_ref[...]], axis=-1)
    o_ref[...] = x_ref[...] + jnp.dot(mixed, w_ref[...].astype(BF16),
                                      preferred_element_type=F32)


OUT_ROW_TILE = 2048


def _out_proj(out_a, out_b, w_out, x):
    m = x.shape[0]
    return pl.pallas_call(
        _out_proj_kernel,
        grid=(m // OUT_ROW_TILE, D_MODEL // COL_TILE),
        in_specs=[
            pl.BlockSpec((OUT_ROW_TILE, D_A), lambda i, j: (i, 0)),
            pl.BlockSpec((OUT_ROW_TILE, D_B), lambda i, j: (i, 0)),
            pl.BlockSpec((D_A + D_B, COL_TILE), lambda i, j: (0, j)),
            pl.BlockSpec((OUT_ROW_TILE, COL_TILE), lambda i, j: (i, j)),
        ],
        out_specs=pl.BlockSpec((OUT_ROW_TILE, COL_TILE), lambda i, j: (i, j)),
        out_shape=jax.ShapeDtypeStruct((m, D_MODEL), F32),
        compiler_params=_params("parallel", "parallel"),
        name="out_proj",
    )(out_a, out_b, w_out, x)


def _mixer(x, b, s, mix_norm, w_in, layer, q_norm_a, k_norm_a, q_norm_b, k_norm_b, forget_bias,
           rel_bias_table, w_out):
    ones = jnp.ones((D_A,), F32)
    head_gain = jnp.concatenate([
        jnp.tile(q_norm_a * ATTN_SCALE, H_A), jnp.tile(k_norm_a, H_A), ones,
        jnp.tile(q_norm_b * (ATTN_SCALE * LOG2E), H_B), jnp.tile(k_norm_b, H_B), ones,
    ]).reshape(1, D_QKV)
    w_f = jnp.pad(w_in[layer, :, D_QKV:], ((0, 0), (0, LANES - H_B))).astype(BF16)
    f_bias = jnp.pad(forget_bias, (0, LANES - H_B)).reshape(1, LANES)
    *qkv_a, q_aug, k_aug, v_t = _proj(x, b, s, mix_norm, w_in, layer, w_f, f_bias, head_gain)

    out_b = _fox(q_aug, k_aug, v_t).reshape(b * s, D_B)

    bias = _band_bias(rel_bias_table)
    parts = [_dilated(qkv, bias, p) for p, qkv in enumerate(qkv_a)]
    out_a = _combine([pt[0] for pt in parts], [pt[1] for pt in parts])

    return _out_proj(out_a, out_b, w_out, x)


def kernel(x, ffn1_norm, ffn1_w_in, ffn1_w_out, mix_norm, w_in, q_norm_a, k_norm_a, q_norm_b,
           k_norm_b, forget_bias, rel_bias_table, w_out, ffn2_norm, ffn2_w_in, ffn2_w_out):
    b, s, d = x.shape
    depth = ffn1_norm.shape[0]
    x = x.reshape(b * s, d)
    for l in range(depth):
        x = _ffn(x, ffn1_norm[l], ffn1_w_in[l], ffn1_w_out[l])
        x = _mixer(x, b, s, mix_norm[l], w_in, l, q_norm_a[l], k_norm_a[l], q_norm_b[l],
                   k_norm_b[l], forget_bias[l], rel_bias_table, w_out[l])
        x = _ffn(x, ffn2_norm[l], ffn2_w_in[l], ffn2_w_out[l])
    return x.reshape(b, s, d)
```

```python
import functools
import math

import numpy as np
import jax
import jax.numpy as jnp
from jax import lax
from jax.experimental import pallas as pl
from jax.experimental.pallas import tpu as pltpu

D_MODEL = 2048
HEAD_DIM = 128
N_HEADS = D_MODEL // HEAD_DIM
H_A = N_HEADS // 2
H_B = N_HEADS - H_A
D_A = H_A * HEAD_DIM
D_B = H_B * HEAD_DIM
D_QKV = 3 * D_A + 3 * D_B
DILATED_PATTERNS = ((128, 1), (512, 4), (2048, 16))
BAND = 128
NUM_BUCKETS = 32
MAX_DISTANCE = 2048
D_FF = ((8 * D_MODEL // 3 + 127) // 128) * 128
RMS_EPS = 1e-6
NEG_INF = -1e30
ATTN_SCALE = HEAD_DIM ** -0.5

LANES = 128
FF_TILE = 512
FFN_ROW_TILE = 2048
FFN_CHUNK = 512
FFN_VMEM_LIMIT = 60 * 1024 * 1024
ROW_TILE = 1024
COL_TILE = 512
FOX_TQ = 512
FOX_TK = 512
FOX_HEADS = 4
VMEM_LIMIT = 56 * 1024 * 1024

F32 = jnp.float32
BF16 = jnp.bfloat16


def _params(*sem):
    return pltpu.CompilerParams(dimension_semantics=sem, vmem_limit_bytes=VMEM_LIMIT)


def _rms_rows(x, gain):
    ms = jnp.mean(x * x, axis=-1, keepdims=True)
    return x * lax.rsqrt(ms + RMS_EPS) * gain


def _ffn_kernel(x_hbm, g_ref, wg_ref, wu_ref, wo_ref, o_hbm, acc_ref, h_ref, in_sem, out_sem):
    i = pl.program_id(0)
    j = pl.program_id(1)
    last_i = pl.num_programs(0) - 1
    last_j = pl.num_programs(1) - 1
    pieces = [(r, c) for r in range(FFN_ROW_TILE // ROW_TILE) for c in range(D_MODEL // COL_TILE)]

    def x_copy(c):
        src = x_hbm.at[pl.ds(i * FFN_ROW_TILE + c * FFN_CHUNK, FFN_CHUNK), :]
        return pltpu.make_async_copy(src, acc_ref.at[pl.ds(c * FFN_CHUNK, FFN_CHUNK), :],
                                     in_sem.at[c])

    def out_copy(row_tile, r, c):
        rows, cols = pl.ds(r * ROW_TILE, ROW_TILE), pl.ds(c * COL_TILE, COL_TILE)
        dst = o_hbm.at[pl.ds(row_tile * FFN_ROW_TILE + r * ROW_TILE, ROW_TILE), cols]
        return pltpu.make_async_copy(acc_ref.at[rows, cols], dst,
                                     out_sem.at[r * (D_MODEL // COL_TILE) + c])

    @pl.when(j == 0)
    def _():
        @pl.when(i > 0)
        def _():
            for r, c in pieces:
                out_copy(i - 1, r, c).wait()

        n_chunks = FFN_ROW_TILE // FFN_CHUNK
        for c in range(n_chunks):
            x_copy(c).start()
        for c in range(n_chunks):
            x_copy(c).wait()
            rows = slice(c * FFN_CHUNK, (c + 1) * FFN_CHUNK)
            h_ref[rows, :] = _rms_rows(acc_ref[rows, :], g_ref[...]).astype(BF16)

    col = lax.broadcasted_iota(jnp.int32, (1, FF_TILE), 1)
    repeated = j * FF_TILE - jnp.minimum(j * FF_TILE, D_FF - FF_TILE)

    def tile(write_back):
        for r in range(FFN_ROW_TILE // ROW_TILE):
            rows = slice(r * ROW_TILE, (r + 1) * ROW_TILE)
            h = h_ref[rows, :]
            gate = jnp.dot(h, wg_ref[...].astype(BF16), preferred_element_type=F32)
            up = jnp.dot(h, wu_ref[...].astype(BF16), preferred_element_type=F32)
            act = gate * (1.0 / (1.0 + jnp.exp(-gate))) * (0.5 * up)
            act = jnp.where(col >= repeated, act, 0.0).astype(BF16)
            for c in range(D_MODEL // COL_TILE):
                sl = slice(c * COL_TILE, (c + 1) * COL_TILE)
                acc_ref[rows, sl] += jnp.dot(act, wo_ref[:, sl].astype(BF16),
                                             preferred_element_type=F32)
                if write_back:
                    out_copy(i, r, c).start()

    @pl.when(j < last_j)
    def _():
        tile(False)

    @pl.when(j == last_j)
    def _():
        tile(True)

        @pl.when(i == last_i)
        def _():
            for r, c in pieces:
                out_copy(i, r, c).wait()


def _ffn(x, gain, w_in, w_out):
    m = x.shape[0]
    n_ff = pl.cdiv(D_FF, FF_TILE)

    def ff_start(j):
        return pl.multiple_of(jnp.minimum(j * FF_TILE, D_FF - FF_TILE), LANES)

    return pl.pallas_call(
        _ffn_kernel,
        grid=(m // FFN_ROW_TILE, n_ff),
        in_specs=[
            pl.BlockSpec(memory_space=pl.ANY),
            pl.BlockSpec((1, D_MODEL), lambda i, j: (0, 0)),
            pl.BlockSpec((pl.Element(D_MODEL), pl.Element(FF_TILE)),
                         lambda i, j: (0, ff_start(j))),
            pl.BlockSpec((pl.Element(D_MODEL), pl.Element(FF_TILE)),
                         lambda i, j: (0, pl.multiple_of(D_FF + ff_start(j), LANES))),
            pl.BlockSpec((pl.Element(FF_TILE), pl.Element(D_MODEL)),
                         lambda i, j: (ff_start(j), 0)),
        ],
        out_specs=pl.BlockSpec(memory_space=pl.ANY),
        out_shape=jax.ShapeDtypeStruct((m, D_MODEL), F32),
        scratch_shapes=[
            pltpu.VMEM((FFN_ROW_TILE, D_MODEL), F32),
            pltpu.VMEM((FFN_ROW_TILE, D_MODEL), BF16),
            pltpu.SemaphoreType.DMA((FFN_ROW_TILE // FFN_CHUNK,)),
            pltpu.SemaphoreType.DMA(((FFN_ROW_TILE // ROW_TILE) * (D_MODEL // COL_TILE),)),
        ],
        compiler_params=pltpu.CompilerParams(
            dimension_semantics=("arbitrary", "arbitrary"), vmem_limit_bytes=FFN_VMEM_LIMIT),
        name="ffn",
    )(x, gain.reshape(1, D_MODEL), w_in, w_in, w_out)


HEADS_PER_TILE = COL_TILE // HEAD_DIM
N_PROJ_TILES = D_QKV // COL_TILE
TILES_PER_GROUP = D_A // COL_TILE
A_TILES = 3 * TILES_PER_GROUP
AUG = 2 * HEAD_DIM
LOG2E = math.log2(math.e)


def _log_gate_scan(z, carry):
    c = (jnp.minimum(z, 0.0) - jnp.log1p(jnp.exp(-jnp.abs(z)))) * LOG2E
    row = lax.broadcasted_iota(jnp.int32, c.shape, 0)
    shift = 1
    while shift < c.shape[0]:
        c = c + jnp.where(row >= shift, pltpu.roll(c, shift, axis=0), 0.0)
        shift *= 2
    return c + carry


def _split3(c):
    hi = c.astype(BF16)
    rest = c - hi.astype(F32)
    mid = rest.astype(BF16)
    lo = (rest - mid.astype(F32)).astype(BF16)
    return hi, mid, lo


def _head_mean_matrix():
    head = np.arange(COL_TILE) // HEAD_DIM
    return jnp.asarray((head[:, None] == head[None, :]) / HEAD_DIM, BF16)


def _gate_placement_matrices():
    mats = np.zeros((2, TILES_PER_GROUP, COL_TILE, COL_TILE), np.float32)
    ones_row = 3 * LANES
    for part in range(TILES_PER_GROUP):
        for hh in range(HEADS_PER_TILE):
            head = part * HEADS_PER_TILE + hh
            col = hh * HEAD_DIM
            for term in range(3):
                mats[0, part, term * LANES + head, col + term] = 1.0
                mats[0, part, ones_row, col + 3 + term] = 1.0
                mats[1, part, term * LANES + head, col + 3 + term] = -1.0
                mats[1, part, ones_row, col + term] = 1.0
    return jnp.asarray(mats.reshape(2 * TILES_PER_GROUP, COL_TILE, COL_TILE), BF16)


def _proj_kernel(x_ref, g_ref, w_ref, wf_ref, fb_ref, hg_ref, mean_ref, place_ref,
                 a1_ref, a4_ref, a16_ref, qb_ref, kb_ref, vt_ref,
                 h_ref, cs_ref, carry_ref, y_ref, yf_ref, *, tiles_per_seq):
    i = pl.program_id(0)
    j = pl.program_id(1)

    @pl.when(j == 0)
    def _():
        h_ref[...] = _rms_rows(x_ref[...], g_ref[...]).astype(BF16)

        @pl.when(i % tiles_per_seq == 0)
        def _():
            carry_ref[...] = jnp.zeros(carry_ref.shape, F32)

        z = jnp.dot(h_ref[...], wf_ref[...], preferred_element_type=F32) + fb_ref[...]
        c = _log_gate_scan(z, carry_ref[...])
        carry_ref[...] = c[ROW_TILE - 1:ROW_TILE, :]
        for term, part in enumerate(_split3(c)):
            cs_ref[:, term * LANES:(term + 1) * LANES] = part
        cs_ref[:, 3 * LANES:] = jnp.ones((ROW_TILE, LANES), BF16)

    def product():
        return lax.dot_general(h_ref[...], w_ref[...].astype(BF16), (((1,), (1,)), ((), ())),
                               preferred_element_type=F32)

    def normed():
        r = product()
        ms = jnp.dot((r * r).astype(BF16), mean_ref[...], preferred_element_type=F32)
        return r * lax.rsqrt(ms + RMS_EPS) * hg_ref[...]

    def store_dilated(y):
        a1_ref[0, 0] = y.astype(BF16)
        fine, coarse = a4_ref.shape[1], a16_ref.shape[1]
        step = coarse // fine
        for c in range(HEADS_PER_TILE):
            sl = slice(c * LANES, (c + 1) * LANES)
            y_ref[c] = y[:, sl]
            for rf in range(fine):
                rows = y_ref[c, pl.ds(rf, ROW_TILE // fine, stride=fine), :]
                a4_ref[0, rf, :, sl] = rows.astype(BF16)
                yf_ref[c, rf] = rows
                for q in range(step):
                    rows = yf_ref[c, rf, pl.ds(q, ROW_TILE // coarse, stride=step), :]
                    a16_ref[0, rf + fine * q, :, sl] = rows.astype(BF16)

    for t in range(N_PROJ_TILES):
        group, part = divmod(t, TILES_PER_GROUP)

        @pl.when(j == t)
        def _(group=group):
            if group in (0, 1):
                store_dilated(normed())
            elif group == 2:
                store_dilated(product())
            elif group in (3, 4):
                out = qb_ref if group == 3 else kb_ref
                y = normed().astype(BF16)
                aug = jnp.dot(cs_ref[...], place_ref[0],
                              preferred_element_type=F32).astype(BF16)
                for hh in range(HEADS_PER_TILE):
                    sl = slice(hh * HEAD_DIM, (hh + 1) * HEAD_DIM)
                    out[0, hh, :, :HEAD_DIM] = y[:, sl]
                    out[0, hh, :, HEAD_DIM:] = aug[:, sl]
            else:
                r = product()
                for hh in range(HEADS_PER_TILE):
                    vt_ref[0, hh] = r[:, hh * HEAD_DIM:(hh + 1) * HEAD_DIM].T.astype(BF16)


def _proj(x, b, s, gain, w_in, layer, w_f, f_bias, head_gain):
    m = x.shape[0]
    tps = s // ROW_TILE

    def a_spec(dilation):
        return pl.BlockSpec((1, dilation, ROW_TILE // dilation, COL_TILE),
                            lambda i, j: (i // tps, 0, i % tps, jnp.minimum(j, A_TILES - 1)))

    def head_block(first):
        return lambda i, j: (i // tps, jnp.clip(j - first, 0, TILES_PER_GROUP - 1), i % tps, 0)

    qb_first = A_TILES
    vt_first = A_TILES + 2 * TILES_PER_GROUP
    dilations = [d for _, d in DILATED_PATTERNS]
    return pl.pallas_call(
        functools.partial(_proj_kernel, tiles_per_seq=tps),
        grid=(m // ROW_TILE, N_PROJ_TILES),
        in_specs=[
            pl.BlockSpec((ROW_TILE, D_MODEL), lambda i, j: (i, 0)),
            pl.BlockSpec((1, D_MODEL), lambda i, j: (0, 0)),
            pl.BlockSpec((None, COL_TILE, D_MODEL), lambda i, j: (layer, j, 0)),
            pl.BlockSpec((D_MODEL, LANES), lambda i, j: (0, 0)),
            pl.BlockSpec((1, LANES), lambda i, j: (0, 0)),
            pl.BlockSpec((1, COL_TILE), lambda i, j: (0, j)),
            pl.BlockSpec((COL_TILE, COL_TILE), lambda i, j: (0, 0)),
            pl.BlockSpec((1, COL_TILE, COL_TILE),
                         lambda i, j: (jnp.clip(j - qb_first, 0, 2 * TILES_PER_GROUP - 1), 0, 0)),
        ],
        out_specs=[a_spec(d) for d in dilations] + [
            pl.BlockSpec((1, HEADS_PER_TILE, ROW_TILE, AUG), head_block(qb_first)),
            pl.BlockSpec((1, HEADS_PER_TILE, ROW_TILE, AUG),
                         head_block(qb_first + TILES_PER_GROUP)),
            pl.BlockSpec((1, HEADS_PER_TILE, HEAD_DIM, ROW_TILE),
                         lambda i, j: (i // tps, jnp.clip(j - vt_first, 0, TILES_PER_GROUP - 1),
                                       0, i % tps)),
        ],
        out_shape=[jax.ShapeDtypeStruct((b, d, s // d, 3 * D_A), BF16) for d in dilations] + [
            jax.ShapeDtypeStruct((b, H_B, s, AUG), BF16),
            jax.ShapeDtypeStruct((b, H_B, s, AUG), BF16),
            jax.ShapeDtypeStruct((b, H_B, HEAD_DIM, s), BF16),
        ],
        scratch_shapes=[
            pltpu.VMEM((ROW_TILE, D_MODEL), BF16),
            pltpu.VMEM((ROW_TILE, 4 * LANES), BF16),
            pltpu.VMEM((1, LANES), F32),
            pltpu.VMEM((HEADS_PER_TILE, ROW_TILE, LANES), F32),
            pltpu.VMEM((HEADS_PER_TILE, dilations[1], ROW_TILE // dilations[1], LANES), F32),
        ],
        compiler_params=_params("arbitrary", "arbitrary"),
        name="proj",
    )(x, gain.reshape(1, D_MODEL), jnp.swapaxes(w_in, 1, 2), w_f, f_bias, head_gain,
      _head_mean_matrix(), _gate_placement_matrices())


def _bucket_steps(dilation):
    dist = np.arange(BAND + 1) * dilation
    max_exact = NUM_BUCKETS // 2
    large = max_exact + np.floor(
        np.log(np.maximum(dist, 1) / max_exact) / math.log(MAX_DISTANCE / max_exact)
        * (NUM_BUCKETS - max_exact)).astype(np.int64)
    bucket = np.where(dist < max_exact, dist, np.minimum(large, NUM_BUCKETS - 1))
    steps = [(0, int(bucket[0]))]
    for delta in range(1, BAND + 1):
        if bucket[delta] != bucket[delta - 1]:
            steps.append((delta, int(bucket[delta])))
    return steps


def _bias_kernel(table_ref, o_ref):
    iq = lax.broadcasted_iota(jnp.int32, (BAND, 2 * BAND), 0)
    ik = lax.broadcasted_iota(jnp.int32, (BAND, 2 * BAND), 1)
    delta = iq + BAND - ik
    in_band = jnp.logical_and(delta >= 0, delta <= BAND)
    for p, (_, dilation) in enumerate(DILATED_PATTERNS):
        steps = _bucket_steps(dilation)
        for h in range(H_A):
            val = jnp.full((BAND, 2 * BAND), table_ref[steps[0][1], h], F32)
            for start, bucket in steps[1:]:
                val = jnp.where(delta >= start, table_ref[bucket, h], val)
            o_ref[p, h] = jnp.where(in_band, val, NEG_INF)


def _band_bias(rel_table):
    n_pat = len(DILATED_PATTERNS)
    return pl.pallas_call(
        _bias_kernel,
        in_specs=[pl.BlockSpec(memory_space=pltpu.SMEM)],
        out_specs=pl.BlockSpec(memory_space=pltpu.VMEM),
        out_shape=jax.ShapeDtypeStruct((n_pat, H_A, BAND, 2 * BAND), F32),
        name="band_bias",
    )(rel_table)


DIL_UNITS = 8


def _dilated_kernel(q_ref, kp_ref, kc_ref, vp_ref, vc_ref, bias_ref, o_ref, st_ref):
    n_cls = q_ref.shape[1]
    n_blk = q_ref.shape[2] // BAND
    first = pl.program_id(2) == 0
    key_lane = lax.broadcasted_iota(jnp.int32, (1, 2 * BAND), 1)
    no_prev = jnp.where(jnp.logical_and(first, key_lane < BAND), NEG_INF, 0.0)
    stat_lane = lax.broadcasted_iota(jnp.int32, (BAND, LANES), 1)
    units = [(c, blk) for c in range(n_cls) for blk in range(n_blk)]

    def keys(prev_ref, cur_ref, c, blk, sl):
        if blk == 0:
            return jnp.concatenate([prev_ref[0, c, :, sl], cur_ref[0, c, :BAND, sl]], axis=0)
        return cur_ref[0, c, (blk - 1) * BAND:(blk + 1) * BAND, sl]

    scores = {}
    for c, blk in units:
        for h in range(H_A):
            sl = slice(h * HEAD_DIM, (h + 1) * HEAD_DIM)
            q = q_ref[0, c, blk * BAND:(blk + 1) * BAND, sl]
            scores[c, blk, h] = lax.dot_general(q, keys(kp_ref, kc_ref, c, blk, sl),
                                                (((1,), (1,)), ((), ())),
                                                preferred_element_type=F32)
    for c, blk in units:
        rows = slice(blk * BAND, (blk + 1) * BAND)
        stats = jnp.zeros((BAND, LANES), F32)
        for h in range(H_A):
            sl = slice(h * HEAD_DIM, (h + 1) * HEAD_DIM)
            s = scores[c, blk, h] + bias_ref[0, h]
            if blk == 0:
                s = s + no_prev
            mx = jnp.max(s, axis=-1, keepdims=True)
            p = jnp.exp(s - mx)
            den = jnp.sum(p, axis=-1, keepdims=True)
            o_ref[0, c, rows, sl] = jnp.dot(p.astype(BF16), keys(vp_ref, vc_ref, c, blk, sl),
                                            preferred_element_type=F32).astype(o_ref.dtype)
            stats = jnp.where(stat_lane == h, mx, stats)
            stats = jnp.where(stat_lane == H_A + h, den, stats)
        st_ref[0, c, rows, :] = stats


def _dilated(qkv, bias, pattern):
    b, dilation, n_sub, _ = qkv.shape
    n_blk = min(DIL_UNITS, n_sub // BAND)
    n_cls = DIL_UNITS // n_blk
    rows = n_blk * BAND
    nb = n_sub // rows

    def cur(which):
        return pl.BlockSpec((1, n_cls, rows, D_A), lambda bi, r, i: (bi, r, i, which))

    def prev(which):
        return pl.BlockSpec((1, n_cls, BAND, D_A),
                            lambda bi, r, i: (bi, r, jnp.maximum(i * n_blk - 1, 0), which))

    return pl.pallas_call(
        _dilated_kernel,
        grid=(b, dilation // n_cls, nb),
        in_specs=[
            cur(0), prev(1), cur(1), prev(2), cur(2),
            pl.BlockSpec((1, H_A, BAND, 2 * BAND), lambda bi, r, i: (pattern, 0, 0, 0)),
        ],
        out_specs=[
            pl.BlockSpec((1, n_cls, rows, D_A), lambda bi, r, i: (bi, r, i, 0)),
            pl.BlockSpec((1, n_cls, rows, LANES), lambda bi, r, i: (bi, r, i, 0)),
        ],
        out_shape=[
            jax.ShapeDtypeStruct((b, dilation, n_sub, D_A), BF16),
            jax.ShapeDtypeStruct((b, dilation, n_sub, LANES), F32),
        ],
        compiler_params=_params("parallel", "parallel", "arbitrary"),
        name=f"dilated_{dilation}",
    )(qkv, qkv, qkv, qkv, qkv, bias)


COMBINE_ROWS = 512


def _combine_kernel(*refs):
    n_pat = len(DILATED_PATTERNS)
    num_refs, st_refs = refs[:n_pat], refs[n_pat:2 * n_pat]
    o_ref, num_buf, st_buf, num_tmp, st_tmp = refs[2 * n_pat:]
    dilations = [ref.shape[1] for ref in num_refs]
    fine = min(d for d in dilations if d > 1)
    heads = [slice(h * HEAD_DIM, (h + 1) * HEAD_DIM) for h in range(H_A)]
    for p, dilation in enumerate(dilations):
        if dilation == 1:
            continue
        step = dilation // fine
        for rf in range(fine):
            rows = pl.ds(rf, COMBINE_ROWS // fine, stride=fine)
            if step == 1:
                st_buf[p, rows, :] = st_refs[p][0, rf]
                for h, sl in enumerate(heads):
                    num_buf[p, h, rows, :] = num_refs[p][0, rf, :, sl].astype(F32)
                continue
            for q in range(step):
                part = pl.ds(q, COMBINE_ROWS // dilation, stride=step)
                st_tmp[part, :] = st_refs[p][0, rf + fine * q]
                for h, sl in enumerate(heads):
                    num_tmp[h, part, :] = num_refs[p][0, rf + fine * q, :, sl].astype(F32)
            st_buf[p, rows, :] = st_tmp[...]
            for h in range(H_A):
                num_buf[p, h, rows, :] = num_tmp[h]
    stats = [st_refs[p][0, 0] if d == 1 else st_buf[p] for p, d in enumerate(dilations)]
    for h in range(H_A):
        sl = slice(h * HEAD_DIM, (h + 1) * HEAD_DIM)
        mxs = [st[:, h:h + 1] for st in stats]
        dens = [st[:, H_A + h:H_A + h + 1] for st in stats]
        mx = functools.reduce(jnp.maximum, mxs)
        num = None
        den = None
        for p, (pden, pmx) in enumerate(zip(dens, mxs)):
            scale = jnp.exp(pmx - mx)
            if dilations[p] == 1:
                pnum = num_refs[p][0, 0, :, sl].astype(F32)
            else:
                pnum = num_buf[p, h]
            num = pnum * scale if num is None else num + pnum * scale
            den = pden * scale if den is None else den + pden * scale
        o_ref[:, sl] = (num / den).astype(BF16)


def _combine(nums, stats):
    b, _, s, _ = nums[0].shape
    tps = s // COMBINE_ROWS

    def spec(arr):
        dilation, width = arr.shape[1], arr.shape[3]
        return pl.BlockSpec((1, dilation, COMBINE_ROWS // dilation, width),
                            lambda i: (i // tps, 0, i % tps, 0))

    n_pat = len(nums)
    fine = min(a.shape[1] for a in nums if a.shape[1] > 1)
    return pl.pallas_call(
        _combine_kernel,
        grid=(b * tps,),
        in_specs=[spec(a) for a in nums] + [spec(a) for a in stats],
        out_specs=pl.BlockSpec((COMBINE_ROWS, D_A), lambda i: (i, 0)),
        out_shape=jax.ShapeDtypeStruct((b * s, D_A), BF16),
        scratch_shapes=[
            pltpu.VMEM((n_pat, H_A, COMBINE_ROWS, HEAD_DIM), F32),
            pltpu.VMEM((n_pat, COMBINE_ROWS, LANES), F32),
            pltpu.VMEM((H_A, COMBINE_ROWS // fine, HEAD_DIM), F32),
            pltpu.VMEM((COMBINE_ROWS // fine, LANES), F32),
        ],
        compiler_params=_params("parallel"),
        name="combine_a",
    )(*nums, *stats)


def _fox_kernel(q_ref, k_ref, vt_ref, o_ref, m_ref, l_ref, acc_ref, sa_ref, sb_ref):
    iq = pl.program_id(2)
    m_ref[...] = jnp.full(m_ref.shape, NEG_INF, F32)
    l_ref[...] = jnp.zeros(l_ref.shape, F32)
    acc_ref[...] = jnp.zeros(acc_ref.shape, F32)

    def scores_into(s_ref, kb):
        start = pl.multiple_of(kb * FOX_TK, FOX_TK)
        for hh in range(FOX_HEADS):
            k = k_ref[0, hh, pl.ds(start, FOX_TK), :]
            s_ref[hh] = lax.dot_general(k, q_ref[0, hh], (((1,), (1,)), ((), ())),
                                        preferred_element_type=F32)

    def accumulate(s_ref, kb, diagonal):
        start = pl.multiple_of(kb * FOX_TK, FOX_TK)
        for hh in range(FOX_HEADS):
            st = s_ref[hh]
            if diagonal:
                key = lax.broadcasted_iota(jnp.int32, st.shape, 0)
                qry = lax.broadcasted_iota(jnp.int32, st.shape, 1)
                st = jnp.where(key <= qry, st, NEG_INF)
            m_prev = m_ref[hh]
            m_new = jnp.maximum(m_prev, jnp.max(st, axis=0, keepdims=True))
            alpha = jnp.exp2(m_prev - m_new)
            p = jnp.exp2(st - m_new)
            l_ref[hh] = alpha * l_ref[hh] + jnp.sum(p, axis=0, keepdims=True)
            vt = vt_ref[0, hh, :, pl.ds(start, FOX_TK)]
            acc_ref[hh] = alpha * acc_ref[hh] + jnp.dot(vt, p.astype(BF16),
                                                        preferred_element_type=F32)
            m_ref[hh] = m_new

    scores_into(sa_ref, 0)

    def pair(p, carry):
        scores_into(sb_ref, 2 * p + 1)
        accumulate(sa_ref, 2 * p, False)
        scores_into(sa_ref, 2 * p + 2)
        accumulate(sb_ref, 2 * p + 1, False)
        return carry

    lax.fori_loop(0, iq // 2, pair, 0)

    @pl.when(iq % 2 == 0)
    def _():
        accumulate(sa_ref, iq, True)

    @pl.when(iq % 2 == 1)
    def _():
        scores_into(sb_ref, iq)
        accumulate(sa_ref, iq - 1, False)
        accumulate(sb_ref, iq, True)
    for hh in range(FOX_HEADS):
        o_ref[0, :, hh * HEAD_DIM:(hh + 1) * HEAD_DIM] = (
            acc_ref[hh] / l_ref[hh]).T.astype(BF16)


def _fox(q_aug, k_aug, v_t):
    b, h_b, s, _ = q_aug.shape
    assert FOX_TQ == FOX_TK
    return pl.pallas_call(
        _fox_kernel,
        grid=(b, h_b // FOX_HEADS, s // FOX_TQ),
        in_specs=[
            pl.BlockSpec((1, FOX_HEADS, FOX_TQ, AUG), lambda bi, h, i: (bi, h, i, 0)),
            pl.BlockSpec((1, FOX_HEADS, s, AUG), lambda bi, h, i: (bi, h, 0, 0)),
            pl.BlockSpec((1, FOX_HEADS, HEAD_DIM, s), lambda bi, h, i: (bi, h, 0, 0)),
        ],
        out_specs=pl.BlockSpec((1, FOX_TQ, FOX_HEADS * HEAD_DIM), lambda bi, h, i: (bi, i, h)),
        out_shape=jax.ShapeDtypeStruct((b, s, h_b * HEAD_DIM), BF16),
        scratch_shapes=[
            pltpu.VMEM((FOX_HEADS, 1, FOX_TQ), F32),
            pltpu.VMEM((FOX_HEADS, 1, FOX_TQ), F32),
            pltpu.VMEM((FOX_HEADS, HEAD_DIM, FOX_TQ), F32),
            pltpu.VMEM((FOX_HEADS, FOX_TK, FOX_TQ), F32),
            pltpu.VMEM((FOX_HEADS, FOX_TK, FOX_TQ), F32),
        ],
        compiler_params=_params("parallel", "parallel", "arbitrary"),
        name="fox",
    )(q_aug, k_aug, v_t)


def _out_proj_kernel(a_ref, b_ref, w_ref, x_ref, o_ref):
    mixed = jnp.concatenate([a_ref[...], b_ref[...]], axis=-1)
    o_ref[...] = x_ref[...] + jnp.dot(mixed, w_ref[...].astype(BF16),
                                      preferred_element_type=F32)


OUT_ROW_TILE = 2048


def _out_proj(out_a, out_b, w_out, x):
    m = x.shape[0]
    return pl.pallas_call(
        _out_proj_kernel,
        grid=(m // OUT_ROW_TILE, D_MODEL // COL_TILE),
        in_specs=[
            pl.BlockSpec((OUT_ROW_TILE, D_A), lambda i, j: (i, 0)),
            pl.BlockSpec((OUT_ROW_TILE, D_B), lambda i, j: (i, 0)),
            pl.BlockSpec((D_A + D_B, COL_TILE), lambda i, j: (0, j)),
            pl.BlockSpec((OUT_ROW_TILE, COL_TILE), lambda i, j: (i, j)),
        ],
        out_specs=pl.BlockSpec((OUT_ROW_TILE, COL_TILE), lambda i, j: (i, j)),
        out_shape=jax.ShapeDtypeStruct((m, D_MODEL), F32),
        compiler_params=_params("parallel", "parallel"),
        name="out_proj",
    )(out_a, out_b, w_out, x)


def _mixer(x, b, s, mix_norm, w_in, layer, q_norm_a, k_norm_a, q_norm_b, k_norm_b, forget_bias,
           rel_bias_table, w_out):
    ones = jnp.ones((D_A,), F32)
    head_gain = jnp.concatenate([
        jnp.tile(q_norm_a * ATTN_SCALE, H_A), jnp.tile(k_norm_a, H_A), ones,
        jnp.tile(q_norm_b * (ATTN_SCALE * LOG2E), H_B), jnp.tile(k_norm_b, H_B), ones,
    ]).reshape(1, D_QKV)
    w_f = jnp.pad(w_in[layer, :, D_QKV:], ((0, 0), (0, LANES - H_B))).astype(BF16)
    f_bias = jnp.pad(forget_bias, (0, LANES - H_B)).reshape(1, LANES)
    *qkv_a, q_aug, k_aug, v_t = _proj(x, b, s, mix_norm, w_in, layer, w_f, f_bias, head_gain)

    out_b = _fox(q_aug, k_aug, v_t).reshape(b * s, D_B)

    bias = _band_bias(rel_bias_table)
    parts = [_dilated(qkv, bias, p) for p, qkv in enumerate(qkv_a)]
    out_a = _combine([pt[0] for pt in parts], [pt[1] for pt in parts])

    return _out_proj(out_a, out_b, w_out, x)


def kernel(x, ffn1_norm, ffn1_w_in, ffn1_w_out, mix_norm, w_in, q_norm_a, k_norm_a, q_norm_b,
           k_norm_b, forget_bias, rel_bias_table, w_out, ffn2_norm, ffn2_w_in, ffn2_w_out):
    b, s, d = x.shape
    depth = ffn1_norm.shape[0]
    x = x.reshape(b * s, d)
    for l in range(depth):
        x = _ffn(x, ffn1_norm[l], ffn1_w_in[l], ffn1_w_out[l])
        x = _mixer(x, b, s, mix_norm[l], w_in, l, q_norm_a[l], k_norm_a[l], q_norm_b[l],
                   k_norm_b[l], forget_bias[l], rel_bias_table, w_out[l])
        x = _ffn(x, ffn2_norm[l], ffn2_w_in[l], ffn2_w_out[l])
    return x.reshape(b, s, d)
```

```python
import functools
import math

import numpy as np
import jax
import jax.numpy as jnp
from jax import lax
from jax.experimental import pallas as pl
from jax.experimental.pallas import tpu as pltpu

D_MODEL = 2048
HEAD_DIM = 128
N_HEADS = D_MODEL // HEAD_DIM
H_A = N_HEADS // 2
H_B = N_HEADS - H_A
D_A = H_A * HEAD_DIM
D_B = H_B * HEAD_DIM
D_QKV = 3 * D_A + 3 * D_B
DILATED_PATTERNS = ((128, 1), (512, 4), (2048, 16))
BAND = 128
NUM_BUCKETS = 32
MAX_DISTANCE = 2048
D_FF = ((8 * D_MODEL // 3 + 127) // 128) * 128
RMS_EPS = 1e-6
NEG_INF = -1e30
ATTN_SCALE = HEAD_DIM ** -0.5

LANES = 128
FF_TILE = 512
FFN_ROW_TILE = 2048
FFN_CHUNK = 512
FFN_VMEM_LIMIT = 60 * 1024 * 1024
ROW_TILE = 1024
COL_TILE = 512
FOX_TQ = 512
FOX_TK = 512
FOX_HEADS = 4
VMEM_LIMIT = 56 * 1024 * 1024

F32 = jnp.float32
BF16 = jnp.bfloat16


def _params(*sem):
    return pltpu.CompilerParams(dimension_semantics=sem, vmem_limit_bytes=VMEM_LIMIT)


def _rms_rows(x, gain):
    ms = jnp.mean(x * x, axis=-1, keepdims=True)
    return x * lax.rsqrt(ms + RMS_EPS) * gain


def _ffn_kernel(x_hbm, g_ref, wg_ref, wu_ref, wo_ref, o_hbm, acc_ref, h_ref, in_sem, out_sem):
    i = pl.program_id(0)
    j = pl.program_id(1)
    last_i = pl.num_programs(0) - 1
    last_j = pl.num_programs(1) - 1
    pieces = [(r, c) for r in range(FFN_ROW_TILE // ROW_TILE) for c in range(D_MODEL // COL_TILE)]

    def x_copy(c):
        src = x_hbm.at[pl.ds(i * FFN_ROW_TILE + c * FFN_CHUNK, FFN_CHUNK), :]
        return pltpu.make_async_copy(src, acc_ref.at[pl.ds(c * FFN_CHUNK, FFN_CHUNK), :],
                                     in_sem.at[c])

    def out_copy(row_tile, r, c):
        rows, cols = pl.ds(r * ROW_TILE, ROW_TILE), pl.ds(c * COL_TILE, COL_TILE)
        dst = o_hbm.at[pl.ds(row_tile * FFN_ROW_TILE + r * ROW_TILE, ROW_TILE), cols]
        return pltpu.make_async_copy(acc_ref.at[rows, cols], dst,
                                     out_sem.at[r * (D_MODEL // COL_TILE) + c])

    @pl.when(j == 0)
    def _():
        @pl.when(i > 0)
        def _():
            for r, c in pieces:
                out_copy(i - 1, r, c).wait()

        n_chunks = FFN_ROW_TILE // FFN_CHUNK
        for c in range(n_chunks):
            x_copy(c).start()
        for c in range(n_chunks):
            x_copy(c).wait()
            rows = slice(c * FFN_CHUNK, (c + 1) * FFN_CHUNK)
            h_ref[rows, :] = _rms_rows(acc_ref[rows, :], g_ref[...]).astype(BF16)

    col = lax.broadcasted_iota(jnp.int32, (1, FF_TILE), 1)
    repeated = j * FF_TILE - jnp.minimum(j * FF_TILE, D_FF - FF_TILE)

    def tile(write_back):
        for r in range(FFN_ROW_TILE // ROW_TILE):
            rows = slice(r * ROW_TILE, (r + 1) * ROW_TILE)
            h = h_ref[rows, :]
            gate = jnp.dot(h, wg_ref[...].astype(BF16), preferred_element_type=F32)
            up = jnp.dot(h, wu_ref[...].astype(BF16), preferred_element_type=F32)
            act = gate * (1.0 / (1.0 + jnp.exp(-gate))) * (0.5 * up)
            act = jnp.where(col >= repeated, act, 0.0).astype(BF16)
            for c in range(D_MODEL // COL_TILE):
                sl = slice(c * COL_TILE, (c + 1) * COL_TILE)
                acc_ref[rows, sl] += jnp.dot(act, wo_ref[:, sl].astype(BF16),
                                             preferred_element_type=F32)
                if write_back:
                    out_copy(i, r, c).start()

    @pl.when(j < last_j)
    def _():
        tile(False)

    @pl.when(j == last_j)
    def _():
        tile(True)

        @pl.when(i == last_i)
        def _():
            for r, c in pieces:
                out_copy(i, r, c).wait()


def _ffn(x, gain, w_in, w_out):
    m = x.shape[0]
    n_ff = pl.cdiv(D_FF, FF_TILE)

    def ff_start(j):
        return pl.multiple_of(jnp.minimum(j * FF_TILE, D_FF - FF_TILE), LANES)

    return pl.pallas_call(
        _ffn_kernel,
        grid=(m // FFN_ROW_TILE, n_ff),
        in_specs=[
            pl.BlockSpec(memory_space=pl.ANY),
            pl.BlockSpec((1, D_MODEL), lambda i, j: (0, 0)),
            pl.BlockSpec((pl.Element(D_MODEL), pl.Element(FF_TILE)),
                         lambda i, j: (0, ff_start(j))),
            pl.BlockSpec((pl.Element(D_MODEL), pl.Element(FF_TILE)),
                         lambda i, j: (0, pl.multiple_of(D_FF + ff_start(j), LANES))),
            pl.BlockSpec((pl.Element(FF_TILE), pl.Element(D_MODEL)),
                         lambda i, j: (ff_start(j), 0)),
        ],
        out_specs=pl.BlockSpec(memory_space=pl.ANY),
        out_shape=jax.ShapeDtypeStruct((m, D_MODEL), F32),
        scratch_shapes=[
            pltpu.VMEM((FFN_ROW_TILE, D_MODEL), F32),
            pltpu.VMEM((FFN_ROW_TILE, D_MODEL), BF16),
            pltpu.SemaphoreType.DMA((FFN_ROW_TILE // FFN_CHUNK,)),
            pltpu.SemaphoreType.DMA(((FFN_ROW_TILE // ROW_TILE) * (D_MODEL // COL_TILE),)),
        ],
        compiler_params=pltpu.CompilerParams(
            dimension_semantics=("arbitrary", "arbitrary"), vmem_limit_bytes=FFN_VMEM_LIMIT),
        name="ffn",
    )(x, gain.reshape(1, D_MODEL), w_in, w_in, w_out)


HEADS_PER_TILE = COL_TILE // HEAD_DIM
N_PROJ_TILES = D_QKV // COL_TILE
TILES_PER_GROUP = D_A // COL_TILE
A_TILES = 3 * TILES_PER_GROUP
AUG = 2 * HEAD_DIM
LOG2E = math.log2(math.e)


def _log_gate_scan(z, carry):
    c = (jnp.minimum(z, 0.0) - jnp.log1p(jnp.exp(-jnp.abs(z)))) * LOG2E
    row = lax.broadcasted_iota(jnp.int32, c.shape, 0)
    shift = 1
    while shift < c.shape[0]:
        c = c + jnp.where(row >= shift, pltpu.roll(c, shift, axis=0), 0.0)
        shift *= 2
    return c + carry


def _split3(c):
    hi = c.astype(BF16)
    rest = c - hi.astype(F32)
    mid = rest.astype(BF16)
    lo = (rest - mid.astype(F32)).astype(BF16)
    return hi, mid, lo


def _head_mean_matrix():
    head = np.arange(COL_TILE) // HEAD_DIM
    return jnp.asarray((head[:, None] == head[None, :]) / HEAD_DIM, BF16)


def _gate_placement_matrices():
    mats = np.zeros((2, TILES_PER_GROUP, COL_TILE, COL_TILE), np.float32)
    ones_row = 3 * LANES
    for part in range(TILES_PER_GROUP):
        for hh in range(HEADS_PER_TILE):
            head = part * HEADS_PER_TILE + hh
            col = hh * HEAD_DIM
            for term in range(3):
                mats[0, part, term * LANES + head, col + term] = 1.0
                mats[0, part, ones_row, col + 3 + term] = 1.0
                mats[1, part, term * LANES + head, col + 3 + term] = -1.0
                mats[1, part, ones_row, col + term] = 1.0
    return jnp.asarray(mats.reshape(2 * TILES_PER_GROUP, COL_TILE, COL_TILE), BF16)


def _proj_kernel(x_ref, g_ref, w_ref, wf_ref, fb_ref, hg_ref, mean_ref, place_ref,
                 a1_ref, a4_ref, a16_ref, qb_ref, kb_ref, vt_ref,
                 h_ref, cs_ref, carry_ref, y_ref, yf_ref, *, tiles_per_seq):
    i = pl.program_id(0)
    j = pl.program_id(1)

    @pl.when(j == 0)
    def _():
        h_ref[...] = _rms_rows(x_ref[...], g_ref[...]).astype(BF16)

        @pl.when(i % tiles_per_seq == 0)
        def _():
            carry_ref[...] = jnp.zeros(carry_ref.shape, F32)

        z = jnp.dot(h_ref[...], wf_ref[...], preferred_element_type=F32) + fb_ref[...]
        c = _log_gate_scan(z, carry_ref[...])
        carry_ref[...] = c[ROW_TILE - 1:ROW_TILE, :]
        for term, part in enumerate(_split3(c)):
            cs_ref[:, term * LANES:(term + 1) * LANES] = part
        cs_ref[:, 3 * LANES:] = jnp.ones((ROW_TILE, LANES), BF16)

    def product(transposed=False):
        lhs, rhs = (w_ref[...].astype(BF16), h_ref[...])
        if not transposed:
            lhs, rhs = rhs, lhs
        return lax.dot_general(lhs, rhs, (((1,), (1,)), ((), ())), preferred_element_type=F32)

    def normed():
        r = product()
        ms = jnp.dot((r * r).astype(BF16), mean_ref[...], preferred_element_type=F32)
        return r * lax.rsqrt(ms + RMS_EPS) * hg_ref[...]

    def store_dilated(y):
        a1_ref[0, 0] = y.astype(BF16)
        fine, coarse = a4_ref.shape[1], a16_ref.shape[1]
        step = coarse // fine
        for c in range(HEADS_PER_TILE):
            sl = slice(c * LANES, (c + 1) * LANES)
            y_ref[c] = y[:, sl]
            for rf in range(fine):
                rows = y_ref[c, pl.ds(rf, ROW_TILE // fine, stride=fine), :]
                a4_ref[0, rf, :, sl] = rows.astype(BF16)
                yf_ref[c, rf] = rows
                for q in range(step):
                    rows = yf_ref[c, rf, pl.ds(q, ROW_TILE // coarse, stride=step), :]
                    a16_ref[0, rf + fine * q, :, sl] = rows.astype(BF16)

    def tiles(first_group, n_groups=1):
        lo = first_group * TILES_PER_GROUP
        return jnp.logical_and(j >= lo, j < lo + n_groups * TILES_PER_GROUP)

    def store_forgetting(out):
        y = normed().astype(BF16)
        aug = jnp.dot(cs_ref[...], place_ref[0], preferred_element_type=F32).astype(BF16)
        for hh in range(HEADS_PER_TILE):
            sl = slice(hh * HEAD_DIM, (hh + 1) * HEAD_DIM)
            out[0, hh, :, :HEAD_DIM] = y[:, sl]
            out[0, hh, :, HEAD_DIM:] = aug[:, sl]

    @pl.when(tiles(0, 2))
    def _():
        store_dilated(normed())

    @pl.when(tiles(2))
    def _():
        store_dilated(product())

    @pl.when(tiles(3))
    def _():
        store_forgetting(qb_ref)

    @pl.when(tiles(4))
    def _():
        store_forgetting(kb_ref)

    @pl.when(tiles(5))
    def _():
        v_t = product(transposed=True).astype(BF16)
        for hh in range(HEADS_PER_TILE):
            vt_ref[0, hh] = v_t[hh * HEAD_DIM:(hh + 1) * HEAD_DIM, :]


def _proj(x, b, s, gain, w_in, layer, w_f, f_bias, head_gain):
    m = x.shape[0]
    tps = s // ROW_TILE

    def a_spec(dilation):
        return pl.BlockSpec((1, dilation, ROW_TILE // dilation, COL_TILE),
                            lambda i, j: (i // tps, 0, i % tps, jnp.minimum(j, A_TILES - 1)))

    def head_block(first):
        return lambda i, j: (i // tps, jnp.clip(j - first, 0, TILES_PER_GROUP - 1), i % tps, 0)

    qb_first = A_TILES
    vt_first = A_TILES + 2 * TILES_PER_GROUP
    dilations = [d for _, d in DILATED_PATTERNS]
    return pl.pallas_call(
        functools.partial(_proj_kernel, tiles_per_seq=tps),
        grid=(m // ROW_TILE, N_PROJ_TILES),
        in_specs=[
            pl.BlockSpec((ROW_TILE, D_MODEL), lambda i, j: (i, 0)),
            pl.BlockSpec((1, D_MODEL), lambda i, j: (0, 0)),
            pl.BlockSpec((None, COL_TILE, D_MODEL), lambda i, j: (layer, j, 0)),
            pl.BlockSpec((D_MODEL, LANES), lambda i, j: (0, 0)),
            pl.BlockSpec((1, LANES), lambda i, j: (0, 0)),
            pl.BlockSpec((1, COL_TILE), lambda i, j: (0, j)),
            pl.BlockSpec((COL_TILE, COL_TILE), lambda i, j: (0, 0)),
            pl.BlockSpec((1, COL_TILE, COL_TILE),
                         lambda i, j: (jnp.clip(j - qb_first, 0, 2 * TILES_PER_GROUP - 1), 0, 0)),
        ],
        out_specs=[a_spec(d) for d in dilations] + [
            pl.BlockSpec((1, HEADS_PER_TILE, ROW_TILE, AUG), head_block(qb_first)),
            pl.BlockSpec((1, HEADS_PER_TILE, ROW_TILE, AUG),
                         head_block(qb_first + TILES_PER_GROUP)),
            pl.BlockSpec((1, HEADS_PER_TILE, HEAD_DIM, ROW_TILE),
                         lambda i, j: (i // tps, jnp.clip(j - vt_first, 0, TILES_PER_GROUP - 1),
                                       0, i % tps)),
        ],
        out_shape=[jax.ShapeDtypeStruct((b, d, s // d, 3 * D_A), BF16) for d in dilations] + [
            jax.ShapeDtypeStruct((b, H_B, s, AUG), BF16),
            jax.ShapeDtypeStruct((b, H_B, s, AUG), BF16),
            jax.ShapeDtypeStruct((b, H_B, HEAD_DIM, s), BF16),
        ],
        scratch_shapes=[
            pltpu.VMEM((ROW_TILE, D_MODEL), BF16),
            pltpu.VMEM((ROW_TILE, 4 * LANES), BF16),
            pltpu.VMEM((1, LANES), F32),
            pltpu.VMEM((HEADS_PER_TILE, ROW_TILE, LANES), F32),
            pltpu.VMEM((HEADS_PER_TILE, dilations[1], ROW_TILE // dilations[1], LANES), F32),
        ],
        compiler_params=_params("arbitrary", "arbitrary"),
        name="proj",
    )(x, gain.reshape(1, D_MODEL), jnp.swapaxes(w_in, 1, 2), w_f, f_bias, head_gain,
      _head_mean_matrix(), _gate_placement_matrices())


def _bucket_steps(dilation):
    dist = np.arange(BAND + 1) * dilation
    max_exact = NUM_BUCKETS // 2
    large = max_exact + np.floor(
        np.log(np.maximum(dist, 1) / max_exact) / math.log(MAX_DISTANCE / max_exact)
        * (NUM_BUCKETS - max_exact)).astype(np.int64)
    bucket = np.where(dist < max_exact, dist, np.minimum(large, NUM_BUCKETS - 1))
    steps = [(0, int(bucket[0]))]
    for delta in range(1, BAND + 1):
        if bucket[delta] != bucket[delta - 1]:
            steps.append((delta, int(bucket[delta])))
    return steps


def _bias_kernel(table_ref, o_ref):
    iq = lax.broadcasted_iota(jnp.int32, (BAND, 2 * BAND), 0)
    ik = lax.broadcasted_iota(jnp.int32, (BAND, 2 * BAND), 1)
    delta = iq + BAND - ik
    in_band = jnp.logical_and(delta >= 0, delta <= BAND)
    for p, (_, dilation) in enumerate(DILATED_PATTERNS):
        steps = _bucket_steps(dilation)
        for h in range(H_A):
            val = jnp.full((BAND, 2 * BAND), table_ref[steps[0][1], h], F32)
            for start, bucket in steps[1:]:
                val = jnp.where(delta >= start, table_ref[bucket, h], val)
            o_ref[p, h] = jnp.where(in_band, val, NEG_INF)


def _band_bias(rel_table):
    n_pat = len(DILATED_PATTERNS)
    return pl.pallas_call(
        _bias_kernel,
        in_specs=[pl.BlockSpec(memory_space=pltpu.SMEM)],
        out_specs=pl.BlockSpec(memory_space=pltpu.VMEM),
        out_shape=jax.ShapeDtypeStruct((n_pat, H_A, BAND, 2 * BAND), F32),
        name="band_bias",
    )(rel_table)


DIL_UNITS = 8


def _dilated_kernel(q_ref, kp_ref, kc_ref, vp_ref, vc_ref, bias_ref, o_ref, st_ref):
    n_cls = q_ref.shape[1]
    n_blk = q_ref.shape[2] // BAND
    first = pl.program_id(2) == 0
    key_lane = lax.broadcasted_iota(jnp.int32, (1, 2 * BAND), 1)
    no_prev = jnp.where(jnp.logical_and(first, key_lane < BAND), NEG_INF, 0.0)
    stat_lane = lax.broadcasted_iota(jnp.int32, (BAND, LANES), 1)
    units = [(c, blk) for c in range(n_cls) for blk in range(n_blk)]

    def keys(prev_ref, cur_ref, c, blk, sl):
        if blk == 0:
            return jnp.concatenate([prev_ref[0, c, :, sl], cur_ref[0, c, :BAND, sl]], axis=0)
        return cur_ref[0, c, (blk - 1) * BAND:(blk + 1) * BAND, sl]

    scores = {}
    for c, blk in units:
        for h in range(H_A):
            sl = slice(h * HEAD_DIM, (h + 1) * HEAD_DIM)
            q = q_ref[0, c, blk * BAND:(blk + 1) * BAND, sl]
            scores[c, blk, h] = lax.dot_general(q, keys(kp_ref, kc_ref, c, blk, sl),
                                                (((1,), (1,)), ((), ())),
                                                preferred_element_type=F32)
    for c, blk in units:
        rows = slice(blk * BAND, (blk + 1) * BAND)
        stats = jnp.zeros((BAND, LANES), F32)
        for h in range(H_A):
            sl = slice(h * HEAD_DIM, (h + 1) * HEAD_DIM)
            s = scores[c, blk, h] + bias_ref[0, h]
            if blk == 0:
                s = s + no_prev
            mx = jnp.max(s, axis=-1, keepdims=True)
            p = jnp.exp(s - mx)
            den = jnp.sum(p, axis=-1, keepdims=True)
            o_ref[0, c, rows, sl] = jnp.dot(p.astype(BF16), keys(vp_ref, vc_ref, c, blk, sl),
                                            preferred_element_type=F32).astype(o_ref.dtype)
            stats = jnp.where(stat_lane == h, mx, stats)
            stats = jnp.where(stat_lane == H_A + h, den, stats)
        st_ref[0, c, rows, :] = stats


def _dilated(qkv, bias, pattern):
    b, dilation, n_sub, _ = qkv.shape
    n_blk = min(DIL_UNITS, n_sub // BAND)
    n_cls = DIL_UNITS // n_blk
    rows = n_blk * BAND
    nb = n_sub // rows

    def cur(which):
        return pl.BlockSpec((1, n_cls, rows, D_A), lambda bi, r, i: (bi, r, i, which))

    def prev(which):
        return pl.BlockSpec((1, n_cls, BAND, D_A),
                            lambda bi, r, i: (bi, r, jnp.maximum(i * n_blk - 1, 0), which))

    return pl.pallas_call(
        _dilated_kernel,
        grid=(b, dilation // n_cls, nb),
        in_specs=[
            cur(0), prev(1), cur(1), prev(2), cur(2),
            pl.BlockSpec((1, H_A, BAND, 2 * BAND), lambda bi, r, i: (pattern, 0, 0, 0)),
        ],
        out_specs=[
            pl.BlockSpec((1, n_cls, rows, D_A), lambda bi, r, i: (bi, r, i, 0)),
            pl.BlockSpec((1, n_cls, rows, LANES), lambda bi, r, i: (bi, r, i, 0)),
        ],
        out_shape=[
            jax.ShapeDtypeStruct((b, dilation, n_sub, D_A), BF16),
            jax.ShapeDtypeStruct((b, dilation, n_sub, LANES), F32),
        ],
        compiler_params=_params("parallel", "parallel", "arbitrary"),
        name=f"dilated_{dilation}",
    )(qkv, qkv, qkv, qkv, qkv, bias)


COMBINE_ROWS = 512


def _combine_kernel(*refs):
    n_pat = len(DILATED_PATTERNS)
    num_refs, st_refs = refs[:n_pat], refs[n_pat:2 * n_pat]
    o_ref, num_buf, st_buf, num_tmp, st_tmp = refs[2 * n_pat:]
    dilations = [ref.shape[1] for ref in num_refs]
    fine = min(d for d in dilations if d > 1)
    heads = [slice(h * HEAD_DIM, (h + 1) * HEAD_DIM) for h in range(H_A)]
    for p, dilation in enumerate(dilations):
        if dilation == 1:
            continue
        step = dilation // fine
        for rf in range(fine):
            rows = pl.ds(rf, COMBINE_ROWS // fine, stride=fine)
            if step == 1:
                st_buf[p, rows, :] = st_refs[p][0, rf]
                for h, sl in enumerate(heads):
                    num_buf[p, h, rows, :] = num_refs[p][0, rf, :, sl].astype(F32)
                continue
            for q in range(step):
                part = pl.ds(q, COMBINE_ROWS // dilation, stride=step)
                st_tmp[part, :] = st_refs[p][0, rf + fine * q]
                for h, sl in enumerate(heads):
                    num_tmp[h, part, :] = num_refs[p][0, rf + fine * q, :, sl].astype(F32)
            st_buf[p, rows, :] = st_tmp[...]
            for h in range(H_A):
                num_buf[p, h, rows, :] = num_tmp[h]
    stats = [st_refs[p][0, 0] if d == 1 else st_buf[p] for p, d in enumerate(dilations)]
    for h in range(H_A):
        sl = slice(h * HEAD_DIM, (h + 1) * HEAD_DIM)
        mxs = [st[:, h:h + 1] for st in stats]
        dens = [st[:, H_A + h:H_A + h + 1] for st in stats]
        mx = functools.reduce(jnp.maximum, mxs)
        num = None
        den = None
        for p, (pden, pmx) in enumerate(zip(dens, mxs)):
            scale = jnp.exp(pmx - mx)
            if dilations[p] == 1:
                pnum = num_refs[p][0, 0, :, sl].astype(F32)
            else:
                pnum = num_buf[p, h]
            num = pnum * scale if num is None else num + pnum * scale
            den = pden * scale if den is None else den + pden * scale
        o_ref[:, sl] = (num / den).astype(BF16)


def _combine(nums, stats):
    b, _, s, _ = nums[0].shape
    tps = s // COMBINE_ROWS

    def spec(arr):
        dilation, width = arr.shape[1], arr.shape[3]
        return pl.BlockSpec((1, dilation, COMBINE_ROWS // dilation, width),
                            lambda i: (i // tps, 0, i % tps, 0))

    n_pat = len(nums)
    fine = min(a.shape[1] for a in nums if a.shape[1] > 1)
    return pl.pallas_call(
        _combine_kernel,
        grid=(b * tps,),
        in_specs=[spec(a) for a in nums] + [spec(a) for a in stats],
        out_specs=pl.BlockSpec((COMBINE_ROWS, D_A), lambda i: (i, 0)),
        out_shape=jax.ShapeDtypeStruct((b * s, D_A), BF16),
        scratch_shapes=[
            pltpu.VMEM((n_pat, H_A, COMBINE_ROWS, HEAD_DIM), F32),
            pltpu.VMEM((n_pat, COMBINE_ROWS, LANES), F32),
            pltpu.VMEM((H_A, COMBINE_ROWS // fine, HEAD_DIM), F32),
            pltpu.VMEM((COMBINE_ROWS // fine, LANES), F32),
        ],
        compiler_params=_params("parallel"),
        name="combine_a",
    )(*nums, *stats)


def _fox_kernel(q_ref, k_ref, vt_ref, o_ref, m_ref, l_ref, acc_ref, sa_ref, sb_ref):
    iq = pl.program_id(2)
    m_ref[...] = jnp.full(m_ref.shape, NEG_INF, F32)
    l_ref[...] = jnp.zeros(l_ref.shape, F32)
    acc_ref[...] = jnp.zeros(acc_ref.shape, F32)

    def scores_into(s_ref, kb):
        start = pl.multiple_of(kb * FOX_TK, FOX_TK)
        for hh in range(FOX_HEADS):
            k = k_ref[0, hh, pl.ds(start, FOX_TK), :]
            s_ref[hh] = lax.dot_general(k, q_ref[0, hh], (((1,), (1,)), ((), ())),
                                        preferred_element_type=F32)

    def accumulate(s_ref, kb, diagonal):
        start = pl.multiple_of(kb * FOX_TK, FOX_TK)
        for hh in range(FOX_HEADS):
            st = s_ref[hh]
            if diagonal:
                key = lax.broadcasted_iota(jnp.int32, st.shape, 0)
                qry = lax.broadcasted_iota(jnp.int32, st.shape, 1)
                st = jnp.where(key <= qry, st, NEG_INF)
            m_prev = m_ref[hh]
            m_new = jnp.maximum(m_prev, jnp.max(st, axis=0, keepdims=True))
            alpha = jnp.exp2(m_prev - m_new)
            p = jnp.exp2(st - m_new)
            l_ref[hh] = alpha * l_ref[hh] + jnp.sum(p, axis=0, keepdims=True)
            vt = vt_ref[0, hh, :, pl.ds(start, FOX_TK)]
            acc_ref[hh] = alpha * acc_ref[hh] + jnp.dot(vt, p.astype(BF16),
                                                        preferred_element_type=F32)
            m_ref[hh] = m_new

    scores_into(sa_ref, 0)

    def pair(p, carry):
        scores_into(sb_ref, 2 * p + 1)
        accumulate(sa_ref, 2 * p, False)
        scores_into(sa_ref, 2 * p + 2)
        accumulate(sb_ref, 2 * p + 1, False)
        return carry

    lax.fori_loop(0, iq // 2, pair, 0)

    @pl.when(iq % 2 == 0)
    def _():
        accumulate(sa_ref, iq, True)

    @pl.when(iq % 2 == 1)
    def _():
        scores_into(sb_ref, iq)
        accumulate(sa_ref, iq - 1, False)
        accumulate(sb_ref, iq, True)
    for hh in range(FOX_HEADS):
        o_ref[0, :, hh * HEAD_DIM:(hh + 1) * HEAD_DIM] = (
            acc_ref[hh] / l_ref[hh]).T.astype(BF16)


def _fox(q_aug, k_aug, v_t):
    b, h_b, s, _ = q_aug.shape
    assert FOX_TQ == FOX_TK
    return pl.pallas_call(
        _fox_kernel,
        grid=(b, h_b // FOX_HEADS, s // FOX_TQ),
        in_specs=[
            pl.BlockSpec((1, FOX_HEADS, FOX_TQ, AUG), lambda bi, h, i: (bi, h, i, 0)),
            pl.BlockSpec((1, FOX_HEADS, s, AUG), lambda bi, h, i: (bi, h, 0, 0)),
            pl.BlockSpec((1, FOX_HEADS, HEAD_DIM, s), lambda bi, h, i: (bi, h, 0, 0)),
        ],
        out_specs=pl.BlockSpec((1, FOX_TQ, FOX_HEADS * HEAD_DIM), lambda bi, h, i: (bi, i, h)),
        out_shape=jax.ShapeDtypeStruct((b, s, h_b * HEAD_DIM), BF16),
        scratch_shapes=[
            pltpu.VMEM((FOX_HEADS, 1, FOX_TQ), F32),
            pltpu.VMEM((FOX_HEADS, 1, FOX_TQ), F32),
            pltpu.VMEM((FOX_HEADS, HEAD_DIM, FOX_TQ), F32),
            pltpu.VMEM((FOX_HEADS, FOX_TK, FOX_TQ), F32),
            pltpu.VMEM((FOX_HEADS, FOX_TK, FOX_TQ), F32),
        ],
        compiler_params=_params("parallel", "parallel", "arbitrary"),
        name="fox",
    )(q_aug, k_aug, v_t)


def _out_proj_kernel(a_ref, b_ref, w_ref, x_ref, o_ref):
    mixed = jnp.concatenate([a_ref[...], b_ref[...]], axis=-1)
    o_ref[...] = x_ref[...] + jnp.dot(mixed, w_ref[...].astype(BF16),
                                      preferred_element_type=F32)


OUT_ROW_TILE = 2048


def _out_proj(out_a, out_b, w_out, x):
    m = x.shape[0]
    return pl.pallas_call(
        _out_proj_kernel,
        grid=(m // OUT_ROW_TILE, D_MODEL // COL_TILE),
        in_specs=[
            pl.BlockSpec((OUT_ROW_TILE, D_A), lambda i, j: (i, 0)),
            pl.BlockSpec((OUT_ROW_TILE, D_B), lambda i, j: (i, 0)),
            pl.BlockSpec((D_A + D_B, COL_TILE), lambda i, j: (0, j)),
            pl.BlockSpec((OUT_ROW_TILE, COL_TILE), lambda i, j: (i, j)),
        ],
        out_specs=pl.BlockSpec((OUT_ROW_TILE, COL_TILE), lambda i, j: (i, j)),
        out_shape=jax.ShapeDtypeStruct((m, D_MODEL), F32),
        compiler_params=_params("parallel", "parallel"),
        name="out_proj",
    )(out_a, out_b, w_out, x)


def _mixer(x, b, s, mix_norm, w_in, layer, q_norm_a, k_norm_a, q_norm_b, k_norm_b, forget_bias,
           rel_bias_table, w_out):
    ones = jnp.ones((D_A,), F32)
    head_gain = jnp.concatenate([
        jnp.tile(q_norm_a * ATTN_SCALE, H_A), jnp.tile(k_norm_a, H_A), ones,
        jnp.tile(q_norm_b * (ATTN_SCALE * LOG2E), H_B), jnp.tile(k_norm_b, H_B), ones,
    ]).reshape(1, D_QKV)
    w_f = jnp.pad(w_in[layer, :, D_QKV:], ((0, 0), (0, LANES - H_B))).astype(BF16)
    f_bias = jnp.pad(forget_bias, (0, LANES - H_B)).reshape(1, LANES)
    *qkv_a, q_aug, k_aug, v_t = _proj(x, b, s, mix_norm, w_in, layer, w_f, f_bias, head_gain)

    out_b = _fox(q_aug, k_aug, v_t).reshape(b * s, D_B)

    bias = _band_bias(rel_bias_table)
    parts = [_dilated(qkv, bias, p) for p, qkv in enumerate(qkv_a)]
    out_a = _combine([pt[0] for pt in parts], [pt[1] for pt in parts])

    return _out_proj(out_a, out_b, w_out, x)


def kernel(x, ffn1_norm, ffn1_w_in, ffn1_w_out, mix_norm, w_in, q_norm_a, k_norm_a, q_norm_b,
           k_norm_b, forget_bias, rel_bias_table, w_out, ffn2_norm, ffn2_w_in, ffn2_w_out):
    b, s, d = x.shape
    depth = ffn1_norm.shape[0]
    x = x.reshape(b * s, d)
    for l in range(depth):
        x = _ffn(x, ffn1_norm[l], ffn1_w_in[l], ffn1_w_out[l])
        x = _mixer(x, b, s, mix_norm[l], w_in, l, q_norm_a[l], k_norm_a[l], q_norm_b[l],
                   k_norm_b[l], forget_bias[l], rel_bias_table, w_out[l])
        x = _ffn(x, ffn2_norm[l], ffn2_w_in[l], ffn2_w_out[l])
    return x.reshape(b, s, d)
```

```python
import functools
import math

import numpy as np
import jax
import jax.numpy as jnp
from jax import lax
from jax.experimental import pallas as pl
from jax.experimental.pallas import tpu as pltpu

D_MODEL = 2048
HEAD_DIM = 128
N_HEADS = D_MODEL // HEAD_DIM
H_A = N_HEADS // 2
H_B = N_HEADS - H_A
D_A = H_A * HEAD_DIM
D_B = H_B * HEAD_DIM
D_QKV = 3 * D_A + 3 * D_B
DILATED_PATTERNS = ((128, 1), (512, 4), (2048, 16))
BAND = 128
NUM_BUCKETS = 32
MAX_DISTANCE = 2048
D_FF = ((8 * D_MODEL // 3 + 127) // 128) * 128
RMS_EPS = 1e-6
NEG_INF = -1e30
ATTN_SCALE = HEAD_DIM ** -0.5

LANES = 128
FF_TILE = 512
FFN_ROW_TILE = 2048
FFN_CHUNK = 512
FFN_VMEM_LIMIT = 60 * 1024 * 1024
ROW_TILE = 1024
COL_TILE = 512
FOX_TQ = 512
FOX_TK = 512
FOX_HEADS = 4
VMEM_LIMIT = 56 * 1024 * 1024

F32 = jnp.float32
BF16 = jnp.bfloat16


def _params(*sem):
    return pltpu.CompilerParams(dimension_semantics=sem, vmem_limit_bytes=VMEM_LIMIT)


def _rms_rows(x, gain):
    ms = jnp.mean(x * x, axis=-1, keepdims=True)
    return x * lax.rsqrt(ms + RMS_EPS) * gain


def _ffn_kernel(x_hbm, g_ref, wg_ref, wu_ref, wo_ref, o_hbm, acc_ref, h_ref, in_sem, out_sem):
    i = pl.program_id(0)
    j = pl.program_id(1)
    last_i = pl.num_programs(0) - 1
    last_j = pl.num_programs(1) - 1
    pieces = [(r, c) for r in range(FFN_ROW_TILE // ROW_TILE) for c in range(D_MODEL // COL_TILE)]

    def x_copy(c):
        src = x_hbm.at[pl.ds(i * FFN_ROW_TILE + c * FFN_CHUNK, FFN_CHUNK), :]
        return pltpu.make_async_copy(src, acc_ref.at[pl.ds(c * FFN_CHUNK, FFN_CHUNK), :],
                                     in_sem.at[c])

    def out_copy(row_tile, r, c):
        rows, cols = pl.ds(r * ROW_TILE, ROW_TILE), pl.ds(c * COL_TILE, COL_TILE)
        dst = o_hbm.at[pl.ds(row_tile * FFN_ROW_TILE + r * ROW_TILE, ROW_TILE), cols]
        return pltpu.make_async_copy(acc_ref.at[rows, cols], dst,
                                     out_sem.at[r * (D_MODEL // COL_TILE) + c])

    @pl.when(j == 0)
    def _():
        @pl.when(i > 0)
        def _():
            for r, c in pieces:
                out_copy(i - 1, r, c).wait()

        n_chunks = FFN_ROW_TILE // FFN_CHUNK
        for c in range(n_chunks):
            x_copy(c).start()
        for c in range(n_chunks):
            x_copy(c).wait()
            rows = slice(c * FFN_CHUNK, (c + 1) * FFN_CHUNK)
            h_ref[rows, :] = _rms_rows(acc_ref[rows, :], g_ref[...]).astype(BF16)

    col = lax.broadcasted_iota(jnp.int32, (1, FF_TILE), 1)
    repeated = j * FF_TILE - jnp.minimum(j * FF_TILE, D_FF - FF_TILE)

    def tile(write_back):
        def half(r, carry):
            rows = pl.ds(pl.multiple_of(r * ROW_TILE, ROW_TILE), ROW_TILE)
            h = h_ref[rows, :]
            gate = jnp.dot(h, wg_ref[...].astype(BF16), preferred_element_type=F32)
            up = jnp.dot(h, wu_ref[...].astype(BF16), preferred_element_type=F32)
            act = gate * (1.0 / (1.0 + jnp.exp(-gate))) * (0.5 * up)
            act = jnp.where(col >= repeated, act, 0.0).astype(BF16)
            for c in range(D_MODEL // COL_TILE):
                sl = slice(c * COL_TILE, (c + 1) * COL_TILE)
                acc_ref[rows, sl] += jnp.dot(act, wo_ref[:, sl].astype(BF16),
                                             preferred_element_type=F32)
                if write_back:
                    out_copy(i, r, c).start()
            return carry

        lax.fori_loop(0, FFN_ROW_TILE // ROW_TILE, half, 0)

    @pl.when(j < last_j)
    def _():
        tile(False)

    @pl.when(j == last_j)
    def _():
        tile(True)

        @pl.when(i == last_i)
        def _():
            for r, c in pieces:
                out_copy(i, r, c).wait()


def _ffn(x, gain, w_in, w_out):
    m = x.shape[0]
    n_ff = pl.cdiv(D_FF, FF_TILE)

    def ff_start(j):
        return pl.multiple_of(jnp.minimum(j * FF_TILE, D_FF - FF_TILE), LANES)

    return pl.pallas_call(
        _ffn_kernel,
        grid=(m // FFN_ROW_TILE, n_ff),
        in_specs=[
            pl.BlockSpec(memory_space=pl.ANY),
            pl.BlockSpec((1, D_MODEL), lambda i, j: (0, 0)),
            pl.BlockSpec((pl.Element(D_MODEL), pl.Element(FF_TILE)),
                         lambda i, j: (0, ff_start(j))),
            pl.BlockSpec((pl.Element(D_MODEL), pl.Element(FF_TILE)),
                         lambda i, j: (0, pl.multiple_of(D_FF + ff_start(j), LANES))),
            pl.BlockSpec((pl.Element(FF_TILE), pl.Element(D_MODEL)),
                         lambda i, j: (ff_start(j), 0)),
        ],
        out_specs=pl.BlockSpec(memory_space=pl.ANY),
        out_shape=jax.ShapeDtypeStruct((m, D_MODEL), F32),
        scratch_shapes=[
            pltpu.VMEM((FFN_ROW_TILE, D_MODEL), F32),
            pltpu.VMEM((FFN_ROW_TILE, D_MODEL), BF16),
            pltpu.SemaphoreType.DMA((FFN_ROW_TILE // FFN_CHUNK,)),
            pltpu.SemaphoreType.DMA(((FFN_ROW_TILE // ROW_TILE) * (D_MODEL // COL_TILE),)),
        ],
        compiler_params=pltpu.CompilerParams(
            dimension_semantics=("arbitrary", "arbitrary"), vmem_limit_bytes=FFN_VMEM_LIMIT),
        name="ffn",
    )(x, gain.reshape(1, D_MODEL), w_in, w_in, w_out)


HEADS_PER_TILE = COL_TILE // HEAD_DIM
N_PROJ_TILES = D_QKV // COL_TILE
TILES_PER_GROUP = D_A // COL_TILE
A_TILES = 3 * TILES_PER_GROUP
AUG = 2 * HEAD_DIM
LOG2E = math.log2(math.e)


def _log_gate_scan(z, carry):
    c = (jnp.minimum(z, 0.0) - jnp.log1p(jnp.exp(-jnp.abs(z)))) * LOG2E
    row = lax.broadcasted_iota(jnp.int32, c.shape, 0)
    shift = 1
    while shift < c.shape[0]:
        c = c + jnp.where(row >= shift, pltpu.roll(c, shift, axis=0), 0.0)
        shift *= 2
    return c + carry


def _split3(c):
    hi = c.astype(BF16)
    rest = c - hi.astype(F32)
    mid = rest.astype(BF16)
    lo = (rest - mid.astype(F32)).astype(BF16)
    return hi, mid, lo


def _head_mean_matrix():
    head = np.arange(COL_TILE) // HEAD_DIM
    return jnp.asarray((head[:, None] == head[None, :]) / HEAD_DIM, BF16)


def _gate_placement_matrices():
    mats = np.zeros((2, TILES_PER_GROUP, COL_TILE, COL_TILE), np.float32)
    ones_row = 3 * LANES
    for part in range(TILES_PER_GROUP):
        for hh in range(HEADS_PER_TILE):
            head = part * HEADS_PER_TILE + hh
            col = hh * HEAD_DIM
            for term in range(3):
                mats[0, part, term * LANES + head, col + term] = 1.0
                mats[0, part, ones_row, col + 3 + term] = 1.0
                mats[1, part, term * LANES + head, col + 3 + term] = -1.0
                mats[1, part, ones_row, col + term] = 1.0
    return jnp.asarray(mats.reshape(2 * TILES_PER_GROUP, COL_TILE, COL_TILE), BF16)


def _proj_kernel(x_ref, g_ref, w_ref, wf_ref, fb_ref, hg_ref, mean_ref, place_ref,
                 a1_ref, a4_ref, a16_ref, qb_ref, kb_ref, vt_ref,
                 h_ref, cs_ref, carry_ref, y_ref, yf_ref, *, tiles_per_seq):
    i = pl.program_id(0)
    j = pl.program_id(1)

    @pl.when(j == 0)
    def _():
        h_ref[...] = _rms_rows(x_ref[...], g_ref[...]).astype(BF16)

        @pl.when(i % tiles_per_seq == 0)
        def _():
            carry_ref[...] = jnp.zeros(carry_ref.shape, F32)

        z = jnp.dot(h_ref[...], wf_ref[...], preferred_element_type=F32) + fb_ref[...]
        c = _log_gate_scan(z, carry_ref[...])
        carry_ref[...] = c[ROW_TILE - 1:ROW_TILE, :]
        for term, part in enumerate(_split3(c)):
            cs_ref[:, term * LANES:(term + 1) * LANES] = part
        cs_ref[:, 3 * LANES:] = jnp.ones((ROW_TILE, LANES), BF16)

    def product(transposed=False):
        lhs, rhs = (w_ref[...].astype(BF16), h_ref[...])
        if not transposed:
            lhs, rhs = rhs, lhs
        return lax.dot_general(lhs, rhs, (((1,), (1,)), ((), ())), preferred_element_type=F32)

    def normed():
        r = product()
        ms = jnp.dot((r * r).astype(BF16), mean_ref[...], preferred_element_type=F32)
        return r * lax.rsqrt(ms + RMS_EPS) * hg_ref[...]

    def store_dilated(y):
        a1_ref[0, 0] = y.astype(BF16)
        fine, coarse = a4_ref.shape[1], a16_ref.shape[1]
        step = coarse // fine
        for c in range(HEADS_PER_TILE):
            sl = slice(c * LANES, (c + 1) * LANES)
            y_ref[c] = y[:, sl]
            for rf in range(fine):
                rows = y_ref[c, pl.ds(rf, ROW_TILE // fine, stride=fine), :]
                a4_ref[0, rf, :, sl] = rows.astype(BF16)
                yf_ref[c, rf] = rows
                for q in range(step):
                    rows = yf_ref[c, rf, pl.ds(q, ROW_TILE // coarse, stride=step), :]
                    a16_ref[0, rf + fine * q, :, sl] = rows.astype(BF16)

    def tiles(first_group, n_groups=1):
        lo = first_group * TILES_PER_GROUP
        return jnp.logical_and(j >= lo, j < lo + n_groups * TILES_PER_GROUP)

    def store_forgetting(out):
        y = normed().astype(BF16)
        aug = jnp.dot(cs_ref[...], place_ref[0], preferred_element_type=F32).astype(BF16)
        for hh in range(HEADS_PER_TILE):
            sl = slice(hh * HEAD_DIM, (hh + 1) * HEAD_DIM)
            out[0, hh, :, :HEAD_DIM] = y[:, sl]
            out[0, hh, :, HEAD_DIM:] = aug[:, sl]

    @pl.when(tiles(0, 2))
    def _():
        store_dilated(normed())

    @pl.when(tiles(2))
    def _():
        store_dilated(product())

    @pl.when(tiles(3))
    def _():
        store_forgetting(qb_ref)

    @pl.when(tiles(4))
    def _():
        store_forgetting(kb_ref)

    @pl.when(tiles(5))
    def _():
        v_t = product(transposed=True).astype(BF16)
        for hh in range(HEADS_PER_TILE):
            vt_ref[0, hh] = v_t[hh * HEAD_DIM:(hh + 1) * HEAD_DIM, :]


def _proj(x, b, s, gain, w_in, layer, w_f, f_bias, head_gain):
    m = x.shape[0]
    tps = s // ROW_TILE

    def a_spec(dilation):
        return pl.BlockSpec((1, dilation, ROW_TILE // dilation, COL_TILE),
                            lambda i, j: (i // tps, 0, i % tps, jnp.minimum(j, A_TILES - 1)))

    def head_block(first):
        return lambda i, j: (i // tps, jnp.clip(j - first, 0, TILES_PER_GROUP - 1), i % tps, 0)

    qb_first = A_TILES
    vt_first = A_TILES + 2 * TILES_PER_GROUP
    dilations = [d for _, d in DILATED_PATTERNS]
    return pl.pallas_call(
        functools.partial(_proj_kernel, tiles_per_seq=tps),
        grid=(m // ROW_TILE, N_PROJ_TILES),
        in_specs=[
            pl.BlockSpec((ROW_TILE, D_MODEL), lambda i, j: (i, 0)),
            pl.BlockSpec((1, D_MODEL), lambda i, j: (0, 0)),
            pl.BlockSpec((None, COL_TILE, D_MODEL), lambda i, j: (layer, j, 0)),
            pl.BlockSpec((D_MODEL, LANES), lambda i, j: (0, 0)),
            pl.BlockSpec((1, LANES), lambda i, j: (0, 0)),
            pl.BlockSpec((1, COL_TILE), lambda i, j: (0, j)),
            pl.BlockSpec((COL_TILE, COL_TILE), lambda i, j: (0, 0)),
            pl.BlockSpec((1, COL_TILE, COL_TILE),
                         lambda i, j: (jnp.clip(j - qb_first, 0, 2 * TILES_PER_GROUP - 1), 0, 0)),
        ],
        out_specs=[a_spec(d) for d in dilations] + [
            pl.BlockSpec((1, HEADS_PER_TILE, ROW_TILE, AUG), head_block(qb_first)),
            pl.BlockSpec((1, HEADS_PER_TILE, ROW_TILE, AUG),
                         head_block(qb_first + TILES_PER_GROUP)),
            pl.BlockSpec((1, HEADS_PER_TILE, HEAD_DIM, ROW_TILE),
                         lambda i, j: (i // tps, jnp.clip(j - vt_first, 0, TILES_PER_GROUP - 1),
                                       0, i % tps)),
        ],
        out_shape=[jax.ShapeDtypeStruct((b, d, s // d, 3 * D_A), BF16) for d in dilations] + [
            jax.ShapeDtypeStruct((b, H_B, s, AUG), BF16),
            jax.ShapeDtypeStruct((b, H_B, s, AUG), BF16),
            jax.ShapeDtypeStruct((b, H_B, HEAD_DIM, s), BF16),
        ],
        scratch_shapes=[
            pltpu.VMEM((ROW_TILE, D_MODEL), BF16),
            pltpu.VMEM((ROW_TILE, 4 * LANES), BF16),
            pltpu.VMEM((1, LANES), F32),
            pltpu.VMEM((HEADS_PER_TILE, ROW_TILE, LANES), F32),
            pltpu.VMEM((HEADS_PER_TILE, dilations[1], ROW_TILE // dilations[1], LANES), F32),
        ],
        compiler_params=_params("arbitrary", "arbitrary"),
        name="proj",
    )(x, gain.reshape(1, D_MODEL), jnp.swapaxes(w_in, 1, 2), w_f, f_bias, head_gain,
      _head_mean_matrix(), _gate_placement_matrices())


def _bucket_steps(dilation):
    dist = np.arange(BAND + 1) * dilation
    max_exact = NUM_BUCKETS // 2
    large = max_exact + np.floor(
        np.log(np.maximum(dist, 1) / max_exact) / math.log(MAX_DISTANCE / max_exact)
        * (NUM_BUCKETS - max_exact)).astype(np.int64)
    bucket = np.where(dist < max_exact, dist, np.minimum(large, NUM_BUCKETS - 1))
    steps = [(0, int(bucket[0]))]
    for delta in range(1, BAND + 1):
        if bucket[delta] != bucket[delta - 1]:
            steps.append((delta, int(bucket[delta])))
    return steps


def _bias_kernel(table_ref, o_ref):
    iq = lax.broadcasted_iota(jnp.int32, (BAND, 2 * BAND), 0)
    ik = lax.broadcasted_iota(jnp.int32, (BAND, 2 * BAND), 1)
    delta = iq + BAND - ik
    in_band = jnp.logical_and(delta >= 0, delta <= BAND)
    for p, (_, dilation) in enumerate(DILATED_PATTERNS):
        steps = _bucket_steps(dilation)
        for h in range(H_A):
            val = jnp.full((BAND, 2 * BAND), table_ref[steps[0][1], h], F32)
            for start, bucket in steps[1:]:
                val = jnp.where(delta >= start, table_ref[bucket, h], val)
            o_ref[p, h] = jnp.where(in_band, val, NEG_INF)


def _band_bias(rel_table):
    n_pat = len(DILATED_PATTERNS)
    return pl.pallas_call(
        _bias_kernel,
        in_specs=[pl.BlockSpec(memory_space=pltpu.SMEM)],
        out_specs=pl.BlockSpec(memory_space=pltpu.VMEM),
        out_shape=jax.ShapeDtypeStruct((n_pat, H_A, BAND, 2 * BAND), F32),
        name="band_bias",
    )(rel_table)


DIL_UNITS = 8


def _dilated_kernel(q_ref, kp_ref, kc_ref, vp_ref, vc_ref, bias_ref, o_ref, st_ref):
    n_cls = q_ref.shape[1]
    n_blk = q_ref.shape[2] // BAND
    first = pl.program_id(2) == 0
    key_lane = lax.broadcasted_iota(jnp.int32, (1, 2 * BAND), 1)
    no_prev = jnp.where(jnp.logical_and(first, key_lane < BAND), NEG_INF, 0.0)
    stat_lane = lax.broadcasted_iota(jnp.int32, (BAND, LANES), 1)
    units = [(c, blk) for c in range(n_cls) for blk in range(n_blk)]

    def keys(prev_ref, cur_ref, c, blk, sl):
        if blk == 0:
            return jnp.concatenate([prev_ref[0, c, :, sl], cur_ref[0, c, :BAND, sl]], axis=0)
        return cur_ref[0, c, (blk - 1) * BAND:(blk + 1) * BAND, sl]

    scores = {}
    for c, blk in units:
        for h in range(H_A):
            sl = slice(h * HEAD_DIM, (h + 1) * HEAD_DIM)
            q = q_ref[0, c, blk * BAND:(blk + 1) * BAND, sl]
            scores[c, blk, h] = lax.dot_general(q, keys(kp_ref, kc_ref, c, blk, sl),
                                                (((1,), (1,)), ((), ())),
                                                preferred_element_type=F32)
    for c, blk in units:
        rows = slice(blk * BAND, (blk + 1) * BAND)
        stats = jnp.zeros((BAND, LANES), F32)
        for h in range(H_A):
            sl = slice(h * HEAD_DIM, (h + 1) * HEAD_DIM)
            s = scores[c, blk, h] + bias_ref[0, h]
            if blk == 0:
                s = s + no_prev
            mx = jnp.max(s, axis=-1, keepdims=True)
            p = jnp.exp(s - mx)
            den = jnp.sum(p, axis=-1, keepdims=True)
            o_ref[0, c, rows, sl] = jnp.dot(p.astype(BF16), keys(vp_ref, vc_ref, c, blk, sl),
                                            preferred_element_type=F32).astype(o_ref.dtype)
            stats = jnp.where(stat_lane == h, mx, stats)
            stats = jnp.where(stat_lane == H_A + h, den, stats)
        st_ref[0, c, rows, :] = stats


def _dilated(qkv, bias, pattern):
    b, dilation, n_sub, _ = qkv.shape
    n_blk = min(DIL_UNITS, n_sub // BAND)
    n_cls = DIL_UNITS // n_blk
    rows = n_blk * BAND
    nb = n_sub // rows

    def cur(which):
        return pl.BlockSpec((1, n_cls, rows, D_A), lambda bi, r, i: (bi, r, i, which))

    def prev(which):
        return pl.BlockSpec((1, n_cls, BAND, D_A),
                            lambda bi, r, i: (bi, r, jnp.maximum(i * n_blk - 1, 0), which))

    return pl.pallas_call(
        _dilated_kernel,
        grid=(b, dilation // n_cls, nb),
        in_specs=[
            cur(0), prev(1), cur(1), prev(2), cur(2),
            pl.BlockSpec((1, H_A, BAND, 2 * BAND), lambda bi, r, i: (pattern, 0, 0, 0)),
        ],
        out_specs=[
            pl.BlockSpec((1, n_cls, rows, D_A), lambda bi, r, i: (bi, r, i, 0)),
            pl.BlockSpec((1, n_cls, rows, LANES), lambda bi, r, i: (bi, r, i, 0)),
        ],
        out_shape=[
            jax.ShapeDtypeStruct((b, dilation, n_sub, D_A), BF16),
            jax.ShapeDtypeStruct((b, dilation, n_sub, LANES), F32),
        ],
        compiler_params=_params("parallel", "parallel", "arbitrary"),
        name=f"dilated_{dilation}",
    )(qkv, qkv, qkv, qkv, qkv, bias)


COMBINE_ROWS = 512


def _combine_kernel(*refs):
    n_pat = len(DILATED_PATTERNS)
    num_refs, st_refs = refs[:n_pat], refs[n_pat:2 * n_pat]
    o_ref, num_buf, st_buf, num_tmp, st_tmp = refs[2 * n_pat:]
    dilations = [ref.shape[1] for ref in num_refs]
    fine = min(d for d in dilations if d > 1)
    heads = [slice(h * HEAD_DIM, (h + 1) * HEAD_DIM) for h in range(H_A)]
    for p, dilation in enumerate(dilations):
        if dilation == 1:
            continue
        step = dilation // fine
        for rf in range(fine):
            rows = pl.ds(rf, COMBINE_ROWS // fine, stride=fine)
            if step == 1:
                st_buf[p, rows, :] = st_refs[p][0, rf]
                for h, sl in enumerate(heads):
                    num_buf[p, h, rows, :] = num_refs[p][0, rf, :, sl].astype(F32)
                continue
            for q in range(step):
                part = pl.ds(q, COMBINE_ROWS // dilation, stride=step)
                st_tmp[part, :] = st_refs[p][0, rf + fine * q]
                for h, sl in enumerate(heads):
                    num_tmp[h, part, :] = num_refs[p][0, rf + fine * q, :, sl].astype(F32)
            st_buf[p, rows, :] = st_tmp[...]
            for h in range(H_A):
                num_buf[p, h, rows, :] = num_tmp[h]
    stats = [st_refs[p][0, 0] if d == 1 else st_buf[p] for p, d in enumerate(dilations)]
    for h in range(H_A):
        sl = slice(h * HEAD_DIM, (h + 1) * HEAD_DIM)
        mxs = [st[:, h:h + 1] for st in stats]
        dens = [st[:, H_A + h:H_A + h + 1] for st in stats]
        mx = functools.reduce(jnp.maximum, mxs)
        num = None
        den = None
        for p, (pden, pmx) in enumerate(zip(dens, mxs)):
            scale = jnp.exp(pmx - mx)
            if dilations[p] == 1:
                pnum = num_refs[p][0, 0, :, sl].astype(F32)
            else:
                pnum = num_buf[p, h]
            num = pnum * scale if num is None else num + pnum * scale
            den = pden * scale if den is None else den + pden * scale
        o_ref[:, sl] = (num / den).astype(BF16)


def _combine(nums, stats):
    b, _, s, _ = nums[0].shape
    tps = s // COMBINE_ROWS

    def spec(arr):
        dilation, width = arr.shape[1], arr.shape[3]
        return pl.BlockSpec((1, dilation, COMBINE_ROWS // dilation, width),
                            lambda i: (i // tps, 0, i % tps, 0))

    n_pat = len(nums)
    fine = min(a.shape[1] for a in nums if a.shape[1] > 1)
    return pl.pallas_call(
        _combine_kernel,
        grid=(b * tps,),
        in_specs=[spec(a) for a in nums] + [spec(a) for a in stats],
        out_specs=pl.BlockSpec((COMBINE_ROWS, D_A), lambda i: (i, 0)),
        out_shape=jax.ShapeDtypeStruct((b * s, D_A), BF16),
        scratch_shapes=[
            pltpu.VMEM((n_pat, H_A, COMBINE_ROWS, HEAD_DIM), F32),
            pltpu.VMEM((n_pat, COMBINE_ROWS, LANES), F32),
            pltpu.VMEM((H_A, COMBINE_ROWS // fine, HEAD_DIM), F32),
            pltpu.VMEM((COMBINE_ROWS // fine, LANES), F32),
        ],
        compiler_params=_params("parallel"),
        name="combine_a",
    )(*nums, *stats)


def _fox_kernel(q_ref, k_ref, vt_ref, o_ref, m_ref, l_ref, acc_ref, sa_ref, sb_ref):
    iq = pl.program_id(2)
    m_ref[...] = jnp.full(m_ref.shape, NEG_INF, F32)
    l_ref[...] = jnp.zeros(l_ref.shape, F32)
    acc_ref[...] = jnp.zeros(acc_ref.shape, F32)

    def scores_into(s_ref, kb):
        start = pl.multiple_of(kb * FOX_TK, FOX_TK)
        for hh in range(FOX_HEADS):
            k = k_ref[0, hh, pl.ds(start, FOX_TK), :]
            s_ref[hh] = lax.dot_general(k, q_ref[0, hh], (((1,), (1,)), ((), ())),
                                        preferred_element_type=F32)

    def accumulate(s_ref, kb, diagonal):
        start = pl.multiple_of(kb * FOX_TK, FOX_TK)
        for hh in range(FOX_HEADS):
            st = s_ref[hh]
            if diagonal:
                key = lax.broadcasted_iota(jnp.int32, st.shape, 0)
                qry = lax.broadcasted_iota(jnp.int32, st.shape, 1)
                st = jnp.where(key <= qry, st, NEG_INF)
            m_prev = m_ref[hh]
            m_new = jnp.maximum(m_prev, jnp.max(st, axis=0, keepdims=True))
            alpha = jnp.exp2(m_prev - m_new)
            p = jnp.exp2(st - m_new)
            l_ref[hh] = alpha * l_ref[hh] + jnp.sum(p, axis=0, keepdims=True)
            vt = vt_ref[0, hh, :, pl.ds(start, FOX_TK)]
            acc_ref[hh] = alpha * acc_ref[hh] + jnp.dot(vt, p.astype(BF16),
                                                        preferred_element_type=F32)
            m_ref[hh] = m_new

    scores_into(sa_ref, 0)

    def pair(p, carry):
        scores_into(sb_ref, 2 * p + 1)
        accumulate(sa_ref, 2 * p, False)
        scores_into(sa_ref, 2 * p + 2)
        accumulate(sb_ref, 2 * p + 1, False)
        return carry

    lax.fori_loop(0, iq // 2, pair, 0)

    @pl.when(iq % 2 == 0)
    def _():
        accumulate(sa_ref, iq, True)

    @pl.when(iq % 2 == 1)
    def _():
        scores_into(sb_ref, iq)
        accumulate(sa_ref, iq - 1, False)
        accumulate(sb_ref, iq, True)
    for hh in range(FOX_HEADS):
        o_ref[0, :, hh * HEAD_DIM:(hh + 1) * HEAD_DIM] = (
            acc_ref[hh] / l_ref[hh]).T.astype(BF16)


def _fox(q_aug, k_aug, v_t):
    b, h_b, s, _ = q_aug.shape
    assert FOX_TQ == FOX_TK
    return pl.pallas_call(
        _fox_kernel,
        grid=(b, h_b // FOX_HEADS, s // FOX_TQ),
        in_specs=[
            pl.BlockSpec((1, FOX_HEADS, FOX_TQ, AUG), lambda bi, h, i: (bi, h, i, 0)),
            pl.BlockSpec((1, FOX_HEADS, s, AUG), lambda bi, h, i: (bi, h, 0, 0)),
            pl.BlockSpec((1, FOX_HEADS, HEAD_DIM, s), lambda bi, h, i: (bi, h, 0, 0)),
        ],
        out_specs=pl.BlockSpec((1, FOX_TQ, FOX_HEADS * HEAD_DIM), lambda bi, h, i: (bi, i, h)),
        out_shape=jax.ShapeDtypeStruct((b, s, h_b * HEAD_DIM), BF16),
        scratch_shapes=[
            pltpu.VMEM((FOX_HEADS, 1, FOX_TQ), F32),
            pltpu.VMEM((FOX_HEADS, 1, FOX_TQ), F32),
            pltpu.VMEM((FOX_HEADS, HEAD_DIM, FOX_TQ), F32),
            pltpu.VMEM((FOX_HEADS, FOX_TK, FOX_TQ), F32),
            pltpu.VMEM((FOX_HEADS, FOX_TK, FOX_TQ), F32),
        ],
        compiler_params=_params("parallel", "parallel", "arbitrary"),
        name="fox",
    )(q_aug, k_aug, v_t)


def _out_proj_kernel(a_ref, b_ref, w_ref, x_ref, o_ref):
    mixed = jnp.concatenate([a_ref[...], b_ref[...]], axis=-1)
    o_ref[...] = x_ref[...] + jnp.dot(mixed, w_ref[...].astype(BF16),
                                      preferred_element_type=F32)


OUT_ROW_TILE = 2048


def _out_proj(out_a, out_b, w_out, x):
    m = x.shape[0]
    return pl.pallas_call(
        _out_proj_kernel,
        grid=(m // OUT_ROW_TILE, D_MODEL // COL_TILE),
        in_specs=[
            pl.BlockSpec((OUT_ROW_TILE, D_A), lambda i, j: (i, 0)),
            pl.BlockSpec((OUT_ROW_TILE, D_B), lambda i, j: (i, 0)),
            pl.BlockSpec((D_A + D_B, COL_TILE), lambda i, j: (0, j)),
            pl.BlockSpec((OUT_ROW_TILE, COL_TILE), lambda i, j: (i, j)),
        ],
        out_specs=pl.BlockSpec((OUT_ROW_TILE, COL_TILE), lambda i, j: (i, j)),
        out_shape=jax.ShapeDtypeStruct((m, D_MODEL), F32),
        compiler_params=_params("parallel", "parallel"),
        name="out_proj",
    )(out_a, out_b, w_out, x)


def _mixer(x, b, s, mix_norm, w_in, layer, q_norm_a, k_norm_a, q_norm_b, k_norm_b, forget_bias,
           rel_bias_table, w_out):
    ones = jnp.ones((D_A,), F32)
    head_gain = jnp.concatenate([
        jnp.tile(q_norm_a * ATTN_SCALE, H_A), jnp.tile(k_norm_a, H_A), ones,
        jnp.tile(q_norm_b * (ATTN_SCALE * LOG2E), H_B), jnp.tile(k_norm_b, H_B), ones,
    ]).reshape(1, D_QKV)
    w_f = jnp.pad(w_in[layer, :, D_QKV:], ((0, 0), (0, LANES - H_B))).astype(BF16)
    f_bias = jnp.pad(forget_bias, (0, LANES - H_B)).reshape(1, LANES)
    *qkv_a, q_aug, k_aug, v_t = _proj(x, b, s, mix_norm, w_in, layer, w_f, f_bias, head_gain)

    out_b = _fox(q_aug, k_aug, v_t).reshape(b * s, D_B)

    bias = _band_bias(rel_bias_table)
    parts = [_dilated(qkv, bias, p) for p, qkv in enumerate(qkv_a)]
    out_a = _combine([pt[0] for pt in parts], [pt[1] for pt in parts])

    return _out_proj(out_a, out_b, w_out, x)


def kernel(x, ffn1_norm, ffn1_w_in, ffn1_w_out, mix_norm, w_in, q_norm_a, k_norm_a, q_norm_b,
           k_norm_b, forget_bias, rel_bias_table, w_out, ffn2_norm, ffn2_w_in, ffn2_w_out):
    b, s, d = x.shape
    depth = ffn1_norm.shape[0]
    x = x.reshape(b * s, d)
    for l in range(depth):
        x = _ffn(x, ffn1_norm[l], ffn1_w_in[l], ffn1_w_out[l])
        x = _mixer(x, b, s, mix_norm[l], w_in, l, q_norm_a[l], k_norm_a[l], q_norm_b[l],
                   k_norm_b[l], forget_bias[l], rel_bias_table, w_out[l])
        x = _ffn(x, ffn2_norm[l], ffn2_w_in[l], ffn2_w_out[l])
    return x.reshape(b, s, d)
```

```python
import functools
import math

import numpy as np
import jax
import jax.numpy as jnp
from jax import lax
from jax.experimental import pallas as pl
from jax.experimental.pallas import tpu as pltpu

D_MODEL = 2048
HEAD_DIM = 128
N_HEADS = D_MODEL // HEAD_DIM
H_A = N_HEADS // 2
H_B = N_HEADS - H_A
D_A = H_A * HEAD_DIM
D_B = H_B * HEAD_DIM
D_QKV = 3 * D_A + 3 * D_B
DILATED_PATTERNS = ((128, 1), (512, 4), (2048, 16))
BAND = 128
NUM_BUCKETS = 32
MAX_DISTANCE = 2048
D_FF = ((8 * D_MODEL // 3 + 127) // 128) * 128
RMS_EPS = 1e-6
NEG_INF = -1e30
ATTN_SCALE = HEAD_DIM ** -0.5

LANES = 128
FF_TILE = 512
FFN_ROW_TILE = 2048
FFN_CHUNK = 256
FFN_VMEM_LIMIT = 60 * 1024 * 1024
ROW_TILE = 1024
COL_TILE = 512
FOX_TQ = 512
FOX_TK = 512
FOX_HEADS = 4
VMEM_LIMIT = 56 * 1024 * 1024

F32 = jnp.float32
BF16 = jnp.bfloat16


def _params(*sem):
    return pltpu.CompilerParams(dimension_semantics=sem, vmem_limit_bytes=VMEM_LIMIT)


def _rms_rows(x, gain):
    ms = jnp.mean(x * x, axis=-1, keepdims=True)
    return x * lax.rsqrt(ms + RMS_EPS) * gain


def _ffn_kernel(x_hbm, g_ref, wg_ref, wu_ref, wo_ref, o_hbm, acc_ref, h_ref, in_sem, out_sem):
    i = pl.program_id(0)
    j = pl.program_id(1)
    last_i = pl.num_programs(0) - 1
    last_j = pl.num_programs(1) - 1
    pieces = [(r, c) for r in range(FFN_ROW_TILE // ROW_TILE) for c in range(D_MODEL // COL_TILE)]

    def x_copy(c):
        src = x_hbm.at[pl.ds(i * FFN_ROW_TILE + c * FFN_CHUNK, FFN_CHUNK), :]
        return pltpu.make_async_copy(src, acc_ref.at[pl.ds(c * FFN_CHUNK, FFN_CHUNK), :],
                                     in_sem.at[c])

    def out_copy(row_tile, r, c):
        rows, cols = pl.ds(r * ROW_TILE, ROW_TILE), pl.ds(c * COL_TILE, COL_TILE)
        dst = o_hbm.at[pl.ds(row_tile * FFN_ROW_TILE + r * ROW_TILE, ROW_TILE), cols]
        return pltpu.make_async_copy(acc_ref.at[rows, cols], dst,
                                     out_sem.at[r * (D_MODEL // COL_TILE) + c])

    @pl.when(j == 0)
    def _():
        @pl.when(i > 0)
        def _():
            for r, c in pieces:
                out_copy(i - 1, r, c).wait()

        n_chunks = FFN_ROW_TILE // FFN_CHUNK
        for c in range(n_chunks):
            x_copy(c).start()
        for c in range(n_chunks):
            x_copy(c).wait()
            rows = slice(c * FFN_CHUNK, (c + 1) * FFN_CHUNK)
            h_ref[rows, :] = _rms_rows(acc_ref[rows, :], g_ref[...]).astype(BF16)

    col = lax.broadcasted_iota(jnp.int32, (1, FF_TILE), 1)
    repeated = j * FF_TILE - jnp.minimum(j * FF_TILE, D_FF - FF_TILE)

    def tile(write_back):
        for r in range(FFN_ROW_TILE // ROW_TILE):
            rows = slice(r * ROW_TILE, (r + 1) * ROW_TILE)
            h = h_ref[rows, :]
            gate = jnp.dot(h, wg_ref[...].astype(BF16), preferred_element_type=F32)
            up = jnp.dot(h, wu_ref[...].astype(BF16), preferred_element_type=F32)
            act = gate * (1.0 / (1.0 + jnp.exp(-gate))) * (0.5 * up)
            act = jnp.where(col >= repeated, act, 0.0).astype(BF16)
            for c in range(D_MODEL // COL_TILE):
                sl = slice(c * COL_TILE, (c + 1) * COL_TILE)
                acc_ref[rows, sl] += jnp.dot(act, wo_ref[:, sl].astype(BF16),
                                             preferred_element_type=F32)
                if write_back:
                    out_copy(i, r, c).start()

    @pl.when(j < last_j)
    def _():
        tile(False)

    @pl.when(j == last_j)
    def _():
        tile(True)

        @pl.when(i == last_i)
        def _():
            for r, c in pieces:
                out_copy(i, r, c).wait()


def _ffn(x, gain, w_in, w_out):
    m = x.shape[0]
    n_ff = pl.cdiv(D_FF, FF_TILE)

    def ff_start(j):
        return pl.multiple_of(jnp.minimum(j * FF_TILE, D_FF - FF_TILE), LANES)

    return pl.pallas_call(
        _ffn_kernel,
        grid=(m // FFN_ROW_TILE, n_ff),
        in_specs=[
            pl.BlockSpec(memory_space=pl.ANY),
            pl.BlockSpec((1, D_MODEL), lambda i, j: (0, 0)),
            pl.BlockSpec((pl.Element(D_MODEL), pl.Element(FF_TILE)),
                         lambda i, j: (0, ff_start(j))),
            pl.BlockSpec((pl.Element(D_MODEL), pl.Element(FF_TILE)),
                         lambda i, j: (0, pl.multiple_of(D_FF + ff_start(j), LANES))),
            pl.BlockSpec((pl.Element(FF_TILE), pl.Element(D_MODEL)),
                         lambda i, j: (ff_start(j), 0)),
        ],
        out_specs=pl.BlockSpec(memory_space=pl.ANY),
        out_shape=jax.ShapeDtypeStruct((m, D_MODEL), F32),
        scratch_shapes=[
            pltpu.VMEM((FFN_ROW_TILE, D_MODEL), F32),
            pltpu.VMEM((FFN_ROW_TILE, D_MODEL), BF16),
            pltpu.SemaphoreType.DMA((FFN_ROW_TILE // FFN_CHUNK,)),
            pltpu.SemaphoreType.DMA(((FFN_ROW_TILE // ROW_TILE) * (D_MODEL // COL_TILE),)),
        ],
        compiler_params=pltpu.CompilerParams(
            dimension_semantics=("arbitrary", "arbitrary"), vmem_limit_bytes=FFN_VMEM_LIMIT),
        name="ffn",
    )(x, gain.reshape(1, D_MODEL), w_in, w_in, w_out)


HEADS_PER_TILE = COL_TILE // HEAD_DIM
N_PROJ_TILES = D_QKV // COL_TILE
TILES_PER_GROUP = D_A // COL_TILE
A_TILES = 3 * TILES_PER_GROUP
AUG = 2 * HEAD_DIM
LOG2E = math.log2(math.e)


def _log_gate_scan(z, carry):
    c = (jnp.minimum(z, 0.0) - jnp.log1p(jnp.exp(-jnp.abs(z)))) * LOG2E
    row = lax.broadcasted_iota(jnp.int32, c.shape, 0)
    shift = 1
    while shift < c.shape[0]:
        c = c + jnp.where(row >= shift, pltpu.roll(c, shift, axis=0), 0.0)
        shift *= 2
    return c + carry


def _split3(c):
    hi = c.astype(BF16)
    rest = c - hi.astype(F32)
    mid = rest.astype(BF16)
    lo = (rest - mid.astype(F32)).astype(BF16)
    return hi, mid, lo


def _head_mean_matrix():
    head = np.arange(COL_TILE) // HEAD_DIM
    return jnp.asarray((head[:, None] == head[None, :]) / HEAD_DIM, BF16)


def _gate_placement_matrices():
    mats = np.zeros((2, TILES_PER_GROUP, COL_TILE, COL_TILE), np.float32)
    ones_row = 3 * LANES
    for part in range(TILES_PER_GROUP):
        for hh in range(HEADS_PER_TILE):
            head = part * HEADS_PER_TILE + hh
            col = hh * HEAD_DIM
            for term in range(3):
                mats[0, part, term * LANES + head, col + term] = 1.0
                mats[0, part, ones_row, col + 3 + term] = 1.0
                mats[1, part, term * LANES + head, col + 3 + term] = -1.0
                mats[1, part, ones_row, col + term] = 1.0
    return jnp.asarray(mats.reshape(2 * TILES_PER_GROUP, COL_TILE, COL_TILE), BF16)


def _proj_kernel(x_ref, g_ref, w_ref, wf_ref, fb_ref, hg_ref, mean_ref, place_ref,
                 a1_ref, a4_ref, a16_ref, qb_ref, kb_ref, vt_ref,
                 h_ref, cs_ref, carry_ref, y_ref, yf_ref, *, tiles_per_seq):
    i = pl.program_id(0)
    j = pl.program_id(1)

    @pl.when(j == 0)
    def _():
        h_ref[...] = _rms_rows(x_ref[...], g_ref[...]).astype(BF16)

        @pl.when(i % tiles_per_seq == 0)
        def _():
            carry_ref[...] = jnp.zeros(carry_ref.shape, F32)

        z = jnp.dot(h_ref[...], wf_ref[...], preferred_element_type=F32) + fb_ref[...]
        c = _log_gate_scan(z, carry_ref[...])
        carry_ref[...] = c[ROW_TILE - 1:ROW_TILE, :]
        for term, part in enumerate(_split3(c)):
            cs_ref[:, term * LANES:(term + 1) * LANES] = part
        cs_ref[:, 3 * LANES:] = jnp.ones((ROW_TILE, LANES), BF16)

    def product(transposed=False):
        lhs, rhs = (w_ref[...].astype(BF16), h_ref[...])
        if not transposed:
            lhs, rhs = rhs, lhs
        return lax.dot_general(lhs, rhs, (((1,), (1,)), ((), ())), preferred_element_type=F32)

    def normed():
        r = product()
        ms = jnp.dot((r * r).astype(BF16), mean_ref[...], preferred_element_type=F32)
        return r * lax.rsqrt(ms + RMS_EPS) * hg_ref[...]

    def store_dilated(y):
        a1_ref[0, 0] = y.astype(BF16)
        fine, coarse = a4_ref.shape[1], a16_ref.shape[1]
        step = coarse // fine
        for c in range(HEADS_PER_TILE):
            sl = slice(c * LANES, (c + 1) * LANES)
            y_ref[c] = y[:, sl]
            for rf in range(fine):
                rows = y_ref[c, pl.ds(rf, ROW_TILE // fine, stride=fine), :]
                a4_ref[0, rf, :, sl] = rows.astype(BF16)
                yf_ref[c, rf] = rows
                for q in range(step):
                    rows = yf_ref[c, rf, pl.ds(q, ROW_TILE // coarse, stride=step), :]
                    a16_ref[0, rf + fine * q, :, sl] = rows.astype(BF16)

    def tiles(first_group, n_groups=1):
        lo = first_group * TILES_PER_GROUP
        return jnp.logical_and(j >= lo, j < lo + n_groups * TILES_PER_GROUP)

    def store_forgetting(out):
        y = normed().astype(BF16)
        aug = jnp.dot(cs_ref[...], place_ref[0], preferred_element_type=F32).astype(BF16)
        for hh in range(HEADS_PER_TILE):
            sl = slice(hh * HEAD_DIM, (hh + 1) * HEAD_DIM)
            out[0, hh, :, :HEAD_DIM] = y[:, sl]
            out[0, hh, :, HEAD_DIM:] = aug[:, sl]

    @pl.when(tiles(0, 2))
    def _():
        store_dilated(normed())

    @pl.when(tiles(2))
    def _():
        store_dilated(product())

    @pl.when(tiles(3))
    def _():
        store_forgetting(qb_ref)

    @pl.when(tiles(4))
    def _():
        store_forgetting(kb_ref)

    @pl.when(tiles(5))
    def _():
        v_t = product(transposed=True).astype(BF16)
        for hh in range(HEADS_PER_TILE):
            vt_ref[0, hh] = v_t[hh * HEAD_DIM:(hh + 1) * HEAD_DIM, :]


def _proj(x, b, s, gain, w_in, layer, w_f, f_bias, head_gain):
    m = x.shape[0]
    tps = s // ROW_TILE

    def a_spec(dilation):
        return pl.BlockSpec((1, dilation, ROW_TILE // dilation, COL_TILE),
                            lambda i, j: (i // tps, 0, i % tps, jnp.minimum(j, A_TILES - 1)))

    def head_block(first):
        return lambda i, j: (i // tps, jnp.clip(j - first, 0, TILES_PER_GROUP - 1), i % tps, 0)

    qb_first = A_TILES
    vt_first = A_TILES + 2 * TILES_PER_GROUP
    dilations = [d for _, d in DILATED_PATTERNS]
    return pl.pallas_call(
        functools.partial(_proj_kernel, tiles_per_seq=tps),
        grid=(m // ROW_TILE, N_PROJ_TILES),
        in_specs=[
            pl.BlockSpec((ROW_TILE, D_MODEL), lambda i, j: (i, 0)),
            pl.BlockSpec((1, D_MODEL), lambda i, j: (0, 0)),
            pl.BlockSpec((None, COL_TILE, D_MODEL), lambda i, j: (layer, j, 0)),
            pl.BlockSpec((D_MODEL, LANES), lambda i, j: (0, 0)),
            pl.BlockSpec((1, LANES), lambda i, j: (0, 0)),
            pl.BlockSpec((1, COL_TILE), lambda i, j: (0, j)),
            pl.BlockSpec((COL_TILE, COL_TILE), lambda i, j: (0, 0)),
            pl.BlockSpec((1, COL_TILE, COL_TILE),
                         lambda i, j: (jnp.clip(j - qb_first, 0, 2 * TILES_PER_GROUP - 1), 0, 0)),
        ],
        out_specs=[a_spec(d) for d in dilations] + [
            pl.BlockSpec((1, HEADS_PER_TILE, ROW_TILE, AUG), head_block(qb_first)),
            pl.BlockSpec((1, HEADS_PER_TILE, ROW_TILE, AUG),
                         head_block(qb_first + TILES_PER_GROUP)),
            pl.BlockSpec((1, HEADS_PER_TILE, HEAD_DIM, ROW_TILE),
                         lambda i, j: (i // tps, jnp.clip(j - vt_first, 0, TILES_PER_GROUP - 1),
                                       0, i % tps)),
        ],
        out_shape=[jax.ShapeDtypeStruct((b, d, s // d, 3 * D_A), BF16) for d in dilations] + [
            jax.ShapeDtypeStruct((b, H_B, s, AUG), BF16),
            jax.ShapeDtypeStruct((b, H_B, s, AUG), BF16),
            jax.ShapeDtypeStruct((b, H_B, HEAD_DIM, s), BF16),
        ],
        scratch_shapes=[
            pltpu.VMEM((ROW_TILE, D_MODEL), BF16),
            pltpu.VMEM((ROW_TILE, 4 * LANES), BF16),
            pltpu.VMEM((1, LANES), F32),
            pltpu.VMEM((HEADS_PER_TILE, ROW_TILE, LANES), F32),
            pltpu.VMEM((HEADS_PER_TILE, dilations[1], ROW_TILE // dilations[1], LANES), F32),
        ],
        compiler_params=_params("arbitrary", "arbitrary"),
        name="proj",
    )(x, gain.reshape(1, D_MODEL), jnp.swapaxes(w_in, 1, 2), w_f, f_bias, head_gain,
      _head_mean_matrix(), _gate_placement_matrices())


def _bucket_steps(dilation):
    dist = np.arange(BAND + 1) * dilation
    max_exact = NUM_BUCKETS // 2
    large = max_exact + np.floor(
        np.log(np.maximum(dist, 1) / max_exact) / math.log(MAX_DISTANCE / max_exact)
        * (NUM_BUCKETS - max_exact)).astype(np.int64)
    bucket = np.where(dist < max_exact, dist, np.minimum(large, NUM_BUCKETS - 1))
    steps = [(0, int(bucket[0]))]
    for delta in range(1, BAND + 1):
        if bucket[delta] != bucket[delta - 1]:
            steps.append((delta, int(bucket[delta])))
    return steps


def _bias_kernel(table_ref, o_ref):
    iq = lax.broadcasted_iota(jnp.int32, (BAND, 2 * BAND), 0)
    ik = lax.broadcasted_iota(jnp.int32, (BAND, 2 * BAND), 1)
    delta = iq + BAND - ik
    in_band = jnp.logical_and(delta >= 0, delta <= BAND)
    for p, (_, dilation) in enumerate(DILATED_PATTERNS):
        steps = _bucket_steps(dilation)
        for h in range(H_A):
            val = jnp.full((BAND, 2 * BAND), table_ref[steps[0][1], h], F32)
            for start, bucket in steps[1:]:
                val = jnp.where(delta >= start, table_ref[bucket, h], val)
            o_ref[p, h] = jnp.where(in_band, val, NEG_INF)


def _band_bias(rel_table):
    n_pat = len(DILATED_PATTERNS)
    return pl.pallas_call(
        _bias_kernel,
        in_specs=[pl.BlockSpec(memory_space=pltpu.SMEM)],
        out_specs=pl.BlockSpec(memory_space=pltpu.VMEM),
        out_shape=jax.ShapeDtypeStruct((n_pat, H_A, BAND, 2 * BAND), F32),
        name="band_bias",
    )(rel_table)


DIL_UNITS = 8


def _dilated_kernel(q_ref, kp_ref, kc_ref, vp_ref, vc_ref, bias_ref, o_ref, st_ref):
    n_cls = q_ref.shape[1]
    n_blk = q_ref.shape[2] // BAND
    first = pl.program_id(2) == 0
    key_lane = lax.broadcasted_iota(jnp.int32, (1, 2 * BAND), 1)
    no_prev = jnp.where(jnp.logical_and(first, key_lane < BAND), NEG_INF, 0.0)
    stat_lane = lax.broadcasted_iota(jnp.int32, (BAND, LANES), 1)
    units = [(c, blk) for c in range(n_cls) for blk in range(n_blk)]

    def keys(prev_ref, cur_ref, c, blk, sl):
        if blk == 0:
            return jnp.concatenate([prev_ref[0, c, :, sl], cur_ref[0, c, :BAND, sl]], axis=0)
        return cur_ref[0, c, (blk - 1) * BAND:(blk + 1) * BAND, sl]

    scores = {}
    for c, blk in units:
        for h in range(H_A):
            sl = slice(h * HEAD_DIM, (h + 1) * HEAD_DIM)
            q = q_ref[0, c, blk * BAND:(blk + 1) * BAND, sl]
            scores[c, blk, h] = lax.dot_general(q, keys(kp_ref, kc_ref, c, blk, sl),
                                                (((1,), (1,)), ((), ())),
                                                preferred_element_type=F32)
    for c, blk in units:
        rows = slice(blk * BAND, (blk + 1) * BAND)
        stats = jnp.zeros((BAND, LANES), F32)
        for h in range(H_A):
            sl = slice(h * HEAD_DIM, (h + 1) * HEAD_DIM)
            s = scores[c, blk, h] + bias_ref[0, h]
            if blk == 0:
                s = s + no_prev
            mx = jnp.max(s, axis=-1, keepdims=True)
            p = jnp.exp(s - mx)
            den = jnp.sum(p, axis=-1, keepdims=True)
            o_ref[0, c, rows, sl] = jnp.dot(p.astype(BF16), keys(vp_ref, vc_ref, c, blk, sl),
                                            preferred_element_type=F32).astype(o_ref.dtype)
            stats = jnp.where(stat_lane == h, mx, stats)
            stats = jnp.where(stat_lane == H_A + h, den, stats)
        st_ref[0, c, rows, :] = stats


def _dilated(qkv, bias, pattern):
    b, dilation, n_sub, _ = qkv.shape
    n_blk = min(DIL_UNITS, n_sub // BAND)
    n_cls = DIL_UNITS // n_blk
    rows = n_blk * BAND
    nb = n_sub // rows

    def cur(which):
        return pl.BlockSpec((1, n_cls, rows, D_A), lambda bi, r, i: (bi, r, i, which))

    def prev(which):
        return pl.BlockSpec((1, n_cls, BAND, D_A),
                            lambda bi, r, i: (bi, r, jnp.maximum(i * n_blk - 1, 0), which))

    return pl.pallas_call(
        _dilated_kernel,
        grid=(b, dilation // n_cls, nb),
        in_specs=[
            cur(0), prev(1), cur(1), prev(2), cur(2),
            pl.BlockSpec((1, H_A, BAND, 2 * BAND), lambda bi, r, i: (pattern, 0, 0, 0)),
        ],
        out_specs=[
            pl.BlockSpec((1, n_cls, rows, D_A), lambda bi, r, i: (bi, r, i, 0)),
            pl.BlockSpec((1, n_cls, rows, LANES), lambda bi, r, i: (bi, r, i, 0)),
        ],
        out_shape=[
            jax.ShapeDtypeStruct((b, dilation, n_sub, D_A), BF16),
            jax.ShapeDtypeStruct((b, dilation, n_sub, LANES), F32),
        ],
        compiler_params=_params("parallel", "parallel", "arbitrary"),
        name=f"dilated_{dilation}",
    )(qkv, qkv, qkv, qkv, qkv, bias)


COMBINE_ROWS = 512


def _combine_kernel(*refs):
    n_pat = len(DILATED_PATTERNS)
    num_refs, st_refs = refs[:n_pat], refs[n_pat:2 * n_pat]
    o_ref, num_buf, st_buf, num_tmp, st_tmp = refs[2 * n_pat:]
    dilations = [ref.shape[1] for ref in num_refs]
    fine = min(d for d in dilations if d > 1)
    heads = [slice(h * HEAD_DIM, (h + 1) * HEAD_DIM) for h in range(H_A)]
    for p, dilation in enumerate(dilations):
        if dilation == 1:
            continue
        step = dilation // fine
        for rf in range(fine):
            rows = pl.ds(rf, COMBINE_ROWS // fine, stride=fine)
            if step == 1:
                st_buf[p, rows, :] = st_refs[p][0, rf]
                for h, sl in enumerate(heads):
                    num_buf[p, h, rows, :] = num_refs[p][0, rf, :, sl].astype(F32)
                continue
            for q in range(step):
                part = pl.ds(q, COMBINE_ROWS // dilation, stride=step)
                st_tmp[part, :] = st_refs[p][0, rf + fine * q]
                for h, sl in enumerate(heads):
                    num_tmp[h, part, :] = num_refs[p][0, rf + fine * q, :, sl].astype(F32)
            st_buf[p, rows, :] = st_tmp[...]
            for h in range(H_A):
                num_buf[p, h, rows, :] = num_tmp[h]
    stats = [st_refs[p][0, 0] if d == 1 else st_buf[p] for p, d in enumerate(dilations)]
    for h in range(H_A):
        sl = slice(h * HEAD_DIM, (h + 1) * HEAD_DIM)
        mxs = [st[:, h:h + 1] for st in stats]
        dens = [st[:, H_A + h:H_A + h + 1] for st in stats]
        mx = functools.reduce(jnp.maximum, mxs)
        num = None
        den = None
        for p, (pden, pmx) in enumerate(zip(dens, mxs)):
            scale = jnp.exp(pmx - mx)
            if dilations[p] == 1:
                pnum = num_refs[p][0, 0, :, sl].astype(F32)
            else:
                pnum = num_buf[p, h]
            num = pnum * scale if num is None else num + pnum * scale
            den = pden * scale if den is None else den + pden * scale
        o_ref[:, sl] = (num / den).astype(BF16)


def _combine(nums, stats):
    b, _, s, _ = nums[0].shape
    tps = s // COMBINE_ROWS

    def spec(arr):
        dilation, width = arr.shape[1], arr.shape[3]
        return pl.BlockSpec((1, dilation, COMBINE_ROWS // dilation, width),
                            lambda i: (i // tps, 0, i % tps, 0))

    n_pat = len(nums)
    fine = min(a.shape[1] for a in nums if a.shape[1] > 1)
    return pl.pallas_call(
        _combine_kernel,
        grid=(b * tps,),
        in_specs=[spec(a) for a in nums] + [spec(a) for a in stats],
        out_specs=pl.BlockSpec((COMBINE_ROWS, D_A), lambda i: (i, 0)),
        out_shape=jax.ShapeDtypeStruct((b * s, D_A), BF16),
        scratch_shapes=[
            pltpu.VMEM((n_pat, H_A, COMBINE_ROWS, HEAD_DIM), F32),
            pltpu.VMEM((n_pat, COMBINE_ROWS, LANES), F32),
            pltpu.VMEM((H_A, COMBINE_ROWS // fine, HEAD_DIM), F32),
            pltpu.VMEM((COMBINE_ROWS // fine, LANES), F32),
        ],
        compiler_params=_params("parallel"),
        name="combine_a",
    )(*nums, *stats)


def _fox_kernel(q_ref, k_ref, vt_ref, o_ref, m_ref, l_ref, acc_ref, sa_ref, sb_ref):
    iq = pl.program_id(2)
    m_ref[...] = jnp.full(m_ref.shape, NEG_INF, F32)
    l_ref[...] = jnp.zeros(l_ref.shape, F32)
    acc_ref[...] = jnp.zeros(acc_ref.shape, F32)

    def scores_into(s_ref, kb):
        start = pl.multiple_of(kb * FOX_TK, FOX_TK)
        for hh in range(FOX_HEADS):
            k = k_ref[0, hh, pl.ds(start, FOX_TK), :]
            s_ref[hh] = lax.dot_general(k, q_ref[0, hh], (((1,), (1,)), ((), ())),
                                        preferred_element_type=F32)

    def accumulate(s_ref, kb, diagonal):
        start = pl.multiple_of(kb * FOX_TK, FOX_TK)
        for hh in range(FOX_HEADS):
            st = s_ref[hh]
            if diagonal:
                key = lax.broadcasted_iota(jnp.int32, st.shape, 0)
                qry = lax.broadcasted_iota(jnp.int32, st.shape, 1)
                st = jnp.where(key <= qry, st, NEG_INF)
            m_prev = m_ref[hh]
            m_new = jnp.maximum(m_prev, jnp.max(st, axis=0, keepdims=True))
            alpha = jnp.exp2(m_prev - m_new)
            p = jnp.exp2(st - m_new)
            l_ref[hh] = alpha * l_ref[hh] + jnp.sum(p, axis=0, keepdims=True)
            vt = vt_ref[0, hh, :, pl.ds(start, FOX_TK)]
            acc_ref[hh] = alpha * acc_ref[hh] + jnp.dot(vt, p.astype(BF16),
                                                        preferred_element_type=F32)
            m_ref[hh] = m_new

    scores_into(sa_ref, 0)

    def pair(p, carry):
        scores_into(sb_ref, 2 * p + 1)
        accumulate(sa_ref, 2 * p, False)
        scores_into(sa_ref, 2 * p + 2)
        accumulate(sb_ref, 2 * p + 1, False)
        return carry

    lax.fori_loop(0, iq // 2, pair, 0)

    @pl.when(iq % 2 == 0)
    def _():
        accumulate(sa_ref, iq, True)

    @pl.when(iq % 2 == 1)
    def _():
        scores_into(sb_ref, iq)
        accumulate(sa_ref, iq - 1, False)
        accumulate(sb_ref, iq, True)
    for hh in range(FOX_HEADS):
        o_ref[0, :, hh * HEAD_DIM:(hh + 1) * HEAD_DIM] = (
            acc_ref[hh] / l_ref[hh]).T.astype(BF16)


def _fox(q_aug, k_aug, v_t):
    b, h_b, s, _ = q_aug.shape
    assert FOX_TQ == FOX_TK
    return pl.pallas_call(
        _fox_kernel,
        grid=(b, h_b // FOX_HEADS, s // FOX_TQ),
        in_specs=[
            pl.BlockSpec((1, FOX_HEADS, FOX_TQ, AUG), lambda bi, h, i: (bi, h, i, 0)),
            pl.BlockSpec((1, FOX_HEADS, s, AUG), lambda bi, h, i: (bi, h, 0, 0)),
            pl.BlockSpec((1, FOX_HEADS, HEAD_DIM, s), lambda bi, h, i: (bi, h, 0, 0)),
        ],
        out_specs=pl.BlockSpec((1, FOX_TQ, FOX_HEADS * HEAD_DIM), lambda bi, h, i: (bi, i, h)),
        out_shape=jax.ShapeDtypeStruct((b, s, h_b * HEAD_DIM), BF16),
        scratch_shapes=[
            pltpu.VMEM((FOX_HEADS, 1, FOX_TQ), F32),
            pltpu.VMEM((FOX_HEADS, 1, FOX_TQ), F32),
            pltpu.VMEM((FOX_HEADS, HEAD_DIM, FOX_TQ), F32),
            pltpu.VMEM((FOX_HEADS, FOX_TK, FOX_TQ), F32),
            pltpu.VMEM((FOX_HEADS, FOX_TK, FOX_TQ), F32),
        ],
        compiler_params=_params("parallel", "parallel", "arbitrary"),
        name="fox",
    )(q_aug, k_aug, v_t)


def _out_proj_kernel(a_ref, b_ref, w_ref, x_ref, o_ref):
    mixed = jnp.concatenate([a_ref[...], b_ref[...]], axis=-1)
    o_ref[...] = x_ref[...] + jnp.dot(mixed, w_ref[...].astype(BF16),
                                      preferred_element_type=F32)


OUT_ROW_TILE = 2048


def _out_proj(out_a, out_b, w_out, x):
    m = x.shape[0]
    return pl.pallas_call(
        _out_proj_kernel,
        grid=(m // OUT_ROW_TILE, D_MODEL // COL_TILE),
        in_specs=[
            pl.BlockSpec((OUT_ROW_TILE, D_A), lambda i, j: (i, 0)),
            pl.BlockSpec((OUT_ROW_TILE, D_B), lambda i, j: (i, 0)),
            pl.BlockSpec((D_A + D_B, COL_TILE), lambda i, j: (0, j)),
            pl.BlockSpec((OUT_ROW_TILE, COL_TILE), lambda i, j: (i, j)),
        ],
        out_specs=pl.BlockSpec((OUT_ROW_TILE, COL_TILE), lambda i, j: (i, j)),
        out_shape=jax.ShapeDtypeStruct((m, D_MODEL), F32),
        compiler_params=_params("parallel", "parallel"),
        name="out_proj",
    )(out_a, out_b, w_out, x)


def _mixer(x, b, s, mix_norm, w_in, layer, q_norm_a, k_norm_a, q_norm_b, k_norm_b, forget_bias,
           rel_bias_table, w_out):
    ones = jnp.ones((D_A,), F32)
    head_gain = jnp.concatenate([
        jnp.tile(q_norm_a * ATTN_SCALE, H_A), jnp.tile(k_norm_a, H_A), ones,
        jnp.tile(q_norm_b * (ATTN_SCALE * LOG2E), H_B), jnp.tile(k_norm_b, H_B), ones,
    ]).reshape(1, D_QKV)
    w_f = jnp.pad(w_in[layer, :, D_QKV:], ((0, 0), (0, LANES - H_B))).astype(BF16)
    f_bias = jnp.pad(forget_bias, (0, LANES - H_B)).reshape(1, LANES)
    *qkv_a, q_aug, k_aug, v_t = _proj(x, b, s, mix_norm, w_in, layer, w_f, f_bias, head_gain)

    out_b = _fox(q_aug, k_aug, v_t).reshape(b * s, D_B)

    bias = _band_bias(rel_bias_table)
    parts = [_dilated(qkv, bias, p) for p, qkv in enumerate(qkv_a)]
    out_a = _combine([pt[0] for pt in parts], [pt[1] for pt in parts])

    return _out_proj(out_a, out_b, w_out, x)


def kernel(x, ffn1_norm, ffn1_w_in, ffn1_w_out, mix_norm, w_in, q_norm_a, k_norm_a, q_norm_b,
           k_norm_b, forget_bias, rel_bias_table, w_out, ffn2_norm, ffn2_w_in, ffn2_w_out):
    b, s, d = x.shape
    depth = ffn1_norm.shape[0]
    x = x.reshape(b * s, d)
    for l in range(depth):
        x = _ffn(x, ffn1_norm[l], ffn1_w_in[l], ffn1_w_out[l])
        x = _mixer(x, b, s, mix_norm[l], w_in, l, q_norm_a[l], k_norm_a[l], q_norm_b[l],
                   k_norm_b[l], forget_bias[l], rel_bias_table, w_out[l])
        x = _ffn(x, ffn2_norm[l], ffn2_w_in[l], ffn2_w_out[l])
    return x.reshape(b, s, d)
```

```python
import functools
import math

import numpy as np
import jax
import jax.numpy as jnp
from jax import lax
from jax.experimental import pallas as pl
from jax.experimental.pallas import tpu as pltpu

D_MODEL = 2048
HEAD_DIM = 128
N_HEADS = D_MODEL // HEAD_DIM
H_A = N_HEADS // 2
H_B = N_HEADS - H_A
D_A = H_A * HEAD_DIM
D_B = H_B * HEAD_DIM
D_QKV = 3 * D_A + 3 * D_B
DILATED_PATTERNS = ((128, 1), (512, 4), (2048, 16))
BAND = 128
NUM_BUCKETS = 32
MAX_DISTANCE = 2048
D_FF = ((8 * D_MODEL // 3 + 127) // 128) * 128
RMS_EPS = 1e-6
NEG_INF = -1e30
ATTN_SCALE = HEAD_DIM ** -0.5

LANES = 128
FF_TILE = 512
FFN_ROW_TILE = 2048
FFN_CHUNK = 256
FFN_VMEM_LIMIT = 60 * 1024 * 1024
ROW_TILE = 1024
COL_TILE = 512
FOX_TQ = 512
FOX_TK = 512
FOX_HEADS = 4
VMEM_LIMIT = 56 * 1024 * 1024

F32 = jnp.float32
BF16 = jnp.bfloat16


def _params(*sem):
    return pltpu.CompilerParams(dimension_semantics=sem, vmem_limit_bytes=VMEM_LIMIT)


def _rms_rows(x, gain):
    ms = jnp.mean(x * x, axis=-1, keepdims=True)
    return x * lax.rsqrt(ms + RMS_EPS) * gain


def _ffn_kernel(x_hbm, g_ref, wg_ref, wu_ref, wo_ref, o_hbm, acc_ref, h_ref, in_sem, out_sem):
    i = pl.program_id(0)
    j = pl.program_id(1)
    last_i = pl.num_programs(0) - 1
    last_j = pl.num_programs(1) - 1
    pieces = [(r, c) for r in range(FFN_ROW_TILE // ROW_TILE) for c in range(D_MODEL // COL_TILE)]

    def x_copy(c):
        src = x_hbm.at[pl.ds(i * FFN_ROW_TILE + c * FFN_CHUNK, FFN_CHUNK), :]
        return pltpu.make_async_copy(src, acc_ref.at[pl.ds(c * FFN_CHUNK, FFN_CHUNK), :],
                                     in_sem.at[c])

    def out_copy(row_tile, r, c):
        rows, cols = pl.ds(r * ROW_TILE, ROW_TILE), pl.ds(c * COL_TILE, COL_TILE)
        dst = o_hbm.at[pl.ds(row_tile * FFN_ROW_TILE + r * ROW_TILE, ROW_TILE), cols]
        return pltpu.make_async_copy(acc_ref.at[rows, cols], dst,
                                     out_sem.at[r * (D_MODEL // COL_TILE) + c])

    @pl.when(j == 0)
    def _():
        @pl.when(i > 0)
        def _():
            for r, c in pieces:
                out_copy(i - 1, r, c).wait()

        n_chunks = FFN_ROW_TILE // FFN_CHUNK
        for c in range(n_chunks):
            x_copy(c).start()
        for c in range(n_chunks):
            x_copy(c).wait()
            rows = slice(c * FFN_CHUNK, (c + 1) * FFN_CHUNK)
            h_ref[rows, :] = _rms_rows(acc_ref[rows, :], g_ref[...]).astype(BF16)

    col = lax.broadcasted_iota(jnp.int32, (1, FF_TILE), 1)
    repeated = j * FF_TILE - jnp.minimum(j * FF_TILE, D_FF - FF_TILE)

    def tile(write_back):
        for r in range(FFN_ROW_TILE // ROW_TILE):
            rows = slice(r * ROW_TILE, (r + 1) * ROW_TILE)
            h = h_ref[rows, :]
            gate = jnp.dot(h, wg_ref[...].astype(BF16), preferred_element_type=F32)
            up = jnp.dot(h, wu_ref[...].astype(BF16), preferred_element_type=F32)
            act = gate * (1.0 / (1.0 + jnp.exp(-gate))) * (0.5 * up)
            act = jnp.where(col >= repeated, act, 0.0).astype(BF16)
            for c in range(D_MODEL // COL_TILE):
                sl = slice(c * COL_TILE, (c + 1) * COL_TILE)
                acc_ref[rows, sl] += jnp.dot(act, wo_ref[:, sl].astype(BF16),
                                             preferred_element_type=F32)
                if write_back:
                    out_copy(i, r, c).start()

    @pl.when(j < last_j)
    def _():
        tile(False)

    @pl.when(j == last_j)
    def _():
        tile(True)

        @pl.when(i == last_i)
        def _():
            for r, c in pieces:
                out_copy(i, r, c).wait()


def _ffn(x, gain, w_in, w_out):
    m = x.shape[0]
    n_ff = pl.cdiv(D_FF, FF_TILE)

    def ff_start(j):
        return pl.multiple_of(jnp.minimum(j * FF_TILE, D_FF - FF_TILE), LANES)

    return pl.pallas_call(
        _ffn_kernel,
        grid=(m // FFN_ROW_TILE, n_ff),
        in_specs=[
            pl.BlockSpec(memory_space=pl.ANY),
            pl.BlockSpec((1, D_MODEL), lambda i, j: (0, 0)),
            pl.BlockSpec((pl.Element(D_MODEL), pl.Element(FF_TILE)),
                         lambda i, j: (0, ff_start(j))),
            pl.BlockSpec((pl.Element(D_MODEL), pl.Element(FF_TILE)),
                         lambda i, j: (0, pl.multiple_of(D_FF + ff_start(j), LANES))),
            pl.BlockSpec((pl.Element(FF_TILE), pl.Element(D_MODEL)),
                         lambda i, j: (ff_start(j), 0)),
        ],
        out_specs=pl.BlockSpec(memory_space=pl.ANY),
        out_shape=jax.ShapeDtypeStruct((m, D_MODEL), F32),
        scratch_shapes=[
            pltpu.VMEM((FFN_ROW_TILE, D_MODEL), F32),
            pltpu.VMEM((FFN_ROW_TILE, D_MODEL), BF16),
            pltpu.SemaphoreType.DMA((FFN_ROW_TILE // FFN_CHUNK,)),
            pltpu.SemaphoreType.DMA(((FFN_ROW_TILE // ROW_TILE) * (D_MODEL // COL_TILE),)),
        ],
        compiler_params=pltpu.CompilerParams(
            dimension_semantics=("arbitrary", "arbitrary"), vmem_limit_bytes=FFN_VMEM_LIMIT),
        name="ffn",
    )(x, gain.reshape(1, D_MODEL), w_in, w_in, w_out)


HEADS_PER_TILE = COL_TILE // HEAD_DIM
N_PROJ_TILES = D_QKV // COL_TILE
TILES_PER_GROUP = D_A // COL_TILE
A_TILES = 3 * TILES_PER_GROUP
AUG = 2 * HEAD_DIM
LOG2E = math.log2(math.e)


def _log_gate_scan(z, carry):
    c = (jnp.minimum(z, 0.0) - jnp.log1p(jnp.exp(-jnp.abs(z)))) * LOG2E
    row = lax.broadcasted_iota(jnp.int32, c.shape, 0)
    shift = 1
    while shift < c.shape[0]:
        c = c + jnp.where(row >= shift, pltpu.roll(c, shift, axis=0), 0.0)
        shift *= 2
    return c + carry


def _split3(c):
    hi = c.astype(BF16)
    rest = c - hi.astype(F32)
    mid = rest.astype(BF16)
    lo = (rest - mid.astype(F32)).astype(BF16)
    return hi, mid, lo


def _head_mean_matrix():
    head = np.arange(COL_TILE) // HEAD_DIM
    return jnp.asarray((head[:, None] == head[None, :]) / HEAD_DIM, BF16)


GATE_ONES_LANE = 3 * H_B


def _gate_placement_matrices():
    mats = np.zeros((2, TILES_PER_GROUP, LANES, COL_TILE), np.float32)
    for part in range(TILES_PER_GROUP):
        for hh in range(HEADS_PER_TILE):
            head = part * HEADS_PER_TILE + hh
            col = hh * HEAD_DIM
            for term in range(3):
                mats[0, part, term * H_B + head, col + term] = 1.0
                mats[0, part, GATE_ONES_LANE, col + 3 + term] = 1.0
                mats[1, part, term * H_B + head, col + 3 + term] = -1.0
                mats[1, part, GATE_ONES_LANE, col + term] = 1.0
    return jnp.asarray(mats.reshape(2 * TILES_PER_GROUP, LANES, COL_TILE), BF16)


def _proj_kernel(x_ref, g_ref, w_ref, wf_ref, fb_ref, hg_ref, mean_ref, place_ref,
                 a1_ref, a4_ref, a16_ref, qb_ref, kb_ref, vt_ref,
                 h_ref, cs_ref, carry_ref, y_ref, yf_ref, *, tiles_per_seq):
    i = pl.program_id(0)
    j = pl.program_id(1)

    @pl.when(j == 0)
    def _():
        h_ref[...] = _rms_rows(x_ref[...], g_ref[...]).astype(BF16)

        @pl.when(i % tiles_per_seq == 0)
        def _():
            carry_ref[...] = jnp.zeros(carry_ref.shape, F32)

        z = jnp.dot(h_ref[...], wf_ref[...], preferred_element_type=F32) + fb_ref[...]
        c = _log_gate_scan(z, carry_ref[...])
        carry_ref[...] = c[ROW_TILE - 1:ROW_TILE, :]
        lane = lax.broadcasted_iota(jnp.int32, c.shape, 1)
        packed = jnp.where(lane == GATE_ONES_LANE, 1.0, 0.0)
        for term, part in enumerate(_split3(c)):
            part = jnp.where(lane < H_B, part.astype(F32), 0.0)
            packed = packed + (pltpu.roll(part, term * H_B, axis=1) if term else part)
        cs_ref[...] = packed.astype(BF16)

    def product(transposed=False):
        lhs, rhs = (w_ref[...].astype(BF16), h_ref[...])
        if not transposed:
            lhs, rhs = rhs, lhs
        return lax.dot_general(lhs, rhs, (((1,), (1,)), ((), ())), preferred_element_type=F32)

    def normed():
        r = product()
        ms = jnp.dot((r * r).astype(BF16), mean_ref[...], preferred_element_type=F32)
        return r * lax.rsqrt(ms + RMS_EPS) * hg_ref[...]

    def store_dilated(y):
        a1_ref[0, 0] = y.astype(BF16)
        fine, coarse = a4_ref.shape[1], a16_ref.shape[1]
        step = coarse // fine
        for c in range(HEADS_PER_TILE):
            sl = slice(c * LANES, (c + 1) * LANES)
            y_ref[c] = y[:, sl]
            for rf in range(fine):
                rows = y_ref[c, pl.ds(rf, ROW_TILE // fine, stride=fine), :]
                a4_ref[0, rf, :, sl] = rows.astype(BF16)
                yf_ref[c, rf] = rows
                for q in range(step):
                    rows = yf_ref[c, rf, pl.ds(q, ROW_TILE // coarse, stride=step), :]
                    a16_ref[0, rf + fine * q, :, sl] = rows.astype(BF16)

    def tiles(first_group, n_groups=1):
        lo = first_group * TILES_PER_GROUP
        return jnp.logical_and(j >= lo, j < lo + n_groups * TILES_PER_GROUP)

    def store_forgetting(out):
        y = normed().astype(BF16)
        aug = jnp.dot(cs_ref[...], place_ref[0], preferred_element_type=F32).astype(BF16)
        for hh in range(HEADS_PER_TILE):
            sl = slice(hh * HEAD_DIM, (hh + 1) * HEAD_DIM)
            out[0, hh, :, :HEAD_DIM] = y[:, sl]
            out[0, hh, :, HEAD_DIM:] = aug[:, sl]

    @pl.when(tiles(0, 2))
    def _():
        store_dilated(normed())

    @pl.when(tiles(2))
    def _():
        store_dilated(product())

    @pl.when(tiles(3))
    def _():
        store_forgetting(qb_ref)

    @pl.when(tiles(4))
    def _():
        store_forgetting(kb_ref)

    @pl.when(tiles(5))
    def _():
        v_t = product(transposed=True).astype(BF16)
        for hh in range(HEADS_PER_TILE):
            vt_ref[0, hh] = v_t[hh * HEAD_DIM:(hh + 1) * HEAD_DIM, :]


def _proj(x, b, s, gain, w_in, layer, w_f, f_bias, head_gain):
    m = x.shape[0]
    tps = s // ROW_TILE

    def a_spec(dilation):
        return pl.BlockSpec((1, dilation, ROW_TILE // dilation, COL_TILE),
                            lambda i, j: (i // tps, 0, i % tps, jnp.minimum(j, A_TILES - 1)))

    def head_block(first):
        return lambda i, j: (i // tps, jnp.clip(j - first, 0, TILES_PER_GROUP - 1), i % tps, 0)

    qb_first = A_TILES
    vt_first = A_TILES + 2 * TILES_PER_GROUP
    dilations = [d for _, d in DILATED_PATTERNS]
    return pl.pallas_call(
        functools.partial(_proj_kernel, tiles_per_seq=tps),
        grid=(m // ROW_TILE, N_PROJ_TILES),
        in_specs=[
            pl.BlockSpec((ROW_TILE, D_MODEL), lambda i, j: (i, 0)),
            pl.BlockSpec((1, D_MODEL), lambda i, j: (0, 0)),
            pl.BlockSpec((None, COL_TILE, D_MODEL), lambda i, j: (layer, j, 0)),
            pl.BlockSpec((D_MODEL, LANES), lambda i, j: (0, 0)),
            pl.BlockSpec((1, LANES), lambda i, j: (0, 0)),
            pl.BlockSpec((1, COL_TILE), lambda i, j: (0, j)),
            pl.BlockSpec((COL_TILE, COL_TILE), lambda i, j: (0, 0)),
            pl.BlockSpec((1, LANES, COL_TILE),
                         lambda i, j: (jnp.clip(j - qb_first, 0, 2 * TILES_PER_GROUP - 1), 0, 0)),
        ],
        out_specs=[a_spec(d) for d in dilations] + [
            pl.BlockSpec((1, HEADS_PER_TILE, ROW_TILE, AUG), head_block(qb_first)),
            pl.BlockSpec((1, HEADS_PER_TILE, ROW_TILE, AUG),
                         head_block(qb_first + TILES_PER_GROUP)),
            pl.BlockSpec((1, HEADS_PER_TILE, HEAD_DIM, ROW_TILE),
                         lambda i, j: (i // tps, jnp.clip(j - vt_first, 0, TILES_PER_GROUP - 1),
                                       0, i % tps)),
        ],
        out_shape=[jax.ShapeDtypeStruct((b, d, s // d, 3 * D_A), BF16) for d in dilations] + [
            jax.ShapeDtypeStruct((b, H_B, s, AUG), BF16),
            jax.ShapeDtypeStruct((b, H_B, s, AUG), BF16),
            jax.ShapeDtypeStruct((b, H_B, HEAD_DIM, s), BF16),
        ],
        scratch_shapes=[
            pltpu.VMEM((ROW_TILE, D_MODEL), BF16),
            pltpu.VMEM((ROW_TILE, LANES), BF16),
            pltpu.VMEM((1, LANES), F32),
            pltpu.VMEM((HEADS_PER_TILE, ROW_TILE, LANES), F32),
            pltpu.VMEM((HEADS_PER_TILE, dilations[1], ROW_TILE // dilations[1], LANES), F32),
        ],
        compiler_params=_params("arbitrary", "arbitrary"),
        name="proj",
    )(x, gain.reshape(1, D_MODEL), jnp.swapaxes(w_in, 1, 2), w_f, f_bias, head_gain,
      _head_mean_matrix(), _gate_placement_matrices())


def _bucket_steps(dilation):
    dist = np.arange(BAND + 1) * dilation
    max_exact = NUM_BUCKETS // 2
    large = max_exact + np.floor(
        np.log(np.maximum(dist, 1) / max_exact) / math.log(MAX_DISTANCE / max_exact)
        * (NUM_BUCKETS - max_exact)).astype(np.int64)
    bucket = np.where(dist < max_exact, dist, np.minimum(large, NUM_BUCKETS - 1))
    steps = [(0, int(bucket[0]))]
    for delta in range(1, BAND + 1):
        if bucket[delta] != bucket[delta - 1]:
            steps.append((delta, int(bucket[delta])))
    return steps


def _bias_kernel(table_ref, o_ref):
    iq = lax.broadcasted_iota(jnp.int32, (BAND, 2 * BAND), 0)
    ik = lax.broadcasted_iota(jnp.int32, (BAND, 2 * BAND), 1)
    delta = iq + BAND - ik
    in_band = jnp.logical_and(delta >= 0, delta <= BAND)
    for p, (_, dilation) in enumerate(DILATED_PATTERNS):
        steps = _bucket_steps(dilation)
        for h in range(H_A):
            val = jnp.full((BAND, 2 * BAND), table_ref[steps[0][1], h], F32)
            for start, bucket in steps[1:]:
                val = jnp.where(delta >= start, table_ref[bucket, h], val)
            o_ref[p, h] = jnp.where(in_band, val, NEG_INF)


def _band_bias(rel_table):
    n_pat = len(DILATED_PATTERNS)
    return pl.pallas_call(
        _bias_kernel,
        in_specs=[pl.BlockSpec(memory_space=pltpu.SMEM)],
        out_specs=pl.BlockSpec(memory_space=pltpu.VMEM),
        out_shape=jax.ShapeDtypeStruct((n_pat, H_A, BAND, 2 * BAND), F32),
        name="band_bias",
    )(rel_table)


DIL_UNITS = 8


def _dilated_kernel(q_ref, kp_ref, kc_ref, vp_ref, vc_ref, bias_ref, o_ref, st_ref):
    n_cls = q_ref.shape[1]
    n_blk = q_ref.shape[2] // BAND
    first = pl.program_id(2) == 0
    key_lane = lax.broadcasted_iota(jnp.int32, (1, 2 * BAND), 1)
    no_prev = jnp.where(jnp.logical_and(first, key_lane < BAND), NEG_INF, 0.0)
    stat_lane = lax.broadcasted_iota(jnp.int32, (BAND, LANES), 1)
    units = [(c, blk) for c in range(n_cls) for blk in range(n_blk)]

    def keys(prev_ref, cur_ref, c, blk, sl):
        if blk == 0:
            return jnp.concatenate([prev_ref[0, c, :, sl], cur_ref[0, c, :BAND, sl]], axis=0)
        return cur_ref[0, c, (blk - 1) * BAND:(blk + 1) * BAND, sl]

    scores = {}
    for c, blk in units:
        for h in range(H_A):
            sl = slice(h * HEAD_DIM, (h + 1) * HEAD_DIM)
            q = q_ref[0, c, blk * BAND:(blk + 1) * BAND, sl]
            scores[c, blk, h] = lax.dot_general(q, keys(kp_ref, kc_ref, c, blk, sl),
                                                (((1,), (1,)), ((), ())),
                                                preferred_element_type=F32)
    for c, blk in units:
        rows = slice(blk * BAND, (blk + 1) * BAND)
        stats = jnp.zeros((BAND, LANES), F32)
        for h in range(H_A):
            sl = slice(h * HEAD_DIM, (h + 1) * HEAD_DIM)
            s = scores[c, blk, h] + bias_ref[0, h]
            if blk == 0:
                s = s + no_prev
            mx = jnp.max(s, axis=-1, keepdims=True)
            p = jnp.exp(s - mx)
            den = jnp.sum(p, axis=-1, keepdims=True)
            o_ref[0, c, rows, sl] = jnp.dot(p.astype(BF16), keys(vp_ref, vc_ref, c, blk, sl),
                                            preferred_element_type=F32).astype(o_ref.dtype)
            stats = jnp.where(stat_lane == h, mx, stats)
            stats = jnp.where(stat_lane == H_A + h, den, stats)
        st_ref[0, c, rows, :] = stats


def _dilated(qkv, bias, pattern):
    b, dilation, n_sub, _ = qkv.shape
    n_blk = min(DIL_UNITS, n_sub // BAND)
    n_cls = DIL_UNITS // n_blk
    rows = n_blk * BAND
    nb = n_sub // rows

    def cur(which):
        return pl.BlockSpec((1, n_cls, rows, D_A), lambda bi, r, i: (bi, r, i, which))

    def prev(which):
        return pl.BlockSpec((1, n_cls, BAND, D_A),
                            lambda bi, r, i: (bi, r, jnp.maximum(i * n_blk - 1, 0), which))

    return pl.pallas_call(
        _dilated_kernel,
        grid=(b, dilation // n_cls, nb),
        in_specs=[
            cur(0), prev(1), cur(1), prev(2), cur(2),
            pl.BlockSpec((1, H_A, BAND, 2 * BAND), lambda bi, r, i: (pattern, 0, 0, 0)),
        ],
        out_specs=[
            pl.BlockSpec((1, n_cls, rows, D_A), lambda bi, r, i: (bi, r, i, 0)),
            pl.BlockSpec((1, n_cls, rows, LANES), lambda bi, r, i: (bi, r, i, 0)),
        ],
        out_shape=[
            jax.ShapeDtypeStruct((b, dilation, n_sub, D_A), BF16),
            jax.ShapeDtypeStruct((b, dilation, n_sub, LANES), F32),
        ],
        compiler_params=_params("parallel", "parallel", "arbitrary"),
        name=f"dilated_{dilation}",
    )(qkv, qkv, qkv, qkv, qkv, bias)


COMBINE_ROWS = 512


def _combine_kernel(*refs):
    n_pat = len(DILATED_PATTERNS)
    num_refs, st_refs = refs[:n_pat], refs[n_pat:2 * n_pat]
    o_ref, num_buf, st_buf, num_tmp, st_tmp = refs[2 * n_pat:]
    dilations = [ref.shape[1] for ref in num_refs]
    fine = min(d for d in dilations if d > 1)
    heads = [slice(h * HEAD_DIM, (h + 1) * HEAD_DIM) for h in range(H_A)]
    for p, dilation in enumerate(dilations):
        if dilation == 1:
            continue
        step = dilation // fine
        for rf in range(fine):
            rows = pl.ds(rf, COMBINE_ROWS // fine, stride=fine)
            if step == 1:
                st_buf[p, rows, :] = st_refs[p][0, rf]
                for h, sl in enumerate(heads):
                    num_buf[p, h, rows, :] = num_refs[p][0, rf, :, sl].astype(F32)
                continue
            for q in range(step):
                part = pl.ds(q, COMBINE_ROWS // dilation, stride=step)
                st_tmp[part, :] = st_refs[p][0, rf + fine * q]
                for h, sl in enumerate(heads):
                    num_tmp[h, part, :] = num_refs[p][0, rf + fine * q, :, sl].astype(F32)
            st_buf[p, rows, :] = st_tmp[...]
            for h in range(H_A):
                num_buf[p, h, rows, :] = num_tmp[h]
    stats = [st_refs[p][0, 0] if d == 1 else st_buf[p] for p, d in enumerate(dilations)]
    for h in range(H_A):
        sl = slice(h * HEAD_DIM, (h + 1) * HEAD_DIM)
        mxs = [st[:, h:h + 1] for st in stats]
        dens = [st[:, H_A + h:H_A + h + 1] for st in stats]
        mx = functools.reduce(jnp.maximum, mxs)
        num = None
        den = None
        for p, (pden, pmx) in enumerate(zip(dens, mxs)):
            scale = jnp.exp(pmx - mx)
            if dilations[p] == 1:
                pnum = num_refs[p][0, 0, :, sl].astype(F32)
            else:
                pnum = num_buf[p, h]
            num = pnum * scale if num is None else num + pnum * scale
            den = pden * scale if den is None else den + pden * scale
        o_ref[:, sl] = (num / den).astype(BF16)


def _combine(nums, stats):
    b, _, s, _ = nums[0].shape
    tps = s // COMBINE_ROWS

    def spec(arr):
        dilation, width = arr.shape[1], arr.shape[3]
        return pl.BlockSpec((1, dilation, COMBINE_ROWS // dilation, width),
                            lambda i: (i // tps, 0, i % tps, 0))

    n_pat = len(nums)
    fine = min(a.shape[1] for a in nums if a.shape[1] > 1)
    return pl.pallas_call(
        _combine_kernel,
        grid=(b * tps,),
        in_specs=[spec(a) for a in nums] + [spec(a) for a in stats],
        out_specs=pl.BlockSpec((COMBINE_ROWS, D_A), lambda i: (i, 0)),
        out_shape=jax.ShapeDtypeStruct((b * s, D_A), BF16),
        scratch_shapes=[
            pltpu.VMEM((n_pat, H_A, COMBINE_ROWS, HEAD_DIM), F32),
            pltpu.VMEM((n_pat, COMBINE_ROWS, LANES), F32),
            pltpu.VMEM((H_A, COMBINE_ROWS // fine, HEAD_DIM), F32),
            pltpu.VMEM((COMBINE_ROWS // fine, LANES), F32),
        ],
        compiler_params=_params("parallel"),
        name="combine_a",
    )(*nums, *stats)


def _fox_kernel(q_ref, k_ref, vt_ref, o_ref, m_ref, l_ref, acc_ref, sa_ref, sb_ref):
    iq = pl.program_id(2)
    m_ref[...] = jnp.full(m_ref.shape, NEG_INF, F32)
    l_ref[...] = jnp.zeros(l_ref.shape, F32)
    acc_ref[...] = jnp.zeros(acc_ref.shape, F32)

    def scores_into(s_ref, kb):
        start = pl.multiple_of(kb * FOX_TK, FOX_TK)
        for hh in range(FOX_HEADS):
            k = k_ref[0, hh, pl.ds(start, FOX_TK), :]
            s_ref[hh] = lax.dot_general(k, q_ref[0, hh], (((1,), (1,)), ((), ())),
                                        preferred_element_type=F32)

    def accumulate(s_ref, kb, diagonal):
        start = pl.multiple_of(kb * FOX_TK, FOX_TK)
        for hh in range(FOX_HEADS):
            st = s_ref[hh]
            if diagonal:
                key = lax.broadcasted_iota(jnp.int32, st.shape, 0)
                qry = lax.broadcasted_iota(jnp.int32, st.shape, 1)
                st = jnp.where(key <= qry, st, NEG_INF)
            m_prev = m_ref[hh]
            m_new = jnp.maximum(m_prev, jnp.max(st, axis=0, keepdims=True))
            alpha = jnp.exp2(m_prev - m_new)
            p = jnp.exp2(st - m_new)
            l_ref[hh] = alpha * l_ref[hh] + jnp.sum(p, axis=0, keepdims=True)
            vt = vt_ref[0, hh, :, pl.ds(start, FOX_TK)]
            acc_ref[hh] = alpha * acc_ref[hh] + jnp.dot(vt, p.astype(BF16),
                                                        preferred_element_type=F32)
            m_ref[hh] = m_new

    scores_into(sa_ref, 0)

    def pair(p, carry):
        scores_into(sb_ref, 2 * p + 1)
        accumulate(sa_ref, 2 * p, False)
        scores_into(sa_ref, 2 * p + 2)
        accumulate(sb_ref, 2 * p + 1, False)
        return carry

    lax.fori_loop(0, iq // 2, pair, 0)

    @pl.when(iq % 2 == 0)
    def _():
        accumulate(sa_ref, iq, True)

    @pl.when(iq % 2 == 1)
    def _():
        scores_into(sb_ref, iq)
        accumulate(sa_ref, iq - 1, False)
        accumulate(sb_ref, iq, True)
    for hh in range(FOX_HEADS):
        o_ref[0, :, hh * HEAD_DIM:(hh + 1) * HEAD_DIM] = (
            acc_ref[hh] / l_ref[hh]).T.astype(BF16)


def _fox(q_aug, k_aug, v_t):
    b, h_b, s, _ = q_aug.shape
    assert FOX_TQ == FOX_TK
    return pl.pallas_call(
        _fox_kernel,
        grid=(b, h_b // FOX_HEADS, s // FOX_TQ),
        in_specs=[
            pl.BlockSpec((1, FOX_HEADS, FOX_TQ, AUG), lambda bi, h, i: (bi, h, i, 0)),
            pl.BlockSpec((1, FOX_HEADS, s, AUG), lambda bi, h, i: (bi, h, 0, 0)),
            pl.BlockSpec((1, FOX_HEADS, HEAD_DIM, s), lambda bi, h, i: (bi, h, 0, 0)),
        ],
        out_specs=pl.BlockSpec((1, FOX_TQ, FOX_HEADS * HEAD_DIM), lambda bi, h, i: (bi, i, h)),
        out_shape=jax.ShapeDtypeStruct((b, s, h_b * HEAD_DIM), BF16),
        scratch_shapes=[
            pltpu.VMEM((FOX_HEADS, 1, FOX_TQ), F32),
            pltpu.VMEM((FOX_HEADS, 1, FOX_TQ), F32),
            pltpu.VMEM((FOX_HEADS, HEAD_DIM, FOX_TQ), F32),
            pltpu.VMEM((FOX_HEADS, FOX_TK, FOX_TQ), F32),
            pltpu.VMEM((FOX_HEADS, FOX_TK, FOX_TQ), F32),
        ],
        compiler_params=_params("parallel", "parallel", "arbitrary"),
        name="fox",
    )(q_aug, k_aug, v_t)


def _out_proj_kernel(a_ref, b_ref, w_ref, x_ref, o_ref):
    mixed = jnp.concatenate([a_ref[...], b_ref[...]], axis=-1)
    o_ref[...] = x_ref[...] + jnp.dot(mixed, w_ref[...].astype(BF16),
                                      preferred_element_type=F32)


OUT_ROW_TILE = 2048


def _out_proj(out_a, out_b, w_out, x):
    m = x.shape[0]
    return pl.pallas_call(
        _out_proj_kernel,
        grid=(m // OUT_ROW_TILE, D_MODEL // COL_TILE),
        in_specs=[
            pl.BlockSpec((OUT_ROW_TILE, D_A), lambda i, j: (i, 0)),
            pl.BlockSpec((OUT_ROW_TILE, D_B), lambda i, j: (i, 0)),
            pl.BlockSpec((D_A + D_B, COL_TILE), lambda i, j: (0, j)),
            pl.BlockSpec((OUT_ROW_TILE, COL_TILE), lambda i, j: (i, j)),
        ],
        out_specs=pl.BlockSpec((OUT_ROW_TILE, COL_TILE), lambda i, j: (i, j)),
        out_shape=jax.ShapeDtypeStruct((m, D_MODEL), F32),
        compiler_params=_params("parallel", "parallel"),
        name="out_proj",
    )(out_a, out_b, w_out, x)


def _mixer(x, b, s, mix_norm, w_in, layer, q_norm_a, k_norm_a, q_norm_b, k_norm_b, forget_bias,
           rel_bias_table, w_out):
    ones = jnp.ones((D_A,), F32)
    head_gain = jnp.concatenate([
        jnp.tile(q_norm_a * ATTN_SCALE, H_A), jnp.tile(k_norm_a, H_A), ones,
        jnp.tile(q_norm_b * (ATTN_SCALE * LOG2E), H_B), jnp.tile(k_norm_b, H_B), ones,
    ]).reshape(1, D_QKV)
    w_f = jnp.pad(w_in[layer, :, D_QKV:], ((0, 0), (0, LANES - H_B))).astype(BF16)
    f_bias = jnp.pad(forget_bias, (0, LANES - H_B)).reshape(1, LANES)
    *qkv_a, q_aug, k_aug, v_t = _proj(x, b, s, mix_norm, w_in, layer, w_f, f_bias, head_gain)

    out_b = _fox(q_aug, k_aug, v_t).reshape(b * s, D_B)

    bias = _band_bias(rel_bias_table)
    parts = [_dilated(qkv, bias, p) for p, qkv in enumerate(qkv_a)]
    out_a = _combine([pt[0] for pt in parts], [pt[1] for pt in parts])

    return _out_proj(out_a, out_b, w_out, x)


def kernel(x, ffn1_norm, ffn1_w_in, ffn1_w_out, mix_norm, w_in, q_norm_a, k_norm_a, q_norm_b,
           k_norm_b, forget_bias, rel_bias_table, w_out, ffn2_norm, ffn2_w_in, ffn2_w_out):
    b, s, d = x.shape
    depth = ffn1_norm.shape[0]
    x = x.reshape(b * s, d)
    for l in range(depth):
        x = _ffn(x, ffn1_norm[l], ffn1_w_in[l], ffn1_w_out[l])
        x = _mixer(x, b, s, mix_norm[l], w_in, l, q_norm_a[l], k_norm_a[l], q_norm_b[l],
                   k_norm_b[l], forget_bias[l], rel_bias_table, w_out[l])
        x = _ffn(x, ffn2_norm[l], ffn2_w_in[l], ffn2_w_out[l])
    return x.reshape(b, s, d)
```

```python
import functools
import math

import numpy as np
import jax
import jax.numpy as jnp
from jax import lax
from jax.experimental import pallas as pl
from jax.experimental.pallas import tpu as pltpu

D_MODEL = 2048
HEAD_DIM = 128
N_HEADS = D_MODEL // HEAD_DIM
H_A = N_HEADS // 2
H_B = N_HEADS - H_A
D_A = H_A * HEAD_DIM
D_B = H_B * HEAD_DIM
D_QKV = 3 * D_A + 3 * D_B
DILATED_PATTERNS = ((128, 1), (512, 4), (2048, 16))
BAND = 128
NUM_BUCKETS = 32
MAX_DISTANCE = 2048
D_FF = ((8 * D_MODEL // 3 + 127) // 128) * 128
RMS_EPS = 1e-6
NEG_INF = -1e30
ATTN_SCALE = HEAD_DIM ** -0.5

LANES = 128
FF_TILE = 512
FFN_ROW_TILE = 2048
FFN_CHUNK = 256
FFN_VMEM_LIMIT = 60 * 1024 * 1024
ROW_TILE = 1024
COL_TILE = 512
FOX_TQ = 512
FOX_TK = 512
FOX_HEADS = 4
VMEM_LIMIT = 56 * 1024 * 1024

F32 = jnp.float32
BF16 = jnp.bfloat16


def _params(*sem):
    return pltpu.CompilerParams(dimension_semantics=sem, vmem_limit_bytes=VMEM_LIMIT)


def _rms_rows(x, gain):
    ms = jnp.mean(x * x, axis=-1, keepdims=True)
    return x * lax.rsqrt(ms + RMS_EPS) * gain


def _ffn_kernel(x_hbm, g_ref, wg_ref, wu_ref, wo_ref, o_hbm, acc_ref, h_ref, in_sem, out_sem):
    i = pl.program_id(0)
    j = pl.program_id(1)
    last_i = pl.num_programs(0) - 1
    last_j = pl.num_programs(1) - 1
    pieces = [(r, c) for r in range(FFN_ROW_TILE // ROW_TILE) for c in range(D_MODEL // COL_TILE)]

    def x_copy(c):
        src = x_hbm.at[pl.ds(i * FFN_ROW_TILE + c * FFN_CHUNK, FFN_CHUNK), :]
        return pltpu.make_async_copy(src, acc_ref.at[pl.ds(c * FFN_CHUNK, FFN_CHUNK), :],
                                     in_sem.at[c])

    def out_copy(row_tile, r, c):
        rows, cols = pl.ds(r * ROW_TILE, ROW_TILE), pl.ds(c * COL_TILE, COL_TILE)
        dst = o_hbm.at[pl.ds(row_tile * FFN_ROW_TILE + r * ROW_TILE, ROW_TILE), cols]
        return pltpu.make_async_copy(acc_ref.at[rows, cols], dst,
                                     out_sem.at[r * (D_MODEL // COL_TILE) + c])

    @pl.when(j == 0)
    def _():
        @pl.when(i > 0)
        def _():
            for r, c in pieces:
                out_copy(i - 1, r, c).wait()

        n_chunks = FFN_ROW_TILE // FFN_CHUNK
        for c in range(n_chunks):
            x_copy(c).start()
        for c in range(n_chunks):
            x_copy(c).wait()
            rows = slice(c * FFN_CHUNK, (c + 1) * FFN_CHUNK)
            h_ref[rows, :] = _rms_rows(acc_ref[rows, :], g_ref[...]).astype(BF16)

    col = lax.broadcasted_iota(jnp.int32, (1, FF_TILE), 1)
    repeated = j * FF_TILE - jnp.minimum(j * FF_TILE, D_FF - FF_TILE)

    def tile(write_back):
        for r in range(FFN_ROW_TILE // ROW_TILE):
            rows = slice(r * ROW_TILE, (r + 1) * ROW_TILE)
            h = h_ref[rows, :]
            gate = jnp.dot(h, wg_ref[...].astype(BF16), preferred_element_type=F32)
            up = jnp.dot(h, wu_ref[...].astype(BF16), preferred_element_type=F32)
            act = gate * (1.0 / (1.0 + jnp.exp(-gate))) * (0.5 * up)
            act = jnp.where(col >= repeated, act, 0.0).astype(BF16)
            for c in range(D_MODEL // COL_TILE):
                sl = slice(c * COL_TILE, (c + 1) * COL_TILE)
                acc_ref[rows, sl] += jnp.dot(act, wo_ref[:, sl].astype(BF16),
                                             preferred_element_type=F32)
                if write_back:
                    out_copy(i, r, c).start()

    @pl.when(j < last_j)
    def _():
        tile(False)

    @pl.when(j == last_j)
    def _():
        tile(True)

        @pl.when(i == last_i)
        def _():
            for r, c in pieces:
                out_copy(i, r, c).wait()


def _ffn(x, gain, w_in, w_out):
    m = x.shape[0]
    n_ff = pl.cdiv(D_FF, FF_TILE)

    def ff_start(j):
        return pl.multiple_of(jnp.minimum(j * FF_TILE, D_FF - FF_TILE), LANES)

    return pl.pallas_call(
        _ffn_kernel,
        grid=(m // FFN_ROW_TILE, n_ff),
        in_specs=[
            pl.BlockSpec(memory_space=pl.ANY),
            pl.BlockSpec((1, D_MODEL), lambda i, j: (0, 0)),
            pl.BlockSpec((pl.Element(D_MODEL), pl.Element(FF_TILE)),
                         lambda i, j: (0, ff_start(j))),
            pl.BlockSpec((pl.Element(D_MODEL), pl.Element(FF_TILE)),
                         lambda i, j: (0, pl.multiple_of(D_FF + ff_start(j), LANES))),
            pl.BlockSpec((pl.Element(FF_TILE), pl.Element(D_MODEL)),
                         lambda i, j: (ff_start(j), 0)),
        ],
        out_specs=pl.BlockSpec(memory_space=pl.ANY),
        out_shape=jax.ShapeDtypeStruct((m, D_MODEL), F32),
        scratch_shapes=[
            pltpu.VMEM((FFN_ROW_TILE, D_MODEL), F32),
            pltpu.VMEM((FFN_ROW_TILE, D_MODEL), BF16),
            pltpu.SemaphoreType.DMA((FFN_ROW_TILE // FFN_CHUNK,)),
            pltpu.SemaphoreType.DMA(((FFN_ROW_TILE // ROW_TILE) * (D_MODEL // COL_TILE),)),
        ],
        compiler_params=pltpu.CompilerParams(
            dimension_semantics=("arbitrary", "arbitrary"), vmem_limit_bytes=FFN_VMEM_LIMIT),
        name="ffn",
    )(x, gain.reshape(1, D_MODEL), w_in, w_in, w_out)


HEADS_PER_TILE = COL_TILE // HEAD_DIM
N_PROJ_TILES = D_QKV // COL_TILE
TILES_PER_GROUP = D_A // COL_TILE
A_TILES = 3 * TILES_PER_GROUP
AUG = 2 * HEAD_DIM
LOG2E = math.log2(math.e)


def _log_gate_scan(z, carry):
    c = (jnp.minimum(z, 0.0) - jnp.log1p(jnp.exp(-jnp.abs(z)))) * LOG2E
    row = lax.broadcasted_iota(jnp.int32, c.shape, 0)
    shift = 1
    while shift < c.shape[0]:
        c = c + jnp.where(row >= shift, pltpu.roll(c, shift, axis=0), 0.0)
        shift *= 2
    return c + carry


def _split3(c):
    hi = c.astype(BF16)
    rest = c - hi.astype(F32)
    mid = rest.astype(BF16)
    lo = (rest - mid.astype(F32)).astype(BF16)
    return hi, mid, lo


MXU_WIDTH = 256


def _head_mean_matrix():
    head = np.arange(MXU_WIDTH) // HEAD_DIM
    return jnp.asarray((head[:, None] == head[None, :]) / HEAD_DIM, BF16)


GATE_ONES_LANE = 3 * H_B


def _gate_placement_matrices():
    mats = np.zeros((2, TILES_PER_GROUP, LANES, COL_TILE), np.float32)
    for part in range(TILES_PER_GROUP):
        for hh in range(HEADS_PER_TILE):
            head = part * HEADS_PER_TILE + hh
            col = hh * HEAD_DIM
            for term in range(3):
                mats[0, part, term * H_B + head, col + term] = 1.0
                mats[0, part, GATE_ONES_LANE, col + 3 + term] = 1.0
                mats[1, part, term * H_B + head, col + 3 + term] = -1.0
                mats[1, part, GATE_ONES_LANE, col + term] = 1.0
    return jnp.asarray(mats.reshape(2 * TILES_PER_GROUP, LANES, COL_TILE), BF16)


def _proj_kernel(x_ref, g_ref, w_ref, wf_ref, fb_ref, hg_ref, mean_ref, place_ref,
                 a1_ref, a4_ref, a16_ref, qb_ref, kb_ref, vt_ref,
                 h_ref, cs_ref, carry_ref, y_ref, yf_ref, *, tiles_per_seq):
    i = pl.program_id(0)
    j = pl.program_id(1)

    @pl.when(j == 0)
    def _():
        h_ref[...] = _rms_rows(x_ref[...], g_ref[...]).astype(BF16)

        @pl.when(i % tiles_per_seq == 0)
        def _():
            carry_ref[...] = jnp.zeros(carry_ref.shape, F32)

        z = jnp.dot(h_ref[...], wf_ref[...], preferred_element_type=F32) + fb_ref[...]
        c = _log_gate_scan(z, carry_ref[...])
        carry_ref[...] = c[ROW_TILE - 1:ROW_TILE, :]
        lane = lax.broadcasted_iota(jnp.int32, c.shape, 1)
        packed = jnp.where(lane == GATE_ONES_LANE, 1.0, 0.0)
        for term, part in enumerate(_split3(c)):
            part = jnp.where(lane < H_B, part.astype(F32), 0.0)
            packed = packed + (pltpu.roll(part, term * H_B, axis=1) if term else part)
        cs_ref[...] = packed.astype(BF16)

    def product(transposed=False):
        lhs, rhs = (w_ref[...].astype(BF16), h_ref[...])
        if not transposed:
            lhs, rhs = rhs, lhs
        return lax.dot_general(lhs, rhs, (((1,), (1,)), ((), ())), preferred_element_type=F32)

    def normed():
        r = product()
        sq = (r * r).astype(BF16)
        ms = jnp.concatenate(
            [jnp.dot(sq[:, k:k + MXU_WIDTH], mean_ref[...], preferred_element_type=F32)
             for k in range(0, COL_TILE, MXU_WIDTH)], axis=1)
        return r * lax.rsqrt(ms + RMS_EPS) * hg_ref[...]

    def store_dilated(y):
        a1_ref[0, 0] = y.astype(BF16)
        fine, coarse = a4_ref.shape[1], a16_ref.shape[1]
        step = coarse // fine
        for c in range(HEADS_PER_TILE):
            sl = slice(c * LANES, (c + 1) * LANES)
            y_ref[c] = y[:, sl]
            for rf in range(fine):
                rows = y_ref[c, pl.ds(rf, ROW_TILE // fine, stride=fine), :]
                a4_ref[0, rf, :, sl] = rows.astype(BF16)
                yf_ref[c, rf] = rows
                for q in range(step):
                    rows = yf_ref[c, rf, pl.ds(q, ROW_TILE // coarse, stride=step), :]
                    a16_ref[0, rf + fine * q, :, sl] = rows.astype(BF16)

    def tiles(first_group, n_groups=1):
        lo = first_group * TILES_PER_GROUP
        return jnp.logical_and(j >= lo, j < lo + n_groups * TILES_PER_GROUP)

    def store_forgetting(out):
        y = normed().astype(BF16)
        aug = jnp.dot(cs_ref[...], place_ref[0], preferred_element_type=F32).astype(BF16)
        for hh in range(HEADS_PER_TILE):
            sl = slice(hh * HEAD_DIM, (hh + 1) * HEAD_DIM)
            out[0, hh, :, :HEAD_DIM] = y[:, sl]
            out[0, hh, :, HEAD_DIM:] = aug[:, sl]

    @pl.when(tiles(0, 2))
    def _():
        store_dilated(normed())

    @pl.when(tiles(2))
    def _():
        store_dilated(product())

    @pl.when(tiles(3))
    def _():
        store_forgetting(qb_ref)

    @pl.when(tiles(4))
    def _():
        store_forgetting(kb_ref)

    @pl.when(tiles(5))
    def _():
        v_t = product(transposed=True).astype(BF16)
        for hh in range(HEADS_PER_TILE):
            vt_ref[0, hh] = v_t[hh * HEAD_DIM:(hh + 1) * HEAD_DIM, :]


def _proj(x, b, s, gain, w_in, layer, w_f, f_bias, head_gain):
    m = x.shape[0]
    tps = s // ROW_TILE

    def a_spec(dilation):
        return pl.BlockSpec((1, dilation, ROW_TILE // dilation, COL_TILE),
                            lambda i, j: (i // tps, 0, i % tps, jnp.minimum(j, A_TILES - 1)))

    def head_block(first):
        return lambda i, j: (i // tps, jnp.clip(j - first, 0, TILES_PER_GROUP - 1), i % tps, 0)

    qb_first = A_TILES
    vt_first = A_TILES + 2 * TILES_PER_GROUP
    dilations = [d for _, d in DILATED_PATTERNS]
    return pl.pallas_call(
        functools.partial(_proj_kernel, tiles_per_seq=tps),
        grid=(m // ROW_TILE, N_PROJ_TILES),
        in_specs=[
            pl.BlockSpec((ROW_TILE, D_MODEL), lambda i, j: (i, 0)),
            pl.BlockSpec((1, D_MODEL), lambda i, j: (0, 0)),
            pl.BlockSpec((None, COL_TILE, D_MODEL), lambda i, j: (layer, j, 0)),
            pl.BlockSpec((D_MODEL, LANES), lambda i, j: (0, 0)),
            pl.BlockSpec((1, LANES), lambda i, j: (0, 0)),
            pl.BlockSpec((1, COL_TILE), lambda i, j: (0, j)),
            pl.BlockSpec((MXU_WIDTH, MXU_WIDTH), lambda i, j: (0, 0)),
            pl.BlockSpec((1, LANES, COL_TILE),
                         lambda i, j: (jnp.clip(j - qb_first, 0, 2 * TILES_PER_GROUP - 1), 0, 0)),
        ],
        out_specs=[a_spec(d) for d in dilations] + [
            pl.BlockSpec((1, HEADS_PER_TILE, ROW_TILE, AUG), head_block(qb_first)),
            pl.BlockSpec((1, HEADS_PER_TILE, ROW_TILE, AUG),
                         head_block(qb_first + TILES_PER_GROUP)),
            pl.BlockSpec((1, HEADS_PER_TILE, HEAD_DIM, ROW_TILE),
                         lambda i, j: (i // tps, jnp.clip(j - vt_first, 0, TILES_PER_GROUP - 1),
                                       0, i % tps)),
        ],
        out_shape=[jax.ShapeDtypeStruct((b, d, s // d, 3 * D_A), BF16) for d in dilations] + [
            jax.ShapeDtypeStruct((b, H_B, s, AUG), BF16),
            jax.ShapeDtypeStruct((b, H_B, s, AUG), BF16),
            jax.ShapeDtypeStruct((b, H_B, HEAD_DIM, s), BF16),
        ],
        scratch_shapes=[
            pltpu.VMEM((ROW_TILE, D_MODEL), BF16),
            pltpu.VMEM((ROW_TILE, LANES), BF16),
            pltpu.VMEM((1, LANES), F32),
            pltpu.VMEM((HEADS_PER_TILE, ROW_TILE, LANES), F32),
            pltpu.VMEM((HEADS_PER_TILE, dilations[1], ROW_TILE // dilations[1], LANES), F32),
        ],
        compiler_params=_params("arbitrary", "arbitrary"),
        name="proj",
    )(x, gain.reshape(1, D_MODEL), jnp.swapaxes(w_in, 1, 2), w_f, f_bias, head_gain,
      _head_mean_matrix(), _gate_placement_matrices())


def _bucket_steps(dilation):
    dist = np.arange(BAND + 1) * dilation
    max_exact = NUM_BUCKETS // 2
    large = max_exact + np.floor(
        np.log(np.maximum(dist, 1) / max_exact) / math.log(MAX_DISTANCE / max_exact)
        * (NUM_BUCKETS - max_exact)).astype(np.int64)
    bucket = np.where(dist < max_exact, dist, np.minimum(large, NUM_BUCKETS - 1))
    steps = [(0, int(bucket[0]))]
    for delta in range(1, BAND + 1):
        if bucket[delta] != bucket[delta - 1]:
            steps.append((delta, int(bucket[delta])))
    return steps


def _bias_kernel(table_ref, o_ref):
    iq = lax.broadcasted_iota(jnp.int32, (BAND, 2 * BAND), 0)
    ik = lax.broadcasted_iota(jnp.int32, (BAND, 2 * BAND), 1)
    delta = iq + BAND - ik
    in_band = jnp.logical_and(delta >= 0, delta <= BAND)
    for p, (_, dilation) in enumerate(DILATED_PATTERNS):
        steps = _bucket_steps(dilation)
        for h in range(H_A):
            val = jnp.full((BAND, 2 * BAND), table_ref[steps[0][1], h], F32)
            for start, bucket in steps[1:]:
                val = jnp.where(delta >= start, table_ref[bucket, h], val)
            o_ref[p, h] = jnp.where(in_band, val, NEG_INF)


def _band_bias(rel_table):
    n_pat = len(DILATED_PATTERNS)
    return pl.pallas_call(
        _bias_kernel,
        in_specs=[pl.BlockSpec(memory_space=pltpu.SMEM)],
        out_specs=pl.BlockSpec(memory_space=pltpu.VMEM),
        out_shape=jax.ShapeDtypeStruct((n_pat, H_A, BAND, 2 * BAND), F32),
        name="band_bias",
    )(rel_table)


DIL_UNITS = 8


def _dilated_kernel(q_ref, kp_ref, kc_ref, vp_ref, vc_ref, bias_ref, o_ref, st_ref):
    n_cls = q_ref.shape[1]
    n_blk = q_ref.shape[2] // BAND
    first = pl.program_id(2) == 0
    key_lane = lax.broadcasted_iota(jnp.int32, (1, 2 * BAND), 1)
    no_prev = jnp.where(jnp.logical_and(first, key_lane < BAND), NEG_INF, 0.0)
    stat_lane = lax.broadcasted_iota(jnp.int32, (BAND, LANES), 1)
    units = [(c, blk) for c in range(n_cls) for blk in range(n_blk)]

    def keys(prev_ref, cur_ref, c, blk, sl):
        if blk == 0:
            return jnp.concatenate([prev_ref[0, c, :, sl], cur_ref[0, c, :BAND, sl]], axis=0)
        return cur_ref[0, c, (blk - 1) * BAND:(blk + 1) * BAND, sl]

    scores = {}
    for c, blk in units:
        for h in range(H_A):
            sl = slice(h * HEAD_DIM, (h + 1) * HEAD_DIM)
            q = q_ref[0, c, blk * BAND:(blk + 1) * BAND, sl]
            scores[c, blk, h] = lax.dot_general(q, keys(kp_ref, kc_ref, c, blk, sl),
                                                (((1,), (1,)), ((), ())),
                                                preferred_element_type=F32)
    for c, blk in units:
        rows = slice(blk * BAND, (blk + 1) * BAND)
        stats = jnp.zeros((BAND, LANES), F32)
        for h in range(H_A):
            sl = slice(h * HEAD_DIM, (h + 1) * HEAD_DIM)
            s = scores[c, blk, h] + bias_ref[0, h]
            if blk == 0:
                s = s + no_prev
            mx = jnp.max(s, axis=-1, keepdims=True)
            p = jnp.exp(s - mx)
            den = jnp.sum(p, axis=-1, keepdims=True)
            o_ref[0, c, rows, sl] = jnp.dot(p.astype(BF16), keys(vp_ref, vc_ref, c, blk, sl),
                                            preferred_element_type=F32).astype(o_ref.dtype)
            stats = jnp.where(stat_lane == h, mx, stats)
            stats = jnp.where(stat_lane == H_A + h, den, stats)
        st_ref[0, c, rows, :] = stats


def _dilated(qkv, bias, pattern):
    b, dilation, n_sub, _ = qkv.shape
    n_blk = min(DIL_UNITS, n_sub // BAND)
    n_cls = DIL_UNITS // n_blk
    rows = n_blk * BAND
    nb = n_sub // rows

    def cur(which):
        return pl.BlockSpec((1, n_cls, rows, D_A), lambda bi, r, i: (bi, r, i, which))

    def prev(which):
        return pl.BlockSpec((1, n_cls, BAND, D_A),
                            lambda bi, r, i: (bi, r, jnp.maximum(i * n_blk - 1, 0), which))

    return pl.pallas_call(
        _dilated_kernel,
        grid=(b, dilation // n_cls, nb),
        in_specs=[
            cur(0), prev(1), cur(1), prev(2), cur(2),
            pl.BlockSpec((1, H_A, BAND, 2 * BAND), lambda bi, r, i: (pattern, 0, 0, 0)),
        ],
        out_specs=[
            pl.BlockSpec((1, n_cls, rows, D_A), lambda bi, r, i: (bi, r, i, 0)),
            pl.BlockSpec((1, n_cls, rows, LANES), lambda bi, r, i: (bi, r, i, 0)),
        ],
        out_shape=[
            jax.ShapeDtypeStruct((b, dilation, n_sub, D_A), BF16),
            jax.ShapeDtypeStruct((b, dilation, n_sub, LANES), F32),
        ],
        compiler_params=_params("parallel", "parallel", "arbitrary"),
        name=f"dilated_{dilation}",
    )(qkv, qkv, qkv, qkv, qkv, bias)


COMBINE_ROWS = 512


def _combine_kernel(*refs):
    n_pat = len(DILATED_PATTERNS)
    num_refs, st_refs = refs[:n_pat], refs[n_pat:2 * n_pat]
    o_ref, num_buf, st_buf, num_tmp, st_tmp = refs[2 * n_pat:]
    dilations = [ref.shape[1] for ref in num_refs]
    fine = min(d for d in dilations if d > 1)
    heads = [slice(h * HEAD_DIM, (h + 1) * HEAD_DIM) for h in range(H_A)]
    for p, dilation in enumerate(dilations):
        if dilation == 1:
            continue
        step = dilation // fine
        for rf in range(fine):
            rows = pl.ds(rf, COMBINE_ROWS // fine, stride=fine)
            if step == 1:
                st_buf[p, rows, :] = st_refs[p][0, rf]
                for h, sl in enumerate(heads):
                    num_buf[p, h, rows, :] = num_refs[p][0, rf, :, sl].astype(F32)
                continue
            for q in range(step):
                part = pl.ds(q, COMBINE_ROWS // dilation, stride=step)
                st_tmp[part, :] = st_refs[p][0, rf + fine * q]
                for h, sl in enumerate(heads):
                    num_tmp[h, part, :] = num_refs[p][0, rf + fine * q, :, sl].astype(F32)
            st_buf[p, rows, :] = st_tmp[...]
            for h in range(H_A):
                num_buf[p, h, rows, :] = num_tmp[h]
    stats = [st_refs[p][0, 0] if d == 1 else st_buf[p] for p, d in enumerate(dilations)]
    for h in range(H_A):
        sl = slice(h * HEAD_DIM, (h + 1) * HEAD_DIM)
        mxs = [st[:, h:h + 1] for st in stats]
        dens = [st[:, H_A + h:H_A + h + 1] for st in stats]
        mx = functools.reduce(jnp.maximum, mxs)
        num = None
        den = None
        for p, (pden, pmx) in enumerate(zip(dens, mxs)):
            scale = jnp.exp(pmx - mx)
            if dilations[p] == 1:
                pnum = num_refs[p][0, 0, :, sl].astype(F32)
            else:
                pnum = num_buf[p, h]
            num = pnum * scale if num is None else num + pnum * scale
            den = pden * scale if den is None else den + pden * scale
        o_ref[:, sl] = (num / den).astype(BF16)


def _combine(nums, stats):
    b, _, s, _ = nums[0].shape
    tps = s // COMBINE_ROWS

    def spec(arr):
        dilation, width = arr.shape[1], arr.shape[3]
        return pl.BlockSpec((1, dilation, COMBINE_ROWS // dilation, width),
                            lambda i: (i // tps, 0, i % tps, 0))

    n_pat = len(nums)
    fine = min(a.shape[1] for a in nums if a.shape[1] > 1)
    return pl.pallas_call(
        _combine_kernel,
        grid=(b * tps,),
        in_specs=[spec(a) for a in nums] + [spec(a) for a in stats],
        out_specs=pl.BlockSpec((COMBINE_ROWS, D_A), lambda i: (i, 0)),
        out_shape=jax.ShapeDtypeStruct((b * s, D_A), BF16),
        scratch_shapes=[
            pltpu.VMEM((n_pat, H_A, COMBINE_ROWS, HEAD_DIM), F32),
            pltpu.VMEM((n_pat, COMBINE_ROWS, LANES), F32),
            pltpu.VMEM((H_A, COMBINE_ROWS // fine, HEAD_DIM), F32),
            pltpu.VMEM((COMBINE_ROWS // fine, LANES), F32),
        ],
        compiler_params=_params("parallel"),
        name="combine_a",
    )(*nums, *stats)


def _fox_kernel(q_ref, k_ref, vt_ref, o_ref, m_ref, l_ref, acc_ref, sa_ref, sb_ref):
    iq = pl.program_id(2)
    m_ref[...] = jnp.full(m_ref.shape, NEG_INF, F32)
    l_ref[...] = jnp.zeros(l_ref.shape, F32)
    acc_ref[...] = jnp.zeros(acc_ref.shape, F32)

    def scores_into(s_ref, kb):
        start = pl.multiple_of(kb * FOX_TK, FOX_TK)
        for hh in range(FOX_HEADS):
            k = k_ref[0, hh, pl.ds(start, FOX_TK), :]
            s_ref[hh] = lax.dot_general(k, q_ref[0, hh], (((1,), (1,)), ((), ())),
                                        preferred_element_type=F32)

    def accumulate(s_ref, kb, diagonal):
        start = pl.multiple_of(kb * FOX_TK, FOX_TK)
        for hh in range(FOX_HEADS):
            st = s_ref[hh]
            if diagonal:
                key = lax.broadcasted_iota(jnp.int32, st.shape, 0)
                qry = lax.broadcasted_iota(jnp.int32, st.shape, 1)
                st = jnp.where(key <= qry, st, NEG_INF)
            m_prev = m_ref[hh]
            m_new = jnp.maximum(m_prev, jnp.max(st, axis=0, keepdims=True))
            alpha = jnp.exp2(m_prev - m_new)
            p = jnp.exp2(st - m_new)
            l_ref[hh] = alpha * l_ref[hh] + jnp.sum(p, axis=0, keepdims=True)
            vt = vt_ref[0, hh, :, pl.ds(start, FOX_TK)]
            acc_ref[hh] = alpha * acc_ref[hh] + jnp.dot(vt, p.astype(BF16),
                                                        preferred_element_type=F32)
            m_ref[hh] = m_new

    scores_into(sa_ref, 0)

    def pair(p, carry):
        scores_into(sb_ref, 2 * p + 1)
        accumulate(sa_ref, 2 * p, False)
        scores_into(sa_ref, 2 * p + 2)
        accumulate(sb_ref, 2 * p + 1, False)
        return carry

    lax.fori_loop(0, iq // 2, pair, 0)

    @pl.when(iq % 2 == 0)
    def _():
        accumulate(sa_ref, iq, True)

    @pl.when(iq % 2 == 1)
    def _():
        scores_into(sb_ref, iq)
        accumulate(sa_ref, iq - 1, False)
        accumulate(sb_ref, iq, True)
    for hh in range(FOX_HEADS):
        o_ref[0, :, hh * HEAD_DIM:(hh + 1) * HEAD_DIM] = (
            acc_ref[hh] / l_ref[hh]).T.astype(BF16)


def _fox(q_aug, k_aug, v_t):
    b, h_b, s, _ = q_aug.shape
    assert FOX_TQ == FOX_TK
    return pl.pallas_call(
        _fox_kernel,
        grid=(b, h_b // FOX_HEADS, s // FOX_TQ),
        in_specs=[
            pl.BlockSpec((1, FOX_HEADS, FOX_TQ, AUG), lambda bi, h, i: (bi, h, i, 0)),
            pl.BlockSpec((1, FOX_HEADS, s, AUG), lambda bi, h, i: (bi, h, 0, 0)),
            pl.BlockSpec((1, FOX_HEADS, HEAD_DIM, s), lambda bi, h, i: (bi, h, 0, 0)),
        ],
        out_specs=pl.BlockSpec((1, FOX_TQ, FOX_HEADS * HEAD_DIM), lambda bi, h, i: (bi, i, h)),
        out_shape=jax.ShapeDtypeStruct((b, s, h_b * HEAD_DIM), BF16),
        scratch_shapes=[
            pltpu.VMEM((FOX_HEADS, 1, FOX_TQ), F32),
            pltpu.VMEM((FOX_HEADS, 1, FOX_TQ), F32),
            pltpu.VMEM((FOX_HEADS, HEAD_DIM, FOX_TQ), F32),
            pltpu.VMEM((FOX_HEADS, FOX_TK, FOX_TQ), F32),
            pltpu.VMEM((FOX_HEADS, FOX_TK, FOX_TQ), F32),
        ],
        compiler_params=_params("parallel", "parallel", "arbitrary"),
        name="fox",
    )(q_aug, k_aug, v_t)


def _out_proj_kernel(a_ref, b_ref, w_ref, x_ref, o_ref):
    mixed = jnp.concatenate([a_ref[...], b_ref[...]], axis=-1)
    o_ref[...] = x_ref[...] + jnp.dot(mixed, w_ref[...].astype(BF16),
                                      preferred_element_type=F32)


OUT_ROW_TILE = 2048


def _out_proj(out_a, out_b, w_out, x):
    m = x.shape[0]
    return pl.pallas_call(
        _out_proj_kernel,
        grid=(m // OUT_ROW_TILE, D_MODEL // COL_TILE),
        in_specs=[
            pl.BlockSpec((OUT_ROW_TILE, D_A), lambda i, j: (i, 0)),
            pl.BlockSpec((OUT_ROW_TILE, D_B), lambda i, j: (i, 0)),
            pl.BlockSpec((D_A + D_B, COL_TILE), lambda i, j: (0, j)),
            pl.BlockSpec((OUT_ROW_TILE, COL_TILE), lambda i, j: (i, j)),
        ],
        out_specs=pl.BlockSpec((OUT_ROW_TILE, COL_TILE), lambda i, j: (i, j)),
        out_shape=jax.ShapeDtypeStruct((m, D_MODEL), F32),
        compiler_params=_params("parallel", "parallel"),
        name="out_proj",
    )(out_a, out_b, w_out, x)


def _mixer(x, b, s, mix_norm, w_in, layer, q_norm_a, k_norm_a, q_norm_b, k_norm_b, forget_bias,
           rel_bias_table, w_out):
    ones = jnp.ones((D_A,), F32)
    head_gain = jnp.concatenate([
        jnp.tile(q_norm_a * ATTN_SCALE, H_A), jnp.tile(k_norm_a, H_A), ones,
        jnp.tile(q_norm_b * (ATTN_SCALE * LOG2E), H_B), jnp.tile(k_norm_b, H_B), ones,
    ]).reshape(1, D_QKV)
    w_f = jnp.pad(w_in[layer, :, D_QKV:], ((0, 0), (0, LANES - H_B))).astype(BF16)
    f_bias = jnp.pad(forget_bias, (0, LANES - H_B)).reshape(1, LANES)
    *qkv_a, q_aug, k_aug, v_t = _proj(x, b, s, mix_norm, w_in, layer, w_f, f_bias, head_gain)

    out_b = _fox(q_aug, k_aug, v_t).reshape(b * s, D_B)

    bias = _band_bias(rel_bias_table)
    parts = [_dilated(qkv, bias, p) for p, qkv in enumerate(qkv_a)]
    out_a = _combine([pt[0] for pt in parts], [pt[1] for pt in parts])

    return _out_proj(out_a, out_b, w_out, x)


def kernel(x, ffn1_norm, ffn1_w_in, ffn1_w_out, mix_norm, w_in, q_norm_a, k_norm_a, q_norm_b,
           k_norm_b, forget_bias, rel_bias_table, w_out, ffn2_norm, ffn2_w_in, ffn2_w_out):
    b, s, d = x.shape
    depth = ffn1_norm.shape[0]
    x = x.reshape(b * s, d)
    for l in range(depth):
        x = _ffn(x, ffn1_norm[l], ffn1_w_in[l], ffn1_w_out[l])
        x = _mixer(x, b, s, mix_norm[l], w_in, l, q_norm_a[l], k_norm_a[l], q_norm_b[l],
                   k_norm_b[l], forget_bias[l], rel_bias_table, w_out[l])
        x = _ffn(x, ffn2_norm[l], ffn2_w_in[l], ffn2_w_out[l])
    return x.reshape(b, s, d)
```

```python
import functools
import math

import numpy as np
import jax
import jax.numpy as jnp
from jax import lax
from jax.experimental import pallas as pl
from jax.experimental.pallas import tpu as pltpu

D_MODEL = 2048
HEAD_DIM = 128
N_HEADS = D_MODEL // HEAD_DIM
H_A = N_HEADS // 2
H_B = N_HEADS - H_A
D_A = H_A * HEAD_DIM
D_B = H_B * HEAD_DIM
D_QKV = 3 * D_A + 3 * D_B
DILATED_PATTERNS = ((128, 1), (512, 4), (2048, 16))
BAND = 128
NUM_BUCKETS = 32
MAX_DISTANCE = 2048
D_FF = ((8 * D_MODEL // 3 + 127) // 128) * 128
RMS_EPS = 1e-6
NEG_INF = -1e30
ATTN_SCALE = HEAD_DIM ** -0.5

LANES = 128
FF_TILE = 512
FFN_ROW_TILE = 2048
FFN_CHUNK = 256
FFN_VMEM_LIMIT = 60 * 1024 * 1024
ROW_TILE = 1024
COL_TILE = 512
FOX_TQ = 512
FOX_TK = 512
FOX_HEADS = 4
VMEM_LIMIT = 56 * 1024 * 1024

F32 = jnp.float32
BF16 = jnp.bfloat16


def _params(*sem):
    return pltpu.CompilerParams(dimension_semantics=sem, vmem_limit_bytes=VMEM_LIMIT)


def _rms_rows(x, gain):
    ms = jnp.mean(x * x, axis=-1, keepdims=True)
    return x * lax.rsqrt(ms + RMS_EPS) * gain


def _ffn_kernel(x_hbm, g_ref, wg_ref, wu_ref, wo_ref, o_hbm, acc_ref, h_ref, in_sem, out_sem):
    i = pl.program_id(0)
    j = pl.program_id(1)
    last_i = pl.num_programs(0) - 1
    last_j = pl.num_programs(1) - 1
    pieces = [(r, c) for r in range(FFN_ROW_TILE // ROW_TILE) for c in range(D_MODEL // COL_TILE)]

    def x_copy(c):
        src = x_hbm.at[pl.ds(i * FFN_ROW_TILE + c * FFN_CHUNK, FFN_CHUNK), :]
        return pltpu.make_async_copy(src, acc_ref.at[pl.ds(c * FFN_CHUNK, FFN_CHUNK), :],
                                     in_sem.at[c])

    def out_copy(row_tile, r, c):
        rows, cols = pl.ds(r * ROW_TILE, ROW_TILE), pl.ds(c * COL_TILE, COL_TILE)
        dst = o_hbm.at[pl.ds(row_tile * FFN_ROW_TILE + r * ROW_TILE, ROW_TILE), cols]
        return pltpu.make_async_copy(acc_ref.at[rows, cols], dst,
                                     out_sem.at[r * (D_MODEL // COL_TILE) + c])

    @pl.when(j == 0)
    def _():
        @pl.when(i > 0)
        def _():
            for r, c in pieces:
                out_copy(i - 1, r, c).wait()

        n_chunks = FFN_ROW_TILE // FFN_CHUNK
        for c in range(n_chunks):
            x_copy(c).start()
        for c in range(n_chunks):
            x_copy(c).wait()
            rows = slice(c * FFN_CHUNK, (c + 1) * FFN_CHUNK)
            h_ref[rows, :] = _rms_rows(acc_ref[rows, :], g_ref[...]).astype(BF16)

    col = lax.broadcasted_iota(jnp.int32, (1, FF_TILE), 1)
    repeated = j * FF_TILE - jnp.minimum(j * FF_TILE, D_FF - FF_TILE)

    def tile(write_back):
        for r in range(FFN_ROW_TILE // ROW_TILE):
            rows = slice(r * ROW_TILE, (r + 1) * ROW_TILE)
            h = h_ref[rows, :]
            gate = jnp.dot(h, wg_ref[...].astype(BF16), preferred_element_type=F32)
            up = jnp.dot(h, wu_ref[...].astype(BF16), preferred_element_type=F32)
            act = gate * (1.0 / (1.0 + jnp.exp(-gate))) * (0.5 * up)
            act = jnp.where(col >= repeated, act, 0.0).astype(BF16)
            for c in range(D_MODEL // COL_TILE):
                sl = slice(c * COL_TILE, (c + 1) * COL_TILE)
                acc_ref[rows, sl] += jnp.dot(act, wo_ref[:, sl].astype(BF16),
                                             preferred_element_type=F32)
                if write_back:
                    out_copy(i, r, c).start()

    @pl.when(j < last_j)
    def _():
        tile(False)

    @pl.when(j == last_j)
    def _():
        tile(True)

        @pl.when(i == last_i)
        def _():
            for r, c in pieces:
                out_copy(i, r, c).wait()


def _ffn(x, gain, w_in, w_out):
    m = x.shape[0]
    n_ff = pl.cdiv(D_FF, FF_TILE)

    def ff_start(j):
        return pl.multiple_of(jnp.minimum(j * FF_TILE, D_FF - FF_TILE), LANES)

    return pl.pallas_call(
        _ffn_kernel,
        grid=(m // FFN_ROW_TILE, n_ff),
        in_specs=[
            pl.BlockSpec(memory_space=pl.ANY),
            pl.BlockSpec((1, D_MODEL), lambda i, j: (0, 0)),
            pl.BlockSpec((pl.Element(D_MODEL), pl.Element(FF_TILE)),
                         lambda i, j: (0, ff_start(j))),
            pl.BlockSpec((pl.Element(D_MODEL), pl.Element(FF_TILE)),
                         lambda i, j: (0, pl.multiple_of(D_FF + ff_start(j), LANES))),
            pl.BlockSpec((pl.Element(FF_TILE), pl.Element(D_MODEL)),
                         lambda i, j: (ff_start(j), 0)),
        ],
        out_specs=pl.BlockSpec(memory_space=pl.ANY),
        out_shape=jax.ShapeDtypeStruct((m, D_MODEL), F32),
        scratch_shapes=[
            pltpu.VMEM((FFN_ROW_TILE, D_MODEL), F32),
            pltpu.VMEM((FFN_ROW_TILE, D_MODEL), BF16),
            pltpu.SemaphoreType.DMA((FFN_ROW_TILE // FFN_CHUNK,)),
            pltpu.SemaphoreType.DMA(((FFN_ROW_TILE // ROW_TILE) * (D_MODEL // COL_TILE),)),
        ],
        compiler_params=pltpu.CompilerParams(
            dimension_semantics=("arbitrary", "arbitrary"), vmem_limit_bytes=FFN_VMEM_LIMIT),
        name="ffn",
    )(x, gain.reshape(1, D_MODEL), w_in, w_in, w_out)


HEADS_PER_TILE = COL_TILE // HEAD_DIM
N_PROJ_TILES = D_QKV // COL_TILE
TILES_PER_GROUP = D_A // COL_TILE
A_TILES = 3 * TILES_PER_GROUP
AUG = 2 * HEAD_DIM
LOG2E = math.log2(math.e)


def _log_gate_scan(z, carry):
    c = (jnp.minimum(z, 0.0) - jnp.log1p(jnp.exp(-jnp.abs(z)))) * LOG2E
    row = lax.broadcasted_iota(jnp.int32, c.shape, 0)
    shift = 1
    while shift < c.shape[0]:
        c = c + jnp.where(row >= shift, pltpu.roll(c, shift, axis=0), 0.0)
        shift *= 2
    return c + carry


def _split3(c):
    hi = c.astype(BF16)
    rest = c - hi.astype(F32)
    mid = rest.astype(BF16)
    lo = (rest - mid.astype(F32)).astype(BF16)
    return hi, mid, lo


MXU_WIDTH = 256


def _head_mean_matrix():
    head = np.arange(MXU_WIDTH) // HEAD_DIM
    return jnp.asarray((head[:, None] == head[None, :]) / HEAD_DIM, BF16)


GATE_ONES_LANE = 3 * H_B


def _gate_placement_matrices():
    mats = np.zeros((2, TILES_PER_GROUP, LANES, COL_TILE), np.float32)
    for part in range(TILES_PER_GROUP):
        for hh in range(HEADS_PER_TILE):
            head = part * HEADS_PER_TILE + hh
            col = hh * HEAD_DIM
            for term in range(3):
                mats[0, part, term * H_B + head, col + term] = 1.0
                mats[0, part, GATE_ONES_LANE, col + 3 + term] = 1.0
                mats[1, part, term * H_B + head, col + 3 + term] = -1.0
                mats[1, part, GATE_ONES_LANE, col + term] = 1.0
    return jnp.asarray(mats.reshape(2 * TILES_PER_GROUP, LANES, COL_TILE), BF16)


def _proj_kernel(x_ref, g_ref, w_ref, wf_ref, fb_ref, hg_ref, mean_ref, place_ref,
                 a1_ref, a4_ref, a16_ref, qk_ref, vt_ref,
                 h_ref, cs_ref, carry_ref, y_ref, yf_ref, *, tiles_per_seq):
    i = pl.program_id(0)
    j = pl.program_id(1)

    @pl.when(j == 0)
    def _():
        h_ref[...] = _rms_rows(x_ref[...], g_ref[...]).astype(BF16)

        @pl.when(i % tiles_per_seq == 0)
        def _():
            carry_ref[...] = jnp.zeros(carry_ref.shape, F32)

        z = jnp.dot(h_ref[...], wf_ref[...], preferred_element_type=F32) + fb_ref[...]
        c = _log_gate_scan(z, carry_ref[...])
        carry_ref[...] = c[ROW_TILE - 1:ROW_TILE, :]
        lane = lax.broadcasted_iota(jnp.int32, c.shape, 1)
        packed = jnp.where(lane == GATE_ONES_LANE, 1.0, 0.0)
        for term, part in enumerate(_split3(c)):
            part = jnp.where(lane < H_B, part.astype(F32), 0.0)
            packed = packed + (pltpu.roll(part, term * H_B, axis=1) if term else part)
        cs_ref[...] = packed.astype(BF16)

    def product(transposed=False):
        lhs, rhs = (w_ref[...].astype(BF16), h_ref[...])
        if not transposed:
            lhs, rhs = rhs, lhs
        return lax.dot_general(lhs, rhs, (((1,), (1,)), ((), ())), preferred_element_type=F32)

    def normed():
        r = product()
        sq = (r * r).astype(BF16)
        ms = jnp.concatenate(
            [jnp.dot(sq[:, k:k + MXU_WIDTH], mean_ref[...], preferred_element_type=F32)
             for k in range(0, COL_TILE, MXU_WIDTH)], axis=1)
        return r * lax.rsqrt(ms + RMS_EPS) * hg_ref[...]

    def store_dilated(y):
        a1_ref[0, 0] = y.astype(BF16)
        fine, coarse = a4_ref.shape[1], a16_ref.shape[1]
        step = coarse // fine
        for c in range(HEADS_PER_TILE):
            sl = slice(c * LANES, (c + 1) * LANES)
            y_ref[c] = y[:, sl]
            for rf in range(fine):
                rows = y_ref[c, pl.ds(rf, ROW_TILE // fine, stride=fine), :]
                a4_ref[0, rf, :, sl] = rows.astype(BF16)
                yf_ref[c, rf] = rows
                for q in range(step):
                    rows = yf_ref[c, rf, pl.ds(q, ROW_TILE // coarse, stride=step), :]
                    a16_ref[0, rf + fine * q, :, sl] = rows.astype(BF16)

    def tiles(first_group, n_groups=1):
        lo = first_group * TILES_PER_GROUP
        return jnp.logical_and(j >= lo, j < lo + n_groups * TILES_PER_GROUP)

    def store_forgetting():
        y = normed().astype(BF16)
        aug = jnp.dot(cs_ref[...], place_ref[0], preferred_element_type=F32).astype(BF16)
        for hh in range(HEADS_PER_TILE):
            sl = slice(hh * HEAD_DIM, (hh + 1) * HEAD_DIM)
            qk_ref[0, hh, :, :HEAD_DIM] = y[:, sl]
            qk_ref[0, hh, :, HEAD_DIM:] = aug[:, sl]

    @pl.when(tiles(0, 2))
    def _():
        store_dilated(normed())

    @pl.when(tiles(2))
    def _():
        store_dilated(product())

    @pl.when(tiles(3, 2))
    def _():
        store_forgetting()

    @pl.when(tiles(5))
    def _():
        v_t = product(transposed=True).astype(BF16)
        for hh in range(HEADS_PER_TILE):
            vt_ref[0, hh] = v_t[hh * HEAD_DIM:(hh + 1) * HEAD_DIM, :]


def _proj(x, b, s, gain, w_in, layer, w_f, f_bias, head_gain):
    m = x.shape[0]
    tps = s // ROW_TILE

    def a_spec(dilation):
        return pl.BlockSpec((1, dilation, ROW_TILE // dilation, COL_TILE),
                            lambda i, j: (i // tps, 0, i % tps, jnp.minimum(j, A_TILES - 1)))

    def qk_block(i, j):
        t = jnp.clip(j - qb_first, 0, 2 * TILES_PER_GROUP - 1)
        return (t // TILES_PER_GROUP, i // tps, t % TILES_PER_GROUP, i % tps, 0)

    qb_first = A_TILES
    vt_first = A_TILES + 2 * TILES_PER_GROUP
    dilations = [d for _, d in DILATED_PATTERNS]
    return pl.pallas_call(
        functools.partial(_proj_kernel, tiles_per_seq=tps),
        grid=(m // ROW_TILE, N_PROJ_TILES),
        in_specs=[
            pl.BlockSpec((ROW_TILE, D_MODEL), lambda i, j: (i, 0)),
            pl.BlockSpec((1, D_MODEL), lambda i, j: (0, 0)),
            pl.BlockSpec((None, COL_TILE, D_MODEL), lambda i, j: (layer, j, 0)),
            pl.BlockSpec((D_MODEL, LANES), lambda i, j: (0, 0)),
            pl.BlockSpec((1, LANES), lambda i, j: (0, 0)),
            pl.BlockSpec((1, COL_TILE), lambda i, j: (0, j)),
            pl.BlockSpec((MXU_WIDTH, MXU_WIDTH), lambda i, j: (0, 0)),
            pl.BlockSpec((1, LANES, COL_TILE),
                         lambda i, j: (jnp.clip(j - qb_first, 0, 2 * TILES_PER_GROUP - 1), 0, 0)),
        ],
        out_specs=[a_spec(d) for d in dilations] + [
            pl.BlockSpec((None, 1, HEADS_PER_TILE, ROW_TILE, AUG), qk_block),
            pl.BlockSpec((1, HEADS_PER_TILE, HEAD_DIM, ROW_TILE),
                         lambda i, j: (i // tps, jnp.clip(j - vt_first, 0, TILES_PER_GROUP - 1),
                                       0, i % tps)),
        ],
        out_shape=[jax.ShapeDtypeStruct((b, d, s // d, 3 * D_A), BF16) for d in dilations] + [
            jax.ShapeDtypeStruct((2, b, H_B, s, AUG), BF16),
            jax.ShapeDtypeStruct((b, H_B, HEAD_DIM, s), BF16),
        ],
        scratch_shapes=[
            pltpu.VMEM((ROW_TILE, D_MODEL), BF16),
            pltpu.VMEM((ROW_TILE, LANES), BF16),
            pltpu.VMEM((1, LANES), F32),
            pltpu.VMEM((HEADS_PER_TILE, ROW_TILE, LANES), F32),
            pltpu.VMEM((HEADS_PER_TILE, dilations[1], ROW_TILE // dilations[1], LANES), F32),
        ],
        compiler_params=_params("arbitrary", "arbitrary"),
        name="proj",
    )(x, gain.reshape(1, D_MODEL), jnp.swapaxes(w_in, 1, 2), w_f, f_bias, head_gain,
      _head_mean_matrix(), _gate_placement_matrices())


def _bucket_steps(dilation):
    dist = np.arange(BAND + 1) * dilation
    max_exact = NUM_BUCKETS // 2
    large = max_exact + np.floor(
        np.log(np.maximum(dist, 1) / max_exact) / math.log(MAX_DISTANCE / max_exact)
        * (NUM_BUCKETS - max_exact)).astype(np.int64)
    bucket = np.where(dist < max_exact, dist, np.minimum(large, NUM_BUCKETS - 1))
    steps = [(0, int(bucket[0]))]
    for delta in range(1, BAND + 1):
        if bucket[delta] != bucket[delta - 1]:
            steps.append((delta, int(bucket[delta])))
    return steps


def _bias_kernel(table_ref, o_ref):
    iq = lax.broadcasted_iota(jnp.int32, (BAND, 2 * BAND), 0)
    ik = lax.broadcasted_iota(jnp.int32, (BAND, 2 * BAND), 1)
    delta = iq + BAND - ik
    in_band = jnp.logical_and(delta >= 0, delta <= BAND)
    for p, (_, dilation) in enumerate(DILATED_PATTERNS):
        steps = _bucket_steps(dilation)
        for h in range(H_A):
            val = jnp.full((BAND, 2 * BAND), table_ref[steps[0][1], h], F32)
            for start, bucket in steps[1:]:
                val = jnp.where(delta >= start, table_ref[bucket, h], val)
            o_ref[p, h] = jnp.where(in_band, val, NEG_INF)


def _band_bias(rel_table):
    n_pat = len(DILATED_PATTERNS)
    return pl.pallas_call(
        _bias_kernel,
        in_specs=[pl.BlockSpec(memory_space=pltpu.SMEM)],
        out_specs=pl.BlockSpec(memory_space=pltpu.VMEM),
        out_shape=jax.ShapeDtypeStruct((n_pat, H_A, BAND, 2 * BAND), F32),
        name="band_bias",
    )(rel_table)


DIL_UNITS = 8


def _dilated_kernel(q_ref, kp_ref, kc_ref, vp_ref, vc_ref, bias_ref, o_ref, st_ref):
    n_cls = q_ref.shape[1]
    n_blk = q_ref.shape[2] // BAND
    first = pl.program_id(2) == 0
    key_lane = lax.broadcasted_iota(jnp.int32, (1, 2 * BAND), 1)
    no_prev = jnp.where(jnp.logical_and(first, key_lane < BAND), NEG_INF, 0.0)
    stat_lane = lax.broadcasted_iota(jnp.int32, (BAND, LANES), 1)
    units = [(c, blk) for c in range(n_cls) for blk in range(n_blk)]

    def keys(prev_ref, cur_ref, c, blk, sl):
        if blk == 0:
            return jnp.concatenate([prev_ref[0, c, :, sl], cur_ref[0, c, :BAND, sl]], axis=0)
        return cur_ref[0, c, (blk - 1) * BAND:(blk + 1) * BAND, sl]

    scores = {}
    for c, blk in units:
        for h in range(H_A):
            sl = slice(h * HEAD_DIM, (h + 1) * HEAD_DIM)
            q = q_ref[0, c, blk * BAND:(blk + 1) * BAND, sl]
            scores[c, blk, h] = lax.dot_general(q, keys(kp_ref, kc_ref, c, blk, sl),
                                                (((1,), (1,)), ((), ())),
                                                preferred_element_type=F32)
    for c, blk in units:
        rows = slice(blk * BAND, (blk + 1) * BAND)
        stats = jnp.zeros((BAND, LANES), F32)
        for h in range(H_A):
            sl = slice(h * HEAD_DIM, (h + 1) * HEAD_DIM)
            s = scores[c, blk, h] + bias_ref[0, h]
            if blk == 0:
                s = s + no_prev
            mx = jnp.max(s, axis=-1, keepdims=True)
            p = jnp.exp(s - mx)
            den = jnp.sum(p, axis=-1, keepdims=True)
            o_ref[0, c, rows, sl] = jnp.dot(p.astype(BF16), keys(vp_ref, vc_ref, c, blk, sl),
                                            preferred_element_type=F32).astype(o_ref.dtype)
            stats = jnp.where(stat_lane == h, mx, stats)
            stats = jnp.where(stat_lane == H_A + h, den, stats)
        st_ref[0, c, rows, :] = stats


def _dilated(qkv, bias, pattern):
    b, dilation, n_sub, _ = qkv.shape
    n_blk = min(DIL_UNITS, n_sub // BAND)
    n_cls = DIL_UNITS // n_blk
    rows = n_blk * BAND
    nb = n_sub // rows

    def cur(which):
        return pl.BlockSpec((1, n_cls, rows, D_A), lambda bi, r, i: (bi, r, i, which))

    def prev(which):
        return pl.BlockSpec((1, n_cls, BAND, D_A),
                            lambda bi, r, i: (bi, r, jnp.maximum(i * n_blk - 1, 0), which))

    return pl.pallas_call(
        _dilated_kernel,
        grid=(b, dilation // n_cls, nb),
        in_specs=[
            cur(0), prev(1), cur(1), prev(2), cur(2),
            pl.BlockSpec((1, H_A, BAND, 2 * BAND), lambda bi, r, i: (pattern, 0, 0, 0)),
        ],
        out_specs=[
            pl.BlockSpec((1, n_cls, rows, D_A), lambda bi, r, i: (bi, r, i, 0)),
            pl.BlockSpec((1, n_cls, rows, LANES), lambda bi, r, i: (bi, r, i, 0)),
        ],
        out_shape=[
            jax.ShapeDtypeStruct((b, dilation, n_sub, D_A), BF16),
            jax.ShapeDtypeStruct((b, dilation, n_sub, LANES), F32),
        ],
        compiler_params=_params("parallel", "parallel", "arbitrary"),
        name=f"dilated_{dilation}",
    )(qkv, qkv, qkv, qkv, qkv, bias)


COMBINE_ROWS = 512


def _combine_kernel(*refs):
    n_pat = len(DILATED_PATTERNS)
    num_refs, st_refs = refs[:n_pat], refs[n_pat:2 * n_pat]
    o_ref, num_buf, st_buf, num_tmp, st_tmp = refs[2 * n_pat:]
    dilations = [ref.shape[1] for ref in num_refs]
    fine = min(d for d in dilations if d > 1)
    heads = [slice(h * HEAD_DIM, (h + 1) * HEAD_DIM) for h in range(H_A)]
    for p, dilation in enumerate(dilations):
        if dilation == 1:
            continue
        step = dilation // fine
        for rf in range(fine):
            rows = pl.ds(rf, COMBINE_ROWS // fine, stride=fine)
            if step == 1:
                st_buf[p, rows, :] = st_refs[p][0, rf]
                for h, sl in enumerate(heads):
                    num_buf[p, h, rows, :] = num_refs[p][0, rf, :, sl].astype(F32)
                continue
            for q in range(step):
                part = pl.ds(q, COMBINE_ROWS // dilation, stride=step)
                st_tmp[part, :] = st_refs[p][0, rf + fine * q]
                for h, sl in enumerate(heads):
                    num_tmp[h, part, :] = num_refs[p][0, rf + fine * q, :, sl].astype(F32)
            st_buf[p, rows, :] = st_tmp[...]
            for h in range(H_A):
                num_buf[p, h, rows, :] = num_tmp[h]
    stats = [st_refs[p][0, 0] if d == 1 else st_buf[p] for p, d in enumerate(dilations)]
    for h in range(H_A):
        sl = slice(h * HEAD_DIM, (h + 1) * HEAD_DIM)
        mxs = [st[:, h:h + 1] for st in stats]
        dens = [st[:, H_A + h:H_A + h + 1] for st in stats]
        mx = functools.reduce(jnp.maximum, mxs)
        num = None
        den = None
        for p, (pden, pmx) in enumerate(zip(dens, mxs)):
            scale = jnp.exp(pmx - mx)
            if dilations[p] == 1:
                pnum = num_refs[p][0, 0, :, sl].astype(F32)
            else:
                pnum = num_buf[p, h]
            num = pnum * scale if num is None else num + pnum * scale
            den = pden * scale if den is None else den + pden * scale
        o_ref[:, sl] = (num / den).astype(BF16)


def _combine(nums, stats):
    b, _, s, _ = nums[0].shape
    tps = s // COMBINE_ROWS

    def spec(arr):
        dilation, width = arr.shape[1], arr.shape[3]
        return pl.BlockSpec((1, dilation, COMBINE_ROWS // dilation, width),
                            lambda i: (i // tps, 0, i % tps, 0))

    n_pat = len(nums)
    fine = min(a.shape[1] for a in nums if a.shape[1] > 1)
    return pl.pallas_call(
        _combine_kernel,
        grid=(b * tps,),
        in_specs=[spec(a) for a in nums] + [spec(a) for a in stats],
        out_specs=pl.BlockSpec((COMBINE_ROWS, D_A), lambda i: (i, 0)),
        out_shape=jax.ShapeDtypeStruct((b * s, D_A), BF16),
        scratch_shapes=[
            pltpu.VMEM((n_pat, H_A, COMBINE_ROWS, HEAD_DIM), F32),
            pltpu.VMEM((n_pat, COMBINE_ROWS, LANES), F32),
            pltpu.VMEM((H_A, COMBINE_ROWS // fine, HEAD_DIM), F32),
            pltpu.VMEM((COMBINE_ROWS // fine, LANES), F32),
        ],
        compiler_params=_params("parallel"),
        name="combine_a",
    )(*nums, *stats)


def _fox_kernel(q_ref, k_ref, vt_ref, o_ref, m_ref, l_ref, acc_ref, sa_ref, sb_ref):
    iq = pl.program_id(2)
    m_ref[...] = jnp.full(m_ref.shape, NEG_INF, F32)
    l_ref[...] = jnp.zeros(l_ref.shape, F32)
    acc_ref[...] = jnp.zeros(acc_ref.shape, F32)

    def scores_into(s_ref, kb):
        start = pl.multiple_of(kb * FOX_TK, FOX_TK)
        for hh in range(FOX_HEADS):
            k = k_ref[0, hh, pl.ds(start, FOX_TK), :]
            s_ref[hh] = lax.dot_general(k, q_ref[0, hh], (((1,), (1,)), ((), ())),
                                        preferred_element_type=F32)

    def accumulate(s_ref, kb, diagonal):
        start = pl.multiple_of(kb * FOX_TK, FOX_TK)
        for hh in range(FOX_HEADS):
            st = s_ref[hh]
            if diagonal:
                key = lax.broadcasted_iota(jnp.int32, st.shape, 0)
                qry = lax.broadcasted_iota(jnp.int32, st.shape, 1)
                st = jnp.where(key <= qry, st, NEG_INF)
            m_prev = m_ref[hh]
            m_new = jnp.maximum(m_prev, jnp.max(st, axis=0, keepdims=True))
            alpha = jnp.exp2(m_prev - m_new)
            p = jnp.exp2(st - m_new)
            l_ref[hh] = alpha * l_ref[hh] + jnp.sum(p, axis=0, keepdims=True)
            vt = vt_ref[0, hh, :, pl.ds(start, FOX_TK)]
            acc_ref[hh] = alpha * acc_ref[hh] + jnp.dot(vt, p.astype(BF16),
                                                        preferred_element_type=F32)
            m_ref[hh] = m_new

    scores_into(sa_ref, 0)

    def pair(p, carry):
        scores_into(sb_ref, 2 * p + 1)
        accumulate(sa_ref, 2 * p, False)
        scores_into(sa_ref, 2 * p + 2)
        accumulate(sb_ref, 2 * p + 1, False)
        return carry

    lax.fori_loop(0, iq // 2, pair, 0)

    @pl.when(iq % 2 == 0)
    def _():
        accumulate(sa_ref, iq, True)

    @pl.when(iq % 2 == 1)
    def _():
        scores_into(sb_ref, iq)
        accumulate(sa_ref, iq - 1, False)
        accumulate(sb_ref, iq, True)
    for hh in range(FOX_HEADS):
        o_ref[0, :, hh * HEAD_DIM:(hh + 1) * HEAD_DIM] = (
            acc_ref[hh] / l_ref[hh]).T.astype(BF16)


def _fox(qk_aug, v_t):
    _, b, h_b, s, _ = qk_aug.shape
    assert FOX_TQ == FOX_TK
    return pl.pallas_call(
        _fox_kernel,
        grid=(b, h_b // FOX_HEADS, s // FOX_TQ),
        in_specs=[
            pl.BlockSpec((None, 1, FOX_HEADS, FOX_TQ, AUG), lambda bi, h, i: (0, bi, h, i, 0)),
            pl.BlockSpec((None, 1, FOX_HEADS, s, AUG), lambda bi, h, i: (1, bi, h, 0, 0)),
            pl.BlockSpec((1, FOX_HEADS, HEAD_DIM, s), lambda bi, h, i: (bi, h, 0, 0)),
        ],
        out_specs=pl.BlockSpec((1, FOX_TQ, FOX_HEADS * HEAD_DIM), lambda bi, h, i: (bi, i, h)),
        out_shape=jax.ShapeDtypeStruct((b, s, h_b * HEAD_DIM), BF16),
        scratch_shapes=[
            pltpu.VMEM((FOX_HEADS, 1, FOX_TQ), F32),
            pltpu.VMEM((FOX_HEADS, 1, FOX_TQ), F32),
            pltpu.VMEM((FOX_HEADS, HEAD_DIM, FOX_TQ), F32),
            pltpu.VMEM((FOX_HEADS, FOX_TK, FOX_TQ), F32),
            pltpu.VMEM((FOX_HEADS, FOX_TK, FOX_TQ), F32),
        ],
        compiler_params=_params("parallel", "parallel", "arbitrary"),
        name="fox",
    )(qk_aug, qk_aug, v_t)


def _out_proj_kernel(a_ref, b_ref, w_ref, x_ref, o_ref):
    mixed = jnp.concatenate([a_ref[...], b_ref[...]], axis=-1)
    o_ref[...] = x_ref[...] + jnp.dot(mixed, w_ref[...].astype(BF16),
                                      preferred_element_type=F32)


OUT_ROW_TILE = 2048


def _out_proj(out_a, out_b, w_out, x):
    m = x.shape[0]
    return pl.pallas_call(
        _out_proj_kernel,
        grid=(m // OUT_ROW_TILE, D_MODEL // COL_TILE),
        in_specs=[
            pl.BlockSpec((OUT_ROW_TILE, D_A), lambda i, j: (i, 0)),
            pl.BlockSpec((OUT_ROW_TILE, D_B), lambda i, j: (i, 0)),
            pl.BlockSpec((D_A + D_B, COL_TILE), lambda i, j: (0, j)),
            pl.BlockSpec((OUT_ROW_TILE, COL_TILE), lambda i, j: (i, j)),
        ],
        out_specs=pl.BlockSpec((OUT_ROW_TILE, COL_TILE), lambda i, j: (i, j)),
        out_shape=jax.ShapeDtypeStruct((m, D_MODEL), F32),
        compiler_params=_params("parallel", "parallel"),
        name="out_proj",
    )(out_a, out_b, w_out, x)


def _mixer(x, b, s, mix_norm, w_in, layer, q_norm_a, k_norm_a, q_norm_b, k_norm_b, forget_bias,
           rel_bias_table, w_out):
    ones = jnp.ones((D_A,), F32)
    head_gain = jnp.concatenate([
        jnp.tile(q_norm_a * ATTN_SCALE, H_A), jnp.tile(k_norm_a, H_A), ones,
        jnp.tile(q_norm_b * (ATTN_SCALE * LOG2E), H_B), jnp.tile(k_norm_b, H_B), ones,
    ]).reshape(1, D_QKV)
    w_f = jnp.pad(w_in[layer, :, D_QKV:], ((0, 0), (0, LANES - H_B))).astype(BF16)
    f_bias = jnp.pad(forget_bias, (0, LANES - H_B)).reshape(1, LANES)
    *qkv_a, qk_aug, v_t = _proj(x, b, s, mix_norm, w_in, layer, w_f, f_bias, head_gain)

    out_b = _fox(qk_aug, v_t).reshape(b * s, D_B)

    bias = _band_bias(rel_bias_table)
    parts = [_dilated(qkv, bias, p) for p, qkv in enumerate(qkv_a)]
    out_a = _combine([pt[0] for pt in parts], [pt[1] for pt in parts])

    return _out_proj(out_a, out_b, w_out, x)


def kernel(x, ffn1_norm, ffn1_w_in, ffn1_w_out, mix_norm, w_in, q_norm_a, k_norm_a, q_norm_b,
           k_norm_b, forget_bias, rel_bias_table, w_out, ffn2_norm, ffn2_w_in, ffn2_w_out):
    b, s, d = x.shape
    depth = ffn1_norm.shape[0]
    x = x.reshape(b * s, d)
    for l in range(depth):
        x = _ffn(x, ffn1_norm[l], ffn1_w_in[l], ffn1_w_out[l])
        x = _mixer(x, b, s, mix_norm[l], w_in, l, q_norm_a[l], k_norm_a[l], q_norm_b[l],
                   k_norm_b[l], forget_bias[l], rel_bias_table, w_out[l])
        x = _ffn(x, ffn2_norm[l], ffn2_w_in[l], ffn2_w_out[l])
    return x.reshape(b, s, d)
```

```python
import functools
import math

import numpy as np
import jax
import jax.numpy as jnp
from jax import lax
from jax.experimental import pallas as pl
from jax.experimental.pallas import tpu as pltpu

D_MODEL = 2048
HEAD_DIM = 128
N_HEADS = D_MODEL // HEAD_DIM
H_A = N_HEADS // 2
H_B = N_HEADS - H_A
D_A = H_A * HEAD_DIM
D_B = H_B * HEAD_DIM
D_QKV = 3 * D_A + 3 * D_B
DILATED_PATTERNS = ((128, 1), (512, 4), (2048, 16))
BAND = 128
NUM_BUCKETS = 32
MAX_DISTANCE = 2048
D_FF = ((8 * D_MODEL // 3 + 127) // 128) * 128
RMS_EPS = 1e-6
NEG_INF = -1e30
ATTN_SCALE = HEAD_DIM ** -0.5

LANES = 128
FF_TILE = 512
FFN_ROW_TILE = 2048
FFN_CHUNK = 256
FFN_VMEM_LIMIT = 60 * 1024 * 1024
ROW_TILE = 1024
COL_TILE = 512
FOX_TQ = 512
FOX_TK = 512
FOX_HEADS = 4
VMEM_LIMIT = 56 * 1024 * 1024

F32 = jnp.float32
BF16 = jnp.bfloat16


def _params(*sem):
    return pltpu.CompilerParams(dimension_semantics=sem, vmem_limit_bytes=VMEM_LIMIT)


def _rms_rows(x, gain):
    ms = jnp.mean(x * x, axis=-1, keepdims=True)
    return x * lax.rsqrt(ms + RMS_EPS) * gain


def _ffn_kernel(x_hbm, g_ref, wg_ref, wu_ref, wo_ref, o_hbm, acc_ref, h_ref, in_sem, out_sem):
    i = pl.program_id(0)
    j = pl.program_id(1)
    last_i = pl.num_programs(0) - 1
    last_j = pl.num_programs(1) - 1
    pieces = [(r, c) for r in range(FFN_ROW_TILE // ROW_TILE) for c in range(D_MODEL // COL_TILE)]

    def x_copy(c):
        src = x_hbm.at[pl.ds(i * FFN_ROW_TILE + c * FFN_CHUNK, FFN_CHUNK), :]
        return pltpu.make_async_copy(src, acc_ref.at[pl.ds(c * FFN_CHUNK, FFN_CHUNK), :],
                                     in_sem.at[c])

    def out_copy(row_tile, r, c):
        rows, cols = pl.ds(r * ROW_TILE, ROW_TILE), pl.ds(c * COL_TILE, COL_TILE)
        dst = o_hbm.at[pl.ds(row_tile * FFN_ROW_TILE + r * ROW_TILE, ROW_TILE), cols]
        return pltpu.make_async_copy(acc_ref.at[rows, cols], dst,
                                     out_sem.at[r * (D_MODEL // COL_TILE) + c])

    @pl.when(j == 0)
    def _():
        @pl.when(i > 0)
        def _():
            for r, c in pieces:
                out_copy(i - 1, r, c).wait()

        n_chunks = FFN_ROW_TILE // FFN_CHUNK
        for c in range(n_chunks):
            x_copy(c).start()
        for c in range(n_chunks):
            x_copy(c).wait()
            rows = slice(c * FFN_CHUNK, (c + 1) * FFN_CHUNK)
            h_ref[rows, :] = _rms_rows(acc_ref[rows, :], g_ref[...]).astype(BF16)

    col = lax.broadcasted_iota(jnp.int32, (1, FF_TILE), 1)
    repeated = j * FF_TILE - jnp.minimum(j * FF_TILE, D_FF - FF_TILE)

    def tile(write_back):
        for r in range(FFN_ROW_TILE // ROW_TILE):
            rows = slice(r * ROW_TILE, (r + 1) * ROW_TILE)
            h = h_ref[rows, :]
            gate = jnp.dot(h, wg_ref[...].astype(BF16), preferred_element_type=F32)
            up = jnp.dot(h, wu_ref[...].astype(BF16), preferred_element_type=F32)
            act = gate * (1.0 / (1.0 + jnp.exp(-gate))) * (0.5 * up)
            act = jnp.where(col >= repeated, act, 0.0).astype(BF16)
            for c in range(D_MODEL // COL_TILE):
                sl = slice(c * COL_TILE, (c + 1) * COL_TILE)
                acc_ref[rows, sl] += jnp.dot(act, wo_ref[:, sl].astype(BF16),
                                             preferred_element_type=F32)
                if write_back:
                    out_copy(i, r, c).start()

    @pl.when(j < last_j)
    def _():
        tile(False)

    @pl.when(j == last_j)
    def _():
        tile(True)

        @pl.when(i == last_i)
        def _():
            for r, c in pieces:
                out_copy(i, r, c).wait()


def _ffn(x, gain, w_in, w_out):
    m = x.shape[0]
    n_ff = pl.cdiv(D_FF, FF_TILE)

    def ff_start(j):
        return pl.multiple_of(jnp.minimum(j * FF_TILE, D_FF - FF_TILE), LANES)

    return pl.pallas_call(
        _ffn_kernel,
        grid=(m // FFN_ROW_TILE, n_ff),
        in_specs=[
            pl.BlockSpec(memory_space=pl.ANY),
            pl.BlockSpec((1, D_MODEL), lambda i, j: (0, 0)),
            pl.BlockSpec((pl.Element(D_MODEL), pl.Element(FF_TILE)),
                         lambda i, j: (0, ff_start(j))),
            pl.BlockSpec((pl.Element(D_MODEL), pl.Element(FF_TILE)),
                         lambda i, j: (0, pl.multiple_of(D_FF + ff_start(j), LANES))),
            pl.BlockSpec((pl.Element(FF_TILE), pl.Element(D_MODEL)),
                         lambda i, j: (ff_start(j), 0)),
        ],
        out_specs=pl.BlockSpec(memory_space=pl.ANY),
        out_shape=jax.ShapeDtypeStruct((m, D_MODEL), F32),
        scratch_shapes=[
            pltpu.VMEM((FFN_ROW_TILE, D_MODEL), F32),
            pltpu.VMEM((FFN_ROW_TILE, D_MODEL), BF16),
            pltpu.SemaphoreType.DMA((FFN_ROW_TILE // FFN_CHUNK,)),
            pltpu.SemaphoreType.DMA(((FFN_ROW_TILE // ROW_TILE) * (D_MODEL // COL_TILE),)),
        ],
        compiler_params=pltpu.CompilerParams(
            dimension_semantics=("arbitrary", "arbitrary"), vmem_limit_bytes=FFN_VMEM_LIMIT),
        name="ffn",
    )(x, gain.reshape(1, D_MODEL), w_in, w_in, w_out)


HEADS_PER_TILE = COL_TILE // HEAD_DIM
N_PROJ_TILES = D_QKV // COL_TILE
TILES_PER_GROUP = D_A // COL_TILE
A_TILES = 3 * TILES_PER_GROUP
AUG = 2 * HEAD_DIM
LOG2E = math.log2(math.e)


def _log_gate_scan(z, carry):
    c = (jnp.minimum(z, 0.0) - jnp.log1p(jnp.exp(-jnp.abs(z)))) * LOG2E
    row = lax.broadcasted_iota(jnp.int32, c.shape, 0)
    shift = 1
    while shift < c.shape[0]:
        c = c + jnp.where(row >= shift, pltpu.roll(c, shift, axis=0), 0.0)
        shift *= 2
    return c + carry


def _split3(c):
    hi = c.astype(BF16)
    rest = c - hi.astype(F32)
    mid = rest.astype(BF16)
    lo = (rest - mid.astype(F32)).astype(BF16)
    return hi, mid, lo


MXU_WIDTH = 256


def _head_mean_matrix():
    head = np.arange(MXU_WIDTH) // HEAD_DIM
    return jnp.asarray((head[:, None] == head[None, :]) / HEAD_DIM, BF16)


GATE_ONES_LANE = 3 * H_B


def _gate_placement_matrices():
    mats = np.zeros((2, TILES_PER_GROUP, LANES, COL_TILE), np.float32)
    for part in range(TILES_PER_GROUP):
        for hh in range(HEADS_PER_TILE):
            head = part * HEADS_PER_TILE + hh
            col = hh * HEAD_DIM
            for term in range(3):
                mats[0, part, term * H_B + head, col + term] = 1.0
                mats[0, part, GATE_ONES_LANE, col + 3 + term] = 1.0
                mats[1, part, term * H_B + head, col + 3 + term] = -1.0
                mats[1, part, GATE_ONES_LANE, col + term] = 1.0
    return jnp.asarray(mats.reshape(2 * TILES_PER_GROUP, LANES, COL_TILE), BF16)


def _proj_kernel(x_ref, g_ref, w_ref, wf_ref, fb_ref, hg_ref, mean_ref, place_ref,
                 a1_ref, a4_ref, a16_ref, qk_ref, vt_ref,
                 h_ref, cs_ref, carry_ref, y_ref, yf_ref, *, tiles_per_seq):
    i = pl.program_id(0)
    j = pl.program_id(1)

    @pl.when(j == 0)
    def _():
        h_ref[...] = _rms_rows(x_ref[...], g_ref[...]).astype(BF16)

        @pl.when(i % tiles_per_seq == 0)
        def _():
            carry_ref[...] = jnp.zeros(carry_ref.shape, F32)

        z = jnp.dot(h_ref[...], wf_ref[...], preferred_element_type=F32) + fb_ref[...]
        c = _log_gate_scan(z, carry_ref[...])
        carry_ref[...] = c[ROW_TILE - 1:ROW_TILE, :]
        lane = lax.broadcasted_iota(jnp.int32, c.shape, 1)
        packed = jnp.where(lane == GATE_ONES_LANE, 1.0, 0.0)
        for term, part in enumerate(_split3(c)):
            part = jnp.where(lane < H_B, part.astype(F32), 0.0)
            packed = packed + (pltpu.roll(part, term * H_B, axis=1) if term else part)
        cs_ref[...] = packed.astype(BF16)

    def product(transposed=False):
        lhs, rhs = (w_ref[...].astype(BF16), h_ref[...])
        if not transposed:
            lhs, rhs = rhs, lhs
        return lax.dot_general(lhs, rhs, (((1,), (1,)), ((), ())), preferred_element_type=F32)

    def normed():
        r = product()
        sq = (r * r).astype(BF16)
        ms = jnp.concatenate(
            [jnp.dot(sq[:, k:k + MXU_WIDTH], mean_ref[...], preferred_element_type=F32)
             for k in range(0, COL_TILE, MXU_WIDTH)], axis=1)
        return r * lax.rsqrt(ms + RMS_EPS) * hg_ref[...]

    def store_dilated(y):
        a1_ref[0, 0] = y.astype(BF16)
        fine, coarse = a4_ref.shape[1], a16_ref.shape[1]
        step = coarse // fine
        for c in range(HEADS_PER_TILE):
            sl = slice(c * LANES, (c + 1) * LANES)
            y_ref[c] = y[:, sl]
            for rf in range(fine):
                rows = y_ref[c, pl.ds(rf, ROW_TILE // fine, stride=fine), :]
                a4_ref[0, rf, :, sl] = rows.astype(BF16)
                yf_ref[c, rf] = rows
                for q in range(step):
                    rows = yf_ref[c, rf, pl.ds(q, ROW_TILE // coarse, stride=step), :]
                    a16_ref[0, rf + fine * q, :, sl] = rows.astype(BF16)

    def tiles(first_group, n_groups=1):
        lo = first_group * TILES_PER_GROUP
        return jnp.logical_and(j >= lo, j < lo + n_groups * TILES_PER_GROUP)

    def store_forgetting():
        y = normed().astype(BF16)
        aug = jnp.dot(cs_ref[...], place_ref[0], preferred_element_type=F32).astype(BF16)
        for hh in range(HEADS_PER_TILE):
            sl = slice(hh * HEAD_DIM, (hh + 1) * HEAD_DIM)
            qk_ref[0, hh, :, :HEAD_DIM] = y[:, sl]
            qk_ref[0, hh, :, HEAD_DIM:] = aug[:, sl]

    @pl.when(tiles(0, 2))
    def _():
        store_dilated(normed())

    @pl.when(tiles(2))
    def _():
        store_dilated(product())

    @pl.when(tiles(3, 2))
    def _():
        store_forgetting()

    @pl.when(tiles(5))
    def _():
        v_t = product(transposed=True).astype(BF16)
        for hh in range(HEADS_PER_TILE):
            vt_ref[0, hh] = v_t[hh * HEAD_DIM:(hh + 1) * HEAD_DIM, :]


def _proj(x, b, s, gain, w_in, layer, w_f, f_bias, head_gain):
    m = x.shape[0]
    tps = s // ROW_TILE

    def a_spec(dilation):
        return pl.BlockSpec((1, dilation, ROW_TILE // dilation, COL_TILE),
                            lambda i, j: (i // tps, 0, i % tps, jnp.minimum(j, A_TILES - 1)))

    def qk_block(i, j):
        t = jnp.clip(j - qb_first, 0, 2 * TILES_PER_GROUP - 1)
        return (t // TILES_PER_GROUP, i // tps, t % TILES_PER_GROUP, i % tps, 0)

    qb_first = A_TILES
    vt_first = A_TILES + 2 * TILES_PER_GROUP
    dilations = [d for _, d in DILATED_PATTERNS]
    return pl.pallas_call(
        functools.partial(_proj_kernel, tiles_per_seq=tps),
        grid=(m // ROW_TILE, N_PROJ_TILES),
        in_specs=[
            pl.BlockSpec((ROW_TILE, D_MODEL), lambda i, j: (i, 0)),
            pl.BlockSpec((1, D_MODEL), lambda i, j: (0, 0)),
            pl.BlockSpec((None, COL_TILE, D_MODEL), lambda i, j: (layer, j, 0)),
            pl.BlockSpec((D_MODEL, LANES), lambda i, j: (0, 0)),
            pl.BlockSpec((1, LANES), lambda i, j: (0, 0)),
            pl.BlockSpec((1, COL_TILE), lambda i, j: (0, j)),
            pl.BlockSpec((MXU_WIDTH, MXU_WIDTH), lambda i, j: (0, 0)),
            pl.BlockSpec((1, LANES, COL_TILE),
                         lambda i, j: (jnp.clip(j - qb_first, 0, 2 * TILES_PER_GROUP - 1), 0, 0)),
        ],
        out_specs=[a_spec(d) for d in dilations] + [
            pl.BlockSpec((None, 1, HEADS_PER_TILE, ROW_TILE, AUG), qk_block),
            pl.BlockSpec((1, HEADS_PER_TILE, HEAD_DIM, ROW_TILE),
                         lambda i, j: (i // tps, jnp.clip(j - vt_first, 0, TILES_PER_GROUP - 1),
                                       0, i % tps)),
        ],
        out_shape=[jax.ShapeDtypeStruct((b, d, s // d, 3 * D_A), BF16) for d in dilations] + [
            jax.ShapeDtypeStruct((2, b, H_B, s, AUG), BF16),
            jax.ShapeDtypeStruct((b, H_B, HEAD_DIM, s), BF16),
        ],
        scratch_shapes=[
            pltpu.VMEM((ROW_TILE, D_MODEL), BF16),
            pltpu.VMEM((ROW_TILE, LANES), BF16),
            pltpu.VMEM((1, LANES), F32),
            pltpu.VMEM((HEADS_PER_TILE, ROW_TILE, LANES), F32),
            pltpu.VMEM((HEADS_PER_TILE, dilations[1], ROW_TILE // dilations[1], LANES), F32),
        ],
        compiler_params=_params("arbitrary", "arbitrary"),
        name="proj",
    )(x, gain.reshape(1, D_MODEL), jnp.swapaxes(w_in, 1, 2), w_f, f_bias, head_gain,
      _head_mean_matrix(), _gate_placement_matrices())


def _bucket_steps(dilation):
    dist = np.arange(BAND + 1) * dilation
    max_exact = NUM_BUCKETS // 2
    large = max_exact + np.floor(
        np.log(np.maximum(dist, 1) / max_exact) / math.log(MAX_DISTANCE / max_exact)
        * (NUM_BUCKETS - max_exact)).astype(np.int64)
    bucket = np.where(dist < max_exact, dist, np.minimum(large, NUM_BUCKETS - 1))
    steps = [(0, int(bucket[0]))]
    for delta in range(1, BAND + 1):
        if bucket[delta] != bucket[delta - 1]:
            steps.append((delta, int(bucket[delta])))
    return steps


def _bias_kernel(table_ref, o_ref):
    iq = lax.broadcasted_iota(jnp.int32, (BAND, 2 * BAND), 0)
    ik = lax.broadcasted_iota(jnp.int32, (BAND, 2 * BAND), 1)
    delta = iq + BAND - ik
    in_band = jnp.logical_and(delta >= 0, delta <= BAND)
    for p, (_, dilation) in enumerate(DILATED_PATTERNS):
        steps = _bucket_steps(dilation)
        for h in range(H_A):
            val = jnp.full((BAND, 2 * BAND), table_ref[steps[0][1], h], F32)
            for start, bucket in steps[1:]:
                val = jnp.where(delta >= start, table_ref[bucket, h], val)
            o_ref[p, h] = jnp.where(in_band, val, NEG_INF)


def _band_bias(rel_table):
    n_pat = len(DILATED_PATTERNS)
    return pl.pallas_call(
        _bias_kernel,
        in_specs=[pl.BlockSpec(memory_space=pltpu.SMEM)],
        out_specs=pl.BlockSpec(memory_space=pltpu.VMEM),
        out_shape=jax.ShapeDtypeStruct((n_pat, H_A, BAND, 2 * BAND), F32),
        name="band_bias",
    )(rel_table)


DIL_UNITS = 8


def _dilated_kernel(q_ref, kp_ref, kc_ref, vp_ref, vc_ref, bias_ref, o_ref, st_ref):
    n_cls = q_ref.shape[1]
    n_blk = q_ref.shape[2] // BAND
    first = pl.program_id(2) == 0
    key_lane = lax.broadcasted_iota(jnp.int32, (1, 2 * BAND), 1)
    no_prev = jnp.where(jnp.logical_and(first, key_lane < BAND), NEG_INF, 0.0)
    stat_lane = lax.broadcasted_iota(jnp.int32, (BAND, LANES), 1)
    units = [(c, blk) for c in range(n_cls) for blk in range(n_blk)]

    def keys(prev_ref, cur_ref, c, blk, sl):
        if blk == 0:
            return jnp.concatenate([prev_ref[0, c, :, sl], cur_ref[0, c, :BAND, sl]], axis=0)
        return cur_ref[0, c, (blk - 1) * BAND:(blk + 1) * BAND, sl]

    scores = {}
    for c, blk in units:
        for h in range(H_A):
            sl = slice(h * HEAD_DIM, (h + 1) * HEAD_DIM)
            q = q_ref[0, c, blk * BAND:(blk + 1) * BAND, sl]
            scores[c, blk, h] = lax.dot_general(q, keys(kp_ref, kc_ref, c, blk, sl),
                                                (((1,), (1,)), ((), ())),
                                                preferred_element_type=F32)
    for c, blk in units:
        rows = slice(blk * BAND, (blk + 1) * BAND)
        stats = jnp.zeros((BAND, LANES), F32)
        for h in range(H_A):
            sl = slice(h * HEAD_DIM, (h + 1) * HEAD_DIM)
            s = scores[c, blk, h] + bias_ref[0, h]
            if blk == 0:
                s = s + no_prev
            mx = jnp.max(s, axis=-1, keepdims=True)
            p = jnp.exp(s - mx)
            den = jnp.sum(p, axis=-1, keepdims=True)
            o_ref[0, c, rows, sl] = jnp.dot(p.astype(BF16), keys(vp_ref, vc_ref, c, blk, sl),
                                            preferred_element_type=F32).astype(o_ref.dtype)
            stats = jnp.where(stat_lane == h, mx, stats)
            stats = jnp.where(stat_lane == H_A + h, den, stats)
        st_ref[0, c, rows, :] = stats


def _dilated(qkv, bias, pattern):
    b, dilation, n_sub, _ = qkv.shape
    n_blk = min(DIL_UNITS, n_sub // BAND)
    n_cls = DIL_UNITS // n_blk
    rows = n_blk * BAND
    nb = n_sub // rows

    def cur(which):
        return pl.BlockSpec((1, n_cls, rows, D_A), lambda bi, r, i: (bi, r, i, which))

    def prev(which):
        return pl.BlockSpec((1, n_cls, BAND, D_A),
                            lambda bi, r, i: (bi, r, jnp.maximum(i * n_blk - 1, 0), which))

    return pl.pallas_call(
        _dilated_kernel,
        grid=(b, dilation // n_cls, nb),
        in_specs=[
            cur(0), prev(1), cur(1), prev(2), cur(2),
            pl.BlockSpec((1, H_A, BAND, 2 * BAND), lambda bi, r, i: (pattern, 0, 0, 0)),
        ],
        out_specs=[
            pl.BlockSpec((1, n_cls, rows, D_A), lambda bi, r, i: (bi, r, i, 0)),
            pl.BlockSpec((1, n_cls, rows, LANES), lambda bi, r, i: (bi, r, i, 0)),
        ],
        out_shape=[
            jax.ShapeDtypeStruct((b, dilation, n_sub, D_A), BF16),
            jax.ShapeDtypeStruct((b, dilation, n_sub, LANES), F32),
        ],
        compiler_params=_params("parallel", "parallel", "arbitrary"),
        name=f"dilated_{dilation}",
    )(qkv, qkv, qkv, qkv, qkv, bias)


COMBINE_ROWS = 512


def _combine_kernel(*refs):
    n_pat = len(DILATED_PATTERNS)
    num_refs, st_refs = refs[:n_pat], refs[n_pat:2 * n_pat]
    o_ref, num_buf, st_buf, num_tmp, st_tmp = refs[2 * n_pat:]
    dilations = [ref.shape[1] for ref in num_refs]
    fine = min(d for d in dilations if d > 1)
    heads = [slice(h * HEAD_DIM, (h + 1) * HEAD_DIM) for h in range(H_A)]
    for p, dilation in enumerate(dilations):
        if dilation == 1:
            continue
        step = dilation // fine
        for rf in range(fine):
            rows = pl.ds(rf, COMBINE_ROWS // fine, stride=fine)
            if step == 1:
                st_buf[p, rows, :] = st_refs[p][0, rf]
                for h, sl in enumerate(heads):
                    num_buf[p, h, rows, :] = num_refs[p][0, rf, :, sl].astype(F32)
                continue
            for q in range(step):
                part = pl.ds(q, COMBINE_ROWS // dilation, stride=step)
                st_tmp[part, :] = st_refs[p][0, rf + fine * q]
                for h, sl in enumerate(heads):
                    num_tmp[h, part, :] = num_refs[p][0, rf + fine * q, :, sl].astype(F32)
            st_buf[p, rows, :] = st_tmp[...]
            for h in range(H_A):
                num_buf[p, h, rows, :] = num_tmp[h]
    stats = [st_refs[p][0, 0] if d == 1 else st_buf[p] for p, d in enumerate(dilations)]
    lane = lax.broadcasted_iota(jnp.int32, stats[0].shape, 1)
    mx = functools.reduce(jnp.maximum, stats)
    scales = [jnp.exp(st - mx) for st in stats]
    den = sum(sc * pltpu.roll(st, LANES - H_A, axis=1) for sc, st in zip(scales, stats))
    den = jnp.where(lane < H_A, den, 1.0)
    weights = [sc / den for sc in scales]
    for h in range(H_A):
        sl = slice(h * HEAD_DIM, (h + 1) * HEAD_DIM)
        out = None
        for p, w in enumerate(weights):
            if dilations[p] == 1:
                pnum = num_refs[p][0, 0, :, sl].astype(F32)
            else:
                pnum = num_buf[p, h]
            out = pnum * w[:, h:h + 1] if out is None else out + pnum * w[:, h:h + 1]
        o_ref[:, sl] = out.astype(BF16)


def _combine(nums, stats):
    b, _, s, _ = nums[0].shape
    tps = s // COMBINE_ROWS

    def spec(arr):
        dilation, width = arr.shape[1], arr.shape[3]
        return pl.BlockSpec((1, dilation, COMBINE_ROWS // dilation, width),
                            lambda i: (i // tps, 0, i % tps, 0))

    n_pat = len(nums)
    fine = min(a.shape[1] for a in nums if a.shape[1] > 1)
    return pl.pallas_call(
        _combine_kernel,
        grid=(b * tps,),
        in_specs=[spec(a) for a in nums] + [spec(a) for a in stats],
        out_specs=pl.BlockSpec((COMBINE_ROWS, D_A), lambda i: (i, 0)),
        out_shape=jax.ShapeDtypeStruct((b * s, D_A), BF16),
        scratch_shapes=[
            pltpu.VMEM((n_pat, H_A, COMBINE_ROWS, HEAD_DIM), F32),
            pltpu.VMEM((n_pat, COMBINE_ROWS, LANES), F32),
            pltpu.VMEM((H_A, COMBINE_ROWS // fine, HEAD_DIM), F32),
            pltpu.VMEM((COMBINE_ROWS // fine, LANES), F32),
        ],
        compiler_params=_params("parallel"),
        name="combine_a",
    )(*nums, *stats)


def _fox_kernel(q_ref, k_ref, vt_ref, o_ref, m_ref, l_ref, acc_ref, sa_ref, sb_ref):
    iq = pl.program_id(2)
    m_ref[...] = jnp.full(m_ref.shape, NEG_INF, F32)
    l_ref[...] = jnp.zeros(l_ref.shape, F32)
    acc_ref[...] = jnp.zeros(acc_ref.shape, F32)

    def scores_into(s_ref, kb):
        start = pl.multiple_of(kb * FOX_TK, FOX_TK)
        for hh in range(FOX_HEADS):
            k = k_ref[0, hh, pl.ds(start, FOX_TK), :]
            s_ref[hh] = lax.dot_general(k, q_ref[0, hh], (((1,), (1,)), ((), ())),
                                        preferred_element_type=F32)

    def accumulate(s_ref, kb, diagonal):
        start = pl.multiple_of(kb * FOX_TK, FOX_TK)
        for hh in range(FOX_HEADS):
            st = s_ref[hh]
            if diagonal:
                key = lax.broadcasted_iota(jnp.int32, st.shape, 0)
                qry = lax.broadcasted_iota(jnp.int32, st.shape, 1)
                st = jnp.where(key <= qry, st, NEG_INF)
            m_prev = m_ref[hh]
            m_new = jnp.maximum(m_prev, jnp.max(st, axis=0, keepdims=True))
            alpha = jnp.exp2(m_prev - m_new)
            p = jnp.exp2(st - m_new)
            l_ref[hh] = alpha * l_ref[hh] + jnp.sum(p, axis=0, keepdims=True)
            vt = vt_ref[0, hh, :, pl.ds(start, FOX_TK)]
            acc_ref[hh] = alpha * acc_ref[hh] + jnp.dot(vt, p.astype(BF16),
                                                        preferred_element_type=F32)
            m_ref[hh] = m_new

    scores_into(sa_ref, 0)

    def pair(p, carry):
        scores_into(sb_ref, 2 * p + 1)
        accumulate(sa_ref, 2 * p, False)
        scores_into(sa_ref, 2 * p + 2)
        accumulate(sb_ref, 2 * p + 1, False)
        return carry

    lax.fori_loop(0, iq // 2, pair, 0)

    @pl.when(iq % 2 == 0)
    def _():
        accumulate(sa_ref, iq, True)

    @pl.when(iq % 2 == 1)
    def _():
        scores_into(sb_ref, iq)
        accumulate(sa_ref, iq - 1, False)
        accumulate(sb_ref, iq, True)
    for hh in range(FOX_HEADS):
        o_ref[0, :, hh * HEAD_DIM:(hh + 1) * HEAD_DIM] = (
            acc_ref[hh] / l_ref[hh]).T.astype(BF16)


def _fox(qk_aug, v_t):
    _, b, h_b, s, _ = qk_aug.shape
    assert FOX_TQ == FOX_TK
    return pl.pallas_call(
        _fox_kernel,
        grid=(b, h_b // FOX_HEADS, s // FOX_TQ),
        in_specs=[
            pl.BlockSpec((None, 1, FOX_HEADS, FOX_TQ, AUG), lambda bi, h, i: (0, bi, h, i, 0)),
            pl.BlockSpec((None, 1, FOX_HEADS, s, AUG), lambda bi, h, i: (1, bi, h, 0, 0)),
            pl.BlockSpec((1, FOX_HEADS, HEAD_DIM, s), lambda bi, h, i: (bi, h, 0, 0)),
        ],
        out_specs=pl.BlockSpec((1, FOX_TQ, FOX_HEADS * HEAD_DIM), lambda bi, h, i: (bi, i, h)),
        out_shape=jax.ShapeDtypeStruct((b, s, h_b * HEAD_DIM), BF16),
        scratch_shapes=[
            pltpu.VMEM((FOX_HEADS, 1, FOX_TQ), F32),
            pltpu.VMEM((FOX_HEADS, 1, FOX_TQ), F32),
            pltpu.VMEM((FOX_HEADS, HEAD_DIM, FOX_TQ), F32),
            pltpu.VMEM((FOX_HEADS, FOX_TK, FOX_TQ), F32),
            pltpu.VMEM((FOX_HEADS, FOX_TK, FOX_TQ), F32),
        ],
        compiler_params=_params("parallel", "parallel", "arbitrary"),
        name="fox",
    )(qk_aug, qk_aug, v_t)


def _out_proj_kernel(a_ref, b_ref, w_ref, x_ref, o_ref):
    mixed = jnp.concatenate([a_ref[...], b_ref[...]], axis=-1)
    o_ref[...] = x_ref[...] + jnp.dot(mixed, w_ref[...].astype(BF16),
                                      preferred_element_type=F32)


OUT_ROW_TILE = 2048


def _out_proj(out_a, out_b, w_out, x):
    m = x.shape[0]
    return pl.pallas_call(
        _out_proj_kernel,
        grid=(m // OUT_ROW_TILE, D_MODEL // COL_TILE),
        in_specs=[
            pl.BlockSpec((OUT_ROW_TILE, D_A), lambda i, j: (i, 0)),
            pl.BlockSpec((OUT_ROW_TILE, D_B), lambda i, j: (i, 0)),
            pl.BlockSpec((D_A + D_B, COL_TILE), lambda i, j: (0, j)),
            pl.BlockSpec((OUT_ROW_TILE, COL_TILE), lambda i, j: (i, j)),
        ],
        out_specs=pl.BlockSpec((OUT_ROW_TILE, COL_TILE), lambda i, j: (i, j)),
        out_shape=jax.ShapeDtypeStruct((m, D_MODEL), F32),
        compiler_params=_params("parallel", "parallel"),
        name="out_proj",
    )(out_a, out_b, w_out, x)


def _mixer(x, b, s, mix_norm, w_in, layer, q_norm_a, k_norm_a, q_norm_b, k_norm_b, forget_bias,
           rel_bias_table, w_out):
    ones = jnp.ones((D_A,), F32)
    head_gain = jnp.concatenate([
        jnp.tile(q_norm_a * ATTN_SCALE, H_A), jnp.tile(k_norm_a, H_A), ones,
        jnp.tile(q_norm_b * (ATTN_SCALE * LOG2E), H_B), jnp.tile(k_norm_b, H_B), ones,
    ]).reshape(1, D_QKV)
    w_f = jnp.pad(w_in[layer, :, D_QKV:], ((0, 0), (0, LANES - H_B))).astype(BF16)
    f_bias = jnp.pad(forget_bias, (0, LANES - H_B)).reshape(1, LANES)
    *qkv_a, qk_aug, v_t = _proj(x, b, s, mix_norm, w_in, layer, w_f, f_bias, head_gain)

    out_b = _fox(qk_aug, v_t).reshape(b * s, D_B)

    bias = _band_bias(rel_bias_table)
    parts = [_dilated(qkv, bias, p) for p, qkv in enumerate(qkv_a)]
    out_a = _combine([pt[0] for pt in parts], [pt[1] for pt in parts])

    return _out_proj(out_a, out_b, w_out, x)


def kernel(x, ffn1_norm, ffn1_w_in, ffn1_w_out, mix_norm, w_in, q_norm_a, k_norm_a, q_norm_b,
           k_norm_b, forget_bias, rel_bias_table, w_out, ffn2_norm, ffn2_w_in, ffn2_w_out):
    b, s, d = x.shape
    depth = ffn1_norm.shape[0]
    x = x.reshape(b * s, d)
    for l in range(depth):
        x = _ffn(x, ffn1_norm[l], ffn1_w_in[l], ffn1_w_out[l])
        x = _mixer(x, b, s, mix_norm[l], w_in, l, q_norm_a[l], k_norm_a[l], q_norm_b[l],
                   k_norm_b[l], forget_bias[l], rel_bias_table, w_out[l])
        x = _ffn(x, ffn2_norm[l], ffn2_w_in[l], ffn2_w_out[l])
    return x.reshape(b, s, d)
```

```python
import functools
import math

import numpy as np
import jax
import jax.numpy as jnp
from jax import lax
from jax.experimental import pallas as pl
from jax.experimental.pallas import tpu as pltpu

D_MODEL = 2048
HEAD_DIM = 128
N_HEADS = D_MODEL // HEAD_DIM
H_A = N_HEADS // 2
H_B = N_HEADS - H_A
D_A = H_A * HEAD_DIM
D_B = H_B * HEAD_DIM
D_QKV = 3 * D_A + 3 * D_B
DILATED_PATTERNS = ((128, 1), (512, 4), (2048, 16))
BAND = 128
NUM_BUCKETS = 32
MAX_DISTANCE = 2048
D_FF = ((8 * D_MODEL // 3 + 127) // 128) * 128
RMS_EPS = 1e-6
NEG_INF = -1e30
ATTN_SCALE = HEAD_DIM ** -0.5

LANES = 128
FF_TILE = 512
FFN_ROW_TILE = 2048
FFN_CHUNK = 256
FFN_VMEM_LIMIT = 60 * 1024 * 1024
ROW_TILE = 1024
COL_TILE = 512
FOX_TQ = 512
FOX_TK = 512
FOX_HEADS = 4
VMEM_LIMIT = 56 * 1024 * 1024

F32 = jnp.float32
BF16 = jnp.bfloat16


def _params(*sem):
    return pltpu.CompilerParams(dimension_semantics=sem, vmem_limit_bytes=VMEM_LIMIT)


def _rms_rows(x, gain):
    ms = jnp.mean(x * x, axis=-1, keepdims=True)
    return x * lax.rsqrt(ms + RMS_EPS) * gain


def _ffn_kernel(x_hbm, g_ref, wg_ref, wu_ref, wo_ref, o_hbm, acc_ref, h_ref, in_sem, out_sem):
    i = pl.program_id(0)
    j = pl.program_id(1)
    last_i = pl.num_programs(0) - 1
    last_j = pl.num_programs(1) - 1
    pieces = [(r, c) for r in range(FFN_ROW_TILE // ROW_TILE) for c in range(D_MODEL // COL_TILE)]

    def x_copy(c):
        src = x_hbm.at[pl.ds(i * FFN_ROW_TILE + c * FFN_CHUNK, FFN_CHUNK), :]
        return pltpu.make_async_copy(src, acc_ref.at[pl.ds(c * FFN_CHUNK, FFN_CHUNK), :],
                                     in_sem.at[c])

    def out_copy(row_tile, r, c):
        rows, cols = pl.ds(r * ROW_TILE, ROW_TILE), pl.ds(c * COL_TILE, COL_TILE)
        dst = o_hbm.at[pl.ds(row_tile * FFN_ROW_TILE + r * ROW_TILE, ROW_TILE), cols]
        return pltpu.make_async_copy(acc_ref.at[rows, cols], dst,
                                     out_sem.at[r * (D_MODEL // COL_TILE) + c])

    @pl.when(j == 0)
    def _():
        @pl.when(i > 0)
        def _():
            for r, c in pieces:
                out_copy(i - 1, r, c).wait()

        n_chunks = FFN_ROW_TILE // FFN_CHUNK
        for c in range(n_chunks):
            x_copy(c).start()
        for c in range(n_chunks):
            x_copy(c).wait()
            rows = slice(c * FFN_CHUNK, (c + 1) * FFN_CHUNK)
            h_ref[rows, :] = _rms_rows(acc_ref[rows, :], g_ref[...]).astype(BF16)

    col = lax.broadcasted_iota(jnp.int32, (1, FF_TILE), 1)
    repeated = j * FF_TILE - jnp.minimum(j * FF_TILE, D_FF - FF_TILE)

    def tile(write_back):
        for r in range(FFN_ROW_TILE // ROW_TILE):
            rows = slice(r * ROW_TILE, (r + 1) * ROW_TILE)
            h = h_ref[rows, :]
            gate = jnp.dot(h, wg_ref[...].astype(BF16), preferred_element_type=F32)
            up = jnp.dot(h, wu_ref[...].astype(BF16), preferred_element_type=F32)
            act = gate * (1.0 / (1.0 + jnp.exp(-gate))) * (0.5 * up)
            act = jnp.where(col >= repeated, act, 0.0).astype(BF16)
            for c in range(D_MODEL // COL_TILE):
                sl = slice(c * COL_TILE, (c + 1) * COL_TILE)
                acc_ref[rows, sl] += jnp.dot(act, wo_ref[:, sl].astype(BF16),
                                             preferred_element_type=F32)
                if write_back:
                    out_copy(i, r, c).start()

    @pl.when(j < last_j)
    def _():
        tile(False)

    @pl.when(j == last_j)
    def _():
        tile(True)

        @pl.when(i == last_i)
        def _():
            for r, c in pieces:
                out_copy(i, r, c).wait()


def _ffn(x, gain, w_in, w_out):
    m = x.shape[0]
    n_ff = pl.cdiv(D_FF, FF_TILE)

    def ff_start(j):
        return pl.multiple_of(jnp.minimum(j * FF_TILE, D_FF - FF_TILE), LANES)

    return pl.pallas_call(
        _ffn_kernel,
        grid=(m // FFN_ROW_TILE, n_ff),
        in_specs=[
            pl.BlockSpec(memory_space=pl.ANY),
            pl.BlockSpec((1, D_MODEL), lambda i, j: (0, 0)),
            pl.BlockSpec((pl.Element(D_MODEL), pl.Element(FF_TILE)),
                         lambda i, j: (0, ff_start(j))),
            pl.BlockSpec((pl.Element(D_MODEL), pl.Element(FF_TILE)),
                         lambda i, j: (0, pl.multiple_of(D_FF + ff_start(j), LANES))),
            pl.BlockSpec((pl.Element(FF_TILE), pl.Element(D_MODEL)),
                         lambda i, j: (ff_start(j), 0)),
        ],
        out_specs=pl.BlockSpec(memory_space=pl.ANY),
        out_shape=jax.ShapeDtypeStruct((m, D_MODEL), F32),
        scratch_shapes=[
            pltpu.VMEM((FFN_ROW_TILE, D_MODEL), F32),
            pltpu.VMEM((FFN_ROW_TILE, D_MODEL), BF16),
            pltpu.SemaphoreType.DMA((FFN_ROW_TILE // FFN_CHUNK,)),
            pltpu.SemaphoreType.DMA(((FFN_ROW_TILE // ROW_TILE) * (D_MODEL // COL_TILE),)),
        ],
        compiler_params=pltpu.CompilerParams(
            dimension_semantics=("arbitrary", "arbitrary"), vmem_limit_bytes=FFN_VMEM_LIMIT),
        name="ffn",
    )(x, gain.reshape(1, D_MODEL), w_in, w_in, w_out)


HEADS_PER_TILE = COL_TILE // HEAD_DIM
N_PROJ_TILES = D_QKV // COL_TILE
TILES_PER_GROUP = D_A // COL_TILE
A_TILES = 3 * TILES_PER_GROUP
AUG = 2 * HEAD_DIM
LOG2E = math.log2(math.e)


def _log_gate_scan(z, carry):
    c = (jnp.minimum(z, 0.0) - jnp.log1p(jnp.exp(-jnp.abs(z)))) * LOG2E
    row = lax.broadcasted_iota(jnp.int32, c.shape, 0)
    shift = 1
    while shift < c.shape[0]:
        c = c + jnp.where(row >= shift, pltpu.roll(c, shift, axis=0), 0.0)
        shift *= 2
    return c + carry


def _split3(c):
    hi = c.astype(BF16)
    rest = c - hi.astype(F32)
    mid = rest.astype(BF16)
    lo = (rest - mid.astype(F32)).astype(BF16)
    return hi, mid, lo


MXU_WIDTH = 256


def _head_mean_matrix():
    head = np.arange(MXU_WIDTH) // HEAD_DIM
    return jnp.asarray((head[:, None] == head[None, :]) / HEAD_DIM, BF16)


GATE_ONES_LANE = 3 * H_B


def _gate_placement_matrices():
    mats = np.zeros((2, TILES_PER_GROUP, LANES, COL_TILE), np.float32)
    for part in range(TILES_PER_GROUP):
        for hh in range(HEADS_PER_TILE):
            head = part * HEADS_PER_TILE + hh
            col = hh * HEAD_DIM
            for term in range(3):
                mats[0, part, term * H_B + head, col + term] = 1.0
                mats[0, part, GATE_ONES_LANE, col + 3 + term] = 1.0
                mats[1, part, term * H_B + head, col + 3 + term] = -1.0
                mats[1, part, GATE_ONES_LANE, col + term] = 1.0
    return jnp.asarray(mats.reshape(2 * TILES_PER_GROUP, LANES, COL_TILE), BF16)


def _proj_kernel(x_ref, g_ref, w_ref, wf_ref, fb_ref, hg_ref, mean_ref, place_ref,
                 a1_ref, a4_ref, a16_ref, qk_ref, vt_ref,
                 h_ref, cs_ref, carry_ref, y_ref, yf_ref, *, tiles_per_seq):
    i = pl.program_id(0)
    j = pl.program_id(1)

    @pl.when(j == 0)
    def _():
        h_ref[...] = _rms_rows(x_ref[...], g_ref[...]).astype(BF16)

        @pl.when(i % tiles_per_seq == 0)
        def _():
            carry_ref[...] = jnp.zeros(carry_ref.shape, F32)

        z = jnp.dot(h_ref[...], wf_ref[...], preferred_element_type=F32) + fb_ref[...]
        c = _log_gate_scan(z, carry_ref[...])
        carry_ref[...] = c[ROW_TILE - 1:ROW_TILE, :]
        lane = lax.broadcasted_iota(jnp.int32, c.shape, 1)
        packed = jnp.where(lane == GATE_ONES_LANE, 1.0, 0.0)
        for term, part in enumerate(_split3(c)):
            part = jnp.where(lane < H_B, part.astype(F32), 0.0)
            packed = packed + (pltpu.roll(part, term * H_B, axis=1) if term else part)
        cs_ref[...] = packed.astype(BF16)

    def product(transposed=False):
        lhs, rhs = (w_ref[...].astype(BF16), h_ref[...])
        if not transposed:
            lhs, rhs = rhs, lhs
        return lax.dot_general(lhs, rhs, (((1,), (1,)), ((), ())), preferred_element_type=F32)

    def normed():
        r = product()
        sq = (r * r).astype(BF16)
        ms = jnp.concatenate(
            [jnp.dot(sq[:, k:k + MXU_WIDTH], mean_ref[...], preferred_element_type=F32)
             for k in range(0, COL_TILE, MXU_WIDTH)], axis=1)
        return r * lax.rsqrt(ms + RMS_EPS) * hg_ref[...]

    def store_dilated(y):
        a1_ref[0, 0] = y.astype(BF16)
        fine, coarse = a4_ref.shape[1], a16_ref.shape[1]
        step = coarse // fine
        for c in range(HEADS_PER_TILE):
            sl = slice(c * LANES, (c + 1) * LANES)
            y_ref[c] = y[:, sl]
            for rf in range(fine):
                rows = y_ref[c, pl.ds(rf, ROW_TILE // fine, stride=fine), :]
                a4_ref[0, rf, :, sl] = rows.astype(BF16)
                yf_ref[c, rf] = rows
                for q in range(step):
                    rows = yf_ref[c, rf, pl.ds(q, ROW_TILE // coarse, stride=step), :]
                    a16_ref[0, rf + fine * q, :, sl] = rows.astype(BF16)

    def tiles(first_group, n_groups=1):
        lo = first_group * TILES_PER_GROUP
        return jnp.logical_and(j >= lo, j < lo + n_groups * TILES_PER_GROUP)

    def store_forgetting():
        y = normed().astype(BF16)
        aug = jnp.dot(cs_ref[...], place_ref[0], preferred_element_type=F32).astype(BF16)
        for hh in range(HEADS_PER_TILE):
            sl = slice(hh * HEAD_DIM, (hh + 1) * HEAD_DIM)
            qk_ref[0, hh, :, :HEAD_DIM] = y[:, sl]
            qk_ref[0, hh, :, HEAD_DIM:] = aug[:, sl]

    @pl.when(tiles(0, 2))
    def _():
        store_dilated(normed())

    @pl.when(tiles(2))
    def _():
        store_dilated(product())

    @pl.when(tiles(3, 2))
    def _():
        store_forgetting()

    @pl.when(tiles(5))
    def _():
        v_t = product(transposed=True).astype(BF16)
        for hh in range(HEADS_PER_TILE):
            vt_ref[0, hh] = v_t[hh * HEAD_DIM:(hh + 1) * HEAD_DIM, :]


def _proj(x, b, s, gain, w_in, layer, w_f, f_bias, head_gain):
    m = x.shape[0]
    tps = s // ROW_TILE

    def a_spec(dilation):
        return pl.BlockSpec((1, dilation, ROW_TILE // dilation, COL_TILE),
                            lambda i, j: (i // tps, 0, i % tps, jnp.minimum(j, A_TILES - 1)))

    def qk_block(i, j):
        t = jnp.clip(j - qb_first, 0, 2 * TILES_PER_GROUP - 1)
        return (t // TILES_PER_GROUP, i // tps, t % TILES_PER_GROUP, i % tps, 0)

    qb_first = A_TILES
    vt_first = A_TILES + 2 * TILES_PER_GROUP
    dilations = [d for _, d in DILATED_PATTERNS]
    return pl.pallas_call(
        functools.partial(_proj_kernel, tiles_per_seq=tps),
        grid=(m // ROW_TILE, N_PROJ_TILES),
        in_specs=[
            pl.BlockSpec((ROW_TILE, D_MODEL), lambda i, j: (i, 0)),
            pl.BlockSpec((1, D_MODEL), lambda i, j: (0, 0)),
            pl.BlockSpec((None, COL_TILE, D_MODEL), lambda i, j: (layer, j, 0)),
            pl.BlockSpec((D_MODEL, LANES), lambda i, j: (0, 0)),
            pl.BlockSpec((1, LANES), lambda i, j: (0, 0)),
            pl.BlockSpec((1, COL_TILE), lambda i, j: (0, j)),
            pl.BlockSpec((MXU_WIDTH, MXU_WIDTH), lambda i, j: (0, 0)),
            pl.BlockSpec((1, LANES, COL_TILE),
                         lambda i, j: (jnp.clip(j - qb_first, 0, 2 * TILES_PER_GROUP - 1), 0, 0)),
        ],
        out_specs=[a_spec(d) for d in dilations] + [
            pl.BlockSpec((None, 1, HEADS_PER_TILE, ROW_TILE, AUG), qk_block),
            pl.BlockSpec((1, HEADS_PER_TILE, HEAD_DIM, ROW_TILE),
                         lambda i, j: (i // tps, jnp.clip(j - vt_first, 0, TILES_PER_GROUP - 1),
                                       0, i % tps)),
        ],
        out_shape=[jax.ShapeDtypeStruct((b, d, s // d, 3 * D_A), BF16) for d in dilations] + [
            jax.ShapeDtypeStruct((2, b, H_B, s, AUG), BF16),
            jax.ShapeDtypeStruct((b, H_B, HEAD_DIM, s), BF16),
        ],
        scratch_shapes=[
            pltpu.VMEM((ROW_TILE, D_MODEL), BF16),
            pltpu.VMEM((ROW_TILE, LANES), BF16),
            pltpu.VMEM((1, LANES), F32),
            pltpu.VMEM((HEADS_PER_TILE, ROW_TILE, LANES), F32),
            pltpu.VMEM((HEADS_PER_TILE, dilations[1], ROW_TILE // dilations[1], LANES), F32),
        ],
        compiler_params=_params("arbitrary", "arbitrary"),
        name="proj",
    )(x, gain.reshape(1, D_MODEL), jnp.swapaxes(w_in, 1, 2), w_f, f_bias, head_gain,
      _head_mean_matrix(), _gate_placement_matrices())


def _bucket_steps(dilation):
    dist = np.arange(BAND + 1) * dilation
    max_exact = NUM_BUCKETS // 2
    large = max_exact + np.floor(
        np.log(np.maximum(dist, 1) / max_exact) / math.log(MAX_DISTANCE / max_exact)
        * (NUM_BUCKETS - max_exact)).astype(np.int64)
    bucket = np.where(dist < max_exact, dist, np.minimum(large, NUM_BUCKETS - 1))
    steps = [(0, int(bucket[0]))]
    for delta in range(1, BAND + 1):
        if bucket[delta] != bucket[delta - 1]:
            steps.append((delta, int(bucket[delta])))
    return steps


def _bias_kernel(table_ref, o_ref):
    iq = lax.broadcasted_iota(jnp.int32, (BAND, 2 * BAND), 0)
    ik = lax.broadcasted_iota(jnp.int32, (BAND, 2 * BAND), 1)
    delta = iq + BAND - ik
    in_band = jnp.logical_and(delta >= 0, delta <= BAND)
    for p, (_, dilation) in enumerate(DILATED_PATTERNS):
        steps = _bucket_steps(dilation)
        for h in range(H_A):
            val = jnp.full((BAND, 2 * BAND), table_ref[steps[0][1], h], F32)
            for start, bucket in steps[1:]:
                val = jnp.where(delta >= start, table_ref[bucket, h], val)
            o_ref[p, h] = jnp.where(in_band, val * LOG2E, NEG_INF)


def _band_bias(rel_table):
    n_pat = len(DILATED_PATTERNS)
    return pl.pallas_call(
        _bias_kernel,
        in_specs=[pl.BlockSpec(memory_space=pltpu.SMEM)],
        out_specs=pl.BlockSpec(memory_space=pltpu.VMEM),
        out_shape=jax.ShapeDtypeStruct((n_pat, H_A, BAND, 2 * BAND), F32),
        name="band_bias",
    )(rel_table)


DIL_UNITS = 8


def _dilated_kernel(q_ref, kp_ref, kc_ref, vp_ref, vc_ref, bias_ref, o_ref, st_ref):
    n_cls = q_ref.shape[1]
    n_blk = q_ref.shape[2] // BAND
    first = pl.program_id(2) == 0
    key_lane = lax.broadcasted_iota(jnp.int32, (1, 2 * BAND), 1)
    no_prev = jnp.where(jnp.logical_and(first, key_lane < BAND), NEG_INF, 0.0)
    stat_lane = lax.broadcasted_iota(jnp.int32, (BAND, LANES), 1)
    units = [(c, blk) for c in range(n_cls) for blk in range(n_blk)]

    def keys(prev_ref, cur_ref, c, blk, sl):
        if blk == 0:
            return jnp.concatenate([prev_ref[0, c, :, sl], cur_ref[0, c, :BAND, sl]], axis=0)
        return cur_ref[0, c, (blk - 1) * BAND:(blk + 1) * BAND, sl]

    scores = {}
    for c, blk in units:
        for h in range(H_A):
            sl = slice(h * HEAD_DIM, (h + 1) * HEAD_DIM)
            q = q_ref[0, c, blk * BAND:(blk + 1) * BAND, sl]
            scores[c, blk, h] = lax.dot_general(q, keys(kp_ref, kc_ref, c, blk, sl),
                                                (((1,), (1,)), ((), ())),
                                                preferred_element_type=F32)
    for c, blk in units:
        rows = slice(blk * BAND, (blk + 1) * BAND)
        stats = jnp.zeros((BAND, LANES), F32)
        for h in range(H_A):
            sl = slice(h * HEAD_DIM, (h + 1) * HEAD_DIM)
            s = scores[c, blk, h] + bias_ref[0, h]
            if blk == 0:
                s = s + no_prev
            mx = jnp.max(s, axis=-1, keepdims=True)
            p = jnp.exp2(s - mx)
            den = jnp.sum(p, axis=-1, keepdims=True)
            o_ref[0, c, rows, sl] = jnp.dot(p.astype(BF16), keys(vp_ref, vc_ref, c, blk, sl),
                                            preferred_element_type=F32).astype(o_ref.dtype)
            stats = jnp.where(stat_lane == h, mx, stats)
            stats = jnp.where(stat_lane == H_A + h, den, stats)
        st_ref[0, c, rows, :] = stats


def _dilated(qkv, bias, pattern):
    b, dilation, n_sub, _ = qkv.shape
    n_blk = min(DIL_UNITS, n_sub // BAND)
    n_cls = DIL_UNITS // n_blk
    rows = n_blk * BAND
    nb = n_sub // rows

    def cur(which):
        return pl.BlockSpec((1, n_cls, rows, D_A), lambda bi, r, i: (bi, r, i, which))

    def prev(which):
        return pl.BlockSpec((1, n_cls, BAND, D_A),
                            lambda bi, r, i: (bi, r, jnp.maximum(i * n_blk - 1, 0), which))

    return pl.pallas_call(
        _dilated_kernel,
        grid=(b, dilation // n_cls, nb),
        in_specs=[
            cur(0), prev(1), cur(1), prev(2), cur(2),
            pl.BlockSpec((1, H_A, BAND, 2 * BAND), lambda bi, r, i: (pattern, 0, 0, 0)),
        ],
        out_specs=[
            pl.BlockSpec((1, n_cls, rows, D_A), lambda bi, r, i: (bi, r, i, 0)),
            pl.BlockSpec((1, n_cls, rows, LANES), lambda bi, r, i: (bi, r, i, 0)),
        ],
        out_shape=[
            jax.ShapeDtypeStruct((b, dilation, n_sub, D_A), BF16),
            jax.ShapeDtypeStruct((b, dilation, n_sub, LANES), F32),
        ],
        compiler_params=_params("parallel", "parallel", "arbitrary"),
        name=f"dilated_{dilation}",
    )(qkv, qkv, qkv, qkv, qkv, bias)


COMBINE_ROWS = 512


def _combine_kernel(*refs):
    n_pat = len(DILATED_PATTERNS)
    num_refs, st_refs = refs[:n_pat], refs[n_pat:2 * n_pat]
    o_ref, num_buf, st_buf, num_tmp, st_tmp = refs[2 * n_pat:]
    dilations = [ref.shape[1] for ref in num_refs]
    fine = min(d for d in dilations if d > 1)
    heads = [slice(h * HEAD_DIM, (h + 1) * HEAD_DIM) for h in range(H_A)]
    for p, dilation in enumerate(dilations):
        if dilation == 1:
            continue
        step = dilation // fine
        for rf in range(fine):
            rows = pl.ds(rf, COMBINE_ROWS // fine, stride=fine)
            if step == 1:
                st_buf[p, rows, :] = st_refs[p][0, rf]
                for h, sl in enumerate(heads):
                    num_buf[p, h, rows, :] = num_refs[p][0, rf, :, sl].astype(F32)
                continue
            for q in range(step):
                part = pl.ds(q, COMBINE_ROWS // dilation, stride=step)
                st_tmp[part, :] = st_refs[p][0, rf + fine * q]
                for h, sl in enumerate(heads):
                    num_tmp[h, part, :] = num_refs[p][0, rf + fine * q, :, sl].astype(F32)
            st_buf[p, rows, :] = st_tmp[...]
            for h in range(H_A):
                num_buf[p, h, rows, :] = num_tmp[h]
    stats = [st_refs[p][0, 0] if d == 1 else st_buf[p] for p, d in enumerate(dilations)]
    lane = lax.broadcasted_iota(jnp.int32, stats[0].shape, 1)
    mx = functools.reduce(jnp.maximum, stats)
    scales = [jnp.exp2(st - mx) for st in stats]
    den = sum(sc * pltpu.roll(st, LANES - H_A, axis=1) for sc, st in zip(scales, stats))
    den = jnp.where(lane < H_A, den, 1.0)
    weights = [sc / den for sc in scales]
    for h in range(H_A):
        sl = slice(h * HEAD_DIM, (h + 1) * HEAD_DIM)
        out = None
        for p, w in enumerate(weights):
            if dilations[p] == 1:
                pnum = num_refs[p][0, 0, :, sl].astype(F32)
            else:
                pnum = num_buf[p, h]
            out = pnum * w[:, h:h + 1] if out is None else out + pnum * w[:, h:h + 1]
        o_ref[:, sl] = out.astype(BF16)


def _combine(nums, stats):
    b, _, s, _ = nums[0].shape
    tps = s // COMBINE_ROWS

    def spec(arr):
        dilation, width = arr.shape[1], arr.shape[3]
        return pl.BlockSpec((1, dilation, COMBINE_ROWS // dilation, width),
                            lambda i: (i // tps, 0, i % tps, 0))

    n_pat = len(nums)
    fine = min(a.shape[1] for a in nums if a.shape[1] > 1)
    return pl.pallas_call(
        _combine_kernel,
        grid=(b * tps,),
        in_specs=[spec(a) for a in nums] + [spec(a) for a in stats],
        out_specs=pl.BlockSpec((COMBINE_ROWS, D_A), lambda i: (i, 0)),
        out_shape=jax.ShapeDtypeStruct((b * s, D_A), BF16),
        scratch_shapes=[
            pltpu.VMEM((n_pat, H_A, COMBINE_ROWS, HEAD_DIM), F32),
            pltpu.VMEM((n_pat, COMBINE_ROWS, LANES), F32),
            pltpu.VMEM((H_A, COMBINE_ROWS // fine, HEAD_DIM), F32),
            pltpu.VMEM((COMBINE_ROWS // fine, LANES), F32),
        ],
        compiler_params=_params("parallel"),
        name="combine_a",
    )(*nums, *stats)


def _fox_kernel(q_ref, k_ref, vt_ref, o_ref, m_ref, l_ref, acc_ref, sa_ref, sb_ref):
    iq = pl.program_id(2)
    m_ref[...] = jnp.full(m_ref.shape, NEG_INF, F32)
    l_ref[...] = jnp.zeros(l_ref.shape, F32)
    acc_ref[...] = jnp.zeros(acc_ref.shape, F32)

    def scores_into(s_ref, kb):
        start = pl.multiple_of(kb * FOX_TK, FOX_TK)
        for hh in range(FOX_HEADS):
            k = k_ref[0, hh, pl.ds(start, FOX_TK), :]
            s_ref[hh] = lax.dot_general(k, q_ref[0, hh], (((1,), (1,)), ((), ())),
                                        preferred_element_type=F32)

    def accumulate(s_ref, kb, diagonal):
        start = pl.multiple_of(kb * FOX_TK, FOX_TK)
        for hh in range(FOX_HEADS):
            st = s_ref[hh]
            if diagonal:
                key = lax.broadcasted_iota(jnp.int32, st.shape, 0)
                qry = lax.broadcasted_iota(jnp.int32, st.shape, 1)
                st = jnp.where(key <= qry, st, NEG_INF)
            m_prev = m_ref[hh]
            m_new = jnp.maximum(m_prev, jnp.max(st, axis=0, keepdims=True))
            alpha = jnp.exp2(m_prev - m_new)
            p = jnp.exp2(st - m_new)
            l_ref[hh] = alpha * l_ref[hh] + jnp.sum(p, axis=0, keepdims=True)
            vt = vt_ref[0, hh, :, pl.ds(start, FOX_TK)]
            acc_ref[hh] = alpha * acc_ref[hh] + jnp.dot(vt, p.astype(BF16),
                                                        preferred_element_type=F32)
            m_ref[hh] = m_new

    scores_into(sa_ref, 0)

    def pair(p, carry):
        scores_into(sb_ref, 2 * p + 1)
        accumulate(sa_ref, 2 * p, False)
        scores_into(sa_ref, 2 * p + 2)
        accumulate(sb_ref, 2 * p + 1, False)
        return carry

    lax.fori_loop(0, iq // 2, pair, 0)

    @pl.when(iq % 2 == 0)
    def _():
        accumulate(sa_ref, iq, True)

    @pl.when(iq % 2 == 1)
    def _():
        scores_into(sb_ref, iq)
        accumulate(sa_ref, iq - 1, False)
        accumulate(sb_ref, iq, True)
    for hh in range(FOX_HEADS):
        o_ref[0, :, hh * HEAD_DIM:(hh + 1) * HEAD_DIM] = (
            acc_ref[hh] / l_ref[hh]).T.astype(BF16)


def _fox(qk_aug, v_t):
    _, b, h_b, s, _ = qk_aug.shape
    assert FOX_TQ == FOX_TK
    return pl.pallas_call(
        _fox_kernel,
        grid=(b, h_b // FOX_HEADS, s // FOX_TQ),
        in_specs=[
            pl.BlockSpec((None, 1, FOX_HEADS, FOX_TQ, AUG), lambda bi, h, i: (0, bi, h, i, 0)),
            pl.BlockSpec((None, 1, FOX_HEADS, s, AUG), lambda bi, h, i: (1, bi, h, 0, 0)),
            pl.BlockSpec((1, FOX_HEADS, HEAD_DIM, s), lambda bi, h, i: (bi, h, 0, 0)),
        ],
        out_specs=pl.BlockSpec((1, FOX_TQ, FOX_HEADS * HEAD_DIM), lambda bi, h, i: (bi, i, h)),
        out_shape=jax.ShapeDtypeStruct((b, s, h_b * HEAD_DIM), BF16),
        scratch_shapes=[
            pltpu.VMEM((FOX_HEADS, 1, FOX_TQ), F32),
            pltpu.VMEM((FOX_HEADS, 1, FOX_TQ), F32),
            pltpu.VMEM((FOX_HEADS, HEAD_DIM, FOX_TQ), F32),
            pltpu.VMEM((FOX_HEADS, FOX_TK, FOX_TQ), F32),
            pltpu.VMEM((FOX_HEADS, FOX_TK, FOX_TQ), F32),
        ],
        compiler_params=_params("parallel", "parallel", "arbitrary"),
        name="fox",
    )(qk_aug, qk_aug, v_t)


def _out_proj_kernel(a_ref, b_ref, w_ref, x_ref, o_ref):
    mixed = jnp.concatenate([a_ref[...], b_ref[...]], axis=-1)
    o_ref[...] = x_ref[...] + jnp.dot(mixed, w_ref[...].astype(BF16),
                                      preferred_element_type=F32)


OUT_ROW_TILE = 2048


def _out_proj(out_a, out_b, w_out, x):
    m = x.shape[0]
    return pl.pallas_call(
        _out_proj_kernel,
        grid=(m // OUT_ROW_TILE, D_MODEL // COL_TILE),
        in_specs=[
            pl.BlockSpec((OUT_ROW_TILE, D_A), lambda i, j: (i, 0)),
            pl.BlockSpec((OUT_ROW_TILE, D_B), lambda i, j: (i, 0)),
            pl.BlockSpec((D_A + D_B, COL_TILE), lambda i, j: (0, j)),
            pl.BlockSpec((OUT_ROW_TILE, COL_TILE), lambda i, j: (i, j)),
        ],
        out_specs=pl.BlockSpec((OUT_ROW_TILE, COL_TILE), lambda i, j: (i, j)),
        out_shape=jax.ShapeDtypeStruct((m, D_MODEL), F32),
        compiler_params=_params("parallel", "parallel"),
        name="out_proj",
    )(out_a, out_b, w_out, x)


def _mixer(x, b, s, mix_norm, w_in, layer, q_norm_a, k_norm_a, q_norm_b, k_norm_b, forget_bias,
           rel_bias_table, w_out):
    ones = jnp.ones((D_A,), F32)
    head_gain = jnp.concatenate([
        jnp.tile(q_norm_a * (ATTN_SCALE * LOG2E), H_A), jnp.tile(k_norm_a, H_A), ones,
        jnp.tile(q_norm_b * (ATTN_SCALE * LOG2E), H_B), jnp.tile(k_norm_b, H_B), ones,
    ]).reshape(1, D_QKV)
    w_f = jnp.pad(w_in[layer, :, D_QKV:], ((0, 0), (0, LANES - H_B))).astype(BF16)
    f_bias = jnp.pad(forget_bias, (0, LANES - H_B)).reshape(1, LANES)
    *qkv_a, qk_aug, v_t = _proj(x, b, s, mix_norm, w_in, layer, w_f, f_bias, head_gain)

    out_b = _fox(qk_aug, v_t).reshape(b * s, D_B)

    bias = _band_bias(rel_bias_table)
    parts = [_dilated(qkv, bias, p) for p, qkv in enumerate(qkv_a)]
    out_a = _combine([pt[0] for pt in parts], [pt[1] for pt in parts])

    return _out_proj(out_a, out_b, w_out, x)


def kernel(x, ffn1_norm, ffn1_w_in, ffn1_w_out, mix_norm, w_in, q_norm_a, k_norm_a, q_norm_b,
           k_norm_b, forget_bias, rel_bias_table, w_out, ffn2_norm, ffn2_w_in, ffn2_w_out):
    b, s, d = x.shape
    depth = ffn1_norm.shape[0]
    x = x.reshape(b * s, d)
    for l in range(depth):
        x = _ffn(x, ffn1_norm[l], ffn1_w_in[l], ffn1_w_out[l])
        x = _mixer(x, b, s, mix_norm[l], w_in, l, q_norm_a[l], k_norm_a[l], q_norm_b[l],
                   k_norm_b[l], forget_bias[l], rel_bias_table, w_out[l])
        x = _ffn(x, ffn2_norm[l], ffn2_w_in[l], ffn2_w_out[l])
    return x.reshape(b, s, d)
```

```python
import functools
import math

import numpy as np
import jax
import jax.numpy as jnp
from jax import lax
from jax.experimental import pallas as pl
from jax.experimental.pallas import tpu as pltpu

D_MODEL = 2048
HEAD_DIM = 128
N_HEADS = D_MODEL // HEAD_DIM
H_A = N_HEADS // 2
H_B = N_HEADS - H_A
D_A = H_A * HEAD_DIM
D_B = H_B * HEAD_DIM
D_QKV = 3 * D_A + 3 * D_B
DILATED_PATTERNS = ((128, 1), (512, 4), (2048, 16))
BAND = 128
NUM_BUCKETS = 32
MAX_DISTANCE = 2048
D_FF = ((8 * D_MODEL // 3 + 127) // 128) * 128
RMS_EPS = 1e-6
NEG_INF = -1e30
ATTN_SCALE = HEAD_DIM ** -0.5

LANES = 128
FF_TILE = 512
FFN_ROW_TILE = 2048
FFN_CHUNK = 256
FFN_VMEM_LIMIT = 60 * 1024 * 1024
ROW_TILE = 1024
COL_TILE = 512
FOX_TQ = 512
FOX_TK = 512
FOX_HEADS = 4
VMEM_LIMIT = 56 * 1024 * 1024

F32 = jnp.float32
BF16 = jnp.bfloat16


def _params(*sem):
    return pltpu.CompilerParams(dimension_semantics=sem, vmem_limit_bytes=VMEM_LIMIT)


def _rms_rows(x, gain):
    ms = jnp.mean(x * x, axis=-1, keepdims=True)
    return x * lax.rsqrt(ms + RMS_EPS) * gain


def _ffn_kernel(x_hbm, g_ref, wg_ref, wu_ref, wo_ref, o_hbm, acc_ref, h_ref, in_sem, out_sem):
    i = pl.program_id(0)
    j = pl.program_id(1)
    last_i = pl.num_programs(0) - 1
    last_j = pl.num_programs(1) - 1
    pieces = [(r, c) for r in range(FFN_ROW_TILE // ROW_TILE) for c in range(D_MODEL // COL_TILE)]

    def x_copy(c):
        src = x_hbm.at[pl.ds(i * FFN_ROW_TILE + c * FFN_CHUNK, FFN_CHUNK), :]
        return pltpu.make_async_copy(src, acc_ref.at[pl.ds(c * FFN_CHUNK, FFN_CHUNK), :],
                                     in_sem.at[c])

    def out_copy(row_tile, r, c):
        rows, cols = pl.ds(r * ROW_TILE, ROW_TILE), pl.ds(c * COL_TILE, COL_TILE)
        dst = o_hbm.at[pl.ds(row_tile * FFN_ROW_TILE + r * ROW_TILE, ROW_TILE), cols]
        return pltpu.make_async_copy(acc_ref.at[rows, cols], dst,
                                     out_sem.at[r * (D_MODEL // COL_TILE) + c])

    @pl.when(j == 0)
    def _():
        @pl.when(i > 0)
        def _():
            for r, c in pieces:
                out_copy(i - 1, r, c).wait()

        n_chunks = FFN_ROW_TILE // FFN_CHUNK
        for c in range(n_chunks):
            x_copy(c).start()
        for c in range(n_chunks):
            x_copy(c).wait()
            rows = slice(c * FFN_CHUNK, (c + 1) * FFN_CHUNK)
            h_ref[rows, :] = _rms_rows(acc_ref[rows, :], g_ref[...]).astype(BF16)

    col = lax.broadcasted_iota(jnp.int32, (1, FF_TILE), 1)
    repeated = j * FF_TILE - jnp.minimum(j * FF_TILE, D_FF - FF_TILE)

    def tile(write_back):
        for r in range(FFN_ROW_TILE // ROW_TILE):
            rows = slice(r * ROW_TILE, (r + 1) * ROW_TILE)
            h = h_ref[rows, :]
            gate = jnp.dot(h, wg_ref[...].astype(BF16), preferred_element_type=F32)
            up = jnp.dot(h, wu_ref[...].astype(BF16), preferred_element_type=F32)
            act = gate * (1.0 / (1.0 + jnp.exp(-gate))) * (0.5 * up)
            act = jnp.where(col >= repeated, act, 0.0).astype(BF16)
            for c in range(D_MODEL // COL_TILE):
                sl = slice(c * COL_TILE, (c + 1) * COL_TILE)
                acc_ref[rows, sl] += jnp.dot(act, wo_ref[:, sl].astype(BF16),
                                             preferred_element_type=F32)
                if write_back:
                    out_copy(i, r, c).start()

    @pl.when(j < last_j)
    def _():
        tile(False)

    @pl.when(j == last_j)
    def _():
        tile(True)

        @pl.when(i == last_i)
        def _():
            for r, c in pieces:
                out_copy(i, r, c).wait()


def _ffn(x, gain, w_in, w_out):
    m = x.shape[0]
    n_ff = pl.cdiv(D_FF, FF_TILE)

    def ff_start(j):
        return pl.multiple_of(jnp.minimum(j * FF_TILE, D_FF - FF_TILE), LANES)

    return pl.pallas_call(
        _ffn_kernel,
        grid=(m // FFN_ROW_TILE, n_ff),
        in_specs=[
            pl.BlockSpec(memory_space=pl.ANY),
            pl.BlockSpec((1, D_MODEL), lambda i, j: (0, 0)),
            pl.BlockSpec((pl.Element(D_MODEL), pl.Element(FF_TILE)),
                         lambda i, j: (0, ff_start(j))),
            pl.BlockSpec((pl.Element(D_MODEL), pl.Element(FF_TILE)),
                         lambda i, j: (0, pl.multiple_of(D_FF + ff_start(j), LANES))),
            pl.BlockSpec((pl.Element(FF_TILE), pl.Element(D_MODEL)),
                         lambda i, j: (ff_start(j), 0)),
        ],
        out_specs=pl.BlockSpec(memory_space=pl.ANY),
        out_shape=jax.ShapeDtypeStruct((m, D_MODEL), F32),
        scratch_shapes=[
            pltpu.VMEM((FFN_ROW_TILE, D_MODEL), F32),
            pltpu.VMEM((FFN_ROW_TILE, D_MODEL), BF16),
            pltpu.SemaphoreType.DMA((FFN_ROW_TILE // FFN_CHUNK,)),
            pltpu.SemaphoreType.DMA(((FFN_ROW_TILE // ROW_TILE) * (D_MODEL // COL_TILE),)),
        ],
        compiler_params=pltpu.CompilerParams(
            dimension_semantics=("arbitrary", "arbitrary"), vmem_limit_bytes=FFN_VMEM_LIMIT),
        name="ffn",
    )(x, gain.reshape(1, D_MODEL), w_in, w_in, w_out)


HEADS_PER_TILE = COL_TILE // HEAD_DIM
N_PROJ_TILES = D_QKV // COL_TILE
TILES_PER_GROUP = D_A // COL_TILE
A_TILES = 3 * TILES_PER_GROUP
AUG = 2 * HEAD_DIM
LOG2E = math.log2(math.e)


def _log_gate_scan(z, carry):
    c = (jnp.minimum(z, 0.0) - jnp.log1p(jnp.exp(-jnp.abs(z)))) * LOG2E
    row = lax.broadcasted_iota(jnp.int32, c.shape, 0)
    shift = 1
    while shift < c.shape[0]:
        c = c + jnp.where(row >= shift, pltpu.roll(c, shift, axis=0), 0.0)
        shift *= 2
    return c + carry


def _split3(c):
    hi = c.astype(BF16)
    rest = c - hi.astype(F32)
    mid = rest.astype(BF16)
    lo = (rest - mid.astype(F32)).astype(BF16)
    return hi, mid, lo


MXU_WIDTH = 256


def _head_mean_matrix():
    head = np.arange(MXU_WIDTH) // HEAD_DIM
    return jnp.asarray((head[:, None] == head[None, :]) / HEAD_DIM, BF16)


GATE_ONES_LANE = 3 * H_B


def _gate_placement_matrices():
    mats = np.zeros((2, TILES_PER_GROUP, LANES, COL_TILE), np.float32)
    for part in range(TILES_PER_GROUP):
        for hh in range(HEADS_PER_TILE):
            head = part * HEADS_PER_TILE + hh
            col = hh * HEAD_DIM
            for term in range(3):
                mats[0, part, term * H_B + head, col + term] = 1.0
                mats[0, part, GATE_ONES_LANE, col + 3 + term] = 1.0
                mats[1, part, term * H_B + head, col + 3 + term] = -1.0
                mats[1, part, GATE_ONES_LANE, col + term] = 1.0
    return jnp.asarray(mats.reshape(2 * TILES_PER_GROUP, LANES, COL_TILE), BF16)


def _proj_kernel(x_ref, g_ref, w_ref, wf_ref, fb_ref, hg_ref, mean_ref, place_ref,
                 a1_ref, a4_ref, a16_ref, qk_ref, vt_ref,
                 h_ref, cs_ref, carry_ref, y_ref, yf_ref, *, tiles_per_seq):
    i = pl.program_id(0)
    j = pl.program_id(1)

    @pl.when(j == 0)
    def _():
        h_ref[...] = _rms_rows(x_ref[...], g_ref[...]).astype(BF16)

        @pl.when(i % tiles_per_seq == 0)
        def _():
            carry_ref[...] = jnp.zeros(carry_ref.shape, F32)

        z = jnp.dot(h_ref[...], wf_ref[...], preferred_element_type=F32) + fb_ref[...]
        c = _log_gate_scan(z, carry_ref[...])
        carry_ref[...] = c[ROW_TILE - 1:ROW_TILE, :]
        lane = lax.broadcasted_iota(jnp.int32, c.shape, 1)
        packed = jnp.where(lane == GATE_ONES_LANE, 1.0, 0.0)
        for term, part in enumerate(_split3(c)):
            part = jnp.where(lane < H_B, part.astype(F32), 0.0)
            packed = packed + (pltpu.roll(part, term * H_B, axis=1) if term else part)
        cs_ref[...] = packed.astype(BF16)

    def product(transposed=False):
        lhs, rhs = (w_ref[...].astype(BF16), h_ref[...])
        if not transposed:
            lhs, rhs = rhs, lhs
        return lax.dot_general(lhs, rhs, (((1,), (1,)), ((), ())), preferred_element_type=F32)

    def normed():
        r = product()
        sq = (r * r).astype(BF16)
        ms = jnp.concatenate(
            [jnp.dot(sq[:, k:k + MXU_WIDTH], mean_ref[...], preferred_element_type=F32)
             for k in range(0, COL_TILE, MXU_WIDTH)], axis=1)
        return r * lax.rsqrt(ms + RMS_EPS) * hg_ref[...]

    def store_dilated(y):
        a1_ref[0, 0] = y.astype(BF16)
        fine, coarse = a4_ref.shape[1], a16_ref.shape[1]
        step = coarse // fine
        for c in range(HEADS_PER_TILE):
            sl = slice(c * LANES, (c + 1) * LANES)
            y_ref[c] = y[:, sl]
            for rf in range(fine):
                rows = y_ref[c, pl.ds(rf, ROW_TILE // fine, stride=fine), :]
                a4_ref[0, rf, :, sl] = rows.astype(BF16)
                yf_ref[c, rf] = rows
                for q in range(step):
                    rows = yf_ref[c, rf, pl.ds(q, ROW_TILE // coarse, stride=step), :]
                    a16_ref[0, rf + fine * q, :, sl] = rows.astype(BF16)

    def tiles(first_group, n_groups=1):
        lo = first_group * TILES_PER_GROUP
        return jnp.logical_and(j >= lo, j < lo + n_groups * TILES_PER_GROUP)

    def store_forgetting():
        y = normed().astype(BF16)
        aug = jnp.dot(cs_ref[...], place_ref[0], preferred_element_type=F32).astype(BF16)
        for hh in range(HEADS_PER_TILE):
            sl = slice(hh * HEAD_DIM, (hh + 1) * HEAD_DIM)
            qk_ref[0, hh, :, :HEAD_DIM] = y[:, sl]
            qk_ref[0, hh, :, HEAD_DIM:] = aug[:, sl]

    @pl.when(tiles(0, 2))
    def _():
        store_dilated(normed())

    @pl.when(tiles(2))
    def _():
        store_dilated(product())

    @pl.when(tiles(3, 2))
    def _():
        store_forgetting()

    @pl.when(tiles(5))
    def _():
        v_t = product(transposed=True).astype(BF16)
        for hh in range(HEADS_PER_TILE):
            vt_ref[0, hh] = v_t[hh * HEAD_DIM:(hh + 1) * HEAD_DIM, :]


def _proj(x, b, s, gain, w_in, layer, w_f, f_bias, head_gain):
    m = x.shape[0]
    tps = s // ROW_TILE

    def a_spec(dilation):
        return pl.BlockSpec((1, dilation, ROW_TILE // dilation, COL_TILE),
                            lambda i, j: (i // tps, 0, i % tps, jnp.minimum(j, A_TILES - 1)))

    def qk_block(i, j):
        t = jnp.clip(j - qb_first, 0, 2 * TILES_PER_GROUP - 1)
        return (t // TILES_PER_GROUP, i // tps, t % TILES_PER_GROUP, i % tps, 0)

    qb_first = A_TILES
    vt_first = A_TILES + 2 * TILES_PER_GROUP
    dilations = [d for _, d in DILATED_PATTERNS]
    return pl.pallas_call(
        functools.partial(_proj_kernel, tiles_per_seq=tps),
        grid=(m // ROW_TILE, N_PROJ_TILES),
        in_specs=[
            pl.BlockSpec((ROW_TILE, D_MODEL), lambda i, j: (i, 0)),
            pl.BlockSpec((1, D_MODEL), lambda i, j: (0, 0)),
            pl.BlockSpec((None, COL_TILE, D_MODEL), lambda i, j: (layer, j, 0)),
            pl.BlockSpec((D_MODEL, LANES), lambda i, j: (0, 0)),
            pl.BlockSpec((1, LANES), lambda i, j: (0, 0)),
            pl.BlockSpec((1, COL_TILE), lambda i, j: (0, j)),
            pl.BlockSpec((MXU_WIDTH, MXU_WIDTH), lambda i, j: (0, 0)),
            pl.BlockSpec((1, LANES, COL_TILE),
                         lambda i, j: (jnp.clip(j - qb_first, 0, 2 * TILES_PER_GROUP - 1), 0, 0)),
        ],
        out_specs=[a_spec(d) for d in dilations] + [
            pl.BlockSpec((None, 1, HEADS_PER_TILE, ROW_TILE, AUG), qk_block),
            pl.BlockSpec((1, HEADS_PER_TILE, HEAD_DIM, ROW_TILE),
                         lambda i, j: (i // tps, jnp.clip(j - vt_first, 0, TILES_PER_GROUP - 1),
                                       0, i % tps)),
        ],
        out_shape=[jax.ShapeDtypeStruct((b, d, s // d, 3 * D_A), BF16) for d in dilations] + [
            jax.ShapeDtypeStruct((2, b, H_B, s, AUG), BF16),
            jax.ShapeDtypeStruct((b, H_B, HEAD_DIM, s), BF16),
        ],
        scratch_shapes=[
            pltpu.VMEM((ROW_TILE, D_MODEL), BF16),
            pltpu.VMEM((ROW_TILE, LANES), BF16),
            pltpu.VMEM((1, LANES), F32),
            pltpu.VMEM((HEADS_PER_TILE, ROW_TILE, LANES), F32),
            pltpu.VMEM((HEADS_PER_TILE, dilations[1], ROW_TILE // dilations[1], LANES), F32),
        ],
        compiler_params=_params("arbitrary", "arbitrary"),
        name="proj",
    )(x, gain.reshape(1, D_MODEL), jnp.swapaxes(w_in, 1, 2), w_f, f_bias, head_gain,
      _head_mean_matrix(), _gate_placement_matrices())


def _bucket_steps(dilation):
    dist = np.arange(BAND + 1) * dilation
    max_exact = NUM_BUCKETS // 2
    large = max_exact + np.floor(
        np.log(np.maximum(dist, 1) / max_exact) / math.log(MAX_DISTANCE / max_exact)
        * (NUM_BUCKETS - max_exact)).astype(np.int64)
    bucket = np.where(dist < max_exact, dist, np.minimum(large, NUM_BUCKETS - 1))
    steps = [(0, int(bucket[0]))]
    for delta in range(1, BAND + 1):
        if bucket[delta] != bucket[delta - 1]:
            steps.append((delta, int(bucket[delta])))
    return steps


def _bias_kernel(table_ref, o_ref):
    iq = lax.broadcasted_iota(jnp.int32, (BAND, 2 * BAND), 0)
    ik = lax.broadcasted_iota(jnp.int32, (BAND, 2 * BAND), 1)
    delta = iq + BAND - ik
    in_band = jnp.logical_and(delta >= 0, delta <= BAND)
    for p, (_, dilation) in enumerate(DILATED_PATTERNS):
        steps = _bucket_steps(dilation)
        for h in range(H_A):
            val = jnp.full((BAND, 2 * BAND), table_ref[steps[0][1], h], F32)
            for start, bucket in steps[1:]:
                val = jnp.where(delta >= start, table_ref[bucket, h], val)
            o_ref[p, h] = jnp.where(in_band, val * LOG2E, NEG_INF)


def _band_bias(rel_table):
    n_pat = len(DILATED_PATTERNS)
    return pl.pallas_call(
        _bias_kernel,
        in_specs=[pl.BlockSpec(memory_space=pltpu.SMEM)],
        out_specs=pl.BlockSpec(memory_space=pltpu.VMEM),
        out_shape=jax.ShapeDtypeStruct((n_pat, H_A, BAND, 2 * BAND), F32),
        name="band_bias",
    )(rel_table)


DIL_UNITS = 8


def _dilated_kernel(q_ref, kp_ref, kc_ref, vp_ref, vc_ref, bias_ref, o_ref, st_ref):
    n_cls = q_ref.shape[1]
    n_blk = q_ref.shape[2] // BAND
    first = pl.program_id(2) == 0
    key_lane = lax.broadcasted_iota(jnp.int32, (1, 2 * BAND), 1)
    no_prev = jnp.where(jnp.logical_and(first, key_lane < BAND), NEG_INF, 0.0)
    stat_lane = lax.broadcasted_iota(jnp.int32, (BAND, LANES), 1)
    units = [(c, blk) for c in range(n_cls) for blk in range(n_blk)]

    def keys(prev_ref, cur_ref, c, blk, sl):
        if blk == 0:
            return jnp.concatenate([prev_ref[0, c, :, sl], cur_ref[0, c, :BAND, sl]], axis=0)
        return cur_ref[0, c, (blk - 1) * BAND:(blk + 1) * BAND, sl]

    scores = {}
    for c, blk in units:
        for h in range(H_A):
            sl = slice(h * HEAD_DIM, (h + 1) * HEAD_DIM)
            q = q_ref[0, c, blk * BAND:(blk + 1) * BAND, sl]
            scores[c, blk, h] = lax.dot_general(q, keys(kp_ref, kc_ref, c, blk, sl),
                                                (((1,), (1,)), ((), ())),
                                                preferred_element_type=F32)
    for c, blk in units:
        rows = slice(blk * BAND, (blk + 1) * BAND)
        stats = jnp.zeros((BAND, LANES), F32)
        for h in range(H_A):
            sl = slice(h * HEAD_DIM, (h + 1) * HEAD_DIM)
            s = scores[c, blk, h] + bias_ref[0, h]
            if blk == 0:
                s = s + no_prev
            mx = jnp.max(s, axis=-1, keepdims=True)
            p = jnp.exp2(s - mx)
            den = jnp.sum(p, axis=-1, keepdims=True)
            o_ref[0, c, rows, sl] = jnp.dot(p.astype(BF16), keys(vp_ref, vc_ref, c, blk, sl),
                                            preferred_element_type=F32).astype(o_ref.dtype)
            stats = jnp.where(stat_lane == h, mx, stats)
            stats = jnp.where(stat_lane == H_A + h, den, stats)
        st_ref[0, c, rows, :] = stats


def _dilated(qkv, bias, pattern):
    b, dilation, n_sub, _ = qkv.shape
    n_blk = min(DIL_UNITS, n_sub // BAND)
    n_cls = DIL_UNITS // n_blk
    rows = n_blk * BAND
    nb = n_sub // rows

    def cur(which):
        return pl.BlockSpec((1, n_cls, rows, D_A), lambda bi, r, i: (bi, r, i, which))

    def prev(which):
        return pl.BlockSpec((1, n_cls, BAND, D_A),
                            lambda bi, r, i: (bi, r, jnp.maximum(i * n_blk - 1, 0), which))

    return pl.pallas_call(
        _dilated_kernel,
        grid=(b, dilation // n_cls, nb),
        in_specs=[
            cur(0), prev(1), cur(1), prev(2), cur(2),
            pl.BlockSpec((1, H_A, BAND, 2 * BAND), lambda bi, r, i: (pattern, 0, 0, 0)),
        ],
        out_specs=[
            pl.BlockSpec((1, n_cls, rows, D_A), lambda bi, r, i: (bi, r, i, 0)),
            pl.BlockSpec((1, n_cls, rows, LANES), lambda bi, r, i: (bi, r, i, 0)),
        ],
        out_shape=[
            jax.ShapeDtypeStruct((b, dilation, n_sub, D_A), BF16),
            jax.ShapeDtypeStruct((b, dilation, n_sub, LANES), F32),
        ],
        compiler_params=_params("parallel", "parallel", "arbitrary"),
        name=f"dilated_{dilation}",
    )(qkv, qkv, qkv, qkv, qkv, bias)


COMBINE_ROWS = 512


def _combine_kernel(*refs):
    n_pat = len(DILATED_PATTERNS)
    num_refs, st_refs = refs[:n_pat], refs[n_pat:2 * n_pat]
    o_ref, num_buf, st_buf, num_tmp, st_tmp = refs[2 * n_pat:]
    dilations = [ref.shape[1] for ref in num_refs]
    fine = min(d for d in dilations if d > 1)
    heads = [slice(h * HEAD_DIM, (h + 1) * HEAD_DIM) for h in range(H_A)]
    for p, dilation in enumerate(dilations):
        if dilation == 1:
            continue
        step = dilation // fine
        for rf in range(fine):
            rows = pl.ds(rf, COMBINE_ROWS // fine, stride=fine)
            if step == 1:
                st_buf[p, rows, :] = st_refs[p][0, rf]
                for h, sl in enumerate(heads):
                    num_buf[p, h, rows, :] = num_refs[p][0, rf, :, sl].astype(F32)
                continue
            for q in range(step):
                part = pl.ds(q, COMBINE_ROWS // dilation, stride=step)
                st_tmp[part, :] = st_refs[p][0, rf + fine * q]
                for h, sl in enumerate(heads):
                    num_tmp[h, part, :] = num_refs[p][0, rf + fine * q, :, sl].astype(F32)
            st_buf[p, rows, :] = st_tmp[...]
            for h in range(H_A):
                num_buf[p, h, rows, :] = num_tmp[h]
    stats = [st_refs[p][0, 0] if d == 1 else st_buf[p] for p, d in enumerate(dilations)]
    lane = lax.broadcasted_iota(jnp.int32, stats[0].shape, 1)
    mx = functools.reduce(jnp.maximum, stats)
    scales = [jnp.exp2(st - mx) for st in stats]
    den = sum(sc * pltpu.roll(st, LANES - H_A, axis=1) for sc, st in zip(scales, stats))
    den = jnp.where(lane < H_A, den, 1.0)
    weights = [sc / den for sc in scales]
    for h in range(H_A):
        sl = slice(h * HEAD_DIM, (h + 1) * HEAD_DIM)
        out = None
        for p, w in enumerate(weights):
            if dilations[p] == 1:
                pnum = num_refs[p][0, 0, :, sl].astype(F32)
            else:
                pnum = num_buf[p, h]
            out = pnum * w[:, h:h + 1] if out is None else out + pnum * w[:, h:h + 1]
        o_ref[:, sl] = out.astype(BF16)


def _combine(nums, stats):
    b, _, s, _ = nums[0].shape
    tps = s // COMBINE_ROWS

    def spec(arr):
        dilation, width = arr.shape[1], arr.shape[3]
        return pl.BlockSpec((1, dilation, COMBINE_ROWS // dilation, width),
                            lambda i: (i // tps, 0, i % tps, 0))

    n_pat = len(nums)
    fine = min(a.shape[1] for a in nums if a.shape[1] > 1)
    return pl.pallas_call(
        _combine_kernel,
        grid=(b * tps,),
        in_specs=[spec(a) for a in nums] + [spec(a) for a in stats],
        out_specs=pl.BlockSpec((COMBINE_ROWS, D_A), lambda i: (i, 0)),
        out_shape=jax.ShapeDtypeStruct((b * s, D_A), BF16),
        scratch_shapes=[
            pltpu.VMEM((n_pat, H_A, COMBINE_ROWS, HEAD_DIM), F32),
            pltpu.VMEM((n_pat, COMBINE_ROWS, LANES), F32),
            pltpu.VMEM((H_A, COMBINE_ROWS // fine, HEAD_DIM), F32),
            pltpu.VMEM((COMBINE_ROWS // fine, LANES), F32),
        ],
        compiler_params=_params("parallel"),
        name="combine_a",
    )(*nums, *stats)


def _fox_kernel(q_ref, k_ref, vt_ref, o_ref, m_ref, l_ref, acc_ref, sa_ref, sb_ref):
    iq = pl.program_id(2)
    m_ref[...] = jnp.full(m_ref.shape, NEG_INF, F32)
    l_ref[...] = jnp.zeros(l_ref.shape, F32)
    acc_ref[...] = jnp.zeros(acc_ref.shape, F32)

    def scores_into(s_ref, kb):
        start = pl.multiple_of(kb * FOX_TK, FOX_TK)
        for hh in range(FOX_HEADS):
            k = k_ref[0, hh, pl.ds(start, FOX_TK), :]
            s_ref[hh] = lax.dot_general(k, q_ref[0, hh], (((1,), (1,)), ((), ())),
                                        preferred_element_type=F32)

    def accumulate(s_ref, kb, diagonal):
        start = pl.multiple_of(kb * FOX_TK, FOX_TK)
        n_parts = 2 if diagonal else 1
        width = FOX_TQ // n_parts
        for hh in range(FOX_HEADS):
            for part in range(n_parts):
                cols = slice(part * width, (part + 1) * width)
                n_keys = (part + 1) * FOX_TK // n_parts
                st = s_ref[hh, :n_keys, cols]
                if diagonal:
                    key = lax.broadcasted_iota(jnp.int32, st.shape, 0)
                    qry = lax.broadcasted_iota(jnp.int32, st.shape, 1) + part * width
                    st = jnp.where(key <= qry, st, NEG_INF)
                m_prev = m_ref[hh, :, cols]
                m_new = jnp.maximum(m_prev, jnp.max(st, axis=0, keepdims=True))
                alpha = jnp.exp2(m_prev - m_new)
                p = jnp.exp2(st - m_new)
                l_ref[hh, :, cols] = alpha * l_ref[hh, :, cols] + jnp.sum(p, axis=0,
                                                                         keepdims=True)
                vt = vt_ref[0, hh, :, pl.ds(start, n_keys)]
                acc_ref[hh, :, cols] = alpha * acc_ref[hh, :, cols] + jnp.dot(
                    vt, p.astype(BF16), preferred_element_type=F32)
                m_ref[hh, :, cols] = m_new

    scores_into(sa_ref, 0)

    def pair(p, carry):
        scores_into(sb_ref, 2 * p + 1)
        accumulate(sa_ref, 2 * p, False)
        scores_into(sa_ref, 2 * p + 2)
        accumulate(sb_ref, 2 * p + 1, False)
        return carry

    lax.fori_loop(0, iq // 2, pair, 0)

    @pl.when(iq % 2 == 0)
    def _():
        accumulate(sa_ref, iq, True)

    @pl.when(iq % 2 == 1)
    def _():
        scores_into(sb_ref, iq)
        accumulate(sa_ref, iq - 1, False)
        accumulate(sb_ref, iq, True)
    for hh in range(FOX_HEADS):
        o_ref[0, :, hh * HEAD_DIM:(hh + 1) * HEAD_DIM] = (
            acc_ref[hh] / l_ref[hh]).T.astype(BF16)


def _fox(qk_aug, v_t):
    _, b, h_b, s, _ = qk_aug.shape
    assert FOX_TQ == FOX_TK
    return pl.pallas_call(
        _fox_kernel,
        grid=(b, h_b // FOX_HEADS, s // FOX_TQ),
        in_specs=[
            pl.BlockSpec((None, 1, FOX_HEADS, FOX_TQ, AUG), lambda bi, h, i: (0, bi, h, i, 0)),
            pl.BlockSpec((None, 1, FOX_HEADS, s, AUG), lambda bi, h, i: (1, bi, h, 0, 0)),
            pl.BlockSpec((1, FOX_HEADS, HEAD_DIM, s), lambda bi, h, i: (bi, h, 0, 0)),
        ],
        out_specs=pl.BlockSpec((1, FOX_TQ, FOX_HEADS * HEAD_DIM), lambda bi, h, i: (bi, i, h)),
        out_shape=jax.ShapeDtypeStruct((b, s, h_b * HEAD_DIM), BF16),
        scratch_shapes=[
            pltpu.VMEM((FOX_HEADS, 1, FOX_TQ), F32),
            pltpu.VMEM((FOX_HEADS, 1, FOX_TQ), F32),
            pltpu.VMEM((FOX_HEADS, HEAD_DIM, FOX_TQ), F32),
            pltpu.VMEM((FOX_HEADS, FOX_TK, FOX_TQ), F32),
            pltpu.VMEM((FOX_HEADS, FOX_TK, FOX_TQ), F32),
        ],
        compiler_params=_params("parallel", "parallel", "arbitrary"),
        name="fox",
    )(qk_aug, qk_aug, v_t)


def _out_proj_kernel(a_ref, b_ref, w_ref, x_ref, o_ref):
    mixed = jnp.concatenate([a_ref[...], b_ref[...]], axis=-1)
    o_ref[...] = x_ref[...] + jnp.dot(mixed, w_ref[...].astype(BF16),
                                      preferred_element_type=F32)


OUT_ROW_TILE = 2048


def _out_proj(out_a, out_b, w_out, x):
    m = x.shape[0]
    return pl.pallas_call(
        _out_proj_kernel,
        grid=(m // OUT_ROW_TILE, D_MODEL // COL_TILE),
        in_specs=[
            pl.BlockSpec((OUT_ROW_TILE, D_A), lambda i, j: (i, 0)),
            pl.BlockSpec((OUT_ROW_TILE, D_B), lambda i, j: (i, 0)),
            pl.BlockSpec((D_A + D_B, COL_TILE), lambda i, j: (0, j)),
            pl.BlockSpec((OUT_ROW_TILE, COL_TILE), lambda i, j: (i, j)),
        ],
        out_specs=pl.BlockSpec((OUT_ROW_TILE, COL_TILE), lambda i, j: (i, j)),
        out_shape=jax.ShapeDtypeStruct((m, D_MODEL), F32),
        compiler_params=_params("parallel", "parallel"),
        name="out_proj",
    )(out_a, out_b, w_out, x)


def _mixer(x, b, s, mix_norm, w_in, layer, q_norm_a, k_norm_a, q_norm_b, k_norm_b, forget_bias,
           rel_bias_table, w_out):
    ones = jnp.ones((D_A,), F32)
    head_gain = jnp.concatenate([
        jnp.tile(q_norm_a * (ATTN_SCALE * LOG2E), H_A), jnp.tile(k_norm_a, H_A), ones,
        jnp.tile(q_norm_b * (ATTN_SCALE * LOG2E), H_B), jnp.tile(k_norm_b, H_B), ones,
    ]).reshape(1, D_QKV)
    w_f = jnp.pad(w_in[layer, :, D_QKV:], ((0, 0), (0, LANES - H_B))).astype(BF16)
    f_bias = jnp.pad(forget_bias, (0, LANES - H_B)).reshape(1, LANES)
    *qkv_a, qk_aug, v_t = _proj(x, b, s, mix_norm, w_in, layer, w_f, f_bias, head_gain)

    out_b = _fox(qk_aug, v_t).reshape(b * s, D_B)

    bias = _band_bias(rel_bias_table)
    parts = [_dilated(qkv, bias, p) for p, qkv in enumerate(qkv_a)]
    out_a = _combine([pt[0] for pt in parts], [pt[1] for pt in parts])

    return _out_proj(out_a, out_b, w_out, x)


def kernel(x, ffn1_norm, ffn1_w_in, ffn1_w_out, mix_norm, w_in, q_norm_a, k_norm_a, q_norm_b,
           k_norm_b, forget_bias, rel_bias_table, w_out, ffn2_norm, ffn2_w_in, ffn2_w_out):
    b, s, d = x.shape
    depth = ffn1_norm.shape[0]
    x = x.reshape(b * s, d)
    for l in range(depth):
        x = _ffn(x, ffn1_norm[l], ffn1_w_in[l], ffn1_w_out[l])
        x = _mixer(x, b, s, mix_norm[l], w_in, l, q_norm_a[l], k_norm_a[l], q_norm_b[l],
                   k_norm_b[l], forget_bias[l], rel_bias_table, w_out[l])
        x = _ffn(x, ffn2_norm[l], ffn2_w_in[l], ffn2_w_out[l])
    return x.reshape(b, s, d)
```

```python
import functools
import math

import numpy as np
import jax
import jax.numpy as jnp
from jax import lax
from jax.experimental import pallas as pl
from jax.experimental.pallas import tpu as pltpu

D_MODEL = 2048
HEAD_DIM = 128
N_HEADS = D_MODEL // HEAD_DIM
H_A = N_HEADS // 2
H_B = N_HEADS - H_A
D_A = H_A * HEAD_DIM
D_B = H_B * HEAD_DIM
D_QKV = 3 * D_A + 3 * D_B
DILATED_PATTERNS = ((128, 1), (512, 4), (2048, 16))
BAND = 128
NUM_BUCKETS = 32
MAX_DISTANCE = 2048
D_FF = ((8 * D_MODEL // 3 + 127) // 128) * 128
RMS_EPS = 1e-6
NEG_INF = -1e30
ATTN_SCALE = HEAD_DIM ** -0.5

LANES = 128
FF_TILE = 512
FFN_ROW_TILE = 2048
FFN_CHUNK = 256
FFN_VMEM_LIMIT = 60 * 1024 * 1024
ROW_TILE = 1024
COL_TILE = 512
FOX_TQ = 512
FOX_TK = 512
FOX_HEADS = 4
VMEM_LIMIT = 56 * 1024 * 1024

F32 = jnp.float32
BF16 = jnp.bfloat16


def _params(*sem):
    return pltpu.CompilerParams(dimension_semantics=sem, vmem_limit_bytes=VMEM_LIMIT)


def _rms_rows(x, gain):
    ms = jnp.mean(x * x, axis=-1, keepdims=True)
    return x * lax.rsqrt(ms + RMS_EPS) * gain


def _ffn_kernel(x_hbm, g_ref, wg_ref, wu_ref, wo_ref, o_hbm, acc_ref, h_ref, in_sem, out_sem):
    i = pl.program_id(0)
    j = pl.program_id(1)
    last_i = pl.num_programs(0) - 1
    last_j = pl.num_programs(1) - 1
    pieces = [(r, c) for r in range(FFN_ROW_TILE // ROW_TILE) for c in range(D_MODEL // COL_TILE)]

    def x_copy(c):
        src = x_hbm.at[pl.ds(i * FFN_ROW_TILE + c * FFN_CHUNK, FFN_CHUNK), :]
        return pltpu.make_async_copy(src, acc_ref.at[pl.ds(c * FFN_CHUNK, FFN_CHUNK), :],
                                     in_sem.at[c])

    def out_copy(row_tile, r, c):
        rows, cols = pl.ds(r * ROW_TILE, ROW_TILE), pl.ds(c * COL_TILE, COL_TILE)
        dst = o_hbm.at[pl.ds(row_tile * FFN_ROW_TILE + r * ROW_TILE, ROW_TILE), cols]
        return pltpu.make_async_copy(acc_ref.at[rows, cols], dst,
                                     out_sem.at[r * (D_MODEL // COL_TILE) + c])

    @pl.when(j == 0)
    def _():
        @pl.when(i > 0)
        def _():
            for r, c in pieces:
                out_copy(i - 1, r, c).wait()

        n_chunks = FFN_ROW_TILE // FFN_CHUNK
        for c in range(n_chunks):
            x_copy(c).start()
        for c in range(n_chunks):
            x_copy(c).wait()
            rows = slice(c * FFN_CHUNK, (c + 1) * FFN_CHUNK)
            h_ref[rows, :] = _rms_rows(acc_ref[rows, :], g_ref[...]).astype(BF16)

    col = lax.broadcasted_iota(jnp.int32, (1, FF_TILE), 1)
    repeated = j * FF_TILE - jnp.minimum(j * FF_TILE, D_FF - FF_TILE)

    def tile(write_back):
        for r in range(FFN_ROW_TILE // ROW_TILE):
            rows = slice(r * ROW_TILE, (r + 1) * ROW_TILE)
            h = h_ref[rows, :]
            gate = jnp.dot(h, wg_ref[...].astype(BF16), preferred_element_type=F32)
            up = jnp.dot(h, wu_ref[...].astype(BF16), preferred_element_type=F32)
            act = gate * (1.0 / (1.0 + jnp.exp(-gate))) * (0.5 * up)
            act = jnp.where(col >= repeated, act, 0.0).astype(BF16)
            for c in range(D_MODEL // COL_TILE):
                sl = slice(c * COL_TILE, (c + 1) * COL_TILE)
                acc_ref[rows, sl] += jnp.dot(act, wo_ref[:, sl].astype(BF16),
                                             preferred_element_type=F32)
                if write_back:
                    out_copy(i, r, c).start()

    @pl.when(j < last_j)
    def _():
        tile(False)

    @pl.when(j == last_j)
    def _():
        tile(True)

        @pl.when(i == last_i)
        def _():
            for r, c in pieces:
                out_copy(i, r, c).wait()


def _ffn(x, gain, w_in, w_out):
    m = x.shape[0]
    n_ff = pl.cdiv(D_FF, FF_TILE)

    def ff_start(j):
        return pl.multiple_of(jnp.minimum(j * FF_TILE, D_FF - FF_TILE), LANES)

    return pl.pallas_call(
        _ffn_kernel,
        grid=(m // FFN_ROW_TILE, n_ff),
        in_specs=[
            pl.BlockSpec(memory_space=pl.ANY),
            pl.BlockSpec((1, D_MODEL), lambda i, j: (0, 0)),
            pl.BlockSpec((pl.Element(D_MODEL), pl.Element(FF_TILE)),
                         lambda i, j: (0, ff_start(j))),
            pl.BlockSpec((pl.Element(D_MODEL), pl.Element(FF_TILE)),
                         lambda i, j: (0, pl.multiple_of(D_FF + ff_start(j), LANES))),
            pl.BlockSpec((pl.Element(FF_TILE), pl.Element(D_MODEL)),
                         lambda i, j: (ff_start(j), 0)),
        ],
        out_specs=pl.BlockSpec(memory_space=pl.ANY),
        out_shape=jax.ShapeDtypeStruct((m, D_MODEL), F32),
        scratch_shapes=[
            pltpu.VMEM((FFN_ROW_TILE, D_MODEL), F32),
            pltpu.VMEM((FFN_ROW_TILE, D_MODEL), BF16),
            pltpu.SemaphoreType.DMA((FFN_ROW_TILE // FFN_CHUNK,)),
            pltpu.SemaphoreType.DMA(((FFN_ROW_TILE // ROW_TILE) * (D_MODEL // COL_TILE),)),
        ],
        compiler_params=pltpu.CompilerParams(
            dimension_semantics=("arbitrary", "arbitrary"), vmem_limit_bytes=FFN_VMEM_LIMIT),
        name="ffn",
    )(x, gain.reshape(1, D_MODEL), w_in, w_in, w_out)


HEADS_PER_TILE = COL_TILE // HEAD_DIM
N_PROJ_TILES = D_QKV // COL_TILE
TILES_PER_GROUP = D_A // COL_TILE
A_TILES = 3 * TILES_PER_GROUP
AUG = 2 * HEAD_DIM
LOG2E = math.log2(math.e)


def _log_gate_scan(z, carry):
    c = (jnp.minimum(z, 0.0) - jnp.log1p(jnp.exp(-jnp.abs(z)))) * LOG2E
    row = lax.broadcasted_iota(jnp.int32, c.shape, 0)
    shift = 1
    while shift < c.shape[0]:
        c = c + jnp.where(row >= shift, pltpu.roll(c, shift, axis=0), 0.0)
        shift *= 2
    return c + carry


def _split3(c):
    hi = c.astype(BF16)
    rest = c - hi.astype(F32)
    mid = rest.astype(BF16)
    lo = (rest - mid.astype(F32)).astype(BF16)
    return hi, mid, lo


MXU_WIDTH = 256


def _head_mean_matrix():
    head = np.arange(MXU_WIDTH) // HEAD_DIM
    return jnp.asarray((head[:, None] == head[None, :]) / HEAD_DIM, BF16)


GATE_ONES_LANE = 3 * H_B


def _gate_placement_matrices():
    mats = np.zeros((2, TILES_PER_GROUP, LANES, COL_TILE), np.float32)
    for part in range(TILES_PER_GROUP):
        for hh in range(HEADS_PER_TILE):
            head = part * HEADS_PER_TILE + hh
            col = hh * HEAD_DIM
            for term in range(3):
                mats[0, part, term * H_B + head, col + term] = 1.0
                mats[0, part, GATE_ONES_LANE, col + 3 + term] = 1.0
                mats[1, part, term * H_B + head, col + 3 + term] = -1.0
                mats[1, part, GATE_ONES_LANE, col + term] = 1.0
    return jnp.asarray(mats.reshape(2 * TILES_PER_GROUP, LANES, COL_TILE), BF16)


def _proj_kernel(x_ref, g_ref, w_ref, wf_ref, fb_ref, hg_ref, mean_ref, place_ref,
                 a1_ref, a4_ref, a16_ref, qk_ref, vt_ref,
                 h_ref, cs_ref, carry_ref, y_ref, yf_ref, *, tiles_per_seq):
    i = pl.program_id(0)
    j = pl.program_id(1)

    @pl.when(j == 0)
    def _():
        h_ref[...] = _rms_rows(x_ref[...], g_ref[...]).astype(BF16)

        @pl.when(i % tiles_per_seq == 0)
        def _():
            carry_ref[...] = jnp.zeros(carry_ref.shape, F32)

        z = jnp.dot(h_ref[...], wf_ref[...], preferred_element_type=F32) + fb_ref[...]
        c = _log_gate_scan(z, carry_ref[...])
        carry_ref[...] = c[ROW_TILE - 1:ROW_TILE, :]
        lane = lax.broadcasted_iota(jnp.int32, c.shape, 1)
        packed = jnp.where(lane == GATE_ONES_LANE, 1.0, 0.0)
        for term, part in enumerate(_split3(c)):
            part = jnp.where(lane < H_B, part.astype(F32), 0.0)
            packed = packed + (pltpu.roll(part, term * H_B, axis=1) if term else part)
        cs_ref[...] = packed.astype(BF16)

    def product(transposed=False):
        lhs, rhs = (w_ref[...].astype(BF16), h_ref[...])
        if not transposed:
            lhs, rhs = rhs, lhs
        return lax.dot_general(lhs, rhs, (((1,), (1,)), ((), ())), preferred_element_type=F32)

    def normed():
        r = product()
        sq = (r * r).astype(BF16)
        ms = jnp.concatenate(
            [jnp.dot(sq[:, k:k + MXU_WIDTH], mean_ref[...], preferred_element_type=F32)
             for k in range(0, COL_TILE, MXU_WIDTH)], axis=1)
        return r * lax.rsqrt(ms + RMS_EPS) * hg_ref[...]

    def store_dilated(y):
        a1_ref[0, 0] = y.astype(BF16)
        fine, coarse = a4_ref.shape[1], a16_ref.shape[1]
        step = coarse // fine
        for c in range(HEADS_PER_TILE):
            sl = slice(c * LANES, (c + 1) * LANES)
            y_ref[c] = y[:, sl]
            for rf in range(fine):
                rows = y_ref[c, pl.ds(rf, ROW_TILE // fine, stride=fine), :]
                a4_ref[0, rf, :, sl] = rows.astype(BF16)
                yf_ref[c, rf] = rows
                for q in range(step):
                    rows = yf_ref[c, rf, pl.ds(q, ROW_TILE // coarse, stride=step), :]
                    a16_ref[0, rf + fine * q, :, sl] = rows.astype(BF16)

    def tiles(first_group, n_groups=1):
        lo = first_group * TILES_PER_GROUP
        return jnp.logical_and(j >= lo, j < lo + n_groups * TILES_PER_GROUP)

    def store_forgetting():
        y = normed().astype(BF16)
        aug = jnp.dot(cs_ref[...], place_ref[0], preferred_element_type=F32).astype(BF16)
        for hh in range(HEADS_PER_TILE):
            sl = slice(hh * HEAD_DIM, (hh + 1) * HEAD_DIM)
            qk_ref[0, hh, :, :HEAD_DIM] = y[:, sl]
            qk_ref[0, hh, :, HEAD_DIM:] = aug[:, sl]

    @pl.when(tiles(0, 2))
    def _():
        store_dilated(normed())

    @pl.when(tiles(2))
    def _():
        store_dilated(product())

    @pl.when(tiles(3, 2))
    def _():
        store_forgetting()

    @pl.when(tiles(5))
    def _():
        v_t = product(transposed=True).astype(BF16)
        for hh in range(HEADS_PER_TILE):
            vt_ref[0, hh] = v_t[hh * HEAD_DIM:(hh + 1) * HEAD_DIM, :]


def _proj(x, b, s, gain, w_in, layer, w_f, f_bias, head_gain):
    m = x.shape[0]
    tps = s // ROW_TILE

    def a_spec(dilation):
        return pl.BlockSpec((1, dilation, ROW_TILE // dilation, COL_TILE),
                            lambda i, j: (i // tps, 0, i % tps, jnp.minimum(j, A_TILES - 1)))

    def qk_block(i, j):
        t = jnp.clip(j - qb_first, 0, 2 * TILES_PER_GROUP - 1)
        return (t // TILES_PER_GROUP, i // tps, t % TILES_PER_GROUP, i % tps, 0)

    qb_first = A_TILES
    vt_first = A_TILES + 2 * TILES_PER_GROUP
    dilations = [d for _, d in DILATED_PATTERNS]
    return pl.pallas_call(
        functools.partial(_proj_kernel, tiles_per_seq=tps),
        grid=(m // ROW_TILE, N_PROJ_TILES),
        in_specs=[
            pl.BlockSpec((ROW_TILE, D_MODEL), lambda i, j: (i, 0)),
            pl.BlockSpec((1, D_MODEL), lambda i, j: (0, 0)),
            pl.BlockSpec((None, COL_TILE, D_MODEL), lambda i, j: (layer, j, 0)),
            pl.BlockSpec((D_MODEL, LANES), lambda i, j: (0, 0)),
            pl.BlockSpec((1, LANES), lambda i, j: (0, 0)),
            pl.BlockSpec((1, COL_TILE), lambda i, j: (0, j)),
            pl.BlockSpec((MXU_WIDTH, MXU_WIDTH), lambda i, j: (0, 0)),
            pl.BlockSpec((1, LANES, COL_TILE),
                         lambda i, j: (jnp.clip(j - qb_first, 0, 2 * TILES_PER_GROUP - 1), 0, 0)),
        ],
        out_specs=[a_spec(d) for d in dilations] + [
            pl.BlockSpec((None, 1, HEADS_PER_TILE, ROW_TILE, AUG), qk_block),
            pl.BlockSpec((1, HEADS_PER_TILE, HEAD_DIM, ROW_TILE),
                         lambda i, j: (i // tps, jnp.clip(j - vt_first, 0, TILES_PER_GROUP - 1),
                                       0, i % tps)),
        ],
        out_shape=[jax.ShapeDtypeStruct((b, d, s // d, 3 * D_A), BF16) for d in dilations] + [
            jax.ShapeDtypeStruct((2, b, H_B, s, AUG), BF16),
            jax.ShapeDtypeStruct((b, H_B, HEAD_DIM, s), BF16),
        ],
        scratch_shapes=[
            pltpu.VMEM((ROW_TILE, D_MODEL), BF16),
            pltpu.VMEM((ROW_TILE, LANES), BF16),
            pltpu.VMEM((1, LANES), F32),
            pltpu.VMEM((HEADS_PER_TILE, ROW_TILE, LANES), F32),
            pltpu.VMEM((HEADS_PER_TILE, dilations[1], ROW_TILE // dilations[1], LANES), F32),
        ],
        compiler_params=_params("arbitrary", "arbitrary"),
        name="proj",
    )(x, gain.reshape(1, D_MODEL), jnp.swapaxes(w_in, 1, 2), w_f, f_bias, head_gain,
      _head_mean_matrix(), _gate_placement_matrices())


def _bucket_steps(dilation):
    dist = np.arange(BAND + 1) * dilation
    max_exact = NUM_BUCKETS // 2
    large = max_exact + np.floor(
        np.log(np.maximum(dist, 1) / max_exact) / math.log(MAX_DISTANCE / max_exact)
        * (NUM_BUCKETS - max_exact)).astype(np.int64)
    bucket = np.where(dist < max_exact, dist, np.minimum(large, NUM_BUCKETS - 1))
    steps = [(0, int(bucket[0]))]
    for delta in range(1, BAND + 1):
        if bucket[delta] != bucket[delta - 1]:
            steps.append((delta, int(bucket[delta])))
    return steps


def _bias_kernel(table_ref, o_ref):
    iq = lax.broadcasted_iota(jnp.int32, (BAND, 2 * BAND), 0)
    ik = lax.broadcasted_iota(jnp.int32, (BAND, 2 * BAND), 1)
    delta = iq + BAND - ik
    in_band = jnp.logical_and(delta >= 0, delta <= BAND)
    for p, (_, dilation) in enumerate(DILATED_PATTERNS):
        steps = _bucket_steps(dilation)
        for h in range(H_A):
            val = jnp.full((BAND, 2 * BAND), table_ref[steps[0][1], h], F32)
            for start, bucket in steps[1:]:
                val = jnp.where(delta >= start, table_ref[bucket, h], val)
            o_ref[p, h] = jnp.where(in_band, val * LOG2E, NEG_INF)


def _band_bias(rel_table):
    n_pat = len(DILATED_PATTERNS)
    return pl.pallas_call(
        _bias_kernel,
        in_specs=[pl.BlockSpec(memory_space=pltpu.SMEM)],
        out_specs=pl.BlockSpec(memory_space=pltpu.VMEM),
        out_shape=jax.ShapeDtypeStruct((n_pat, H_A, BAND, 2 * BAND), F32),
        name="band_bias",
    )(rel_table)


DIL_UNITS = 16


def _dilated_kernel(q_ref, kp_ref, kc_ref, vp_ref, vc_ref, bias_ref, o_ref, st_ref):
    n_cls = q_ref.shape[1]
    n_blk = q_ref.shape[2] // BAND
    first = pl.program_id(2) == 0
    key_lane = lax.broadcasted_iota(jnp.int32, (1, 2 * BAND), 1)
    no_prev = jnp.where(jnp.logical_and(first, key_lane < BAND), NEG_INF, 0.0)
    stat_lane = lax.broadcasted_iota(jnp.int32, (BAND, LANES), 1)
    units = [(c, blk) for c in range(n_cls) for blk in range(n_blk)]

    def keys(prev_ref, cur_ref, c, blk, sl):
        if blk == 0:
            return jnp.concatenate([prev_ref[0, c, :, sl], cur_ref[0, c, :BAND, sl]], axis=0)
        return cur_ref[0, c, (blk - 1) * BAND:(blk + 1) * BAND, sl]

    scores = {}
    for c, blk in units:
        for h in range(H_A):
            sl = slice(h * HEAD_DIM, (h + 1) * HEAD_DIM)
            q = q_ref[0, c, blk * BAND:(blk + 1) * BAND, sl]
            scores[c, blk, h] = lax.dot_general(q, keys(kp_ref, kc_ref, c, blk, sl),
                                                (((1,), (1,)), ((), ())),
                                                preferred_element_type=F32)
    for c, blk in units:
        rows = slice(blk * BAND, (blk + 1) * BAND)
        stats = jnp.zeros((BAND, LANES), F32)
        for h in range(H_A):
            sl = slice(h * HEAD_DIM, (h + 1) * HEAD_DIM)
            s = scores[c, blk, h] + bias_ref[0, h]
            if blk == 0:
                s = s + no_prev
            mx = jnp.max(s, axis=-1, keepdims=True)
            p = jnp.exp2(s - mx)
            den = jnp.sum(p, axis=-1, keepdims=True)
            o_ref[0, c, rows, sl] = jnp.dot(p.astype(BF16), keys(vp_ref, vc_ref, c, blk, sl),
                                            preferred_element_type=F32).astype(o_ref.dtype)
            stats = jnp.where(stat_lane == h, mx, stats)
            stats = jnp.where(stat_lane == H_A + h, den, stats)
        st_ref[0, c, rows, :] = stats


def _dilated(qkv, bias, pattern):
    b, dilation, n_sub, _ = qkv.shape
    n_blk = min(DIL_UNITS, n_sub // BAND)
    n_cls = DIL_UNITS // n_blk
    rows = n_blk * BAND
    nb = n_sub // rows

    def cur(which):
        return pl.BlockSpec((1, n_cls, rows, D_A), lambda bi, r, i: (bi, r, i, which))

    def prev(which):
        return pl.BlockSpec((1, n_cls, BAND, D_A),
                            lambda bi, r, i: (bi, r, jnp.maximum(i * n_blk - 1, 0), which))

    return pl.pallas_call(
        _dilated_kernel,
        grid=(b, dilation // n_cls, nb),
        in_specs=[
            cur(0), prev(1), cur(1), prev(2), cur(2),
            pl.BlockSpec((1, H_A, BAND, 2 * BAND), lambda bi, r, i: (pattern, 0, 0, 0)),
        ],
        out_specs=[
            pl.BlockSpec((1, n_cls, rows, D_A), lambda bi, r, i: (bi, r, i, 0)),
            pl.BlockSpec((1, n_cls, rows, LANES), lambda bi, r, i: (bi, r, i, 0)),
        ],
        out_shape=[
            jax.ShapeDtypeStruct((b, dilation, n_sub, D_A), BF16),
            jax.ShapeDtypeStruct((b, dilation, n_sub, LANES), F32),
        ],
        compiler_params=_params("parallel", "parallel", "arbitrary"),
        name=f"dilated_{dilation}",
    )(qkv, qkv, qkv, qkv, qkv, bias)


COMBINE_ROWS = 512


def _combine_kernel(*refs):
    n_pat = len(DILATED_PATTERNS)
    num_refs, st_refs = refs[:n_pat], refs[n_pat:2 * n_pat]
    o_ref, num_buf, st_buf, num_tmp, st_tmp = refs[2 * n_pat:]
    dilations = [ref.shape[1] for ref in num_refs]
    fine = min(d for d in dilations if d > 1)
    heads = [slice(h * HEAD_DIM, (h + 1) * HEAD_DIM) for h in range(H_A)]
    for p, dilation in enumerate(dilations):
        if dilation == 1:
            continue
        step = dilation // fine
        for rf in range(fine):
            rows = pl.ds(rf, COMBINE_ROWS // fine, stride=fine)
            if step == 1:
                st_buf[p, rows, :] = st_refs[p][0, rf]
                for h, sl in enumerate(heads):
                    num_buf[p, h, rows, :] = num_refs[p][0, rf, :, sl].astype(F32)
                continue
            for q in range(step):
                part = pl.ds(q, COMBINE_ROWS // dilation, stride=step)
                st_tmp[part, :] = st_refs[p][0, rf + fine * q]
                for h, sl in enumerate(heads):
                    num_tmp[h, part, :] = num_refs[p][0, rf + fine * q, :, sl].astype(F32)
            st_buf[p, rows, :] = st_tmp[...]
            for h in range(H_A):
                num_buf[p, h, rows, :] = num_tmp[h]
    stats = [st_refs[p][0, 0] if d == 1 else st_buf[p] for p, d in enumerate(dilations)]
    lane = lax.broadcasted_iota(jnp.int32, stats[0].shape, 1)
    mx = functools.reduce(jnp.maximum, stats)
    scales = [jnp.exp2(st - mx) for st in stats]
    den = sum(sc * pltpu.roll(st, LANES - H_A, axis=1) for sc, st in zip(scales, stats))
    den = jnp.where(lane < H_A, den, 1.0)
    weights = [sc / den for sc in scales]
    for h in range(H_A):
        sl = slice(h * HEAD_DIM, (h + 1) * HEAD_DIM)
        out = None
        for p, w in enumerate(weights):
            if dilations[p] == 1:
                pnum = num_refs[p][0, 0, :, sl].astype(F32)
            else:
                pnum = num_buf[p, h]
            out = pnum * w[:, h:h + 1] if out is None else out + pnum * w[:, h:h + 1]
        o_ref[:, sl] = out.astype(BF16)


def _combine(nums, stats):
    b, _, s, _ = nums[0].shape
    tps = s // COMBINE_ROWS

    def spec(arr):
        dilation, width = arr.shape[1], arr.shape[3]
        return pl.BlockSpec((1, dilation, COMBINE_ROWS // dilation, width),
                            lambda i: (i // tps, 0, i % tps, 0))

    n_pat = len(nums)
    fine = min(a.shape[1] for a in nums if a.shape[1] > 1)
    return pl.pallas_call(
        _combine_kernel,
        grid=(b * tps,),
        in_specs=[spec(a) for a in nums] + [spec(a) for a in stats],
        out_specs=pl.BlockSpec((COMBINE_ROWS, D_A), lambda i: (i, 0)),
        out_shape=jax.ShapeDtypeStruct((b * s, D_A), BF16),
        scratch_shapes=[
            pltpu.VMEM((n_pat, H_A, COMBINE_ROWS, HEAD_DIM), F32),
            pltpu.VMEM((n_pat, COMBINE_ROWS, LANES), F32),
            pltpu.VMEM((H_A, COMBINE_ROWS // fine, HEAD_DIM), F32),
            pltpu.VMEM((COMBINE_ROWS // fine, LANES), F32),
        ],
        compiler_params=_params("parallel"),
        name="combine_a",
    )(*nums, *stats)


def _fox_kernel(q_ref, k_ref, vt_ref, o_ref, m_ref, l_ref, acc_ref, sa_ref, sb_ref):
    iq = pl.program_id(2)
    m_ref[...] = jnp.full(m_ref.shape, NEG_INF, F32)
    l_ref[...] = jnp.zeros(l_ref.shape, F32)
    acc_ref[...] = jnp.zeros(acc_ref.shape, F32)

    def scores_into(s_ref, kb):
        start = pl.multiple_of(kb * FOX_TK, FOX_TK)
        for hh in range(FOX_HEADS):
            k = k_ref[0, hh, pl.ds(start, FOX_TK), :]
            s_ref[hh] = lax.dot_general(k, q_ref[0, hh], (((1,), (1,)), ((), ())),
                                        preferred_element_type=F32)

    def accumulate(s_ref, kb, diagonal):
        start = pl.multiple_of(kb * FOX_TK, FOX_TK)
        n_parts = 2 if diagonal else 1
        width = FOX_TQ // n_parts
        for hh in range(FOX_HEADS):
            for part in range(n_parts):
                cols = slice(part * width, (part + 1) * width)
                n_keys = (part + 1) * FOX_TK // n_parts
                st = s_ref[hh, :n_keys, cols]
                if diagonal:
                    key = lax.broadcasted_iota(jnp.int32, st.shape, 0)
                    qry = lax.broadcasted_iota(jnp.int32, st.shape, 1) + part * width
                    st = jnp.where(key <= qry, st, NEG_INF)
                m_prev = m_ref[hh, :, cols]
                m_new = jnp.maximum(m_prev, jnp.max(st, axis=0, keepdims=True))
                alpha = jnp.exp2(m_prev - m_new)
                p = jnp.exp2(st - m_new)
                l_ref[hh, :, cols] = alpha * l_ref[hh, :, cols] + jnp.sum(p, axis=0,
                                                                         keepdims=True)
                vt = vt_ref[0, hh, :, pl.ds(start, n_keys)]
                acc_ref[hh, :, cols] = alpha * acc_ref[hh, :, cols] + jnp.dot(
                    vt, p.astype(BF16), preferred_element_type=F32)
                m_ref[hh, :, cols] = m_new

    scores_into(sa_ref, 0)

    def pair(p, carry):
        scores_into(sb_ref, 2 * p + 1)
        accumulate(sa_ref, 2 * p, False)
        scores_into(sa_ref, 2 * p + 2)
        accumulate(sb_ref, 2 * p + 1, False)
        return carry

    lax.fori_loop(0, iq // 2, pair, 0)

    @pl.when(iq % 2 == 0)
    def _():
        accumulate(sa_ref, iq, True)

    @pl.when(iq % 2 == 1)
    def _():
        scores_into(sb_ref, iq)
        accumulate(sa_ref, iq - 1, False)
        accumulate(sb_ref, iq, True)
    for hh in range(FOX_HEADS):
        o_ref[0, :, hh * HEAD_DIM:(hh + 1) * HEAD_DIM] = (
            acc_ref[hh] / l_ref[hh]).T.astype(BF16)


def _fox(qk_aug, v_t):
    _, b, h_b, s, _ = qk_aug.shape
    assert FOX_TQ == FOX_TK
    return pl.pallas_call(
        _fox_kernel,
        grid=(b, h_b // FOX_HEADS, s // FOX_TQ),
        in_specs=[
            pl.BlockSpec((None, 1, FOX_HEADS, FOX_TQ, AUG), lambda bi, h, i: (0, bi, h, i, 0)),
            pl.BlockSpec((None, 1, FOX_HEADS, s, AUG), lambda bi, h, i: (1, bi, h, 0, 0)),
            pl.BlockSpec((1, FOX_HEADS, HEAD_DIM, s), lambda bi, h, i: (bi, h, 0, 0)),
        ],
        out_specs=pl.BlockSpec((1, FOX_TQ, FOX_HEADS * HEAD_DIM), lambda bi, h, i: (bi, i, h)),
        out_shape=jax.ShapeDtypeStruct((b, s, h_b * HEAD_DIM), BF16),
        scratch_shapes=[
            pltpu.VMEM((FOX_HEADS, 1, FOX_TQ), F32),
            pltpu.VMEM((FOX_HEADS, 1, FOX_TQ), F32),
            pltpu.VMEM((FOX_HEADS, HEAD_DIM, FOX_TQ), F32),
            pltpu.VMEM((FOX_HEADS, FOX_TK, FOX_TQ), F32),
            pltpu.VMEM((FOX_HEADS, FOX_TK, FOX_TQ), F32),
        ],
        compiler_params=_params("parallel", "parallel", "arbitrary"),
        name="fox",
    )(qk_aug, qk_aug, v_t)


def _out_proj_kernel(a_ref, b_ref, w_ref, x_ref, o_ref):
    mixed = jnp.concatenate([a_ref[...], b_ref[...]], axis=-1)
    o_ref[...] = x_ref[...] + jnp.dot(mixed, w_ref[...].astype(BF16),
                                      preferred_element_type=F32)


OUT_ROW_TILE = 2048


def _out_proj(out_a, out_b, w_out, x):
    m = x.shape[0]
    return pl.pallas_call(
        _out_proj_kernel,
        grid=(m // OUT_ROW_TILE, D_MODEL // COL_TILE),
        in_specs=[
            pl.BlockSpec((OUT_ROW_TILE, D_A), lambda i, j: (i, 0)),
            pl.BlockSpec((OUT_ROW_TILE, D_B), lambda i, j: (i, 0)),
            pl.BlockSpec((D_A + D_B, COL_TILE), lambda i, j: (0, j)),
            pl.BlockSpec((OUT_ROW_TILE, COL_TILE), lambda i, j: (i, j)),
        ],
        out_specs=pl.BlockSpec((OUT_ROW_TILE, COL_TILE), lambda i, j: (i, j)),
        out_shape=jax.ShapeDtypeStruct((m, D_MODEL), F32),
        compiler_params=_params("parallel", "parallel"),
        name="out_proj",
    )(out_a, out_b, w_out, x)


def _mixer(x, b, s, mix_norm, w_in, layer, q_norm_a, k_norm_a, q_norm_b, k_norm_b, forget_bias,
           rel_bias_table, w_out):
    ones = jnp.ones((D_A,), F32)
    head_gain = jnp.concatenate([
        jnp.tile(q_norm_a * (ATTN_SCALE * LOG2E), H_A), jnp.tile(k_norm_a, H_A), ones,
        jnp.tile(q_norm_b * (ATTN_SCALE * LOG2E), H_B), jnp.tile(k_norm_b, H_B), ones,
    ]).reshape(1, D_QKV)
    w_f = jnp.pad(w_in[layer, :, D_QKV:], ((0, 0), (0, LANES - H_B))).astype(BF16)
    f_bias = jnp.pad(forget_bias, (0, LANES - H_B)).reshape(1, LANES)
    *qkv_a, qk_aug, v_t = _proj(x, b, s, mix_norm, w_in, layer, w_f, f_bias, head_gain)

    out_b = _fox(qk_aug, v_t).reshape(b * s, D_B)

    bias = _band_bias(rel_bias_table)
    parts = [_dilated(qkv, bias, p) for p, qkv in enumerate(qkv_a)]
    out_a = _combine([pt[0] for pt in parts], [pt[1] for pt in parts])

    return _out_proj(out_a, out_b, w_out, x)


def kernel(x, ffn1_norm, ffn1_w_in, ffn1_w_out, mix_norm, w_in, q_norm_a, k_norm_a, q_norm_b,
           k_norm_b, forget_bias, rel_bias_table, w_out, ffn2_norm, ffn2_w_in, ffn2_w_out):
    b, s, d = x.shape
    depth = ffn1_norm.shape[0]
    x = x.reshape(b * s, d)
    for l in range(depth):
        x = _ffn(x, ffn1_norm[l], ffn1_w_in[l], ffn1_w_out[l])
        x = _mixer(x, b, s, mix_norm[l], w_in, l, q_norm_a[l], k_norm_a[l], q_norm_b[l],
                   k_norm_b[l], forget_bias[l], rel_bias_table, w_out[l])
        x = _ffn(x, ffn2_norm[l], ffn2_w_in[l], ffn2_w_out[l])
    return x.reshape(b, s, d)
```

```python
import functools
import math

import numpy as np
import jax
import jax.numpy as jnp
from jax import lax
from jax.experimental import pallas as pl
from jax.experimental.pallas import tpu as pltpu

D_MODEL = 2048
HEAD_DIM = 128
N_HEADS = D_MODEL // HEAD_DIM
H_A = N_HEADS // 2
H_B = N_HEADS - H_A
D_A = H_A * HEAD_DIM
D_B = H_B * HEAD_DIM
D_QKV = 3 * D_A + 3 * D_B
DILATED_PATTERNS = ((128, 1), (512, 4), (2048, 16))
BAND = 128
NUM_BUCKETS = 32
MAX_DISTANCE = 2048
D_FF = ((8 * D_MODEL // 3 + 127) // 128) * 128
RMS_EPS = 1e-6
NEG_INF = -1e30
ATTN_SCALE = HEAD_DIM ** -0.5

LANES = 128
FF_TILE = 512
FFN_ROW_TILE = 2048
FFN_CHUNK = 256
FFN_VMEM_LIMIT = 60 * 1024 * 1024
ROW_TILE = 1024
COL_TILE = 512
FOX_TQ = 512
FOX_TK = 512
FOX_HEADS = 4
VMEM_LIMIT = 56 * 1024 * 1024

F32 = jnp.float32
BF16 = jnp.bfloat16


def _params(*sem):
    return pltpu.CompilerParams(dimension_semantics=sem, vmem_limit_bytes=VMEM_LIMIT)


def _rms_rows(x, gain):
    ms = jnp.mean(x * x, axis=-1, keepdims=True)
    return x * lax.rsqrt(ms + RMS_EPS) * gain


def _ffn_kernel(x_hbm, g_ref, wg_ref, wu_ref, wo_ref, o_hbm, acc_ref, h_ref, in_sem, out_sem):
    i = pl.program_id(0)
    j = pl.program_id(1)
    last_i = pl.num_programs(0) - 1
    last_j = pl.num_programs(1) - 1
    pieces = [(r, c) for r in range(FFN_ROW_TILE // ROW_TILE) for c in range(D_MODEL // COL_TILE)]

    def x_copy(c):
        src = x_hbm.at[pl.ds(i * FFN_ROW_TILE + c * FFN_CHUNK, FFN_CHUNK), :]
        return pltpu.make_async_copy(src, acc_ref.at[pl.ds(c * FFN_CHUNK, FFN_CHUNK), :],
                                     in_sem.at[c])

    def out_copy(row_tile, r, c):
        rows, cols = pl.ds(r * ROW_TILE, ROW_TILE), pl.ds(c * COL_TILE, COL_TILE)
        dst = o_hbm.at[pl.ds(row_tile * FFN_ROW_TILE + r * ROW_TILE, ROW_TILE), cols]
        return pltpu.make_async_copy(acc_ref.at[rows, cols], dst,
                                     out_sem.at[r * (D_MODEL // COL_TILE) + c])

    @pl.when(j == 0)
    def _():
        @pl.when(i > 0)
        def _():
            for r, c in pieces:
                out_copy(i - 1, r, c).wait()

        n_chunks = FFN_ROW_TILE // FFN_CHUNK
        for c in range(n_chunks):
            x_copy(c).start(priority=c % 2)
        for c in range(n_chunks):
            x_copy(c).wait()
            rows = slice(c * FFN_CHUNK, (c + 1) * FFN_CHUNK)
            h_ref[rows, :] = _rms_rows(acc_ref[rows, :], g_ref[...]).astype(BF16)

    col = lax.broadcasted_iota(jnp.int32, (1, FF_TILE), 1)
    repeated = j * FF_TILE - jnp.minimum(j * FF_TILE, D_FF - FF_TILE)

    def tile(write_back):
        for r in range(FFN_ROW_TILE // ROW_TILE):
            rows = slice(r * ROW_TILE, (r + 1) * ROW_TILE)
            h = h_ref[rows, :]
            gate = jnp.dot(h, wg_ref[...].astype(BF16), preferred_element_type=F32)
            up = jnp.dot(h, wu_ref[...].astype(BF16), preferred_element_type=F32)
            act = gate * (1.0 / (1.0 + jnp.exp(-gate))) * (0.5 * up)
            act = jnp.where(col >= repeated, act, 0.0).astype(BF16)
            for c in range(D_MODEL // COL_TILE):
                sl = slice(c * COL_TILE, (c + 1) * COL_TILE)
                acc_ref[rows, sl] += jnp.dot(act, wo_ref[:, sl].astype(BF16),
                                             preferred_element_type=F32)
                if write_back:
                    out_copy(i, r, c).start(priority=c % 2)

    @pl.when(j < last_j)
    def _():
        tile(False)

    @pl.when(j == last_j)
    def _():
        tile(True)

        @pl.when(i == last_i)
        def _():
            for r, c in pieces:
                out_copy(i, r, c).wait()


def _ffn(x, gain, w_in, w_out):
    m = x.shape[0]
    n_ff = pl.cdiv(D_FF, FF_TILE)

    def ff_start(j):
        return pl.multiple_of(jnp.minimum(j * FF_TILE, D_FF - FF_TILE), LANES)

    return pl.pallas_call(
        _ffn_kernel,
        grid=(m // FFN_ROW_TILE, n_ff),
        in_specs=[
            pl.BlockSpec(memory_space=pl.ANY),
            pl.BlockSpec((1, D_MODEL), lambda i, j: (0, 0)),
            pl.BlockSpec((pl.Element(D_MODEL), pl.Element(FF_TILE)),
                         lambda i, j: (0, ff_start(j))),
            pl.BlockSpec((pl.Element(D_MODEL), pl.Element(FF_TILE)),
                         lambda i, j: (0, pl.multiple_of(D_FF + ff_start(j), LANES))),
            pl.BlockSpec((pl.Element(FF_TILE), pl.Element(D_MODEL)),
                         lambda i, j: (ff_start(j), 0)),
        ],
        out_specs=pl.BlockSpec(memory_space=pl.ANY),
        out_shape=jax.ShapeDtypeStruct((m, D_MODEL), F32),
        scratch_shapes=[
            pltpu.VMEM((FFN_ROW_TILE, D_MODEL), F32),
            pltpu.VMEM((FFN_ROW_TILE, D_MODEL), BF16),
            pltpu.SemaphoreType.DMA((FFN_ROW_TILE // FFN_CHUNK,)),
            pltpu.SemaphoreType.DMA(((FFN_ROW_TILE // ROW_TILE) * (D_MODEL // COL_TILE),)),
        ],
        compiler_params=pltpu.CompilerParams(
            dimension_semantics=("arbitrary", "arbitrary"), vmem_limit_bytes=FFN_VMEM_LIMIT),
        name="ffn",
    )(x, gain.reshape(1, D_MODEL), w_in, w_in, w_out)


HEADS_PER_TILE = COL_TILE // HEAD_DIM
N_PROJ_TILES = D_QKV // COL_TILE
TILES_PER_GROUP = D_A // COL_TILE
A_TILES = 3 * TILES_PER_GROUP
AUG = 2 * HEAD_DIM
LOG2E = math.log2(math.e)


def _log_gate_scan(z, carry):
    c = (jnp.minimum(z, 0.0) - jnp.log1p(jnp.exp(-jnp.abs(z)))) * LOG2E
    row = lax.broadcasted_iota(jnp.int32, c.shape, 0)
    shift = 1
    while shift < c.shape[0]:
        c = c + jnp.where(row >= shift, pltpu.roll(c, shift, axis=0), 0.0)
        shift *= 2
    return c + carry


def _split3(c):
    hi = c.astype(BF16)
    rest = c - hi.astype(F32)
    mid = rest.astype(BF16)
    lo = (rest - mid.astype(F32)).astype(BF16)
    return hi, mid, lo


MXU_WIDTH = 256


def _head_mean_matrix():
    head = np.arange(MXU_WIDTH) // HEAD_DIM
    return jnp.asarray((head[:, None] == head[None, :]) / HEAD_DIM, BF16)


GATE_ONES_LANE = 3 * H_B


def _gate_placement_matrices():
    mats = np.zeros((2, TILES_PER_GROUP, LANES, COL_TILE), np.float32)
    for part in range(TILES_PER_GROUP):
        for hh in range(HEADS_PER_TILE):
            head = part * HEADS_PER_TILE + hh
            col = hh * HEAD_DIM
            for term in range(3):
                mats[0, part, term * H_B + head, col + term] = 1.0
                mats[0, part, GATE_ONES_LANE, col + 3 + term] = 1.0
                mats[1, part, term * H_B + head, col + 3 + term] = -1.0
                mats[1, part, GATE_ONES_LANE, col + term] = 1.0
    return jnp.asarray(mats.reshape(2 * TILES_PER_GROUP, LANES, COL_TILE), BF16)


def _proj_kernel(x_ref, g_ref, w_ref, wf_ref, fb_ref, hg_ref, mean_ref, place_ref,
                 a1_ref, a4_ref, a16_ref, qk_ref, vt_ref,
                 h_ref, cs_ref, carry_ref, y_ref, yf_ref, *, tiles_per_seq):
    i = pl.program_id(0)
    j = pl.program_id(1)

    @pl.when(j == 0)
    def _():
        h_ref[...] = _rms_rows(x_ref[...], g_ref[...]).astype(BF16)

        @pl.when(i % tiles_per_seq == 0)
        def _():
            carry_ref[...] = jnp.zeros(carry_ref.shape, F32)

        z = jnp.dot(h_ref[...], wf_ref[...], preferred_element_type=F32) + fb_ref[...]
        c = _log_gate_scan(z, carry_ref[...])
        carry_ref[...] = c[ROW_TILE - 1:ROW_TILE, :]
        lane = lax.broadcasted_iota(jnp.int32, c.shape, 1)
        packed = jnp.where(lane == GATE_ONES_LANE, 1.0, 0.0)
        for term, part in enumerate(_split3(c)):
            part = jnp.where(lane < H_B, part.astype(F32), 0.0)
            packed = packed + (pltpu.roll(part, term * H_B, axis=1) if term else part)
        cs_ref[...] = packed.astype(BF16)

    def product(transposed=False):
        lhs, rhs = (w_ref[...].astype(BF16), h_ref[...])
        if not transposed:
            lhs, rhs = rhs, lhs
        return lax.dot_general(lhs, rhs, (((1,), (1,)), ((), ())), preferred_element_type=F32)

    def normed():
        r = product()
        sq = (r * r).astype(BF16)
        ms = jnp.concatenate(
            [jnp.dot(sq[:, k:k + MXU_WIDTH], mean_ref[...], preferred_element_type=F32)
             for k in range(0, COL_TILE, MXU_WIDTH)], axis=1)
        return r * lax.rsqrt(ms + RMS_EPS) * hg_ref[...]

    def store_dilated(y):
        a1_ref[0, 0] = y.astype(BF16)
        fine, coarse = a4_ref.shape[1], a16_ref.shape[1]
        step = coarse // fine
        for c in range(HEADS_PER_TILE):
            sl = slice(c * LANES, (c + 1) * LANES)
            y_ref[c] = y[:, sl]
            for rf in range(fine):
                rows = y_ref[c, pl.ds(rf, ROW_TILE // fine, stride=fine), :]
                a4_ref[0, rf, :, sl] = rows.astype(BF16)
                yf_ref[c, rf] = rows
                for q in range(step):
                    rows = yf_ref[c, rf, pl.ds(q, ROW_TILE // coarse, stride=step), :]
                    a16_ref[0, rf + fine * q, :, sl] = rows.astype(BF16)

    def tiles(first_group, n_groups=1):
        lo = first_group * TILES_PER_GROUP
        return jnp.logical_and(j >= lo, j < lo + n_groups * TILES_PER_GROUP)

    def store_forgetting():
        y = normed().astype(BF16)
        aug = jnp.dot(cs_ref[...], place_ref[0], preferred_element_type=F32).astype(BF16)
        for hh in range(HEADS_PER_TILE):
            sl = slice(hh * HEAD_DIM, (hh + 1) * HEAD_DIM)
            qk_ref[0, hh, :, :HEAD_DIM] = y[:, sl]
            qk_ref[0, hh, :, HEAD_DIM:] = aug[:, sl]

    @pl.when(tiles(0, 2))
    def _():
        store_dilated(normed())

    @pl.when(tiles(2))
    def _():
        store_dilated(product())

    @pl.when(tiles(3, 2))
    def _():
        store_forgetting()

    @pl.when(tiles(5))
    def _():
        v_t = product(transposed=True).astype(BF16)
        for hh in range(HEADS_PER_TILE):
            vt_ref[0, hh] = v_t[hh * HEAD_DIM:(hh + 1) * HEAD_DIM, :]


def _proj(x, b, s, gain, w_in, layer, w_f, f_bias, head_gain):
    m = x.shape[0]
    tps = s // ROW_TILE

    def a_spec(dilation):
        return pl.BlockSpec((1, dilation, ROW_TILE // dilation, COL_TILE),
                            lambda i, j: (i // tps, 0, i % tps, jnp.minimum(j, A_TILES - 1)))

    def qk_block(i, j):
        t = jnp.clip(j - qb_first, 0, 2 * TILES_PER_GROUP - 1)
        return (t // TILES_PER_GROUP, i // tps, t % TILES_PER_GROUP, i % tps, 0)

    qb_first = A_TILES
    vt_first = A_TILES + 2 * TILES_PER_GROUP
    dilations = [d for _, d in DILATED_PATTERNS]
    return pl.pallas_call(
        functools.partial(_proj_kernel, tiles_per_seq=tps),
        grid=(m // ROW_TILE, N_PROJ_TILES),
        in_specs=[
            pl.BlockSpec((ROW_TILE, D_MODEL), lambda i, j: (i, 0)),
            pl.BlockSpec((1, D_MODEL), lambda i, j: (0, 0)),
            pl.BlockSpec((None, COL_TILE, D_MODEL), lambda i, j: (layer, j, 0)),
            pl.BlockSpec((D_MODEL, LANES), lambda i, j: (0, 0)),
            pl.BlockSpec((1, LANES), lambda i, j: (0, 0)),
            pl.BlockSpec((1, COL_TILE), lambda i, j: (0, j)),
            pl.BlockSpec((MXU_WIDTH, MXU_WIDTH), lambda i, j: (0, 0)),
            pl.BlockSpec((1, LANES, COL_TILE),
                         lambda i, j: (jnp.clip(j - qb_first, 0, 2 * TILES_PER_GROUP - 1), 0, 0)),
        ],
        out_specs=[a_spec(d) for d in dilations] + [
            pl.BlockSpec((None, 1, HEADS_PER_TILE, ROW_TILE, AUG), qk_block),
            pl.BlockSpec((1, HEADS_PER_TILE, HEAD_DIM, ROW_TILE),
                         lambda i, j: (i // tps, jnp.clip(j - vt_first, 0, TILES_PER_GROUP - 1),
                                       0, i % tps)),
        ],
        out_shape=[jax.ShapeDtypeStruct((b, d, s // d, 3 * D_A), BF16) for d in dilations] + [
            jax.ShapeDtypeStruct((2, b, H_B, s, AUG), BF16),
            jax.ShapeDtypeStruct((b, H_B, HEAD_DIM, s), BF16),
        ],
        scratch_shapes=[
            pltpu.VMEM((ROW_TILE, D_MODEL), BF16),
            pltpu.VMEM((ROW_TILE, LANES), BF16),
            pltpu.VMEM((1, LANES), F32),
            pltpu.VMEM((HEADS_PER_TILE, ROW_TILE, LANES), F32),
            pltpu.VMEM((HEADS_PER_TILE, dilations[1], ROW_TILE // dilations[1], LANES), F32),
        ],
        compiler_params=_params("arbitrary", "arbitrary"),
        name="proj",
    )(x, gain.reshape(1, D_MODEL), jnp.swapaxes(w_in, 1, 2), w_f, f_bias, head_gain,
      _head_mean_matrix(), _gate_placement_matrices())


def _bucket_steps(dilation):
    dist = np.arange(BAND + 1) * dilation
    max_exact = NUM_BUCKETS // 2
    large = max_exact + np.floor(
        np.log(np.maximum(dist, 1) / max_exact) / math.log(MAX_DISTANCE / max_exact)
        * (NUM_BUCKETS - max_exact)).astype(np.int64)
    bucket = np.where(dist < max_exact, dist, np.minimum(large, NUM_BUCKETS - 1))
    steps = [(0, int(bucket[0]))]
    for delta in range(1, BAND + 1):
        if bucket[delta] != bucket[delta - 1]:
            steps.append((delta, int(bucket[delta])))
    return steps


def _bias_kernel(table_ref, o_ref):
    iq = lax.broadcasted_iota(jnp.int32, (BAND, 2 * BAND), 0)
    ik = lax.broadcasted_iota(jnp.int32, (BAND, 2 * BAND), 1)
    delta = iq + BAND - ik
    in_band = jnp.logical_and(delta >= 0, delta <= BAND)
    for p, (_, dilation) in enumerate(DILATED_PATTERNS):
        steps = _bucket_steps(dilation)
        for h in range(H_A):
            val = jnp.full((BAND, 2 * BAND), table_ref[steps[0][1], h], F32)
            for start, bucket in steps[1:]:
                val = jnp.where(delta >= start, table_ref[bucket, h], val)
            o_ref[p, h] = jnp.where(in_band, val * LOG2E, NEG_INF)


def _band_bias(rel_table):
    n_pat = len(DILATED_PATTERNS)
    return pl.pallas_call(
        _bias_kernel,
        in_specs=[pl.BlockSpec(memory_space=pltpu.SMEM)],
        out_specs=pl.BlockSpec(memory_space=pltpu.VMEM),
        out_shape=jax.ShapeDtypeStruct((n_pat, H_A, BAND, 2 * BAND), F32),
        name="band_bias",
    )(rel_table)


DIL_UNITS = 8


def _dilated_kernel(q_ref, kp_ref, kc_ref, vp_ref, vc_ref, bias_ref, o_ref, st_ref):
    n_cls = q_ref.shape[1]
    n_blk = q_ref.shape[2] // BAND
    first = pl.program_id(2) == 0
    key_lane = lax.broadcasted_iota(jnp.int32, (1, 2 * BAND), 1)
    no_prev = jnp.where(jnp.logical_and(first, key_lane < BAND), NEG_INF, 0.0)
    stat_lane = lax.broadcasted_iota(jnp.int32, (BAND, LANES), 1)
    units = [(c, blk) for c in range(n_cls) for blk in range(n_blk)]

    def keys(prev_ref, cur_ref, c, blk, sl):
        if blk == 0:
            return jnp.concatenate([prev_ref[0, c, :, sl], cur_ref[0, c, :BAND, sl]], axis=0)
        return cur_ref[0, c, (blk - 1) * BAND:(blk + 1) * BAND, sl]

    scores = {}
    for c, blk in units:
        for h in range(H_A):
            sl = slice(h * HEAD_DIM, (h + 1) * HEAD_DIM)
            q = q_ref[0, c, blk * BAND:(blk + 1) * BAND, sl]
            scores[c, blk, h] = lax.dot_general(q, keys(kp_ref, kc_ref, c, blk, sl),
                                                (((1,), (1,)), ((), ())),
                                                preferred_element_type=F32)
    for c, blk in units:
        rows = slice(blk * BAND, (blk + 1) * BAND)
        stats = jnp.zeros((BAND, LANES), F32)
        for h in range(H_A):
            sl = slice(h * HEAD_DIM, (h + 1) * HEAD_DIM)
            s = scores[c, blk, h] + bias_ref[0, h]
            if blk == 0:
                s = s + no_prev
            mx = jnp.max(s, axis=-1, keepdims=True)
            p = jnp.exp2(s - mx)
            den = jnp.sum(p, axis=-1, keepdims=True)
            o_ref[0, c, rows, sl] = jnp.dot(p.astype(BF16), keys(vp_ref, vc_ref, c, blk, sl),
                                            preferred_element_type=F32).astype(o_ref.dtype)
            stats = jnp.where(stat_lane == h, mx, stats)
            stats = jnp.where(stat_lane == H_A + h, den, stats)
        st_ref[0, c, rows, :] = stats


def _dilated(qkv, bias, pattern):
    b, dilation, n_sub, _ = qkv.shape
    n_blk = min(DIL_UNITS, n_sub // BAND)
    n_cls = DIL_UNITS // n_blk
    rows = n_blk * BAND
    nb = n_sub // rows

    def cur(which):
        return pl.BlockSpec((1, n_cls, rows, D_A), lambda bi, r, i: (bi, r, i, which))

    def prev(which):
        return pl.BlockSpec((1, n_cls, BAND, D_A),
                            lambda bi, r, i: (bi, r, jnp.maximum(i * n_blk - 1, 0), which))

    return pl.pallas_call(
        _dilated_kernel,
        grid=(b, dilation // n_cls, nb),
        in_specs=[
            cur(0), prev(1), cur(1), prev(2), cur(2),
            pl.BlockSpec((1, H_A, BAND, 2 * BAND), lambda bi, r, i: (pattern, 0, 0, 0)),
        ],
        out_specs=[
            pl.BlockSpec((1, n_cls, rows, D_A), lambda bi, r, i: (bi, r, i, 0)),
            pl.BlockSpec((1, n_cls, rows, LANES), lambda bi, r, i: (bi, r, i, 0)),
        ],
        out_shape=[
            jax.ShapeDtypeStruct((b, dilation, n_sub, D_A), BF16),
            jax.ShapeDtypeStruct((b, dilation, n_sub, LANES), F32),
        ],
        compiler_params=_params("parallel", "parallel", "arbitrary"),
        name=f"dilated_{dilation}",
    )(qkv, qkv, qkv, qkv, qkv, bias)


COMBINE_ROWS = 512


def _combine_kernel(*refs):
    n_pat = len(DILATED_PATTERNS)
    num_refs, st_refs = refs[:n_pat], refs[n_pat:2 * n_pat]
    o_ref, num_buf, st_buf, num_tmp, st_tmp = refs[2 * n_pat:]
    dilations = [ref.shape[1] for ref in num_refs]
    fine = min(d for d in dilations if d > 1)
    heads = [slice(h * HEAD_DIM, (h + 1) * HEAD_DIM) for h in range(H_A)]
    for p, dilation in enumerate(dilations):
        if dilation == 1:
            continue
        step = dilation // fine
        for rf in range(fine):
            rows = pl.ds(rf, COMBINE_ROWS // fine, stride=fine)
            if step == 1:
                st_buf[p, rows, :] = st_refs[p][0, rf]
                for h, sl in enumerate(heads):
                    num_buf[p, h, rows, :] = num_refs[p][0, rf, :, sl].astype(F32)
                continue
            for q in range(step):
                part = pl.ds(q, COMBINE_ROWS // dilation, stride=step)
                st_tmp[part, :] = st_refs[p][0, rf + fine * q]
                for h, sl in enumerate(heads):
                    num_tmp[h, part, :] = num_refs[p][0, rf + fine * q, :, sl].astype(F32)
            st_buf[p, rows, :] = st_tmp[...]
            for h in range(H_A):
                num_buf[p, h, rows, :] = num_tmp[h]
    stats = [st_refs[p][0, 0] if d == 1 else st_buf[p] for p, d in enumerate(dilations)]
    lane = lax.broadcasted_iota(jnp.int32, stats[0].shape, 1)
    mx = functools.reduce(jnp.maximum, stats)
    scales = [jnp.exp2(st - mx) for st in stats]
    den = sum(sc * pltpu.roll(st, LANES - H_A, axis=1) for sc, st in zip(scales, stats))
    den = jnp.where(lane < H_A, den, 1.0)
    weights = [sc / den for sc in scales]
    for h in range(H_A):
        sl = slice(h * HEAD_DIM, (h + 1) * HEAD_DIM)
        out = None
        for p, w in enumerate(weights):
            if dilations[p] == 1:
                pnum = num_refs[p][0, 0, :, sl].astype(F32)
            else:
                pnum = num_buf[p, h]
            out = pnum * w[:, h:h + 1] if out is None else out + pnum * w[:, h:h + 1]
        o_ref[:, sl] = out.astype(BF16)


def _combine(nums, stats):
    b, _, s, _ = nums[0].shape
    tps = s // COMBINE_ROWS

    def spec(arr):
        dilation, width = arr.shape[1], arr.shape[3]
        return pl.BlockSpec((1, dilation, COMBINE_ROWS // dilation, width),
                            lambda i: (i // tps, 0, i % tps, 0))

    n_pat = len(nums)
    fine = min(a.shape[1] for a in nums if a.shape[1] > 1)
    return pl.pallas_call(
        _combine_kernel,
        grid=(b * tps,),
        in_specs=[spec(a) for a in nums] + [spec(a) for a in stats],
        out_specs=pl.BlockSpec((COMBINE_ROWS, D_A), lambda i: (i, 0)),
        out_shape=jax.ShapeDtypeStruct((b * s, D_A), BF16),
        scratch_shapes=[
            pltpu.VMEM((n_pat, H_A, COMBINE_ROWS, HEAD_DIM), F32),
            pltpu.VMEM((n_pat, COMBINE_ROWS, LANES), F32),
            pltpu.VMEM((H_A, COMBINE_ROWS // fine, HEAD_DIM), F32),
            pltpu.VMEM((COMBINE_ROWS // fine, LANES), F32),
        ],
        compiler_params=_params("parallel"),
        name="combine_a",
    )(*nums, *stats)


def _fox_kernel(q_ref, k_ref, vt_ref, o_ref, m_ref, l_ref, acc_ref, sa_ref, sb_ref):
    iq = pl.program_id(2)
    m_ref[...] = jnp.full(m_ref.shape, NEG_INF, F32)
    l_ref[...] = jnp.zeros(l_ref.shape, F32)
    acc_ref[...] = jnp.zeros(acc_ref.shape, F32)

    def scores_into(s_ref, kb):
        start = pl.multiple_of(kb * FOX_TK, FOX_TK)
        for hh in range(FOX_HEADS):
            k = k_ref[0, hh, pl.ds(start, FOX_TK), :]
            s_ref[hh] = lax.dot_general(k, q_ref[0, hh], (((1,), (1,)), ((), ())),
                                        preferred_element_type=F32)

    def accumulate(s_ref, kb, diagonal):
        start = pl.multiple_of(kb * FOX_TK, FOX_TK)
        n_parts = 2 if diagonal else 1
        width = FOX_TQ // n_parts
        for hh in range(FOX_HEADS):
            for part in range(n_parts):
                cols = slice(part * width, (part + 1) * width)
                n_keys = (part + 1) * FOX_TK // n_parts
                st = s_ref[hh, :n_keys, cols]
                if diagonal:
                    key = lax.broadcasted_iota(jnp.int32, st.shape, 0)
                    qry = lax.broadcasted_iota(jnp.int32, st.shape, 1) + part * width
                    st = jnp.where(key <= qry, st, NEG_INF)
                m_prev = m_ref[hh, :, cols]
                m_new = jnp.maximum(m_prev, jnp.max(st, axis=0, keepdims=True))
                alpha = jnp.exp2(m_prev - m_new)
                p = jnp.exp2(st - m_new)
                l_ref[hh, :, cols] = alpha * l_ref[hh, :, cols] + jnp.sum(p, axis=0,
                                                                         keepdims=True)
                vt = vt_ref[0, hh, :, pl.ds(start, n_keys)]
                acc_ref[hh, :, cols] = alpha * acc_ref[hh, :, cols] + jnp.dot(
                    vt, p.astype(BF16), preferred_element_type=F32)
                m_ref[hh, :, cols] = m_new

    scores_into(sa_ref, 0)

    def pair(p, carry):
        scores_into(sb_ref, 2 * p + 1)
        accumulate(sa_ref, 2 * p, False)
        scores_into(sa_ref, 2 * p + 2)
        accumulate(sb_ref, 2 * p + 1, False)
        return carry

    lax.fori_loop(0, iq // 2, pair, 0)

    @pl.when(iq % 2 == 0)
    def _():
        accumulate(sa_ref, iq, True)

    @pl.when(iq % 2 == 1)
    def _():
        scores_into(sb_ref, iq)
        accumulate(sa_ref, iq - 1, False)
        accumulate(sb_ref, iq, True)
    for hh in range(FOX_HEADS):
        o_ref[0, :, hh * HEAD_DIM:(hh + 1) * HEAD_DIM] = (
            acc_ref[hh] / l_ref[hh]).T.astype(BF16)


def _fox(qk_aug, v_t):
    _, b, h_b, s, _ = qk_aug.shape
    assert FOX_TQ == FOX_TK
    return pl.pallas_call(
        _fox_kernel,
        grid=(b, h_b // FOX_HEADS, s // FOX_TQ),
        in_specs=[
            pl.BlockSpec((None, 1, FOX_HEADS, FOX_TQ, AUG), lambda bi, h, i: (0, bi, h, i, 0)),
            pl.BlockSpec((None, 1, FOX_HEADS, s, AUG), lambda bi, h, i: (1, bi, h, 0, 0)),
            pl.BlockSpec((1, FOX_HEADS, HEAD_DIM, s), lambda bi, h, i: (bi, h, 0, 0)),
        ],
        out_specs=pl.BlockSpec((1, FOX_TQ, FOX_HEADS * HEAD_DIM), lambda bi, h, i: (bi, i, h)),
        out_shape=jax.ShapeDtypeStruct((b, s, h_b * HEAD_DIM), BF16),
        scratch_shapes=[
            pltpu.VMEM((FOX_HEADS, 1, FOX_TQ), F32),
            pltpu.VMEM((FOX_HEADS, 1, FOX_TQ), F32),
            pltpu.VMEM((FOX_HEADS, HEAD_DIM, FOX_TQ), F32),
            pltpu.VMEM((FOX_HEADS, FOX_TK, FOX_TQ), F32),
            pltpu.VMEM((FOX_HEADS, FOX_TK, FOX_TQ), F32),
        ],
        compiler_params=_params("parallel", "parallel", "arbitrary"),
        name="fox",
    )(qk_aug, qk_aug, v_t)


def _out_proj_kernel(a_ref, b_ref, w_ref, x_ref, o_ref):
    mixed = jnp.concatenate([a_ref[...], b_ref[...]], axis=-1)
    o_ref[...] = x_ref[...] + jnp.dot(mixed, w_ref[...].astype(BF16),
                                      preferred_element_type=F32)


OUT_ROW_TILE = 2048


def _out_proj(out_a, out_b, w_out, x):
    m = x.shape[0]
    return pl.pallas_call(
        _out_proj_kernel,
        grid=(m // OUT_ROW_TILE, D_MODEL // COL_TILE),
        in_specs=[
            pl.BlockSpec((OUT_ROW_TILE, D_A), lambda i, j: (i, 0)),
            pl.BlockSpec((OUT_ROW_TILE, D_B), lambda i, j: (i, 0)),
            pl.BlockSpec((D_A + D_B, COL_TILE), lambda i, j: (0, j)),
            pl.BlockSpec((OUT_ROW_TILE, COL_TILE), lambda i, j: (i, j)),
        ],
        out_specs=pl.BlockSpec((OUT_ROW_TILE, COL_TILE), lambda i, j: (i, j)),
        out_shape=jax.ShapeDtypeStruct((m, D_MODEL), F32),
        compiler_params=_params("parallel", "parallel"),
        name="out_proj",
    )(out_a, out_b, w_out, x)


def _mixer(x, b, s, mix_norm, w_in, layer, q_norm_a, k_norm_a, q_norm_b, k_norm_b, forget_bias,
           rel_bias_table, w_out):
    ones = jnp.ones((D_A,), F32)
    head_gain = jnp.concatenate([
        jnp.tile(q_norm_a * (ATTN_SCALE * LOG2E), H_A), jnp.tile(k_norm_a, H_A), ones,
        jnp.tile(q_norm_b * (ATTN_SCALE * LOG2E), H_B), jnp.tile(k_norm_b, H_B), ones,
    ]).reshape(1, D_QKV)
    w_f = jnp.pad(w_in[layer, :, D_QKV:], ((0, 0), (0, LANES - H_B))).astype(BF16)
    f_bias = jnp.pad(forget_bias, (0, LANES - H_B)).reshape(1, LANES)
    *qkv_a, qk_aug, v_t = _proj(x, b, s, mix_norm, w_in, layer, w_f, f_bias, head_gain)

    out_b = _fox(qk_aug, v_t).reshape(b * s, D_B)

    bias = _band_bias(rel_bias_table)
    parts = [_dilated(qkv, bias, p) for p, qkv in enumerate(qkv_a)]
    out_a = _combine([pt[0] for pt in parts], [pt[1] for pt in parts])

    return _out_proj(out_a, out_b, w_out, x)


def kernel(x, ffn1_norm, ffn1_w_in, ffn1_w_out, mix_norm, w_in, q_norm_a, k_norm_a, q_norm_b,
           k_norm_b, forget_bias, rel_bias_table, w_out, ffn2_norm, ffn2_w_in, ffn2_w_out):
    b, s, d = x.shape
    depth = ffn1_norm.shape[0]
    x = x.reshape(b * s, d)
    for l in range(depth):
        x = _ffn(x, ffn1_norm[l], ffn1_w_in[l], ffn1_w_out[l])
        x = _mixer(x, b, s, mix_norm[l], w_in, l, q_norm_a[l], k_norm_a[l], q_norm_b[l],
                   k_norm_b[l], forget_bias[l], rel_bias_table, w_out[l])
        x = _ffn(x, ffn2_norm[l], ffn2_w_in[l], ffn2_w_out[l])
    return x.reshape(b, s, d)
```
